```python
import jax
import jax.numpy as jnp
from jax import lax
import numpy as np

D_MODEL = 1024
BATCH = 8
SEQ = 2048
DEPTH = 2
DEC_BATCH = 32
DEC_SEQ = 4
PAST_LEN = 16384
PAGE_SIZE = 128

N_A_LAYERS = DEPTH // 2
N_B_LAYERS = DEPTH - N_A_LAYERS
GLA_HEADS = 4
GLA_QK = D_MODEL // 2
GLA_V = D_MODEL
GLA_DK = GLA_QK // GLA_HEADS
GLA_DV = GLA_V // GLA_HEADS
GLA_RANK = 16
GLA_TAU = 16.0
GLA_CHUNK = 64
GLA_IN = 2 * GLA_QK + GLA_V + GLA_RANK + GLA_V
GLA_NORM_EPS = 1e-6
DIL_HEADS = 16
DIL_HD = D_MODEL // DIL_HEADS
DIL_WINDOWS = (128, 512, 2048)
DIL_DILATIONS = (1, 4, 16)
N_GROUPS = len(DIL_WINDOWS)
MAX_WINDOW = max(DIL_WINDOWS)
DIL_SCALE = DIL_HD ** -0.5
ROPE_THETA = 10000.0
D_FF = 2816
FFN_RES = 0.5
ALPHA = (2 * DEPTH) ** 0.25
BETA = (8 * DEPTH) ** -0.25
LN_EPS = 1e-5
N_MOD = 9

kernel_name = 'yoco_gla_dilated_macaron_step'


def layer_norm(x, g, b):
    xf = x.astype(jnp.float32)
    mu = jnp.mean(xf, axis=-1, keepdims=True)
    var = jnp.mean(jnp.square(xf - mu), axis=-1, keepdims=True)
    return ((xf - mu) * lax.rsqrt(var + LN_EPS) * g + b).astype(x.dtype)


def modulate(x, shift, scale):
    return x * (1.0 + scale[:, None, :]) + shift[:, None, :]


def post_norm(x, y, gate, g, b):
    return layer_norm(ALPHA * x + (1.0 + gate[:, None, :]) * y, g, b)


def swiglu(h, w_up, w_down):
    a, u = jnp.split(h @ w_up, 2, axis=-1)
    return (jax.nn.silu(a) * u) @ w_down


def rope(x, positions):
    half = x.shape[-1] // 2
    inv = ROPE_THETA ** (-jnp.arange(half, dtype=jnp.float32) / half)
    ang = positions.astype(jnp.float32)[:, None] * inv[None, :]
    cos, sin = jnp.cos(ang)[None, :, None, :], jnp.sin(ang)[None, :, None, :]
    xf = x.astype(jnp.float32)
    x1, x2 = xf[..., :half], xf[..., half:]
    return jnp.concatenate([x1 * cos - x2 * sin, x2 * cos + x1 * sin], axis=-1).astype(x.dtype)


def masked_softmax(s, mask):
    s = jnp.where(mask, s, -jnp.inf)
    m = jnp.max(s, axis=-1, keepdims=True)
    e = jnp.exp(s - m)
    den = jnp.sum(e, axis=-1, keepdims=True)
    return e / den, (m + jnp.log(den))[..., 0]


def gla_chunk_step(s0, chunk):
    q, k, v, la = chunk
    C = q.shape[1]
    b = jnp.cumsum(la, axis=1)
    o_inter = jnp.einsum('bchk,bhkv->bchv', q * jnp.exp(b), s0)
    causal = jnp.asarray(np.tril(np.ones((C, C), dtype=bool)))
    diff = b[:, :, None] - b[:, None, :]
    decay = jnp.exp(jnp.where(causal[None, :, :, None, None], diff, -jnp.inf))
    scores = jnp.sum(q[:, :, None] * k[:, None, :] * decay, axis=-1)
    o_intra = jnp.einsum('bijh,bjhv->bihv', scores, v)
    b_last = b[:, -1]
    s_new = jnp.exp(b_last)[..., None] * s0 + jnp.einsum(
        'bjhk,bjhv->bhkv', k * jnp.exp(b_last[:, None] - b), v)
    return s_new, o_inter + o_intra


def gla_recurrence(q, k, v, la, s0):
    B, T = q.shape[:2]
    C = GLA_CHUNK if T % GLA_CHUNK == 0 else T
    n = T // C

    def to_chunks(a):
        return jnp.moveaxis(a.reshape(B, n, C, *a.shape[2:]), 1, 0)

    s_fin, o = lax.scan(gla_chunk_step, s0, (to_chunks(q), to_chunks(k), to_chunks(v), to_chunks(la)))
    return jnp.moveaxis(o, 0, 1).reshape(B, T, GLA_HEADS, GLA_DV), s_fin


def gla_mixer(h, s0, w_in, w_gate2, b_gate, g_onorm, w_out):
    B, T, _ = h.shape
    proj = h @ w_in
    q, k, v, g_lr, r = jnp.split(
        proj, [GLA_QK, 2 * GLA_QK, 2 * GLA_QK + GLA_V, 2 * GLA_QK + GLA_V + GLA_RANK], axis=-1)
    f32 = jnp.float32
    q = q.reshape(B, T, GLA_HEADS, GLA_DK).astype(f32) * (GLA_DK ** -0.5)
    k = k.reshape(B, T, GLA_HEADS, GLA_DK).astype(f32)
    v = v.reshape(B, T, GLA_HEADS, GLA_DV).astype(f32)
    la = (jax.nn.log_sigmoid((g_lr @ w_gate2 + b_gate).astype(f32)) / GLA_TAU).reshape(B, T, GLA_HEADS, GLA_DK)
    o, s_fin = gla_recurrence(q, k, v, la, s0.astype(f32))
    o = o * lax.rsqrt(jnp.mean(jnp.square(o), axis=-1, keepdims=True) + GLA_NORM_EPS) * g_onorm
    o = o.astype(h.dtype).reshape(B, T, GLA_V) * jax.nn.silu(r)
    return o @ w_out, s_fin.astype(s0.dtype)


def dilated_group_prompt(q, k, v, dil, steps):
    B, S, H, E = q.shape
    L = S // dil
    nb = -(-L // steps)
    Lp = nb * steps

    def by_residue(a):
        return a.reshape(B, L, dil, H, E).transpose(0, 2, 1, 3, 4)

    qb = jnp.pad(by_residue(q), ((0, 0), (0, 0), (0, Lp - L), (0, 0), (0, 0))).reshape(B, dil, nb, steps, H, E)

    def key_blocks(a):
        ap = jnp.pad(by_residue(a), ((0, 0), (0, 0), (steps, Lp - L), (0, 0), (0, 0)))
        return jnp.concatenate([ap[:, :, :Lp].reshape(B, dil, nb, steps, H, E),
                                ap[:, :, steps:].reshape(B, dil, nb, steps, H, E)], axis=3)

    kb, vb = key_blocks(k), key_blocks(v)
    s = jnp.einsum('bdiqhe,bdikhe->bdihqk', qb, kb, preferred_element_type=jnp.float32) * DIL_SCALE
    qi = np.arange(steps)[:, None]
    ki = np.arange(2 * steps)[None, :]
    dist = qi - ki + steps
    band = (dist >= 0) & (dist <= steps)
    mask = band[None] & ((ki >= steps)[None] | (np.arange(nb)[:, None, None] > 0))
    p, lse = masked_softmax(s, jnp.asarray(mask)[None, None, :, None])
    o = jnp.einsum('bdihqk,bdikhe->bdiqhe', p, vb.astype(jnp.float32))
    o = o.reshape(B, dil, Lp, H, E)[:, :, :L].transpose(0, 2, 1, 3, 4).reshape(B, S, H, E)
    lse = lse.transpose(0, 1, 2, 4, 3).reshape(B, dil, Lp, H)[:, :, :L].transpose(0, 2, 1, 3).reshape(B, S, H)
    return o, lse


def dilated_group_sample(q, k_ctx, v_ctx, dil, steps, wb):
    T = q.shape[1]
    idx = wb + np.arange(T)[:, None] - dil * np.arange(steps + 1)[None, :]
    valid = idx >= 0
    idx = np.maximum(idx, 0)
    kg = k_ctx[:, idx]
    vg = v_ctx[:, idx]
    s = jnp.einsum('bthe,btjhe->bhtj', q, kg, preferred_element_type=jnp.float32) * DIL_SCALE
    p, lse = masked_softmax(s, jnp.asarray(valid)[None, None])
    o = jnp.einsum('bhtj,btjhe->bthe', p, vg.astype(jnp.float32))
    return o, lse.transpose(0, 2, 1)


def dilated_mixer(h, positions, k_ctx, v_ctx, w_q, w_out, is_prompt):
    B, T, _ = h.shape
    q = rope((h @ w_q).reshape(B, T, N_GROUPS * DIL_HEADS, DIL_HD), positions)
    q = q.reshape(B, T, N_GROUPS, DIL_HEADS, DIL_HD)
    outs, lses = [], []
    for g in range(N_GROUPS):
        dil = DIL_DILATIONS[g]
        steps = DIL_WINDOWS[g] // dil
        if is_prompt:
            o, lse = dilated_group_prompt(q[:, :, g], k_ctx, v_ctx, dil, steps)
        else:
            o, lse = dilated_group_sample(q[:, :, g], k_ctx, v_ctx, dil, steps, k_ctx.shape[1] - T)
        outs.append(o)
        lses.append(lse)
    w = jax.nn.softmax(jnp.stack(lses), axis=0)
    o = jnp.einsum('gbth,gbthe->bthe', w, jnp.stack(outs))
    return o.astype(h.dtype).reshape(B, T, DIL_HEADS * DIL_HD) @ w_out


def trunk(x, c, positions, gla_states, k_past, v_past, p):
    B, T, _ = x.shape
    silu_c = jax.nn.silu(c)
    new_states = []
    k_rows = v_rows = k_ctx = v_ctx = None
    for l in range(DEPTH):
        if l == N_A_LAYERS:
            kv_shift, kv_scale = jnp.split(silu_c @ p['w_ada_kv'] + p['b_ada_kv'], 2, axis=-1)
            k_rows, v_rows = jnp.split(modulate(x, kv_shift, kv_scale) @ p['w_kv'], 2, axis=-1)
            k_rows = rope(k_rows.reshape(B, T, DIL_HEADS, DIL_HD), positions)
            v_rows = v_rows.reshape(B, T, DIL_HEADS, DIL_HD)
            if k_past is None:
                k_ctx, v_ctx = k_rows, v_rows
            else:
                k_ctx = jnp.concatenate([k_past.astype(k_rows.dtype), k_rows], axis=1)
                v_ctx = jnp.concatenate([v_past.astype(v_rows.dtype), v_rows], axis=1)
        mods = (silu_c @ p['w_ada'][l] + p['b_ada'][l]).reshape(B, N_MOD, D_MODEL)
        sh1, sc1, g1, sh2, sc2, g2, sh3, sc3, g3 = [mods[:, i] for i in range(N_MOD)]
        y = swiglu(modulate(x, sh1, sc1), p['w_ffn1_up'][l], p['w_ffn1_down'][l])
        x = post_norm(x, FFN_RES * y, g1, p['ln_g'][l, 0], p['ln_b'][l, 0])
        h = modulate(x, sh2, sc2)
        if l < N_A_LAYERS:
            y, s_new = gla_mixer(h, gla_states[l], p['w_in_a'][l], p['w_gate2_a'][l], p['b_gate_a'][l],
                                 p['g_onorm_a'][l], p['w_out_a'][l])
            new_states.append(s_new)
        else:
            j = l - N_A_LAYERS
            y = dilated_mixer(h, positions, k_ctx, v_ctx, p['w_q_b'][j], p['w_out_b'][j], k_past is None)
        x = post_norm(x, y, g2, p['ln_g'][l, 1], p['ln_b'][l, 1])
        y = swiglu(modulate(x, sh3, sc3), p['w_ffn2_up'][l], p['w_ffn2_down'][l])
        x = post_norm(x, FFN_RES * y, g3, p['ln_g'][l, 2], p['ln_b'][l, 2])
    return x, jnp.stack(new_states), k_rows, v_rows


def setup_inputs(seed: int = 0) -> dict:
    key = jax.random.key(seed)
    ks = iter(jax.random.split(key, 32))

    def nrm(shape, s):
        return jax.random.normal(next(ks), shape, jnp.float32) * s

    wb = min(MAX_WINDOW, PAST_LEN)
    return {
        'x_prompt': nrm((BATCH, SEQ, D_MODEL), 1.0),
        'x_sample': nrm((DEC_BATCH, DEC_SEQ, D_MODEL), 1.0),
        'state_gla': nrm((N_A_LAYERS, DEC_BATCH, GLA_HEADS, GLA_DK, GLA_DV), 1.0),
        'cache_k': nrm((DEC_BATCH, wb, DIL_HEADS, DIL_HD), 1.0),
        'cache_v': nrm((DEC_BATCH, wb, DIL_HEADS, DIL_HD), 1.0),
        'c_prompt': nrm((BATCH, D_MODEL), 1.0),
        'c_sample': nrm((DEC_BATCH, D_MODEL), 1.0),
        'w_ada': nrm((DEPTH, D_MODEL, N_MOD * D_MODEL), 0.01),
        'b_ada': nrm((DEPTH, N_MOD * D_MODEL), 0.01),
        'ln_g': 1.0 + nrm((DEPTH, 3, D_MODEL), 0.02),
        'ln_b': nrm((DEPTH, 3, D_MODEL), 0.02),
        'w_ffn1_up': nrm((DEPTH, D_MODEL, 2 * D_FF), D_MODEL ** -0.5),
        'w_ffn1_down': nrm((DEPTH, D_FF, D_MODEL), D_FF ** -0.5 * BETA),
        'w_ffn2_up': nrm((DEPTH, D_MODEL, 2 * D_FF), D_MODEL ** -0.5),
        'w_ffn2_down': nrm((DEPTH, D_FF, D_MODEL), D_FF ** -0.5 * BETA),
        'w_in_a': nrm((N_A_LAYERS, D_MODEL, GLA_IN), D_MODEL ** -0.5),
        'w_gate2_a': nrm((N_A_LAYERS, GLA_RANK, GLA_QK), GLA_RANK ** -0.5),
        'b_gate_a': nrm((N_A_LAYERS, GLA_QK), 0.1),
        'g_onorm_a': 1.0 + nrm((N_A_LAYERS, GLA_DV), 0.02),
        'w_out_a': nrm((N_A_LAYERS, GLA_V, D_MODEL), GLA_V ** -0.5 * BETA),
        'w_ada_kv': nrm((D_MODEL, 2 * D_MODEL), 0.01),
        'b_ada_kv': nrm((2 * D_MODEL,), 0.01),
        'w_kv': nrm((D_MODEL, 2 * DIL_HEADS * DIL_HD), D_MODEL ** -0.5),
        'w_q_b': nrm((N_B_LAYERS, D_MODEL, N_GROUPS * DIL_HEADS * DIL_HD), D_MODEL ** -0.5),
        'w_out_b': nrm((N_B_LAYERS, DIL_HEADS * DIL_HD, D_MODEL), (DIL_HEADS * DIL_HD) ** -0.5 * BETA),
    }


def reference(x_prompt, x_sample, state_gla, cache_k, cache_v, c_prompt, c_sample,
              w_ada, b_ada, ln_g, ln_b, w_ffn1_up, w_ffn1_down, w_ffn2_up, w_ffn2_down,
              w_in_a, w_gate2_a, b_gate_a, g_onorm_a, w_out_a,
              w_ada_kv, b_ada_kv, w_kv, w_q_b, w_out_b):
    p = {
        'w_ada': w_ada, 'b_ada': b_ada, 'ln_g': ln_g, 'ln_b': ln_b,
        'w_ffn1_up': w_ffn1_up, 'w_ffn1_down': w_ffn1_down,
        'w_ffn2_up': w_ffn2_up, 'w_ffn2_down': w_ffn2_down,
        'w_in_a': w_in_a, 'w_gate2_a': w_gate2_a, 'b_gate_a': b_gate_a,
        'g_onorm_a': g_onorm_a, 'w_out_a': w_out_a,
        'w_ada_kv': w_ada_kv, 'b_ada_kv': b_ada_kv, 'w_kv': w_kv,
        'w_q_b': w_q_b, 'w_out_b': w_out_b,
    }
    n_prompt, t_prompt = x_prompt.shape[0], x_prompt.shape[1]
    t_sample = x_sample.shape[1]
    zero_state = jnp.zeros((N_A_LAYERS, n_prompt, GLA_HEADS, GLA_DK, GLA_DV), x_prompt.dtype)
    y_prompt, state_gla_prompt, k_p, v_p = trunk(
        x_prompt, c_prompt, jnp.arange(t_prompt), zero_state, None, None, p)
    keep = min(MAX_WINDOW, t_prompt)
    y_sample, state_gla_sample, k_rows_sample, v_rows_sample = trunk(
        x_sample, c_sample, PAST_LEN + jnp.arange(t_sample), state_gla, cache_k, cache_v, p)
    return (y_prompt, y_sample, state_gla_prompt, state_gla_sample,
            k_p[:, -keep:], v_p[:, -keep:], k_rows_sample, v_rows_sample)
```

```python
import functools

import numpy as np
import jax
import jax.numpy as jnp
from jax import lax
from jax.experimental import pallas as pl
from jax.experimental.pallas import tpu as pltpu

F32 = jnp.float32
BF16 = jnp.bfloat16

D_MODEL = 1024
DEPTH = 2
PAST_LEN = 16384
N_A_LAYERS = DEPTH // 2
GLA_HEADS = 4
GLA_QK = D_MODEL // 2
GLA_V = D_MODEL
GLA_DK = GLA_QK // GLA_HEADS
GLA_DV = GLA_V // GLA_HEADS
GLA_RANK = 16
GLA_TAU = 16.0
GLA_NORM_EPS = 1e-6
DIL_HEADS = 16
DIL_HD = D_MODEL // DIL_HEADS
DIL_WINDOWS = (128, 512, 2048)
DIL_DILATIONS = (1, 4, 16)
N_GROUPS = len(DIL_WINDOWS)
DIL_STEPS = 128
MAX_WINDOW = max(DIL_WINDOWS)
DIL_SCALE = DIL_HD ** -0.5
ROPE_THETA = 10000.0
D_FF = 2816
FFN_RES = 0.5
ALPHA = (2 * DEPTH) ** 0.25
LN_EPS = 1e-5
N_MOD = 9

LANES = 128
ROW_TILE = 512
FF_CHUNK = D_FF // 2
GLA_CHUNK = 128
GLA_TILE = 512
GLA_FACTOR_BOUND = 80.0
ATT_BLOCK = DIL_STEPS
VMEM_LIMIT = 56 * 1024 * 1024


def _params(*sem):
    return pltpu.CompilerParams(dimension_semantics=sem, vmem_limit_bytes=VMEM_LIMIT)


def _silu(x):
    return x / (1.0 + jnp.exp(-x))


def _layer_norm(z, g, b):
    mu = jnp.mean(z, axis=-1, keepdims=True)
    zc = z - mu
    var = jnp.mean(zc * zc, axis=-1, keepdims=True)
    return zc * lax.rsqrt(var + LN_EPS) * g + b


def _dot(a, b):
    return jnp.dot(a, b, preferred_element_type=F32)


def _dot_nt(a, b):
    return lax.dot_general(a, b, (((1,), (1,)), ((), ())), preferred_element_type=F32)


def _rope(x, cos_t, sin_t):
    width = x.shape[1]
    lane = lax.broadcasted_iota(jnp.int32, x.shape, 1)
    first_half = (lane % DIL_HD) < (DIL_HD // 2)
    partner = jnp.where(first_half, pltpu.roll(x, width - DIL_HD // 2, 1), pltpu.roll(x, DIL_HD // 2, 1))
    reps = width // LANES
    cos_w = jnp.concatenate([cos_t] * reps, axis=1)
    sin_w = jnp.concatenate([sin_t] * reps, axis=1)
    return x * cos_w + partner * sin_w


def _ada_kernel(c_ref, w_ref, b_ref, o_ref):
    s = _silu(c_ref[...]).astype(BF16)
    o_ref[...] = _dot(s, w_ref[...].astype(BF16)) + b_ref[...]


def _ada_call(c_all, w, b, tn):
    n_l, _, n = w.shape
    rows = c_all.shape[0]
    return pl.pallas_call(
        _ada_kernel,
        grid=(n_l, n // tn),
        in_specs=[
            pl.BlockSpec((rows, D_MODEL), lambda l, j: (0, 0)),
            pl.BlockSpec((None, D_MODEL, tn), lambda l, j: (l, 0, j)),
            pl.BlockSpec((None, 1, tn), lambda l, j: (l, 0, j)),
        ],
        out_specs=pl.BlockSpec((None, rows, tn), lambda l, j: (l, 0, j)),
        out_shape=jax.ShapeDtypeStruct((n_l, rows, n), F32),
        compiler_params=_params("arbitrary", "arbitrary"),
        name="ada_mod",
    )(c_all, w, b.reshape(n_l, 1, n))


class _Rows:
    def __init__(self, n_rows, rows_per_sample, mods, kv_mods):
        self.n_rows = n_rows
        self.per_row = rows_per_sample < 8
        self.tile = n_rows if self.per_row else min(ROW_TILE, rows_per_sample)
        self.steps = n_rows // self.tile
        self.tiles_per_sample = 1 if self.per_row else rows_per_sample // self.tile
        self.mods = mods
        self.kv_mods = kv_mods

    def row_spec(self, width):
        return pl.BlockSpec((self.tile, width), lambda i: (i, 0))

    def mod_spec(self, layer, k):
        if self.per_row:
            return pl.BlockSpec((None, self.tile, D_MODEL), lambda i: (layer, 0, k))
        tps = self.tiles_per_sample
        return pl.BlockSpec((None, None, 1, D_MODEL), lambda i: (layer, i // tps, 0, k))


def _const_spec(shape, index):
    return pl.BlockSpec(shape, lambda i: index, pipeline_mode=pl.Buffered(1))


def _ln_specs(layer, j):
    return [_const_spec((None, None, 1, D_MODEL), (layer, j, 0, 0))] * 2


def _ffn_kernel(x_ref, sh_ref, sc_ref, gt_ref, wup_ref, wdn_ref, g_ref, b_ref, o_ref):
    x = x_ref[...]
    h = (x * (1.0 + sc_ref[...]) + sh_ref[...]).astype(BF16)
    y = None
    for lo in range(0, D_FF, FF_CHUNK):
        a = _dot(h, wup_ref[:, lo:lo + FF_CHUNK])
        u = _dot(h, wup_ref[:, D_FF + lo:D_FF + lo + FF_CHUNK])
        part = _dot((_silu(a) * u).astype(BF16), wdn_ref[lo:lo + FF_CHUNK, :])
        y = part if y is None else y + part
    z = ALPHA * x + (1.0 + gt_ref[...]) * (FFN_RES * y)
    o_ref[...] = _layer_norm(z, g_ref[...], b_ref[...])


def _ffn_call(rows, x, layer, mod0, ln_j, w_up, w_down, ln_g, ln_b):
    return pl.pallas_call(
        _ffn_kernel,
        grid=(rows.steps,),
        in_specs=[
            rows.row_spec(D_MODEL),
            rows.mod_spec(layer, mod0), rows.mod_spec(layer, mod0 + 1), rows.mod_spec(layer, mod0 + 2),
            _const_spec((None, D_MODEL, 2 * D_FF), (layer, 0, 0)),
            _const_spec((None, D_FF, D_MODEL), (layer, 0, 0)),
        ] + _ln_specs(layer, ln_j),
        out_specs=rows.row_spec(D_MODEL),
        out_shape=jax.ShapeDtypeStruct((rows.n_rows, D_MODEL), F32),
        compiler_params=_params("arbitrary"),
        name="ffn",
    )(x, rows.mods, rows.mods, rows.mods, w_up, w_down, ln_g, ln_b)


def _gla_proj_kernel(x_ref, sh_ref, sc_ref, wqkv_ref, wg_ref, wr_ref, wg2_ref, bg_ref,
                     q_ref, k_ref, v_ref, la_ref, r_ref):
    h = (x_ref[...] * (1.0 + sc_ref[...]) + sh_ref[...]).astype(BF16)
    qkv = _dot(h, wqkv_ref[...])
    q_ref[...] = qkv[:, :GLA_QK] * (GLA_DK ** -0.5)
    k_ref[...] = qkv[:, GLA_QK:2 * GLA_QK]
    v_ref[...] = qkv[:, 2 * GLA_QK:]
    g_lr = _dot(h, wg_ref[...]).astype(BF16)
    gate = _dot(g_lr, wg2_ref[...]) + bg_ref[...]
    log_sig = jnp.minimum(gate, 0.0) - jnp.log(1.0 + jnp.exp(-jnp.abs(gate)))
    la_ref[...] = log_sig / GLA_TAU
    r_ref[...] = _dot(h, wr_ref[...])


def _gla_proj_call(rows, x, layer, w_qkv, w_g, w_r, w_g2, b_g):
    widths = (GLA_QK, GLA_QK, GLA_V, GLA_QK, GLA_V)
    return pl.pallas_call(
        _gla_proj_kernel,
        grid=(rows.steps,),
        in_specs=[
            rows.row_spec(D_MODEL),
            rows.mod_spec(layer, 3), rows.mod_spec(layer, 4),
            _const_spec((D_MODEL, 2 * GLA_QK + GLA_V), (0, 0)),
            _const_spec((D_MODEL, LANES), (0, 0)),
            _const_spec((D_MODEL, GLA_V), (0, 0)),
            _const_spec((LANES, GLA_QK), (0, 0)),
            _const_spec((1, GLA_QK), (0, 0)),
        ],
        out_specs=[rows.row_spec(w) for w in widths],
        out_shape=[jax.ShapeDtypeStruct((rows.n_rows, w), F32) for w in widths],
        compiler_params=_params("arbitrary"),
        name="gla_proj",
    )(x, rows.mods, rows.mods, w_qkv, w_g, w_r, w_g2, b_g)


def _gla_scores_factored(qh, kh, bh, b_last):
    mid = 0.5 * b_last
    q_t = (qh * jnp.exp(bh - mid)).astype(BF16)
    k_t = (kh * jnp.exp(mid - bh)).astype(BF16)
    return _dot_nt(q_t, k_t)


def _gla_scores_elementwise(qh, kh, bh, k_scr, b_scr):
    k_scr[...] = kh
    b_scr[...] = bh
    col = lax.broadcasted_iota(jnp.int32, (GLA_CHUNK, GLA_CHUNK), 1)

    def body(j, acc):
        kj = k_scr[pl.ds(j, 1), :]
        bj = b_scr[pl.ds(j, 1), :]
        term = qh * kj * jnp.exp(jnp.minimum(bh - bj, 0.0))
        return acc + jnp.where(col == j, jnp.sum(term, axis=1, keepdims=True), 0.0)

    return lax.fori_loop(0, GLA_CHUNK, body, jnp.zeros((GLA_CHUNK, GLA_CHUNK), F32))


def _gla_rec_kernel(q_ref, k_ref, v_ref, la_ref, s0_ref, o_ref, sfin_ref, s_scr, k_scr, b_scr):
    t = pl.program_id(1)

    @pl.when(t == 0)
    def _():
        s_scr[...] = s0_ref[...]

    row = lax.broadcasted_iota(jnp.int32, (GLA_CHUNK, GLA_CHUNK), 0)
    col = lax.broadcasted_iota(jnp.int32, (GLA_CHUNK, GLA_CHUNK), 1)
    causal = col <= row
    tri = causal.astype(F32)

    def chunk(c, carry):
        r0 = pl.multiple_of(c * GLA_CHUNK, GLA_CHUNK)
        q = q_ref[pl.ds(r0, GLA_CHUNK), :]
        k = k_ref[pl.ds(r0, GLA_CHUNK), :]
        la = la_ref[pl.ds(r0, GLA_CHUNK), :]
        b = jnp.dot(tri, la, precision=lax.Precision.HIGHEST, preferred_element_type=F32)
        b_last = b[GLA_CHUNK - 1:GLA_CHUNK, :]
        factorable = jnp.max(-b_last) <= GLA_FACTOR_BOUND
        q_in = q * jnp.exp(b)
        k_out = k * jnp.exp(b_last - b)
        for hd in range(GLA_HEADS):
            ks = slice(hd * GLA_DK, (hd + 1) * GLA_DK)
            vs = slice(hd * GLA_DV, (hd + 1) * GLA_DV)
            vh = v_ref[pl.ds(r0, GLA_CHUNK), vs].astype(BF16)
            s_prev = s_scr[hd]
            o_inter = _dot(q_in[:, ks].astype(BF16), s_prev.astype(BF16))
            qh, kh, bh, blh = q[:, ks], k[:, ks], b[:, ks], b_last[:, ks]
            scores = lax.cond(
                factorable,
                lambda: _gla_scores_factored(qh, kh, bh, blh),
                lambda: _gla_scores_elementwise(qh, kh, bh, k_scr, b_scr))
            scores = jnp.where(causal, scores, 0.0)
            o_ref[pl.ds(r0, GLA_CHUNK), vs] = o_inter + _dot(scores.astype(BF16), vh)
            decay_col = jnp.transpose(jnp.broadcast_to(jnp.exp(blh), (GLA_DK, GLA_DK)))[:, :1]
            k_out_t = jnp.transpose(k_out[:, ks]).astype(BF16)
            s_scr[hd] = decay_col * s_prev + _dot(k_out_t, vh)
        return carry

    lax.fori_loop(0, GLA_TILE // GLA_CHUNK, chunk, 0)

    @pl.when(t == pl.num_programs(1) - 1)
    def _():
        sfin_ref[...] = s_scr[...]


def _gla_rec_call(q, k, v, la, s0):
    n_b, n_t = q.shape[:2]

    def seq(width):
        return pl.BlockSpec((None, GLA_TILE, width), lambda b, t: (b, t, 0))

    state = pl.BlockSpec((None, GLA_HEADS, GLA_DK, GLA_DV), lambda b, t: (b, 0, 0, 0))
    return pl.pallas_call(
        _gla_rec_kernel,
        grid=(n_b, n_t // GLA_TILE),
        in_specs=[seq(GLA_QK), seq(GLA_QK), seq(GLA_V), seq(GLA_QK), state],
        out_specs=[seq(GLA_V), state],
        out_shape=[jax.ShapeDtypeStruct((n_b, n_t, GLA_V), F32),
                   jax.ShapeDtypeStruct((n_b, GLA_HEADS, GLA_DK, GLA_DV), F32)],
        scratch_shapes=[pltpu.VMEM((GLA_HEADS, GLA_DK, GLA_DV), F32),
                        pltpu.VMEM((GLA_CHUNK, GLA_DK), F32),
                        pltpu.VMEM((GLA_CHUNK, GLA_DK), F32)],
        compiler_params=_params("arbitrary", "arbitrary"),
        name="gla_recurrence",
    )(q, k, v, la, s0)


def _gla_step_kernel(qt_ref, kt_ref, lat_ref, v_ref, s0_ref, o_ref, sfin_ref):
    n_t = v_ref.shape[0]
    for hd in range(GLA_HEADS):
        vs = slice(hd * GLA_DV, (hd + 1) * GLA_DV)
        s = s0_ref[hd]
        qt, kt, lat = qt_ref[hd], kt_ref[hd], lat_ref[hd]
        for t in range(n_t):
            s = jnp.exp(lat[:, t:t + 1]) * s + kt[:, t:t + 1] * v_ref[t:t + 1, vs]
            o_ref[t:t + 1, vs] = jnp.sum(qt[:, t:t + 1] * s, axis=0, keepdims=True)
        sfin_ref[hd] = s


def _gla_step_call(qt, kt, lat, v, s0):
    n_b, _, _, n_t = qt.shape
    col = pl.BlockSpec((None, GLA_HEADS, GLA_DK, n_t), lambda b: (b, 0, 0, 0))
    tok = pl.BlockSpec((None, n_t, GLA_V), lambda b: (b, 0, 0))
    state = pl.BlockSpec((None, GLA_HEADS, GLA_DK, GLA_DV), lambda b: (b, 0, 0, 0))
    return pl.pallas_call(
        _gla_step_kernel,
        grid=(n_b,),
        in_specs=[col, col, col, tok, state],
        out_specs=[tok, state],
        out_shape=[jax.ShapeDtypeStruct((n_b, n_t, GLA_V), F32),
                   jax.ShapeDtypeStruct((n_b, GLA_HEADS, GLA_DK, GLA_DV), F32)],
        compiler_params=_params("arbitrary"),
        name="gla_step",
    )(qt, kt, lat, v, s0)


def _gla_out_kernel(o_ref, r_ref, x_ref, gt_ref, gon_ref, wout_ref, g_ref, b_ref, y_ref):
    o = o_ref[...]
    parts = []
    for hd in range(GLA_HEADS):
        oh = o[:, hd * GLA_DV:(hd + 1) * GLA_DV]
        ms = jnp.mean(oh * oh, axis=-1, keepdims=True)
        parts.append(oh * lax.rsqrt(ms + GLA_NORM_EPS) * gon_ref[...])
    gated = (jnp.concatenate(parts, axis=1) * _silu(r_ref[...])).astype(BF16)
    y = _dot(gated, wout_ref[...])
    z = ALPHA * x_ref[...] + (1.0 + gt_ref[...]) * y
    y_ref[...] = _layer_norm(z, g_ref[...], b_ref[...])


def _gla_out_call(rows, o, r, x, layer, g_onorm, w_out, ln_g, ln_b):
    return pl.pallas_call(
        _gla_out_kernel,
        grid=(rows.steps,),
        in_specs=[
            rows.row_spec(GLA_V), rows.row_spec(GLA_V), rows.row_spec(D_MODEL),
            rows.mod_spec(layer, 5),
            _const_spec((1, GLA_DV), (0, 0)),
            _const_spec((GLA_V, D_MODEL), (0, 0)),
        ] + _ln_specs(layer, 1),
        out_specs=rows.row_spec(D_MODEL),
        out_shape=jax.ShapeDtypeStruct((rows.n_rows, D_MODEL), F32),
        compiler_params=_params("arbitrary"),
        name="gla_out",
    )(o, r, x, rows.mods, g_onorm, w_out, ln_g, ln_b)


def _kv_kernel(x_ref, sh_ref, sc_ref, cos_ref, sin_ref, w_ref, k_ref, v_ref, kb_ref, vb_ref):
    h = (x_ref[...] * (1.0 + sc_ref[...]) + sh_ref[...]).astype(BF16)
    kv = _dot(h, w_ref[...])
    k = _rope(kv[:, :D_MODEL], cos_ref[...], sin_ref[...])
    v = kv[:, D_MODEL:]
    k_ref[...] = k
    v_ref[...] = v
    kb_ref[...] = k.astype(BF16)
    vb_ref[...] = v.astype(BF16)


def _rope_spec(rows):
    if rows.per_row:
        return pl.BlockSpec((rows.tile, LANES), lambda i: (0, 0))
    tps = rows.tiles_per_sample
    return pl.BlockSpec((rows.tile, LANES), lambda i: (i % tps, 0))


def _kv_mod_spec(rows, k):
    if rows.per_row:
        return pl.BlockSpec((None, rows.tile, D_MODEL), lambda i: (0, 0, k))
    tps = rows.tiles_per_sample
    return pl.BlockSpec((None, None, 1, D_MODEL), lambda i: (0, i // tps, 0, k))


def _kv_call(rows, x, cos_t, sin_t, w_kv):
    return pl.pallas_call(
        _kv_kernel,
        grid=(rows.steps,),
        in_specs=[
            rows.row_spec(D_MODEL),
            _kv_mod_spec(rows, 0), _kv_mod_spec(rows, 1),
            _rope_spec(rows), _rope_spec(rows),
            _const_spec((D_MODEL, 2 * D_MODEL), (0, 0)),
        ],
        out_specs=[rows.row_spec(D_MODEL)] * 4,
        out_shape=[jax.ShapeDtypeStruct((rows.n_rows, D_MODEL), F32)] * 2
        + [jax.ShapeDtypeStruct((rows.n_rows, D_MODEL), BF16)] * 2,
        compiler_params=_params("arbitrary"),
        name="kv_proj",
    )(x, rows.kv_mods, rows.kv_mods, cos_t, sin_t, w_kv)


def _q_kernel(x_ref, sh_ref, sc_ref, cos_ref, sin_ref, w_ref, q_ref):
    h = (x_ref[...] * (1.0 + sc_ref[...]) + sh_ref[...]).astype(BF16)
    q = _rope(_dot(h, w_ref[...]), cos_ref[...], sin_ref[...])
    q_ref[...] = (q * DIL_SCALE).astype(BF16)


def _q_call(rows, x, layer, cos_t, sin_t, w_q):
    width = N_GROUPS * D_MODEL
    return pl.pallas_call(
        _q_kernel,
        grid=(rows.steps,),
        in_specs=[
            rows.row_spec(D_MODEL),
            rows.mod_spec(layer, 3), rows.mod_spec(layer, 4),
            _rope_spec(rows), _rope_spec(rows),
            _const_spec((D_MODEL, width), (0, 0)),
        ],
        out_specs=rows.row_spec(width),
        out_shape=jax.ShapeDtypeStruct((rows.n_rows, width), BF16),
        compiler_params=_params("arbitrary"),
        name="q_proj",
    )(x, rows.mods, rows.mods, cos_t, sin_t, w_q)


def _pair_softmax_pv(scores, v_pair, even_lane):
    half = scores.shape[0] // 2
    m = jnp.max(scores, axis=-1, keepdims=True)
    e = jnp.exp(scores - m)
    den = jnp.sum(e, axis=-1, keepdims=True)
    o2 = _dot(e.astype(BF16), v_pair) / den
    lse2 = m + jnp.log(den)
    o = jnp.where(even_lane, o2[:half], o2[half:])
    lse = jnp.where(even_lane, lse2[:half], lse2[half:])
    return o, lse


def _dil_prompt_kernel(has_prev, q_ref, *refs):
    if has_prev:
        kp_ref, kc_ref, vp_ref, vc_ref, o_ref, lse_ref = refs
    else:
        kc_ref, vc_ref, o_ref, lse_ref = refs
    blk = pl.program_id(2)
    n_keys = (2 if has_prev else 1) * ATT_BLOCK
    lane = lax.broadcasted_iota(jnp.int32, (ATT_BLOCK, LANES), 1)
    even_lane = lane < DIL_HD
    qi = lax.broadcasted_iota(jnp.int32, (2 * ATT_BLOCK, n_keys), 0) % ATT_BLOCK
    kj = lax.broadcasted_iota(jnp.int32, (2 * ATT_BLOCK, n_keys), 1)
    if has_prev:
        dist = qi + ATT_BLOCK - kj
        valid = (dist >= 0) & (dist <= DIL_STEPS) & ((kj >= ATT_BLOCK) | (blk > 0))
    else:
        valid = kj <= qi
    for p in range(DIL_HEADS // 2):
        sl = slice(p * LANES, (p + 1) * LANES)
        qp = q_ref[:, sl]
        zero = jnp.zeros_like(qp)
        q2 = jnp.concatenate([jnp.where(even_lane, qp, zero), jnp.where(even_lane, zero, qp)], axis=0)
        if has_prev:
            k_pair = jnp.concatenate([kp_ref[:, sl], kc_ref[:, sl]], axis=0)
            v_pair = jnp.concatenate([vp_ref[:, sl], vc_ref[:, sl]], axis=0)
        else:
            k_pair, v_pair = kc_ref[:, sl], vc_ref[:, sl]
        scores = jnp.where(valid, _dot_nt(q2, k_pair), -jnp.inf)
        o, lse = _pair_softmax_pv(scores, v_pair, even_lane)
        o_ref[:, sl] = o
        lse_ref[:, sl] = lse


def _dil_prompt_call(q, kb, vb, group):
    n_b, n_t = kb.shape[:2]
    dil = DIL_DILATIONS[group]
    length = n_t // dil
    n_blk = length // ATT_BLOCK
    has_prev = n_blk > 1
    q_view = q.reshape(n_b, length, dil * N_GROUPS * D_MODEL)
    k_view = kb.reshape(n_b, length, dil * D_MODEL)
    v_view = vb.reshape(n_b, length, dil * D_MODEL)
    blk_shape = (None, ATT_BLOCK, D_MODEL)
    cur = pl.BlockSpec(blk_shape, lambda b, r, i: (b, i, r))
    prev = pl.BlockSpec(blk_shape, lambda b, r, i: (b, jnp.maximum(i - 1, 0), r))
    q_spec = pl.BlockSpec(blk_shape, lambda b, r, i: (b, i, r * N_GROUPS + group))
    if has_prev:
        in_specs, operands = [q_spec, prev, cur, prev, cur], (q_view, k_view, k_view, v_view, v_view)
    else:
        in_specs, operands = [q_spec, cur, cur], (q_view, k_view, v_view)
    o, lse = pl.pallas_call(
        functools.partial(_dil_prompt_kernel, has_prev),
        grid=(n_b, dil, n_blk),
        in_specs=in_specs,
        out_specs=[cur, cur],
        out_shape=[jax.ShapeDtypeStruct((n_b, length, dil * D_MODEL), F32)] * 2,
        compiler_params=_params("arbitrary", "arbitrary", "arbitrary"),
        name="dilated_prompt",
    )(*operands)
    return o.reshape(n_b * n_t, D_MODEL), lse.reshape(n_b * n_t, D_MODEL)


SAMPLE_Q_ROWS = 16


def _dil_sample_kernel(q_ref, kc_ref, vc_ref, kn_ref, vn_ref, bias_c_ref, bias_n_ref, o_ref, lse_ref):
    lane = lax.broadcasted_iota(jnp.int32, (SAMPLE_Q_ROWS, LANES), 1)
    even_lane = lane < DIL_HD
    bias = jnp.concatenate([bias_c_ref[...], bias_n_ref[...]], axis=1)
    for p in range(DIL_HEADS // 2):
        sl = slice(p * LANES, (p + 1) * LANES)
        qp = q_ref[:, sl]
        zero = jnp.zeros_like(qp)
        q2 = jnp.concatenate([jnp.where(even_lane, qp, zero), jnp.where(even_lane, zero, qp)], axis=0)
        k_pair = jnp.concatenate([kc_ref[:, sl].astype(BF16), kn_ref[:, sl]], axis=0)
        v_pair = jnp.concatenate([vc_ref[:, sl].astype(BF16), vn_ref[:, sl]], axis=0)
        scores = _dot_nt(q2, k_pair) + bias
        o, lse = _pair_softmax_pv(scores, v_pair, even_lane)
        o_ref[:, sl] = o
        lse_ref[:, sl] = lse


def _sample_bias(n_t, wb):
    rows = 2 * SAMPLE_Q_ROWS
    bias_c = np.full((rows, wb), -np.inf, np.float32)
    bias_n = np.full((rows, LANES), -np.inf, np.float32)
    for r in range(rows):
        g, t = min((r % SAMPLE_Q_ROWS) // n_t, N_GROUPS - 1), (r % SAMPLE_Q_ROWS) % n_t
        dil = DIL_DILATIONS[g]
        for j in range(DIL_STEPS + 1):
            idx = wb + t - dil * j
            if idx >= wb:
                bias_n[r, idx - wb] = 0.0
            elif idx >= 0:
                bias_c[r, idx] = 0.0
    return jnp.asarray(bias_c), jnp.asarray(bias_n)


def _dil_sample_call(q16, cache_k, cache_v, kn, vn, n_t):
    n_b, wb = cache_k.shape[:2]
    bias_c, bias_n = _sample_bias(n_t, wb)

    def per_b(rows, width=D_MODEL):
        return pl.BlockSpec((None, rows, width), lambda b: (b, 0, 0))

    return pl.pallas_call(
        _dil_sample_kernel,
        grid=(n_b,),
        in_specs=[per_b(SAMPLE_Q_ROWS), per_b(wb), per_b(wb), per_b(LANES), per_b(LANES),
                  _const_spec((2 * SAMPLE_Q_ROWS, wb), (0, 0)), _const_spec((2 * SAMPLE_Q_ROWS, LANES), (0, 0))],
        out_specs=[per_b(SAMPLE_Q_ROWS)] * 2,
        out_shape=[jax.ShapeDtypeStruct((n_b, SAMPLE_Q_ROWS, D_MODEL), F32)] * 2,
        compiler_params=_params("arbitrary"),
        name="dilated_sample",
    )(q16, cache_k, cache_v, kn, vn, bias_c, bias_n)


def _att_out_kernel(o0_ref, o1_ref, o2_ref, l0_ref, l1_ref, l2_ref, x_ref, gt_ref, wout_ref, g_ref, b_ref, y_ref):
    l0, l1, l2 = l0_ref[...], l1_ref[...], l2_ref[...]
    m = jnp.maximum(jnp.maximum(l0, l1), l2)
    e0, e1, e2 = jnp.exp(l0 - m), jnp.exp(l1 - m), jnp.exp(l2 - m)
    o = (e0 * o0_ref[...] + e1 * o1_ref[...] + e2 * o2_ref[...]) / (e0 + e1 + e2)
    y = _dot(o.astype(BF16), wout_ref[...])
    z = ALPHA * x_ref[...] + (1.0 + gt_ref[...]) * y
    y_ref[...] = _layer_norm(z, g_ref[...], b_ref[...])


def _att_out_call(rows, outs, lses, x, layer, w_out, ln_g, ln_b):
    return pl.pallas_call(
        _att_out_kernel,
        grid=(rows.steps,),
        in_specs=[rows.row_spec(D_MODEL)] * 7 + [
            rows.mod_spec(layer, 5),
            _const_spec((D_MODEL, D_MODEL), (0, 0)),
        ] + _ln_specs(layer, 1),
        out_specs=rows.row_spec(D_MODEL),
        out_shape=jax.ShapeDtypeStruct((rows.n_rows, D_MODEL), F32),
        compiler_params=_params("arbitrary"),
        name="att_out",
    )(*outs, *lses, x, rows.mods, w_out, ln_g, ln_b)


def _rope_tables(positions):
    half = DIL_HD // 2
    inv = ROPE_THETA ** (-jnp.arange(half, dtype=F32) / half)
    ang = positions.astype(F32)[:, None] * inv[None, :]
    cos, sin = jnp.cos(ang), jnp.sin(ang)
    return jnp.concatenate([cos, cos, cos, cos], axis=1), jnp.concatenate([-sin, sin, -sin, sin], axis=1)


def _trunk(rows, x, n_b, n_t, positions, gla_state, cache, w):
    cos_t, sin_t = _rope_tables(positions)
    if rows.per_row:
        cos_t, sin_t = jnp.tile(cos_t, (n_b, 1)), jnp.tile(sin_t, (n_b, 1))
    ffn = functools.partial(_ffn_call, rows)

    x = ffn(x, 0, 0, 0, w['up1'], w['down1'], w['ln_g'], w['ln_b'])
    q, k, v, la, r = _gla_proj_call(rows, x, 0, w['gla_qkv'], w['gla_g'], w['gla_r'], w['gla_g2'], w['gla_bg'])
    if rows.per_row:
        def cols(a):
            return a.reshape(n_b, n_t, GLA_HEADS, GLA_DK).transpose(0, 2, 3, 1)
        o, s_new = _gla_step_call(cols(q), cols(k), cols(la), v.reshape(n_b, n_t, GLA_V), gla_state)
    else:
        o, s_new = _gla_rec_call(q.reshape(n_b, n_t, GLA_QK), k.reshape(n_b, n_t, GLA_QK),
                                 v.reshape(n_b, n_t, GLA_V), la.reshape(n_b, n_t, GLA_QK), gla_state)
    x = _gla_out_call(rows, o.reshape(n_b * n_t, GLA_V), r, x, 0, w['gla_gon'], w['gla_out'], w['ln_g'], w['ln_b'])
    x = ffn(x, 0, 6, 2, w['up2'], w['down2'], w['ln_g'], w['ln_b'])

    k_rows, v_rows, kb, vb = _kv_call(rows, x, cos_t, sin_t, w['kv'])

    x = ffn(x, 1, 0, 0, w['up1'], w['down1'], w['ln_g'], w['ln_b'])
    qd = _q_call(rows, x, 1, cos_t, sin_t, w['q'])
    if rows.per_row:
        cache_k, cache_v = cache
        q16 = qd.reshape(n_b, n_t, N_GROUPS, D_MODEL).transpose(0, 2, 1, 3).reshape(n_b, N_GROUPS * n_t, D_MODEL)
        q16 = jnp.pad(q16, ((0, 0), (0, SAMPLE_Q_ROWS - N_GROUPS * n_t), (0, 0)))
        pad_new = ((0, 0), (0, LANES - n_t), (0, 0))
        o_all, lse_all = _dil_sample_call(
            q16, cache_k.reshape(n_b, -1, D_MODEL), cache_v.reshape(n_b, -1, D_MODEL),
            jnp.pad(kb.reshape(n_b, n_t, D_MODEL), pad_new), jnp.pad(vb.reshape(n_b, n_t, D_MODEL), pad_new), n_t)

        def group_rows(a, g):
            return a[:, g * n_t:(g + 1) * n_t].reshape(n_b * n_t, D_MODEL)
        outs = [group_rows(o_all, g) for g in range(N_GROUPS)]
        lses = [group_rows(lse_all, g) for g in range(N_GROUPS)]
    else:
        qd3, kb3, vb3 = (a.reshape(n_b, n_t, -1) for a in (qd, kb, vb))
        outs, lses = zip(*[_dil_prompt_call(qd3, kb3, vb3, g) for g in range(N_GROUPS)])
    x = _att_out_call(rows, outs, lses, x, 1, w['att_out'], w['ln_g'], w['ln_b'])
    x = ffn(x, 1, 6, 2, w['up2'], w['down2'], w['ln_g'], w['ln_b'])
    return x, s_new, k_rows, v_rows


def kernel(x_prompt, x_sample, state_gla, cache_k, cache_v, c_prompt, c_sample, w_ada, b_ada, ln_g, ln_b, w_ffn1_up, w_ffn1_down, w_ffn2_up, w_ffn2_down, w_in_a, w_gate2_a, b_gate_a, g_onorm_a, w_out_a, w_ada_kv, b_ada_kv, w_kv, w_q_b, w_out_b):
    n_bp, n_tp = x_prompt.shape[:2]
    n_bs, n_ts = x_sample.shape[:2]
    assert N_A_LAYERS == 1 and DEPTH == 2
    assert n_tp % (ATT_BLOCK * DIL_DILATIONS[-1]) == 0 and n_tp % GLA_TILE == 0 and n_tp <= MAX_WINDOW
    assert N_GROUPS * n_ts <= SAMPLE_Q_ROWS and n_ts < 8

    c_all = jnp.concatenate([c_prompt, c_sample], axis=0)
    mods = _ada_call(c_all, w_ada, b_ada, 1536)
    kv_mods = _ada_call(c_all, w_ada_kv[None], b_ada_kv[None], 1024)
    rows_p = _Rows(n_bp * n_tp, n_tp, mods[:, :n_bp, None, :], kv_mods[:, :n_bp, None, :])
    rows_s = _Rows(n_bs * n_ts, n_ts, jnp.repeat(mods[:, n_bp:], n_ts, axis=1),
                   jnp.repeat(kv_mods[:, n_bp:], n_ts, axis=1))

    w_in = w_in_a[0]
    gate_lo = 2 * GLA_QK + GLA_V
    w = {
        'up1': w_ffn1_up.astype(BF16), 'down1': w_ffn1_down.astype(BF16),
        'up2': w_ffn2_up.astype(BF16), 'down2': w_ffn2_down.astype(BF16),
        'ln_g': ln_g[:, :, None, :], 'ln_b': ln_b[:, :, None, :],
        'gla_qkv': w_in[:, :gate_lo].astype(BF16),
        'gla_g': jnp.pad(w_in[:, gate_lo:gate_lo + GLA_RANK], ((0, 0), (0, LANES - GLA_RANK))).astype(BF16),
        'gla_r': w_in[:, gate_lo + GLA_RANK:].astype(BF16),
        'gla_g2': jnp.pad(w_gate2_a[0], ((0, LANES - GLA_RANK), (0, 0))).astype(BF16),
        'gla_bg': b_gate_a[0][None, :],
        'gla_gon': g_onorm_a[0][None, :],
        'gla_out': w_out_a[0].astype(BF16),
        'kv': w_kv.astype(BF16),
        'q': w_q_b[0].astype(BF16),
        'att_out': w_out_b[0].astype(BF16),
    }

    zero_state = jnp.zeros((n_bp, GLA_HEADS, GLA_DK, GLA_DV), F32)
    y_p, s_p, k_p, v_p = _trunk(rows_p, x_prompt.reshape(-1, D_MODEL), n_bp, n_tp,
                                jnp.arange(n_tp), zero_state, None, w)
    y_s, s_s, k_s, v_s = _trunk(rows_s, x_sample.reshape(-1, D_MODEL), n_bs, n_ts,
                                PAST_LEN + jnp.arange(n_ts), state_gla[0], (cache_k, cache_v), w)
    kv_shape_p = (n_bp, n_tp, DIL_HEADS, DIL_HD)
    kv_shape_s = (n_bs, n_ts, DIL_HEADS, DIL_HD)
    return (y_p.reshape(n_bp, n_tp, D_MODEL), y_s.reshape(n_bs, n_ts, D_MODEL),
            s_p[None], s_s[None],
            k_p.reshape(kv_shape_p), v_p.reshape(kv_shape_p),
            k_s.reshape(kv_shape_s), v_s.reshape(kv_shape_s))
```

```python
import functools

import jax
import jax.numpy as jnp
from jax import lax
from jax.experimental import pallas as pl
from jax.experimental.pallas import tpu as pltpu

F32 = jnp.float32
BF16 = jnp.bfloat16

D_MODEL = 1024
DEPTH = 2
PAST_LEN = 16384
N_A_LAYERS = DEPTH // 2
GLA_HEADS = 4
GLA_QK = D_MODEL // 2
GLA_V = D_MODEL
GLA_DK = GLA_QK // GLA_HEADS
GLA_DV = GLA_V // GLA_HEADS
GLA_RANK = 16
GLA_TAU = 16.0
GLA_NORM_EPS = 1e-6
DIL_HEADS = 16
DIL_HD = D_MODEL // DIL_HEADS
DIL_WINDOWS = (128, 512, 2048)
DIL_DILATIONS = (1, 4, 16)
N_GROUPS = len(DIL_WINDOWS)
DIL_STEPS = 128
MAX_WINDOW = max(DIL_WINDOWS)
DIL_SCALE = DIL_HD ** -0.5
ROPE_THETA = 10000.0
D_FF = 2816
FFN_RES = 0.5
ALPHA = (2 * DEPTH) ** 0.25
LN_EPS = 1e-5
N_MOD = 9

LANES = 128
ROW_TILE = 512
FF_CHUNK = D_FF // 2
GLA_CHUNK = 128
GLA_TILE = 512
GLA_FACTOR_BOUND = 80.0
ATT_BLOCK = DIL_STEPS
VMEM_LIMIT = 56 * 1024 * 1024


def _params(*sem):
    return pltpu.CompilerParams(dimension_semantics=sem, vmem_limit_bytes=VMEM_LIMIT)


def _silu(x):
    return x / (1.0 + jnp.exp(-x))


def _layer_norm(z, g, b):
    mu = jnp.mean(z, axis=-1, keepdims=True)
    zc = z - mu
    var = jnp.mean(zc * zc, axis=-1, keepdims=True)
    return zc * lax.rsqrt(var + LN_EPS) * g + b


def _dot(a, b):
    return jnp.dot(a, b, preferred_element_type=F32)


def _dot_nt(a, b):
    return lax.dot_general(a, b, (((1,), (1,)), ((), ())), preferred_element_type=F32)


def _rope(x, cos_t, sin_t):
    width = x.shape[1]
    lane = lax.broadcasted_iota(jnp.int32, x.shape, 1)
    first_half = (lane % DIL_HD) < (DIL_HD // 2)
    partner = jnp.where(first_half, pltpu.roll(x, width - DIL_HD // 2, 1), pltpu.roll(x, DIL_HD // 2, 1))
    reps = width // LANES
    cos_w = jnp.concatenate([cos_t] * reps, axis=1)
    sin_w = jnp.concatenate([sin_t] * reps, axis=1)
    return x * cos_w + partner * sin_w


def _ada_kernel(c_ref, w_ref, b_ref, o_ref):
    s = _silu(c_ref[...]).astype(BF16)
    o_ref[...] = _dot(s, w_ref[...].astype(BF16)) + b_ref[...]


def _ada_call(c_all, w, b, tn):
    n_l, _, n = w.shape
    rows = c_all.shape[0]
    return pl.pallas_call(
        _ada_kernel,
        grid=(n_l, n // tn),
        in_specs=[
            pl.BlockSpec((rows, D_MODEL), lambda l, j: (0, 0)),
            pl.BlockSpec((None, D_MODEL, tn), lambda l, j: (l, 0, j)),
            pl.BlockSpec((None, 1, tn), lambda l, j: (l, 0, j)),
        ],
        out_specs=pl.BlockSpec((None, rows, tn), lambda l, j: (l, 0, j)),
        out_shape=jax.ShapeDtypeStruct((n_l, rows, n), F32),
        compiler_params=_params("arbitrary", "arbitrary"),
        name="ada_mod",
    )(c_all, w, b.reshape(n_l, 1, n))


class _Rows:
    def __init__(self, n_rows, rows_per_sample, mods, kv_mods):
        self.n_rows = n_rows
        self.rows_per_sample = rows_per_sample
        self.per_row = rows_per_sample < 8
        self.tile = n_rows if self.per_row else min(ROW_TILE, rows_per_sample)
        self.steps = n_rows // self.tile
        self.tiles_per_sample = 1 if self.per_row else rows_per_sample // self.tile
        self.mods = mods
        self.kv_mods = kv_mods

    def row_spec(self, width):
        return pl.BlockSpec((self.tile, width), lambda i: (i, 0))

    def mod_spec(self, layer, k):
        if self.per_row:
            return pl.BlockSpec((None, self.tile, D_MODEL), lambda i: (layer, 0, k))
        tps = self.tiles_per_sample
        return pl.BlockSpec((None, None, 1, D_MODEL), lambda i: (layer, i // tps, 0, k))


def _const_spec(shape, index):
    return pl.BlockSpec(shape, lambda i: index, pipeline_mode=pl.Buffered(1))


def _ln_specs(layer, j):
    return [_const_spec((None, None, 1, D_MODEL), (layer, j, 0, 0))] * 2


def _ffn_kernel(x_ref, sh_ref, sc_ref, gt_ref, wup_ref, wdn_ref, g_ref, b_ref, o_ref):
    x = x_ref[...]
    h = (x * (1.0 + sc_ref[...]) + sh_ref[...]).astype(BF16)
    y = None
    for lo in range(0, D_FF, FF_CHUNK):
        a = _dot(h, wup_ref[:, lo:lo + FF_CHUNK])
        u = _dot(h, wup_ref[:, D_FF + lo:D_FF + lo + FF_CHUNK])
        part = _dot((_silu(a) * u).astype(BF16), wdn_ref[lo:lo + FF_CHUNK, :])
        y = part if y is None else y + part
    z = ALPHA * x + (1.0 + gt_ref[...]) * (FFN_RES * y)
    o_ref[...] = _layer_norm(z, g_ref[...], b_ref[...])


def _ffn_call(rows, x, layer, mod0, ln_j, w_up, w_down, ln_g, ln_b):
    return pl.pallas_call(
        _ffn_kernel,
        grid=(rows.steps,),
        in_specs=[
            rows.row_spec(D_MODEL),
            rows.mod_spec(layer, mod0), rows.mod_spec(layer, mod0 + 1), rows.mod_spec(layer, mod0 + 2),
            _const_spec((None, D_MODEL, 2 * D_FF), (layer, 0, 0)),
            _const_spec((None, D_FF, D_MODEL), (layer, 0, 0)),
        ] + _ln_specs(layer, ln_j),
        out_specs=rows.row_spec(D_MODEL),
        out_shape=jax.ShapeDtypeStruct((rows.n_rows, D_MODEL), F32),
        compiler_params=_params("arbitrary"),
        name="ffn",
    )(x, rows.mods, rows.mods, rows.mods, w_up, w_down, ln_g, ln_b)


def _gla_proj_kernel(x_ref, sh_ref, sc_ref, wqkv_ref, wg_ref, wr_ref, wg2_ref, bg_ref,
                     q_ref, k_ref, v_ref, la_ref, r_ref):
    h = (x_ref[...] * (1.0 + sc_ref[...]) + sh_ref[...]).astype(BF16)
    qkv = _dot(h, wqkv_ref[...])
    q_ref[...] = qkv[:, :GLA_QK] * (GLA_DK ** -0.5)
    k_ref[...] = qkv[:, GLA_QK:2 * GLA_QK]
    v_ref[...] = qkv[:, 2 * GLA_QK:]
    g_lr = _dot(h, wg_ref[...]).astype(BF16)
    gate = _dot(g_lr, wg2_ref[...]) + bg_ref[...]
    log_sig = jnp.minimum(gate, 0.0) - jnp.log(1.0 + jnp.exp(-jnp.abs(gate)))
    la_ref[...] = log_sig / GLA_TAU
    r_ref[...] = _dot(h, wr_ref[...])


def _gla_proj_call(rows, x, layer, w_qkv, w_g, w_r, w_g2, b_g):
    widths = (GLA_QK, GLA_QK, GLA_V, GLA_QK, GLA_V)
    return pl.pallas_call(
        _gla_proj_kernel,
        grid=(rows.steps,),
        in_specs=[
            rows.row_spec(D_MODEL),
            rows.mod_spec(layer, 3), rows.mod_spec(layer, 4),
            _const_spec((D_MODEL, 2 * GLA_QK + GLA_V), (0, 0)),
            _const_spec((D_MODEL, LANES), (0, 0)),
            _const_spec((D_MODEL, GLA_V), (0, 0)),
            _const_spec((LANES, GLA_QK), (0, 0)),
            _const_spec((1, GLA_QK), (0, 0)),
        ],
        out_specs=[rows.row_spec(w) for w in widths],
        out_shape=[jax.ShapeDtypeStruct((rows.n_rows, w), F32) for w in widths],
        compiler_params=_params("arbitrary"),
        name="gla_proj",
    )(x, rows.mods, rows.mods, w_qkv, w_g, w_r, w_g2, b_g)


def _gla_scores_factored(qh, kh, bh, b_last):
    mid = 0.5 * b_last
    q_t = (qh * jnp.exp(bh - mid)).astype(BF16)
    k_t = (kh * jnp.exp(mid - bh)).astype(BF16)
    return _dot_nt(q_t, k_t)


def _gla_scores_elementwise(qh, kh, bh, k_scr, b_scr):
    k_scr[...] = kh
    b_scr[...] = bh
    col = lax.broadcasted_iota(jnp.int32, (GLA_CHUNK, GLA_CHUNK), 1)

    def body(j, acc):
        kj = k_scr[pl.ds(j, 1), :]
        bj = b_scr[pl.ds(j, 1), :]
        term = qh * kj * jnp.exp(jnp.minimum(bh - bj, 0.0))
        return acc + jnp.where(col == j, jnp.sum(term, axis=1, keepdims=True), 0.0)

    return lax.fori_loop(0, GLA_CHUNK, body, jnp.zeros((GLA_CHUNK, GLA_CHUNK), F32))


def _gla_rec_kernel(q_ref, k_ref, v_ref, la_ref, s0_ref, o_ref, sfin_ref, s_scr, k_scr, b_scr):
    t = pl.program_id(1)

    @pl.when(t == 0)
    def _():
        s_scr[...] = s0_ref[...]

    row = lax.broadcasted_iota(jnp.int32, (GLA_CHUNK, GLA_CHUNK), 0)
    col = lax.broadcasted_iota(jnp.int32, (GLA_CHUNK, GLA_CHUNK), 1)
    causal = col <= row
    tri = causal.astype(F32)

    def chunk(c, carry):
        r0 = pl.multiple_of(c * GLA_CHUNK, GLA_CHUNK)
        q = q_ref[pl.ds(r0, GLA_CHUNK), :]
        k = k_ref[pl.ds(r0, GLA_CHUNK), :]
        la = la_ref[pl.ds(r0, GLA_CHUNK), :]
        b = jnp.dot(tri, la, precision=lax.Precision.HIGHEST, preferred_element_type=F32)
        b_last = b[GLA_CHUNK - 1:GLA_CHUNK, :]
        factorable = jnp.max(-b_last) <= GLA_FACTOR_BOUND
        q_in = q * jnp.exp(b)
        k_out = k * jnp.exp(b_last - b)
        for hd in range(GLA_HEADS):
            ks = slice(hd * GLA_DK, (hd + 1) * GLA_DK)
            vs = slice(hd * GLA_DV, (hd + 1) * GLA_DV)
            vh = v_ref[pl.ds(r0, GLA_CHUNK), vs].astype(BF16)
            s_prev = s_scr[hd]
            o_inter = _dot(q_in[:, ks].astype(BF16), s_prev.astype(BF16))
            qh, kh, bh, blh = q[:, ks], k[:, ks], b[:, ks], b_last[:, ks]
            scores = lax.cond(
                factorable,
                lambda: _gla_scores_factored(qh, kh, bh, blh),
                lambda: _gla_scores_elementwise(qh, kh, bh, k_scr, b_scr))
            scores = jnp.where(causal, scores, 0.0)
            o_ref[pl.ds(r0, GLA_CHUNK), vs] = o_inter + _dot(scores.astype(BF16), vh)
            decay_col = jnp.transpose(jnp.broadcast_to(jnp.exp(blh), (GLA_DK, GLA_DK)))[:, :1]
            k_out_t = jnp.transpose(k_out[:, ks]).astype(BF16)
            s_scr[hd] = decay_col * s_prev + _dot(k_out_t, vh)
        return carry

    lax.fori_loop(0, GLA_TILE // GLA_CHUNK, chunk, 0)

    @pl.when(t == pl.num_programs(1) - 1)
    def _():
        sfin_ref[...] = s_scr[...]


def _gla_rec_call(q, k, v, la, s0):
    n_b, n_t = q.shape[:2]

    def seq(width):
        return pl.BlockSpec((None, GLA_TILE, width), lambda b, t: (b, t, 0))

    state = pl.BlockSpec((None, GLA_HEADS, GLA_DK, GLA_DV), lambda b, t: (b, 0, 0, 0))
    return pl.pallas_call(
        _gla_rec_kernel,
        grid=(n_b, n_t // GLA_TILE),
        in_specs=[seq(GLA_QK), seq(GLA_QK), seq(GLA_V), seq(GLA_QK), state],
        out_specs=[seq(GLA_V), state],
        out_shape=[jax.ShapeDtypeStruct((n_b, n_t, GLA_V), F32),
                   jax.ShapeDtypeStruct((n_b, GLA_HEADS, GLA_DK, GLA_DV), F32)],
        scratch_shapes=[pltpu.VMEM((GLA_HEADS, GLA_DK, GLA_DV), F32),
                        pltpu.VMEM((GLA_CHUNK, GLA_DK), F32),
                        pltpu.VMEM((GLA_CHUNK, GLA_DK), F32)],
        compiler_params=_params("arbitrary", "arbitrary"),
        name="gla_recurrence",
    )(q, k, v, la, s0)


def _gla_step_kernel(qt_ref, kt_ref, lat_ref, v_ref, s0_ref, o_ref, sfin_ref):
    n_t = v_ref.shape[0]
    for hd in range(GLA_HEADS):
        vs = slice(hd * GLA_DV, (hd + 1) * GLA_DV)
        s = s0_ref[hd]
        qt, kt, lat = qt_ref[hd], kt_ref[hd], lat_ref[hd]
        for t in range(n_t):
            s = jnp.exp(lat[:, t:t + 1]) * s + kt[:, t:t + 1] * v_ref[t:t + 1, vs]
            o_ref[t:t + 1, vs] = jnp.sum(qt[:, t:t + 1] * s, axis=0, keepdims=True)
        sfin_ref[hd] = s


def _gla_step_call(qt, kt, lat, v, s0):
    n_b, _, _, n_t = qt.shape
    col = pl.BlockSpec((None, GLA_HEADS, GLA_DK, n_t), lambda b: (b, 0, 0, 0))
    tok = pl.BlockSpec((None, n_t, GLA_V), lambda b: (b, 0, 0))
    state = pl.BlockSpec((None, GLA_HEADS, GLA_DK, GLA_DV), lambda b: (b, 0, 0, 0))
    return pl.pallas_call(
        _gla_step_kernel,
        grid=(n_b,),
        in_specs=[col, col, col, tok, state],
        out_specs=[tok, state],
        out_shape=[jax.ShapeDtypeStruct((n_b, n_t, GLA_V), F32),
                   jax.ShapeDtypeStruct((n_b, GLA_HEADS, GLA_DK, GLA_DV), F32)],
        compiler_params=_params("arbitrary"),
        name="gla_step",
    )(qt, kt, lat, v, s0)


def _gla_out_kernel(o_ref, r_ref, x_ref, gt_ref, gon_ref, wout_ref, g_ref, b_ref, y_ref):
    o = o_ref[...]
    parts = []
    for hd in range(GLA_HEADS):
        oh = o[:, hd * GLA_DV:(hd + 1) * GLA_DV]
        ms = jnp.mean(oh * oh, axis=-1, keepdims=True)
        parts.append(oh * lax.rsqrt(ms + GLA_NORM_EPS) * gon_ref[...])
    gated = (jnp.concatenate(parts, axis=1) * _silu(r_ref[...])).astype(BF16)
    y = _dot(gated, wout_ref[...])
    z = ALPHA * x_ref[...] + (1.0 + gt_ref[...]) * y
    y_ref[...] = _layer_norm(z, g_ref[...], b_ref[...])


def _gla_out_call(rows, o, r, x, layer, g_onorm, w_out, ln_g, ln_b):
    return pl.pallas_call(
        _gla_out_kernel,
        grid=(rows.steps,),
        in_specs=[
            rows.row_spec(GLA_V), rows.row_spec(GLA_V), rows.row_spec(D_MODEL),
            rows.mod_spec(layer, 5),
            _const_spec((1, GLA_DV), (0, 0)),
            _const_spec((GLA_V, D_MODEL), (0, 0)),
        ] + _ln_specs(layer, 1),
        out_specs=rows.row_spec(D_MODEL),
        out_shape=jax.ShapeDtypeStruct((rows.n_rows, D_MODEL), F32),
        compiler_params=_params("arbitrary"),
        name="gla_out",
    )(o, r, x, rows.mods, g_onorm, w_out, ln_g, ln_b)


def _store_by_residue(x, out_refs, scr):
    n_rows, width = x.shape
    n_col = width // LANES
    if any(o.shape[0] > 1 for o in out_refs):
        for c in range(n_col):
            scr[c] = x[:, c * LANES:(c + 1) * LANES]
    for o_ref in out_refs:
        dil = o_ref.shape[0]
        if dil == 1:
            o_ref[0] = x.astype(BF16)
            continue
        for r in range(dil):
            o_ref[r] = jnp.concatenate(
                [scr[c, pl.ds(r, n_rows // dil, stride=dil), :] for c in range(n_col)], axis=1).astype(BF16)


def _load_by_token(x_ref, scr):
    dil, n_per, width = x_ref.shape
    if dil == 1:
        return x_ref[0].astype(F32)
    n_col = width // LANES
    for r in range(dil):
        xr = x_ref[r].astype(F32)
        for c in range(n_col):
            scr[c, pl.ds(r, n_per, stride=dil), :] = xr[:, c * LANES:(c + 1) * LANES]
    if n_col == 1:
        return scr[0]
    return jnp.concatenate([scr[c] for c in range(n_col)], axis=1)


def _kv_kernel(by_residue, x_ref, sh_ref, sc_ref, cos_ref, sin_ref, w_ref, k_ref, v_ref, *rest):
    h = (x_ref[...] * (1.0 + sc_ref[...]) + sh_ref[...]).astype(BF16)
    kv = _dot(h, w_ref[...])
    k = _rope(kv[:, :D_MODEL], cos_ref[...], sin_ref[...])
    v = kv[:, D_MODEL:]
    k_ref[...] = k.reshape(k_ref.shape)
    v_ref[...] = v.reshape(v_ref.shape)
    if by_residue:
        scr = rest[-1]
        _store_by_residue(k, rest[0:N_GROUPS], scr)
        _store_by_residue(v, rest[N_GROUPS:2 * N_GROUPS], scr)


def _rope_spec(rows):
    if rows.per_row:
        return pl.BlockSpec((rows.tile, LANES), lambda i: (0, 0))
    tps = rows.tiles_per_sample
    return pl.BlockSpec((rows.tile, LANES), lambda i: (i % tps, 0))


def _kv_mod_spec(rows, k):
    if rows.per_row:
        return pl.BlockSpec((None, rows.tile, D_MODEL), lambda i: (0, 0, k))
    tps = rows.tiles_per_sample
    return pl.BlockSpec((None, None, 1, D_MODEL), lambda i: (0, i // tps, 0, k))


def _residue_spec(rows, dil, width):
    tps = rows.tiles_per_sample
    return pl.BlockSpec((None, dil, rows.tile // dil, width), lambda i: (i // tps, 0, i % tps, 0))


def _residue_shapes(rows, width, dtype):
    n_b = rows.n_rows // rows.rows_per_sample
    return [jax.ShapeDtypeStruct((n_b, dil, rows.rows_per_sample // dil, width), dtype) for dil in DIL_DILATIONS]


def _permute_scratch():
    return pltpu.VMEM((D_MODEL // LANES, ROW_TILE, LANES), F32)


def _kv_call(rows, x, cos_t, sin_t, w_kv):
    by_residue = not rows.per_row
    head_spec = pl.BlockSpec((rows.tile, DIL_HEADS, DIL_HD), lambda i: (i, 0, 0))
    out_specs = [head_spec] * 2
    out_shape = [jax.ShapeDtypeStruct((rows.n_rows, DIL_HEADS, DIL_HD), F32)] * 2
    scratch = []
    if by_residue:
        out_specs += [_residue_spec(rows, dil, D_MODEL) for dil in DIL_DILATIONS] * 2
        out_shape += _residue_shapes(rows, D_MODEL, BF16) * 2
        scratch = [_permute_scratch()]
    return pl.pallas_call(
        functools.partial(_kv_kernel, by_residue),
        grid=(rows.steps,),
        in_specs=[
            rows.row_spec(D_MODEL),
            _kv_mod_spec(rows, 0), _kv_mod_spec(rows, 1),
            _rope_spec(rows), _rope_spec(rows),
            _const_spec((D_MODEL, 2 * D_MODEL), (0, 0)),
        ],
        out_specs=out_specs,
        out_shape=out_shape,
        scratch_shapes=scratch,
        compiler_params=_params("arbitrary"),
        name="kv_proj",
    )(x, rows.kv_mods, rows.kv_mods, cos_t, sin_t, w_kv)


def _q_kernel(by_residue, x_ref, sh_ref, sc_ref, cos_ref, sin_ref, w_ref, *rest):
    h = (x_ref[...] * (1.0 + sc_ref[...]) + sh_ref[...]).astype(BF16)
    q = _rope(_dot(h, w_ref[...]), cos_ref[...], sin_ref[...]) * DIL_SCALE
    if by_residue:
        for g in range(N_GROUPS):
            _store_by_residue(q[:, g * D_MODEL:(g + 1) * D_MODEL], rest[g:g + 1], rest[-1])
    else:
        rest[0][...] = q.reshape(rest[0].shape)


def _q_call(rows, x, layer, cos_t, sin_t, w_q):
    width = N_GROUPS * D_MODEL
    by_residue = not rows.per_row
    if by_residue:
        out_specs = [_residue_spec(rows, dil, D_MODEL) for dil in DIL_DILATIONS]
        out_shape = _residue_shapes(rows, D_MODEL, BF16)
        scratch = [_permute_scratch()]
    else:
        out_specs = pl.BlockSpec((rows.tile, N_GROUPS * DIL_HEADS, DIL_HD), lambda i: (i, 0, 0))
        out_shape = jax.ShapeDtypeStruct((rows.n_rows, N_GROUPS * DIL_HEADS, DIL_HD), F32)
        scratch = []
    return pl.pallas_call(
        functools.partial(_q_kernel, by_residue),
        grid=(rows.steps,),
        in_specs=[
            rows.row_spec(D_MODEL),
            rows.mod_spec(layer, 3), rows.mod_spec(layer, 4),
            _rope_spec(rows), _rope_spec(rows),
            _const_spec((D_MODEL, width), (0, 0)),
        ],
        out_specs=out_specs,
        out_shape=out_shape,
        scratch_shapes=scratch,
        compiler_params=_params("arbitrary"),
        name="q_proj",
    )(x, rows.mods, rows.mods, cos_t, sin_t, w_q)


def _pair_softmax_pv(scores, v_pair, even_lane):
    half = scores.shape[0] // 2
    m = jnp.max(scores, axis=-1, keepdims=True)
    e = jnp.exp(scores - m)
    den = jnp.sum(e, axis=-1, keepdims=True)
    o2 = _dot(e.astype(BF16), v_pair) / den
    lse2 = m + jnp.log(den)
    o = jnp.where(even_lane, o2[:half], o2[half:])
    return o, lse2[:half], lse2[half:]


def _dil_prompt_kernel(has_prev, q_ref, *refs):
    if has_prev:
        kp_ref, kc_ref, vp_ref, vc_ref, o_ref, lse_ref = refs
    else:
        kc_ref, vc_ref, o_ref, lse_ref = refs
    blk = pl.program_id(2)
    n_keys = (2 if has_prev else 1) * ATT_BLOCK
    lane = lax.broadcasted_iota(jnp.int32, (ATT_BLOCK, LANES), 1)
    even_lane = lane < DIL_HD
    qi = lax.broadcasted_iota(jnp.int32, (2 * ATT_BLOCK, n_keys), 0) % ATT_BLOCK
    kj = lax.broadcasted_iota(jnp.int32, (2 * ATT_BLOCK, n_keys), 1)
    if has_prev:
        dist = qi + ATT_BLOCK - kj
        valid = (dist >= 0) & (dist <= DIL_STEPS) & ((kj >= ATT_BLOCK) | (blk > 0))
    else:
        valid = kj <= qi
    lse_all = jnp.zeros((ATT_BLOCK, LANES), F32)
    for p in range(DIL_HEADS // 2):
        sl = slice(p * LANES, (p + 1) * LANES)
        qp = q_ref[:, sl]
        zero = jnp.zeros_like(qp)
        q2 = jnp.concatenate([jnp.where(even_lane, qp, zero), jnp.where(even_lane, zero, qp)], axis=0)
        if has_prev:
            k_pair = jnp.concatenate([kp_ref[:, sl], kc_ref[:, sl]], axis=0)
            v_pair = jnp.concatenate([vp_ref[:, sl], vc_ref[:, sl]], axis=0)
        else:
            k_pair, v_pair = kc_ref[:, sl], vc_ref[:, sl]
        scores = jnp.where(valid, _dot_nt(q2, k_pair), -jnp.inf)
        o, lse_even, lse_odd = _pair_softmax_pv(scores, v_pair, even_lane)
        o_ref[:, sl] = o.astype(BF16)
        lse_all = jnp.where(lane == 2 * p, lse_even, jnp.where(lane == 2 * p + 1, lse_odd, lse_all))
    lse_ref[...] = lse_all


def _dil_prompt_call(q, kb, vb, group):
    n_b, dil, length, _ = kb.shape
    n_blk = length // ATT_BLOCK
    has_prev = n_blk > 1

    def blk(width, index_map):
        return pl.BlockSpec((None, None, ATT_BLOCK, width), index_map)

    cur = blk(D_MODEL, lambda b, r, i: (b, r, i, 0))
    prev = blk(D_MODEL, lambda b, r, i: (b, r, jnp.maximum(i - 1, 0), 0))
    if has_prev:
        in_specs, operands = [cur, prev, cur, prev, cur], (q, kb, kb, vb, vb)
    else:
        in_specs, operands = [cur, cur, cur], (q, kb, vb)
    return pl.pallas_call(
        functools.partial(_dil_prompt_kernel, has_prev),
        grid=(n_b, dil, n_blk),
        in_specs=in_specs,
        out_specs=[cur, blk(LANES, lambda b, r, i: (b, r, i, 0))],
        out_shape=[jax.ShapeDtypeStruct((n_b, dil, length, D_MODEL), BF16),
                   jax.ShapeDtypeStruct((n_b, dil, length, LANES), F32)],
        compiler_params=_params("arbitrary", "arbitrary", "arbitrary"),
        name="dilated_prompt",
    )(*operands)


SPAN = DIL_DILATIONS[-1]
NEAR_BLOCKS = DIL_WINDOWS[1] // SPAN


def _attend(q, keys, vals):
    scores = [jnp.sum(k * q, axis=-1, keepdims=True) for k in keys]
    m = functools.reduce(jnp.maximum, [jnp.max(s, axis=0, keepdims=True) for s in scores])
    e = [jnp.exp(s - m) for s in scores]
    den = sum(jnp.sum(ei, axis=0, keepdims=True) for ei in e)[0]
    o = sum(jnp.sum(ei * v, axis=0) for ei, v in zip(e, vals))
    return o / den, m[0] + jnp.log(den)


def _dil_sample_kernel(q_ref, kfar_ref, knear_ref, vfar_ref, vnear_ref, knew_ref, vnew_ref, o_ref):
    n_t = q_ref.shape[0]
    d1 = DIL_DILATIONS[1]
    for t in range(n_t):
        recent = DIL_STEPS // SPAN
        k0 = knear_ref[NEAR_BLOCKS - recent:].reshape(DIL_STEPS, DIL_HEADS, DIL_HD)[t:]
        v0 = vnear_ref[NEAR_BLOCKS - recent:].reshape(DIL_STEPS, DIL_HEADS, DIL_HD)[t:]
        parts = [_attend(q_ref[t, 0], [k0, knew_ref[0:t + 1]], [v0, vnew_ref[0:t + 1]])]
        k1 = [knear_ref[:, t + d1 * u] for u in range(SPAN // d1)] + [knew_ref[t:t + 1]]
        v1 = [vnear_ref[:, t + d1 * u] for u in range(SPAN // d1)] + [vnew_ref[t:t + 1]]
        parts.append(_attend(q_ref[t, 1], k1, v1))
        parts.append(_attend(q_ref[t, 2], [kfar_ref[:, t], knear_ref[:, t], knew_ref[t:t + 1]],
                             [vfar_ref[:, t], vnear_ref[:, t], vnew_ref[t:t + 1]]))
        m = functools.reduce(jnp.maximum, [lse for _, lse in parts])
        wts = [jnp.exp(lse - m) for _, lse in parts]
        o_ref[t] = sum(wt * o for wt, (o, _) in zip(wts, parts)) / sum(wts)


def _dil_sample_call(q, cache_k, cache_v, k_new, v_new):
    n_b, n_t = q.shape[:2]
    wb = cache_k.shape[1]
    assert wb == DIL_STEPS * SPAN and n_t <= DIL_DILATIONS[1] and DIL_DILATIONS[0] == 1
    n_blocks = wb // SPAN
    blocked = (n_b, n_blocks, SPAN, DIL_HEADS, DIL_HD)
    far = pl.BlockSpec((None, n_blocks - NEAR_BLOCKS, n_t, DIL_HEADS, DIL_HD), lambda b: (b, 0, 0, 0, 0))
    near = pl.BlockSpec((None, NEAR_BLOCKS, SPAN, DIL_HEADS, DIL_HD),
                        lambda b: (b, n_blocks // NEAR_BLOCKS - 1, 0, 0, 0))
    new = pl.BlockSpec((None, n_t, DIL_HEADS, DIL_HD), lambda b: (b, 0, 0, 0))
    ck, cv = cache_k.reshape(blocked), cache_v.reshape(blocked)
    return pl.pallas_call(
        _dil_sample_kernel,
        grid=(n_b,),
        in_specs=[pl.BlockSpec((None, n_t, N_GROUPS, DIL_HEADS, DIL_HD), lambda b: (b, 0, 0, 0, 0)),
                  far, near, far, near, new, new],
        out_specs=new,
        out_shape=jax.ShapeDtypeStruct((n_b, n_t, DIL_HEADS, DIL_HD), F32),
        compiler_params=_params("arbitrary"),
        name="dilated_sample",
    )(q, ck, ck, cv, cv, k_new, v_new)


def _att_merge(o_refs, lse_refs, o_scr, lse_scr):
    lses = [_load_by_token(l_ref, lse_scr) for l_ref in lse_refs]
    m = functools.reduce(jnp.maximum, lses)
    e = [jnp.exp(l - m) for l in lses]
    den = sum(e)
    lane = lax.broadcasted_iota(jnp.int32, (lses[0].shape[0], LANES), 1)
    even_lane = lane < DIL_HD
    merged = None
    for o_ref, eg in zip(o_refs, e):
        wt = eg / den
        wide = jnp.concatenate(
            [jnp.where(even_lane, wt[:, 2 * p:2 * p + 1], wt[:, 2 * p + 1:2 * p + 2]) for p in range(DIL_HEADS // 2)],
            axis=1)
        part = wide * _load_by_token(o_ref, o_scr)
        merged = part if merged is None else merged + part
    return merged


def _att_out_kernel(merged_input, *refs):
    if merged_input:
        o_ref, x_ref, gt_ref, wout_ref, g_ref, b_ref, y_ref = refs
        o = jnp.concatenate([o_ref[:, hd, :] for hd in range(DIL_HEADS)], axis=1)
    else:
        o_refs, lse_refs = refs[0:N_GROUPS], refs[N_GROUPS:2 * N_GROUPS]
        x_ref, gt_ref, wout_ref, g_ref, b_ref, y_ref, o_scr, lse_scr = refs[2 * N_GROUPS:]
        o = _att_merge(o_refs, lse_refs, o_scr, lse_scr)
    y = _dot(o.astype(BF16), wout_ref[...])
    z = ALPHA * x_ref[...] + (1.0 + gt_ref[...]) * y
    y_ref[...] = _layer_norm(z, g_ref[...], b_ref[...])


def _att_out_call(rows, att, x, layer, w_out, ln_g, ln_b):
    merged_input = rows.per_row
    if merged_input:
        att_specs = [pl.BlockSpec((rows.tile, DIL_HEADS, DIL_HD), lambda i: (i, 0, 0))]
        operands, scratch = [att], []
    else:
        outs, lses = att
        att_specs = ([_residue_spec(rows, dil, D_MODEL) for dil in DIL_DILATIONS]
                     + [_residue_spec(rows, dil, LANES) for dil in DIL_DILATIONS])
        operands = list(outs) + list(lses)
        scratch = [_permute_scratch(), pltpu.VMEM((1, ROW_TILE, LANES), F32)]
    return pl.pallas_call(
        functools.partial(_att_out_kernel, merged_input),
        grid=(rows.steps,),
        in_specs=att_specs + [
            rows.row_spec(D_MODEL),
            rows.mod_spec(layer, 5),
            _const_spec((D_MODEL, D_MODEL), (0, 0)),
        ] + _ln_specs(layer, 1),
        out_specs=rows.row_spec(D_MODEL),
        out_shape=jax.ShapeDtypeStruct((rows.n_rows, D_MODEL), F32),
        scratch_shapes=scratch,
        compiler_params=_params("arbitrary"),
        name="att_out",
    )(*operands, x, rows.mods, w_out, ln_g, ln_b)


def _rope_tables(positions):
    half = DIL_HD // 2
    inv = ROPE_THETA ** (-jnp.arange(half, dtype=F32) / half)
    ang = positions.astype(F32)[:, None] * inv[None, :]
    cos, sin = jnp.cos(ang), jnp.sin(ang)
    return jnp.concatenate([cos, cos, cos, cos], axis=1), jnp.concatenate([-sin, sin, -sin, sin], axis=1)


def _trunk(rows, x, n_b, n_t, positions, gla_state, cache, w):
    cos_t, sin_t = _rope_tables(positions)
    if rows.per_row:
        cos_t, sin_t = jnp.tile(cos_t, (n_b, 1)), jnp.tile(sin_t, (n_b, 1))
    ffn = functools.partial(_ffn_call, rows)

    x = ffn(x, 0, 0, 0, w['up1'], w['down1'], w['ln_g'], w['ln_b'])
    q, k, v, la, r = _gla_proj_call(rows, x, 0, w['gla_qkv'], w['gla_g'], w['gla_r'], w['gla_g2'], w['gla_bg'])
    if rows.per_row:
        def cols(a):
            return a.reshape(n_b, n_t, GLA_HEADS, GLA_DK).transpose(0, 2, 3, 1)
        o, s_new = _gla_step_call(cols(q), cols(k), cols(la), v.reshape(n_b, n_t, GLA_V), gla_state)
    else:
        o, s_new = _gla_rec_call(q.reshape(n_b, n_t, GLA_QK), k.reshape(n_b, n_t, GLA_QK),
                                 v.reshape(n_b, n_t, GLA_V), la.reshape(n_b, n_t, GLA_QK), gla_state)
    x = _gla_out_call(rows, o.reshape(n_b * n_t, GLA_V), r, x, 0, w['gla_gon'], w['gla_out'], w['ln_g'], w['ln_b'])
    x = ffn(x, 0, 6, 2, w['up2'], w['down2'], w['ln_g'], w['ln_b'])

    k_rows, v_rows, *kv_groups = _kv_call(rows, x, cos_t, sin_t, w['kv'])
    kv_shape = (n_b, n_t, DIL_HEADS, DIL_HD)
    k_rows, v_rows = k_rows.reshape(kv_shape), v_rows.reshape(kv_shape)

    x = ffn(x, 1, 0, 0, w['up1'], w['down1'], w['ln_g'], w['ln_b'])
    qd = _q_call(rows, x, 1, cos_t, sin_t, w['q'])
    if rows.per_row:
        cache_k, cache_v = cache
        att = _dil_sample_call(qd.reshape(n_b, n_t, N_GROUPS, DIL_HEADS, DIL_HD), cache_k, cache_v, k_rows, v_rows)
        att = att.reshape(n_b * n_t, DIL_HEADS, DIL_HD)
    else:
        att = tuple(zip(*[_dil_prompt_call(qd[g], kv_groups[g], kv_groups[N_GROUPS + g], g) for g in range(N_GROUPS)]))
    x = _att_out_call(rows, att, x, 1, w['att_out'], w['ln_g'], w['ln_b'])
    x = ffn(x, 1, 6, 2, w['up2'], w['down2'], w['ln_g'], w['ln_b'])
    return x, s_new, k_rows, v_rows


def kernel(x_prompt, x_sample, state_gla, cache_k, cache_v, c_prompt, c_sample, w_ada, b_ada, ln_g, ln_b, w_ffn1_up, w_ffn1_down, w_ffn2_up, w_ffn2_down, w_in_a, w_gate2_a, b_gate_a, g_onorm_a, w_out_a, w_ada_kv, b_ada_kv, w_kv, w_q_b, w_out_b):
    n_bp, n_tp = x_prompt.shape[:2]
    n_bs, n_ts = x_sample.shape[:2]
    assert N_A_LAYERS == 1 and DEPTH == 2
    assert n_tp % (ATT_BLOCK * DIL_DILATIONS[-1]) == 0 and n_tp % GLA_TILE == 0 and n_tp <= MAX_WINDOW
    assert n_ts < 8

    c_all = jnp.concatenate([c_prompt, c_sample], axis=0)
    mods = _ada_call(c_all, w_ada, b_ada, 1536)
    kv_mods = _ada_call(c_all, w_ada_kv[None], b_ada_kv[None], 1024)
    rows_p = _Rows(n_bp * n_tp, n_tp, mods[:, :n_bp, None, :], kv_mods[:, :n_bp, None, :])
    rows_s = _Rows(n_bs * n_ts, n_ts, jnp.repeat(mods[:, n_bp:], n_ts, axis=1),
                   jnp.repeat(kv_mods[:, n_bp:], n_ts, axis=1))

    w_in = w_in_a[0]
    gate_lo = 2 * GLA_QK + GLA_V
    w = {
        'up1': w_ffn1_up.astype(BF16), 'down1': w_ffn1_down.astype(BF16),
        'up2': w_ffn2_up.astype(BF16), 'down2': w_ffn2_down.astype(BF16),
        'ln_g': ln_g[:, :, None, :], 'ln_b': ln_b[:, :, None, :],
        'gla_qkv': w_in[:, :gate_lo].astype(BF16),
        'gla_g': jnp.pad(w_in[:, gate_lo:gate_lo + GLA_RANK], ((0, 0), (0, LANES - GLA_RANK))).astype(BF16),
        'gla_r': w_in[:, gate_lo + GLA_RANK:].astype(BF16),
        'gla_g2': jnp.pad(w_gate2_a[0], ((0, LANES - GLA_RANK), (0, 0))).astype(BF16),
        'gla_bg': b_gate_a[0][None, :],
        'gla_gon': g_onorm_a[0][None, :],
        'gla_out': w_out_a[0].astype(BF16),
        'kv': w_kv.astype(BF16),
        'q': w_q_b[0].astype(BF16),
        'att_out': w_out_b[0].astype(BF16),
    }

    zero_state = jnp.zeros((n_bp, GLA_HEADS, GLA_DK, GLA_DV), F32)
    y_p, s_p, k_p, v_p = _trunk(rows_p, x_prompt.reshape(-1, D_MODEL), n_bp, n_tp,
                                jnp.arange(n_tp), zero_state, None, w)
    y_s, s_s, k_s, v_s = _trunk(rows_s, x_sample.reshape(-1, D_MODEL), n_bs, n_ts,
                                PAST_LEN + jnp.arange(n_ts), state_gla[0], (cache_k, cache_v), w)
    return (y_p.reshape(n_bp, n_tp, D_MODEL), y_s.reshape(n_bs, n_ts, D_MODEL),
            s_p[None], s_s[None], k_p, v_p, k_s, v_s)
```

```python
import functools

import jax
import jax.numpy as jnp
from jax import lax
from jax.experimental import pallas as pl
from jax.experimental.pallas import tpu as pltpu

F32 = jnp.float32
BF16 = jnp.bfloat16

D_MODEL = 1024
DEPTH = 2
PAST_LEN = 16384
N_A_LAYERS = DEPTH // 2
GLA_HEADS = 4
GLA_QK = D_MODEL // 2
GLA_V = D_MODEL
GLA_DK = GLA_QK // GLA_HEADS
GLA_DV = GLA_V // GLA_HEADS
GLA_RANK = 16
GLA_TAU = 16.0
GLA_NORM_EPS = 1e-6
DIL_HEADS = 16
DIL_HD = D_MODEL // DIL_HEADS
DIL_WINDOWS = (128, 512, 2048)
DIL_DILATIONS = (1, 4, 16)
N_GROUPS = len(DIL_WINDOWS)
DIL_STEPS = 128
MAX_WINDOW = max(DIL_WINDOWS)
DIL_SCALE = DIL_HD ** -0.5
ROPE_THETA = 10000.0
D_FF = 2816
FFN_RES = 0.5
ALPHA = (2 * DEPTH) ** 0.25
LN_EPS = 1e-5
N_MOD = 9

LANES = 128
ROW_TILE = 512
FF_CHUNK = D_FF // 2
GLA_CHUNK = 128
GLA_TILE = 512
GLA_FACTOR_BOUND = 80.0
ATT_BLOCK = DIL_STEPS
VMEM_LIMIT = 56 * 1024 * 1024


def _params(*sem):
    return pltpu.CompilerParams(dimension_semantics=sem, vmem_limit_bytes=VMEM_LIMIT)


def _silu(x):
    return x / (1.0 + jnp.exp(-x))


def _layer_norm(z, g, b):
    mu = jnp.mean(z, axis=-1, keepdims=True)
    zc = z - mu
    var = jnp.mean(zc * zc, axis=-1, keepdims=True)
    return zc * lax.rsqrt(var + LN_EPS) * g + b


def _dot(a, b):
    return jnp.dot(a, b, preferred_element_type=F32)


def _dot_nt(a, b):
    return lax.dot_general(a, b, (((1,), (1,)), ((), ())), preferred_element_type=F32)


def _rope(x, cos_t, sin_t):
    width = x.shape[1]
    lane = lax.broadcasted_iota(jnp.int32, x.shape, 1)
    first_half = (lane % DIL_HD) < (DIL_HD // 2)
    partner = jnp.where(first_half, pltpu.roll(x, width - DIL_HD // 2, 1), pltpu.roll(x, DIL_HD // 2, 1))
    reps = width // LANES
    cos_w = jnp.concatenate([cos_t] * reps, axis=1)
    sin_w = jnp.concatenate([sin_t] * reps, axis=1)
    return x * cos_w + partner * sin_w


def _ada_kernel(c_ref, w_ref, b_ref, o_ref):
    s = _silu(c_ref[...]).astype(BF16)
    o_ref[...] = _dot(s, w_ref[...].astype(BF16)) + b_ref[...]


def _ada_call(c_all, w, b, tn):
    n_l, _, n = w.shape
    rows = c_all.shape[0]
    return pl.pallas_call(
        _ada_kernel,
        grid=(n_l, n // tn),
        in_specs=[
            pl.BlockSpec((rows, D_MODEL), lambda l, j: (0, 0)),
            pl.BlockSpec((None, D_MODEL, tn), lambda l, j: (l, 0, j)),
            pl.BlockSpec((None, 1, tn), lambda l, j: (l, 0, j)),
        ],
        out_specs=pl.BlockSpec((None, rows, tn), lambda l, j: (l, 0, j)),
        out_shape=jax.ShapeDtypeStruct((n_l, rows, n), F32),
        compiler_params=_params("arbitrary", "arbitrary"),
        name="ada_mod",
    )(c_all, w, b.reshape(n_l, 1, n))


class _Rows:
    def __init__(self, n_rows, rows_per_sample, mods, kv_mods):
        self.n_rows = n_rows
        self.rows_per_sample = rows_per_sample
        self.per_row = rows_per_sample < 8
        self.tile = n_rows if self.per_row else min(ROW_TILE, rows_per_sample)
        self.steps = n_rows // self.tile
        self.tiles_per_sample = 1 if self.per_row else rows_per_sample // self.tile
        self.mods = mods
        self.kv_mods = kv_mods

    def row_spec(self, width):
        return pl.BlockSpec((self.tile, width), lambda i: (i, 0))

    def mod_spec(self, layer, k):
        if self.per_row:
            return pl.BlockSpec((None, self.tile, D_MODEL), lambda i: (layer, 0, k))
        tps = self.tiles_per_sample
        return pl.BlockSpec((None, None, 1, D_MODEL), lambda i: (layer, i // tps, 0, k))


def _const_spec(shape, index):
    return pl.BlockSpec(shape, lambda i: index, pipeline_mode=pl.Buffered(1))


def _ln_specs(layer, j):
    return [_const_spec((None, None, 1, D_MODEL), (layer, j, 0, 0))] * 2


def _ffn_kernel(x_ref, sh_ref, sc_ref, gt_ref, wup_ref, wdn_ref, g_ref, b_ref, o_ref):
    x = x_ref[...]
    h = (x * (1.0 + sc_ref[...]) + sh_ref[...]).astype(BF16)
    y = None
    for lo in range(0, D_FF, FF_CHUNK):
        a = _dot(h, wup_ref[:, lo:lo + FF_CHUNK])
        u = _dot(h, wup_ref[:, D_FF + lo:D_FF + lo + FF_CHUNK])
        part = _dot((_silu(a) * u).astype(BF16), wdn_ref[lo:lo + FF_CHUNK, :])
        y = part if y is None else y + part
    z = ALPHA * x + (1.0 + gt_ref[...]) * (FFN_RES * y)
    o_ref[...] = _layer_norm(z, g_ref[...], b_ref[...])


def _ffn_call(rows, x, layer, mod0, ln_j, w_up, w_down, ln_g, ln_b):
    return pl.pallas_call(
        _ffn_kernel,
        grid=(rows.steps,),
        in_specs=[
            rows.row_spec(D_MODEL),
            rows.mod_spec(layer, mod0), rows.mod_spec(layer, mod0 + 1), rows.mod_spec(layer, mod0 + 2),
            _const_spec((None, D_MODEL, 2 * D_FF), (layer, 0, 0)),
            _const_spec((None, D_FF, D_MODEL), (layer, 0, 0)),
        ] + _ln_specs(layer, ln_j),
        out_specs=rows.row_spec(D_MODEL),
        out_shape=jax.ShapeDtypeStruct((rows.n_rows, D_MODEL), F32),
        compiler_params=_params("arbitrary"),
        name="ffn",
    )(x, rows.mods, rows.mods, rows.mods, w_up, w_down, ln_g, ln_b)


def _gla_proj_kernel(x_ref, sh_ref, sc_ref, wqkv_ref, wg_ref, wr_ref, wg2_ref, bg_ref,
                     q_ref, k_ref, v_ref, la_ref, r_ref):
    h = (x_ref[...] * (1.0 + sc_ref[...]) + sh_ref[...]).astype(BF16)
    qkv = _dot(h, wqkv_ref[...])
    q_ref[...] = qkv[:, :GLA_QK] * (GLA_DK ** -0.5)
    k_ref[...] = qkv[:, GLA_QK:2 * GLA_QK]
    v_ref[...] = qkv[:, 2 * GLA_QK:]
    g_lr = _dot(h, wg_ref[...]).astype(BF16)
    gate = _dot(g_lr, wg2_ref[...]) + bg_ref[...]
    log_sig = jnp.minimum(gate, 0.0) - jnp.log(1.0 + jnp.exp(-jnp.abs(gate)))
    la_ref[...] = log_sig / GLA_TAU
    r_ref[...] = _dot(h, wr_ref[...])


def _gla_proj_call(rows, x, layer, w_qkv, w_g, w_r, w_g2, b_g):
    widths = (GLA_QK, GLA_QK, GLA_V, GLA_QK, GLA_V)
    return pl.pallas_call(
        _gla_proj_kernel,
        grid=(rows.steps,),
        in_specs=[
            rows.row_spec(D_MODEL),
            rows.mod_spec(layer, 3), rows.mod_spec(layer, 4),
            _const_spec((D_MODEL, 2 * GLA_QK + GLA_V), (0, 0)),
            _const_spec((D_MODEL, LANES), (0, 0)),
            _const_spec((D_MODEL, GLA_V), (0, 0)),
            _const_spec((LANES, GLA_QK), (0, 0)),
            _const_spec((1, GLA_QK), (0, 0)),
        ],
        out_specs=[rows.row_spec(w) for w in widths],
        out_shape=[jax.ShapeDtypeStruct((rows.n_rows, w), F32) for w in widths],
        compiler_params=_params("arbitrary"),
        name="gla_proj",
    )(x, rows.mods, rows.mods, w_qkv, w_g, w_r, w_g2, b_g)


def _gla_scores_factored(qh, kh, bh, b_last):
    mid = 0.5 * b_last
    q_t = (qh * jnp.exp(bh - mid)).astype(BF16)
    k_t = (kh * jnp.exp(mid - bh)).astype(BF16)
    return _dot_nt(q_t, k_t)


def _gla_scores_elementwise(qh, kh, bh, k_scr, b_scr):
    k_scr[...] = kh
    b_scr[...] = bh
    col = lax.broadcasted_iota(jnp.int32, (GLA_CHUNK, GLA_CHUNK), 1)

    def body(j, acc):
        kj = k_scr[pl.ds(j, 1), :]
        bj = b_scr[pl.ds(j, 1), :]
        term = qh * kj * jnp.exp(jnp.minimum(bh - bj, 0.0))
        return acc + jnp.where(col == j, jnp.sum(term, axis=1, keepdims=True), 0.0)

    return lax.fori_loop(0, GLA_CHUNK, body, jnp.zeros((GLA_CHUNK, GLA_CHUNK), F32))


def _gla_rec_kernel(q_ref, k_ref, v_ref, la_ref, s0_ref, o_ref, sfin_ref, s_scr, k_scr, b_scr):
    t = pl.program_id(1)

    @pl.when(t == 0)
    def _():
        s_scr[...] = s0_ref[...]

    row = lax.broadcasted_iota(jnp.int32, (GLA_CHUNK, GLA_CHUNK), 0)
    col = lax.broadcasted_iota(jnp.int32, (GLA_CHUNK, GLA_CHUNK), 1)
    causal = col <= row
    tri = causal.astype(F32)

    def chunk(c, carry):
        r0 = pl.multiple_of(c * GLA_CHUNK, GLA_CHUNK)
        q = q_ref[pl.ds(r0, GLA_CHUNK), :]
        k = k_ref[pl.ds(r0, GLA_CHUNK), :]
        la = la_ref[pl.ds(r0, GLA_CHUNK), :]
        b = jnp.dot(tri, la, precision=lax.Precision.HIGHEST, preferred_element_type=F32)
        b_last = b[GLA_CHUNK - 1:GLA_CHUNK, :]
        factorable = jnp.max(-b_last) <= GLA_FACTOR_BOUND
        q_in = q * jnp.exp(b)
        k_out = k * jnp.exp(b_last - b)
        for hd in range(GLA_HEADS):
            ks = slice(hd * GLA_DK, (hd + 1) * GLA_DK)
            vs = slice(hd * GLA_DV, (hd + 1) * GLA_DV)
            vh = v_ref[pl.ds(r0, GLA_CHUNK), vs].astype(BF16)
            s_prev = s_scr[hd]
            o_inter = _dot(q_in[:, ks].astype(BF16), s_prev.astype(BF16))
            qh, kh, bh, blh = q[:, ks], k[:, ks], b[:, ks], b_last[:, ks]
            scores = lax.cond(
                factorable,
                lambda: _gla_scores_factored(qh, kh, bh, blh),
                lambda: _gla_scores_elementwise(qh, kh, bh, k_scr, b_scr))
            scores = jnp.where(causal, scores, 0.0)
            o_ref[pl.ds(r0, GLA_CHUNK), vs] = o_inter + _dot(scores.astype(BF16), vh)
            decay_col = jnp.transpose(jnp.broadcast_to(jnp.exp(blh), (GLA_DK, GLA_DK)))[:, :1]
            k_out_t = jnp.transpose(k_out[:, ks]).astype(BF16)
            s_scr[hd] = decay_col * s_prev + _dot(k_out_t, vh)
        return carry

    lax.fori_loop(0, GLA_TILE // GLA_CHUNK, chunk, 0)

    @pl.when(t == pl.num_programs(1) - 1)
    def _():
        sfin_ref[...] = s_scr[...]


def _gla_rec_call(q, k, v, la, s0):
    n_b, n_t = q.shape[:2]

    def seq(width):
        return pl.BlockSpec((None, GLA_TILE, width), lambda b, t: (b, t, 0))

    state = pl.BlockSpec((None, GLA_HEADS, GLA_DK, GLA_DV), lambda b, t: (b, 0, 0, 0))
    return pl.pallas_call(
        _gla_rec_kernel,
        grid=(n_b, n_t // GLA_TILE),
        in_specs=[seq(GLA_QK), seq(GLA_QK), seq(GLA_V), seq(GLA_QK), state],
        out_specs=[seq(GLA_V), state],
        out_shape=[jax.ShapeDtypeStruct((n_b, n_t, GLA_V), F32),
                   jax.ShapeDtypeStruct((n_b, GLA_HEADS, GLA_DK, GLA_DV), F32)],
        scratch_shapes=[pltpu.VMEM((GLA_HEADS, GLA_DK, GLA_DV), F32),
                        pltpu.VMEM((GLA_CHUNK, GLA_DK), F32),
                        pltpu.VMEM((GLA_CHUNK, GLA_DK), F32)],
        compiler_params=_params("arbitrary", "arbitrary"),
        name="gla_recurrence",
    )(q, k, v, la, s0)


def _gla_step_kernel(qt_ref, kt_ref, lat_ref, v_ref, s0_ref, o_ref, sfin_ref):
    n_t = v_ref.shape[0]
    for hd in range(GLA_HEADS):
        vs = slice(hd * GLA_DV, (hd + 1) * GLA_DV)
        s = s0_ref[hd]
        qt, kt, lat = qt_ref[hd], kt_ref[hd], lat_ref[hd]
        for t in range(n_t):
            s = jnp.exp(lat[:, t:t + 1]) * s + kt[:, t:t + 1] * v_ref[t:t + 1, vs]
            o_ref[t:t + 1, vs] = jnp.sum(qt[:, t:t + 1] * s, axis=0, keepdims=True)
        sfin_ref[hd] = s


def _gla_step_call(qt, kt, lat, v, s0):
    n_b, _, _, n_t = qt.shape
    col = pl.BlockSpec((None, GLA_HEADS, GLA_DK, n_t), lambda b: (b, 0, 0, 0))
    tok = pl.BlockSpec((None, n_t, GLA_V), lambda b: (b, 0, 0))
    state = pl.BlockSpec((None, GLA_HEADS, GLA_DK, GLA_DV), lambda b: (b, 0, 0, 0))
    return pl.pallas_call(
        _gla_step_kernel,
        grid=(n_b,),
        in_specs=[col, col, col, tok, state],
        out_specs=[tok, state],
        out_shape=[jax.ShapeDtypeStruct((n_b, n_t, GLA_V), F32),
                   jax.ShapeDtypeStruct((n_b, GLA_HEADS, GLA_DK, GLA_DV), F32)],
        compiler_params=_params("arbitrary"),
        name="gla_step",
    )(qt, kt, lat, v, s0)


def _gla_out_kernel(o_ref, r_ref, x_ref, gt_ref, gon_ref, wout_ref, g_ref, b_ref, y_ref):
    o = o_ref[...]
    parts = []
    for hd in range(GLA_HEADS):
        oh = o[:, hd * GLA_DV:(hd + 1) * GLA_DV]
        ms = jnp.mean(oh * oh, axis=-1, keepdims=True)
        parts.append(oh * lax.rsqrt(ms + GLA_NORM_EPS) * gon_ref[...])
    gated = (jnp.concatenate(parts, axis=1) * _silu(r_ref[...])).astype(BF16)
    y = _dot(gated, wout_ref[...])
    z = ALPHA * x_ref[...] + (1.0 + gt_ref[...]) * y
    y_ref[...] = _layer_norm(z, g_ref[...], b_ref[...])


def _gla_out_call(rows, o, r, x, layer, g_onorm, w_out, ln_g, ln_b):
    return pl.pallas_call(
        _gla_out_kernel,
        grid=(rows.steps,),
        in_specs=[
            rows.row_spec(GLA_V), rows.row_spec(GLA_V), rows.row_spec(D_MODEL),
            rows.mod_spec(layer, 5),
            _const_spec((1, GLA_DV), (0, 0)),
            _const_spec((GLA_V, D_MODEL), (0, 0)),
        ] + _ln_specs(layer, 1),
        out_specs=rows.row_spec(D_MODEL),
        out_shape=jax.ShapeDtypeStruct((rows.n_rows, D_MODEL), F32),
        compiler_params=_params("arbitrary"),
        name="gla_out",
    )(o, r, x, rows.mods, g_onorm, w_out, ln_g, ln_b)


def _store_by_residue(x, out_refs, scr):
    n_rows, width = x.shape
    n_col = width // LANES
    if any(o.shape[0] > 1 for o in out_refs):
        for c in range(n_col):
            scr[c] = x[:, c * LANES:(c + 1) * LANES]
    for o_ref in out_refs:
        dil = o_ref.shape[0]
        if dil == 1:
            o_ref[0] = x.astype(BF16)
            continue
        for r in range(dil):
            o_ref[r] = jnp.concatenate(
                [scr[c, pl.ds(r, n_rows // dil, stride=dil), :] for c in range(n_col)], axis=1).astype(BF16)


def _load_by_token(x_ref, scr):
    dil, n_per, width = x_ref.shape
    if dil == 1:
        return x_ref[0].astype(F32)
    n_col = width // LANES
    for r in range(dil):
        xr = x_ref[r].astype(F32)
        for c in range(n_col):
            scr[c, pl.ds(r, n_per, stride=dil), :] = xr[:, c * LANES:(c + 1) * LANES]
    if n_col == 1:
        return scr[0]
    return jnp.concatenate([scr[c] for c in range(n_col)], axis=1)


def _kv_kernel(by_residue, x_ref, sh_ref, sc_ref, cos_ref, sin_ref, w_ref, k_ref, v_ref, *rest):
    h = (x_ref[...] * (1.0 + sc_ref[...]) + sh_ref[...]).astype(BF16)
    kv = _dot(h, w_ref[...])
    k = _rope(kv[:, :D_MODEL], cos_ref[...], sin_ref[...])
    v = kv[:, D_MODEL:]
    if by_residue:
        k_ref[...] = jnp.transpose(k)
        v_ref[...] = jnp.transpose(v)
    else:
        k_ref[...] = k.reshape(k_ref.shape)
        v_ref[...] = v.reshape(v_ref.shape)
    if by_residue:
        scr = rest[-1]
        _store_by_residue(k, rest[0:N_GROUPS], scr)
        _store_by_residue(v, rest[N_GROUPS:2 * N_GROUPS], scr)


def _rope_spec(rows):
    if rows.per_row:
        return pl.BlockSpec((rows.tile, LANES), lambda i: (0, 0))
    tps = rows.tiles_per_sample
    return pl.BlockSpec((rows.tile, LANES), lambda i: (i % tps, 0))


def _kv_mod_spec(rows, k):
    if rows.per_row:
        return pl.BlockSpec((None, rows.tile, D_MODEL), lambda i: (0, 0, k))
    tps = rows.tiles_per_sample
    return pl.BlockSpec((None, None, 1, D_MODEL), lambda i: (0, i // tps, 0, k))


def _residue_spec(rows, dil, width):
    tps = rows.tiles_per_sample
    return pl.BlockSpec((None, dil, rows.tile // dil, width), lambda i: (i // tps, 0, i % tps, 0))


def _residue_shapes(rows, width, dtype):
    n_b = rows.n_rows // rows.rows_per_sample
    return [jax.ShapeDtypeStruct((n_b, dil, rows.rows_per_sample // dil, width), dtype) for dil in DIL_DILATIONS]


def _permute_scratch():
    return pltpu.VMEM((D_MODEL // LANES, ROW_TILE, LANES), F32)


def _kv_call(rows, x, cos_t, sin_t, w_kv):
    by_residue = not rows.per_row
    scratch = []
    if by_residue:
        tps = rows.tiles_per_sample
        n_b = rows.n_rows // rows.rows_per_sample
        out_specs = [pl.BlockSpec((None, D_MODEL, rows.tile), lambda i: (i // tps, 0, i % tps))] * 2
        out_shape = [jax.ShapeDtypeStruct((n_b, D_MODEL, rows.rows_per_sample), F32)] * 2
    else:
        out_specs = [pl.BlockSpec((rows.tile, DIL_HEADS, DIL_HD), lambda i: (i, 0, 0))] * 2
        out_shape = [jax.ShapeDtypeStruct((rows.n_rows, DIL_HEADS, DIL_HD), F32)] * 2
    if by_residue:
        out_specs += [_residue_spec(rows, dil, D_MODEL) for dil in DIL_DILATIONS] * 2
        out_shape += _residue_shapes(rows, D_MODEL, BF16) * 2
        scratch = [_permute_scratch()]
    return pl.pallas_call(
        functools.partial(_kv_kernel, by_residue),
        grid=(rows.steps,),
        in_specs=[
            rows.row_spec(D_MODEL),
            _kv_mod_spec(rows, 0), _kv_mod_spec(rows, 1),
            _rope_spec(rows), _rope_spec(rows),
            _const_spec((D_MODEL, 2 * D_MODEL), (0, 0)),
        ],
        out_specs=out_specs,
        out_shape=out_shape,
        scratch_shapes=scratch,
        compiler_params=_params("arbitrary"),
        name="kv_proj",
    )(x, rows.kv_mods, rows.kv_mods, cos_t, sin_t, w_kv)


def _q_kernel(by_residue, x_ref, sh_ref, sc_ref, cos_ref, sin_ref, w_ref, *rest):
    h = (x_ref[...] * (1.0 + sc_ref[...]) + sh_ref[...]).astype(BF16)
    q = _rope(_dot(h, w_ref[...]), cos_ref[...], sin_ref[...]) * DIL_SCALE
    if by_residue:
        for g in range(N_GROUPS):
            _store_by_residue(q[:, g * D_MODEL:(g + 1) * D_MODEL], rest[g:g + 1], rest[-1])
    else:
        rest[0][...] = q.reshape(rest[0].shape)


def _q_call(rows, x, layer, cos_t, sin_t, w_q):
    width = N_GROUPS * D_MODEL
    by_residue = not rows.per_row
    if by_residue:
        out_specs = [_residue_spec(rows, dil, D_MODEL) for dil in DIL_DILATIONS]
        out_shape = _residue_shapes(rows, D_MODEL, BF16)
        scratch = [_permute_scratch()]
    else:
        out_specs = pl.BlockSpec((rows.tile, N_GROUPS * DIL_HEADS, DIL_HD), lambda i: (i, 0, 0))
        out_shape = jax.ShapeDtypeStruct((rows.n_rows, N_GROUPS * DIL_HEADS, DIL_HD), F32)
        scratch = []
    return pl.pallas_call(
        functools.partial(_q_kernel, by_residue),
        grid=(rows.steps,),
        in_specs=[
            rows.row_spec(D_MODEL),
            rows.mod_spec(layer, 3), rows.mod_spec(layer, 4),
            _rope_spec(rows), _rope_spec(rows),
            _const_spec((D_MODEL, width), (0, 0)),
        ],
        out_specs=out_specs,
        out_shape=out_shape,
        scratch_shapes=scratch,
        compiler_params=_params("arbitrary"),
        name="q_proj",
    )(x, rows.mods, rows.mods, cos_t, sin_t, w_q)


def _pair_softmax_pv(scores, v_pair, even_lane):
    half = scores.shape[0] // 2
    m = jnp.max(scores, axis=-1, keepdims=True)
    e = jnp.exp(scores - m)
    den = jnp.sum(e, axis=-1, keepdims=True)
    o2 = _dot(e.astype(BF16), v_pair) / den
    lse2 = m + jnp.log(den)
    o = jnp.where(even_lane, o2[:half], o2[half:])
    return o, lse2[:half], lse2[half:]


def _dil_prompt_kernel(has_prev, q_ref, *refs):
    if has_prev:
        kp_ref, kc_ref, vp_ref, vc_ref, o_ref, lse_ref = refs
    else:
        kc_ref, vc_ref, o_ref, lse_ref = refs
    blk = pl.program_id(2)
    n_keys = (2 if has_prev else 1) * ATT_BLOCK
    lane = lax.broadcasted_iota(jnp.int32, (ATT_BLOCK, LANES), 1)
    even_lane = lane < DIL_HD
    qi = lax.broadcasted_iota(jnp.int32, (2 * ATT_BLOCK, n_keys), 0) % ATT_BLOCK
    kj = lax.broadcasted_iota(jnp.int32, (2 * ATT_BLOCK, n_keys), 1)
    if has_prev:
        dist = qi + ATT_BLOCK - kj
        valid = (dist >= 0) & (dist <= DIL_STEPS) & ((kj >= ATT_BLOCK) | (blk > 0))
    else:
        valid = kj <= qi
    lse_all = jnp.zeros((ATT_BLOCK, LANES), F32)
    for p in range(DIL_HEADS // 2):
        sl = slice(p * LANES, (p + 1) * LANES)
        qp = q_ref[:, sl]
        zero = jnp.zeros_like(qp)
        q2 = jnp.concatenate([jnp.where(even_lane, qp, zero), jnp.where(even_lane, zero, qp)], axis=0)
        if has_prev:
            k_pair = jnp.concatenate([kp_ref[:, sl], kc_ref[:, sl]], axis=0)
            v_pair = jnp.concatenate([vp_ref[:, sl], vc_ref[:, sl]], axis=0)
        else:
            k_pair, v_pair = kc_ref[:, sl], vc_ref[:, sl]
        scores = jnp.where(valid, _dot_nt(q2, k_pair), -jnp.inf)
        o, lse_even, lse_odd = _pair_softmax_pv(scores, v_pair, even_lane)
        o_ref[:, sl] = o.astype(BF16)
        lse_all = jnp.where(lane == 2 * p, lse_even, jnp.where(lane == 2 * p + 1, lse_odd, lse_all))
    lse_ref[...] = lse_all


def _dil_prompt_call(q, kb, vb, group):
    n_b, dil, length, _ = kb.shape
    n_blk = length // ATT_BLOCK
    has_prev = n_blk > 1

    def blk(width, index_map):
        return pl.BlockSpec((None, None, ATT_BLOCK, width), index_map)

    cur = blk(D_MODEL, lambda b, r, i: (b, r, i, 0))
    prev = blk(D_MODEL, lambda b, r, i: (b, r, jnp.maximum(i - 1, 0), 0))
    if has_prev:
        in_specs, operands = [cur, prev, cur, prev, cur], (q, kb, kb, vb, vb)
    else:
        in_specs, operands = [cur, cur, cur], (q, kb, vb)
    return pl.pallas_call(
        functools.partial(_dil_prompt_kernel, has_prev),
        grid=(n_b, dil, n_blk),
        in_specs=in_specs,
        out_specs=[cur, blk(LANES, lambda b, r, i: (b, r, i, 0))],
        out_shape=[jax.ShapeDtypeStruct((n_b, dil, length, D_MODEL), BF16),
                   jax.ShapeDtypeStruct((n_b, dil, length, LANES), F32)],
        compiler_params=_params("arbitrary", "arbitrary", "arbitrary"),
        name="dilated_prompt",
    )(*operands)


SAMPLE_HEAD_CHUNK = 4


def _sample_group_dense(t, q_col, k_win, v_win, k_new, v_new):
    n_t = k_new.shape[-1]
    pos = lax.broadcasted_iota(jnp.int32, (1, 1, k_win.shape[-1]), 2)
    pos_new = lax.broadcasted_iota(jnp.int32, (1, 1, n_t), 2)
    s_c = jnp.where(pos >= t, jnp.sum(k_win * q_col, axis=1, keepdims=True), -jnp.inf)
    s_n = jnp.where(pos_new <= t, jnp.sum(k_new * q_col, axis=1, keepdims=True), -jnp.inf)
    m = jnp.maximum(jnp.max(s_c, axis=2, keepdims=True), jnp.max(s_n, axis=2, keepdims=True))
    e_c, e_n = jnp.exp(s_c - m), jnp.exp(s_n - m)
    den = jnp.sum(e_c, axis=2, keepdims=True) + jnp.sum(e_n, axis=2, keepdims=True)
    o = (jnp.sum(v_win * e_c, axis=2, keepdims=True) + jnp.sum(v_new * e_n, axis=2, keepdims=True)) / den
    return o, m + jnp.log(den)


def _sample_group_dilated(dil, q_g, k_win, v_win, k_new, v_new):
    n_t = q_g.shape[-1]
    width = k_win.shape[-1]
    n_col = width // LANES
    res = lax.broadcasted_iota(jnp.int32, (1, 1, LANES), 2) % dil
    res_w = lax.broadcasted_iota(jnp.int32, (1, 1, width), 2) % dil
    q_pat = jnp.zeros(q_g.shape[:2] + (LANES,), F32)
    for t in range(n_t):
        q_pat = jnp.where(res == t, q_g[:, :, t:t + 1], q_pat)
    s_row = jnp.sum(k_win * jnp.concatenate([q_pat] * n_col, axis=2), axis=1, keepdims=True)
    s_new = jnp.sum(k_new * q_g, axis=1, keepdims=True)
    m_t = [jnp.maximum(jnp.max(jnp.where(res_w == t, s_row, -jnp.inf), axis=2, keepdims=True), s_new[:, :, t:t + 1])
           for t in range(n_t)]
    m_row = jnp.zeros_like(s_row)
    for t in range(n_t):
        m_row = jnp.where(res_w == t, m_t[t], m_row)
    e = jnp.exp(jnp.where(res_w < n_t, s_row, -jnp.inf) - m_row)
    acc = v_win * e
    acc_fold = sum(acc[:, :, c * LANES:(c + 1) * LANES] for c in range(n_col))
    e_fold = sum(e[:, :, c * LANES:(c + 1) * LANES] for c in range(n_col))
    out = []
    for t in range(n_t):
        e_new = jnp.exp(s_new[:, :, t:t + 1] - m_t[t])
        den = jnp.sum(jnp.where(res == t, e_fold, 0.0), axis=2, keepdims=True) + e_new
        o = (jnp.sum(jnp.where(res == t, acc_fold, 0.0), axis=2, keepdims=True) + e_new * v_new[:, :, t:t + 1]) / den
        out.append((o, m_t[t] + jnp.log(den)))
    return out


def _dil_sample_kernel(q_ref, k_ref, v_ref, kn_ref, vn_ref, o_ref):
    n_t = q_ref.shape[-1]
    n_s = k_ref.shape[-1]
    pos_new = lax.broadcasted_iota(jnp.int32, (1, 1, n_t), 2)

    def head_chunk(i, carry):
        hs = pl.ds(pl.multiple_of(i * SAMPLE_HEAD_CHUNK, SAMPLE_HEAD_CHUNK), SAMPLE_HEAD_CHUNK)
        k_new, v_new = kn_ref[hs], vn_ref[hs]
        groups = []
        for g, dil in enumerate(DIL_DILATIONS):
            lo = n_s - DIL_STEPS * dil
            k_win, v_win = k_ref[hs, :, lo:], v_ref[hs, :, lo:]
            if dil == 1:
                groups.append([_sample_group_dense(t, q_ref[g, hs, :, t:t + 1], k_win, v_win, k_new, v_new)
                               for t in range(n_t)])
            else:
                groups.append(_sample_group_dilated(dil, q_ref[g, hs], k_win, v_win, k_new, v_new))
        merged = jnp.zeros(k_new.shape, F32)
        for t in range(n_t):
            parts = [grp[t] for grp in groups]
            m = functools.reduce(jnp.maximum, [lse for _, lse in parts])
            wts = [jnp.exp(lse - m) for _, lse in parts]
            o = sum(wt * o for wt, (o, _) in zip(wts, parts)) / sum(wts)
            merged = jnp.where(pos_new == t, o, merged)
        o_ref[hs] = merged
        return carry

    lax.fori_loop(0, DIL_HEADS // SAMPLE_HEAD_CHUNK, head_chunk, 0)


def _dil_sample_call(q, cache_k, cache_v, k_new, v_new):
    n_b, n_t = q.shape[0], q.shape[-1]
    wb = cache_k.shape[-1]
    assert wb == DIL_STEPS * DIL_DILATIONS[-1] and DIL_DILATIONS[0] == 1 and n_t <= DIL_DILATIONS[1]
    assert all(wb % dil == 0 and LANES % dil == 0 for dil in DIL_DILATIONS)
    cache = pl.BlockSpec((None, DIL_HEADS, DIL_HD, wb), lambda b: (b, 0, 0, 0))
    new = pl.BlockSpec((None, DIL_HEADS, DIL_HD, n_t), lambda b: (b, 0, 0, 0))
    return pl.pallas_call(
        _dil_sample_kernel,
        grid=(n_b,),
        in_specs=[pl.BlockSpec((None, N_GROUPS, DIL_HEADS, DIL_HD, n_t), lambda b: (b, 0, 0, 0, 0)),
                  cache, cache, new, new],
        out_specs=new,
        out_shape=jax.ShapeDtypeStruct((n_b, DIL_HEADS, DIL_HD, n_t), F32),
        compiler_params=_params("arbitrary"),
        name="dilated_sample",
    )(q, cache_k, cache_v, k_new, v_new)


def _att_merge(o_refs, lse_refs, o_scr, lse_scr):
    lses = [_load_by_token(l_ref, lse_scr) for l_ref in lse_refs]
    m = functools.reduce(jnp.maximum, lses)
    e = [jnp.exp(l - m) for l in lses]
    den = sum(e)
    lane = lax.broadcasted_iota(jnp.int32, (lses[0].shape[0], LANES), 1)
    even_lane = lane < DIL_HD
    merged = None
    for o_ref, eg in zip(o_refs, e):
        wt = eg / den
        wide = jnp.concatenate(
            [jnp.where(even_lane, wt[:, 2 * p:2 * p + 1], wt[:, 2 * p + 1:2 * p + 2]) for p in range(DIL_HEADS // 2)],
            axis=1)
        part = wide * _load_by_token(o_ref, o_scr)
        merged = part if merged is None else merged + part
    return merged


def _att_out_kernel(merged_input, *refs):
    if merged_input:
        o_ref, x_ref, gt_ref, wout_ref, g_ref, b_ref, y_ref = refs
        o = o_ref[...]
    else:
        o_refs, lse_refs = refs[0:N_GROUPS], refs[N_GROUPS:2 * N_GROUPS]
        x_ref, gt_ref, wout_ref, g_ref, b_ref, y_ref, o_scr, lse_scr = refs[2 * N_GROUPS:]
        o = _att_merge(o_refs, lse_refs, o_scr, lse_scr)
    y = _dot(o.astype(BF16), wout_ref[...])
    z = ALPHA * x_ref[...] + (1.0 + gt_ref[...]) * y
    y_ref[...] = _layer_norm(z, g_ref[...], b_ref[...])


def _att_out_call(rows, att, x, layer, w_out, ln_g, ln_b):
    merged_input = rows.per_row
    if merged_input:
        att_specs = [rows.row_spec(D_MODEL)]
        operands, scratch = [att], []
    else:
        outs, lses = att
        att_specs = ([_residue_spec(rows, dil, D_MODEL) for dil in DIL_DILATIONS]
                     + [_residue_spec(rows, dil, LANES) for dil in DIL_DILATIONS])
        operands = list(outs) + list(lses)
        scratch = [_permute_scratch(), pltpu.VMEM((1, ROW_TILE, LANES), F32)]
    return pl.pallas_call(
        functools.partial(_att_out_kernel, merged_input),
        grid=(rows.steps,),
        in_specs=att_specs + [
            rows.row_spec(D_MODEL),
            rows.mod_spec(layer, 5),
            _const_spec((D_MODEL, D_MODEL), (0, 0)),
        ] + _ln_specs(layer, 1),
        out_specs=rows.row_spec(D_MODEL),
        out_shape=jax.ShapeDtypeStruct((rows.n_rows, D_MODEL), F32),
        scratch_shapes=scratch,
        compiler_params=_params("arbitrary"),
        name="att_out",
    )(*operands, x, rows.mods, w_out, ln_g, ln_b)


def _rope_tables(positions):
    half = DIL_HD // 2
    inv = ROPE_THETA ** (-jnp.arange(half, dtype=F32) / half)
    ang = positions.astype(F32)[:, None] * inv[None, :]
    cos, sin = jnp.cos(ang), jnp.sin(ang)
    return jnp.concatenate([cos, cos, cos, cos], axis=1), jnp.concatenate([-sin, sin, -sin, sin], axis=1)


def _trunk(rows, x, n_b, n_t, positions, gla_state, cache, w):
    cos_t, sin_t = _rope_tables(positions)
    if rows.per_row:
        cos_t, sin_t = jnp.tile(cos_t, (n_b, 1)), jnp.tile(sin_t, (n_b, 1))
    ffn = functools.partial(_ffn_call, rows)

    x = ffn(x, 0, 0, 0, w['up1'], w['down1'], w['ln_g'], w['ln_b'])
    q, k, v, la, r = _gla_proj_call(rows, x, 0, w['gla_qkv'], w['gla_g'], w['gla_r'], w['gla_g2'], w['gla_bg'])
    if rows.per_row:
        def cols(a):
            return a.reshape(n_b, n_t, GLA_HEADS, GLA_DK).transpose(0, 2, 3, 1)
        o, s_new = _gla_step_call(cols(q), cols(k), cols(la), v.reshape(n_b, n_t, GLA_V), gla_state)
    else:
        o, s_new = _gla_rec_call(q.reshape(n_b, n_t, GLA_QK), k.reshape(n_b, n_t, GLA_QK),
                                 v.reshape(n_b, n_t, GLA_V), la.reshape(n_b, n_t, GLA_QK), gla_state)
    x = _gla_out_call(rows, o.reshape(n_b * n_t, GLA_V), r, x, 0, w['gla_gon'], w['gla_out'], w['ln_g'], w['ln_b'])
    x = ffn(x, 0, 6, 2, w['up2'], w['down2'], w['ln_g'], w['ln_b'])

    k_rows, v_rows, *kv_groups = _kv_call(rows, x, cos_t, sin_t, w['kv'])
    if rows.per_row:
        k_rows, v_rows = (a.reshape(n_b, n_t, DIL_HEADS, DIL_HD) for a in (k_rows, v_rows))
    else:
        k_rows, v_rows = (a.reshape(n_b, DIL_HEADS, DIL_HD, n_t).transpose(0, 3, 1, 2) for a in (k_rows, v_rows))

    x = ffn(x, 1, 0, 0, w['up1'], w['down1'], w['ln_g'], w['ln_b'])
    qd = _q_call(rows, x, 1, cos_t, sin_t, w['q'])
    if rows.per_row:
        cache_k, cache_v = (a.transpose(0, 2, 3, 1) for a in cache)
        q_t = qd.reshape(n_b, n_t, N_GROUPS, DIL_HEADS, DIL_HD).transpose(0, 2, 3, 4, 1)
        att = _dil_sample_call(q_t, cache_k, cache_v, k_rows.transpose(0, 2, 3, 1), v_rows.transpose(0, 2, 3, 1))
        att = att.transpose(0, 3, 1, 2).reshape(n_b * n_t, D_MODEL)
    else:
        att = tuple(zip(*[_dil_prompt_call(qd[g], kv_groups[g], kv_groups[N_GROUPS + g], g) for g in range(N_GROUPS)]))
    x = _att_out_call(rows, att, x, 1, w['att_out'], w['ln_g'], w['ln_b'])
    x = ffn(x, 1, 6, 2, w['up2'], w['down2'], w['ln_g'], w['ln_b'])
    return x, s_new, k_rows, v_rows


def kernel(x_prompt, x_sample, state_gla, cache_k, cache_v, c_prompt, c_sample, w_ada, b_ada, ln_g, ln_b, w_ffn1_up, w_ffn1_down, w_ffn2_up, w_ffn2_down, w_in_a, w_gate2_a, b_gate_a, g_onorm_a, w_out_a, w_ada_kv, b_ada_kv, w_kv, w_q_b, w_out_b):
    n_bp, n_tp = x_prompt.shape[:2]
    n_bs, n_ts = x_sample.shape[:2]
    assert N_A_LAYERS == 1 and DEPTH == 2
    assert n_tp % (ATT_BLOCK * DIL_DILATIONS[-1]) == 0 and n_tp % GLA_TILE == 0 and n_tp <= MAX_WINDOW
    assert n_ts < 8

    c_all = jnp.concatenate([c_prompt, c_sample], axis=0)
    mods = _ada_call(c_all, w_ada, b_ada, 1536)
    kv_mods = _ada_call(c_all, w_ada_kv[None], b_ada_kv[None], 1024)
    rows_p = _Rows(n_bp * n_tp, n_tp, mods[:, :n_bp, None, :], kv_mods[:, :n_bp, None, :])
    rows_s = _Rows(n_bs * n_ts, n_ts, jnp.repeat(mods[:, n_bp:], n_ts, axis=1),
                   jnp.repeat(kv_mods[:, n_bp:], n_ts, axis=1))

    w_in = w_in_a[0]
    gate_lo = 2 * GLA_QK + GLA_V
    w = {
        'up1': w_ffn1_up.astype(BF16), 'down1': w_ffn1_down.astype(BF16),
        'up2': w_ffn2_up.astype(BF16), 'down2': w_ffn2_down.astype(BF16),
        'ln_g': ln_g[:, :, None, :], 'ln_b': ln_b[:, :, None, :],
        'gla_qkv': w_in[:, :gate_lo].astype(BF16),
        'gla_g': jnp.pad(w_in[:, gate_lo:gate_lo + GLA_RANK], ((0, 0), (0, LANES - GLA_RANK))).astype(BF16),
        'gla_r': w_in[:, gate_lo + GLA_RANK:].astype(BF16),
        'gla_g2': jnp.pad(w_gate2_a[0], ((0, LANES - GLA_RANK), (0, 0))).astype(BF16),
        'gla_bg': b_gate_a[0][None, :],
        'gla_gon': g_onorm_a[0][None, :],
        'gla_out': w_out_a[0].astype(BF16),
        'kv': w_kv.astype(BF16),
        'q': w_q_b[0].astype(BF16),
        'att_out': w_out_b[0].astype(BF16),
    }

    zero_state = jnp.zeros((n_bp, GLA_HEADS, GLA_DK, GLA_DV), F32)
    y_p, s_p, k_p, v_p = _trunk(rows_p, x_prompt.reshape(-1, D_MODEL), n_bp, n_tp,
                                jnp.arange(n_tp), zero_state, None, w)
    y_s, s_s, k_s, v_s = _trunk(rows_s, x_sample.reshape(-1, D_MODEL), n_bs, n_ts,
                                PAST_LEN + jnp.arange(n_ts), state_gla[0], (cache_k, cache_v), w)
    return (y_p.reshape(n_bp, n_tp, D_MODEL), y_s.reshape(n_bs, n_ts, D_MODEL),
            s_p[None], s_s[None], k_p, v_p, k_s, v_s)
```

```python
import functools

import jax
import jax.numpy as jnp
from jax import lax
from jax.experimental import pallas as pl
from jax.experimental.pallas import tpu as pltpu

F32 = jnp.float32
BF16 = jnp.bfloat16

D_MODEL = 1024
DEPTH = 2
PAST_LEN = 16384
N_A_LAYERS = DEPTH // 2
GLA_HEADS = 4
GLA_QK = D_MODEL // 2
GLA_V = D_MODEL
GLA_DK = GLA_QK // GLA_HEADS
GLA_DV = GLA_V // GLA_HEADS
GLA_RANK = 16
GLA_TAU = 16.0
GLA_NORM_EPS = 1e-6
DIL_HEADS = 16
DIL_HD = D_MODEL // DIL_HEADS
DIL_WINDOWS = (128, 512, 2048)
DIL_DILATIONS = (1, 4, 16)
N_GROUPS = len(DIL_WINDOWS)
DIL_STEPS = 128
MAX_WINDOW = max(DIL_WINDOWS)
DIL_SCALE = DIL_HD ** -0.5
ROPE_THETA = 10000.0
D_FF = 2816
FFN_RES = 0.5
ALPHA = (2 * DEPTH) ** 0.25
LN_EPS = 1e-5
N_MOD = 9

LANES = 128
ROW_TILE = 512
FF_CHUNK = 256
GLA_CHUNK = 128
GLA_TILE = 512
GLA_FACTOR_BOUND = 80.0
ATT_BLOCK = DIL_STEPS
VMEM_LIMIT = 56 * 1024 * 1024


def _params(*sem):
    return pltpu.CompilerParams(dimension_semantics=sem, vmem_limit_bytes=VMEM_LIMIT)


def _silu(x):
    return x / (1.0 + jnp.exp(-x))


def _layer_norm(z, g, b):
    mu = jnp.mean(z, axis=-1, keepdims=True)
    zc = z - mu
    var = jnp.mean(zc * zc, axis=-1, keepdims=True)
    return zc * lax.rsqrt(var + LN_EPS) * g + b


def _dot(a, b):
    return jnp.dot(a, b, preferred_element_type=F32)


def _dot_nt(a, b):
    return lax.dot_general(a, b, (((1,), (1,)), ((), ())), preferred_element_type=F32)


def _rope(x, cos_t, sin_t):
    width = x.shape[1]
    lane = lax.broadcasted_iota(jnp.int32, x.shape, 1)
    first_half = (lane % DIL_HD) < (DIL_HD // 2)
    partner = jnp.where(first_half, pltpu.roll(x, width - DIL_HD // 2, 1), pltpu.roll(x, DIL_HD // 2, 1))
    reps = width // LANES
    cos_w = jnp.concatenate([cos_t] * reps, axis=1)
    sin_w = jnp.concatenate([sin_t] * reps, axis=1)
    return x * cos_w + partner * sin_w


def _ada_kernel(c_ref, w_ref, b_ref, o_ref):
    s = _silu(c_ref[...]).astype(BF16)
    o_ref[...] = _dot(s, w_ref[...].astype(BF16)) + b_ref[...]


def _ada_call(c_all, w, b, tn):
    n_l, _, n = w.shape
    rows = c_all.shape[0]
    return pl.pallas_call(
        _ada_kernel,
        grid=(n_l, n // tn),
        in_specs=[
            pl.BlockSpec((rows, D_MODEL), lambda l, j: (0, 0)),
            pl.BlockSpec((None, D_MODEL, tn), lambda l, j: (l, 0, j)),
            pl.BlockSpec((None, 1, tn), lambda l, j: (l, 0, j)),
        ],
        out_specs=pl.BlockSpec((None, rows, tn), lambda l, j: (l, 0, j)),
        out_shape=jax.ShapeDtypeStruct((n_l, rows, n), F32),
        compiler_params=_params("arbitrary", "arbitrary"),
        name="ada_mod",
    )(c_all, w, b.reshape(n_l, 1, n))


class _Rows:
    def __init__(self, n_rows, rows_per_sample, mods, kv_mods):
        self.n_rows = n_rows
        self.rows_per_sample = rows_per_sample
        self.per_row = rows_per_sample < 8
        self.tile = n_rows if self.per_row else min(ROW_TILE, rows_per_sample)
        self.steps = n_rows // self.tile
        self.tiles_per_sample = 1 if self.per_row else rows_per_sample // self.tile
        self.mods = mods
        self.kv_mods = kv_mods

    def row_spec(self, width):
        return pl.BlockSpec((self.tile, width), lambda i: (i, 0))

    def mod_spec(self, layer, k):
        if self.per_row:
            return pl.BlockSpec((None, self.tile, D_MODEL), lambda i: (layer, 0, k))
        tps = self.tiles_per_sample
        return pl.BlockSpec((None, None, 1, D_MODEL), lambda i: (layer, i // tps, 0, k))


def _const_spec(shape, index):
    return pl.BlockSpec(shape, lambda i: index, pipeline_mode=pl.Buffered(1))


def _ln_specs(layer, j):
    return [_const_spec((None, None, 1, D_MODEL), (layer, j, 0, 0))] * 2


def _ffn_kernel(x_ref, sh_ref, sc_ref, gt_ref, wup_ref, wdn_ref, g_ref, b_ref, o_ref):
    x = x_ref[...]
    h = (x * (1.0 + sc_ref[...]) + sh_ref[...]).astype(BF16)
    y = None
    for lo in range(0, D_FF, FF_CHUNK):
        a = _dot(h, wup_ref[:, lo:lo + FF_CHUNK])
        u = _dot(h, wup_ref[:, D_FF + lo:D_FF + lo + FF_CHUNK])
        part = _dot((_silu(a) * u).astype(BF16), wdn_ref[lo:lo + FF_CHUNK, :])
        y = part if y is None else y + part
    z = ALPHA * x + (1.0 + gt_ref[...]) * (FFN_RES * y)
    o_ref[...] = _layer_norm(z, g_ref[...], b_ref[...])


def _ffn_call(rows, x, layer, mod0, ln_j, w_up, w_down, ln_g, ln_b):
    return pl.pallas_call(
        _ffn_kernel,
        grid=(rows.steps,),
        in_specs=[
            rows.row_spec(D_MODEL),
            rows.mod_spec(layer, mod0), rows.mod_spec(layer, mod0 + 1), rows.mod_spec(layer, mod0 + 2),
            _const_spec((None, D_MODEL, 2 * D_FF), (layer, 0, 0)),
            _const_spec((None, D_FF, D_MODEL), (layer, 0, 0)),
        ] + _ln_specs(layer, ln_j),
        out_specs=rows.row_spec(D_MODEL),
        out_shape=jax.ShapeDtypeStruct((rows.n_rows, D_MODEL), F32),
        compiler_params=_params("arbitrary"),
        name="ffn",
    )(x, rows.mods, rows.mods, rows.mods, w_up, w_down, ln_g, ln_b)


def _gla_proj_kernel(x_ref, sh_ref, sc_ref, wqkv_ref, wg_ref, wr_ref, wg2_ref, bg_ref,
                     q_ref, k_ref, v_ref, la_ref, r_ref):
    h = (x_ref[...] * (1.0 + sc_ref[...]) + sh_ref[...]).astype(BF16)
    qkv = _dot(h, wqkv_ref[...])
    q_ref[...] = qkv[:, :GLA_QK] * (GLA_DK ** -0.5)
    k_ref[...] = qkv[:, GLA_QK:2 * GLA_QK]
    v_ref[...] = qkv[:, 2 * GLA_QK:]
    g_lr = _dot(h, wg_ref[...]).astype(BF16)
    gate = _dot(g_lr, wg2_ref[...]) + bg_ref[...]
    log_sig = jnp.minimum(gate, 0.0) - jnp.log(1.0 + jnp.exp(-jnp.abs(gate)))
    la_ref[...] = log_sig / GLA_TAU
    r_ref[...] = _dot(h, wr_ref[...])


def _gla_proj_call(rows, x, layer, w_qkv, w_g, w_r, w_g2, b_g):
    widths = (GLA_QK, GLA_QK, GLA_V, GLA_QK, GLA_V)
    return pl.pallas_call(
        _gla_proj_kernel,
        grid=(rows.steps,),
        in_specs=[
            rows.row_spec(D_MODEL),
            rows.mod_spec(layer, 3), rows.mod_spec(layer, 4),
            _const_spec((D_MODEL, 2 * GLA_QK + GLA_V), (0, 0)),
            _const_spec((D_MODEL, LANES), (0, 0)),
            _const_spec((D_MODEL, GLA_V), (0, 0)),
            _const_spec((LANES, GLA_QK), (0, 0)),
            _const_spec((1, GLA_QK), (0, 0)),
        ],
        out_specs=[rows.row_spec(w) for w in widths],
        out_shape=[jax.ShapeDtypeStruct((rows.n_rows, w), F32) for w in widths],
        compiler_params=_params("arbitrary"),
        name="gla_proj",
    )(x, rows.mods, rows.mods, w_qkv, w_g, w_r, w_g2, b_g)


def _gla_scores_factored(qh, kh, bh, b_last):
    mid = 0.5 * b_last
    q_t = (qh * jnp.exp(bh - mid)).astype(BF16)
    k_t = (kh * jnp.exp(mid - bh)).astype(BF16)
    return _dot_nt(q_t, k_t)


def _gla_scores_elementwise(qh, kh, bh, k_scr, b_scr):
    k_scr[...] = kh
    b_scr[...] = bh
    col = lax.broadcasted_iota(jnp.int32, (GLA_CHUNK, GLA_CHUNK), 1)

    def body(j, acc):
        kj = k_scr[pl.ds(j, 1), :]
        bj = b_scr[pl.ds(j, 1), :]
        term = qh * kj * jnp.exp(jnp.minimum(bh - bj, 0.0))
        return acc + jnp.where(col == j, jnp.sum(term, axis=1, keepdims=True), 0.0)

    return lax.fori_loop(0, GLA_CHUNK, body, jnp.zeros((GLA_CHUNK, GLA_CHUNK), F32))


def _gla_chunk(rows, b, states, score_fn, q_ref, k_ref, v_ref, o_ref, causal):
    q, k = q_ref[rows, :], k_ref[rows, :]
    b_last = b[GLA_CHUNK - 1:GLA_CHUNK, :]
    q_in = q * jnp.exp(b)
    k_out = k * jnp.exp(b_last - b)
    new_states = []
    for hd, s_prev in enumerate(states):
        ks = slice(hd * GLA_DK, (hd + 1) * GLA_DK)
        vs = slice(hd * GLA_DV, (hd + 1) * GLA_DV)
        vh = v_ref[rows, vs].astype(BF16)
        o_inter = _dot(q_in[:, ks].astype(BF16), s_prev.astype(BF16))
        scores = jnp.where(causal, score_fn(q[:, ks], k[:, ks], b[:, ks], b_last[:, ks]), 0.0)
        o_ref[rows, vs] = o_inter + _dot(scores.astype(BF16), vh)
        decay_col = jnp.transpose(jnp.broadcast_to(jnp.exp(b_last[:, ks]), (GLA_DK, GLA_DK)))[:, :1]
        k_out_t = jnp.transpose(k_out[:, ks]).astype(BF16)
        new_states.append(decay_col * s_prev + _dot(k_out_t, vh))
    return new_states


def _gla_rec_kernel(q_ref, k_ref, v_ref, la_ref, s0_ref, o_ref, sfin_ref, s_scr, k_scr, b_scr):
    t = pl.program_id(1)

    @pl.when(t == 0)
    def _():
        s_scr[...] = s0_ref[...]

    row = lax.broadcasted_iota(jnp.int32, (GLA_CHUNK, GLA_CHUNK), 0)
    col = lax.broadcasted_iota(jnp.int32, (GLA_CHUNK, GLA_CHUNK), 1)
    causal = col <= row
    tri = causal.astype(BF16)
    n_chunks = GLA_TILE // GLA_CHUNK

    def decay_prefix(rows):
        rest = la_ref[rows, :]
        total = None
        for _ in range(3):
            piece = rest.astype(BF16)
            rest = rest - piece.astype(F32)
            part = _dot(tri, piece)
            total = part if total is None else total + part
        return total

    chunk_rows = [slice(c * GLA_CHUNK, (c + 1) * GLA_CHUNK) for c in range(n_chunks)]
    prefixes = [decay_prefix(rows) for rows in chunk_rows]
    total_decay = functools.reduce(jnp.maximum, [-b[GLA_CHUNK - 1:GLA_CHUNK, :] for b in prefixes])
    factorable = jnp.max(total_decay) <= GLA_FACTOR_BOUND

    def refs():
        return q_ref, k_ref, v_ref, o_ref, causal

    @pl.when(factorable)
    def _():
        states = [s_scr[hd] for hd in range(GLA_HEADS)]
        for rows, b in zip(chunk_rows, prefixes):
            states = _gla_chunk(rows, b, states, _gla_scores_factored, *refs())
        for hd in range(GLA_HEADS):
            s_scr[hd] = states[hd]

    @pl.when(jnp.logical_not(factorable))
    def _():
        def chunk(c, carry):
            rows = pl.ds(pl.multiple_of(c * GLA_CHUNK, GLA_CHUNK), GLA_CHUNK)
            states = [s_scr[hd] for hd in range(GLA_HEADS)]
            elementwise = functools.partial(_gla_scores_elementwise, k_scr=k_scr, b_scr=b_scr)
            states = _gla_chunk(rows, decay_prefix(rows), states,
                                lambda qh, kh, bh, _: elementwise(qh, kh, bh), *refs())
            for hd in range(GLA_HEADS):
                s_scr[hd] = states[hd]
            return carry

        lax.fori_loop(0, n_chunks, chunk, 0)

    @pl.when(t == pl.num_programs(1) - 1)
    def _():
        sfin_ref[...] = s_scr[...]


def _gla_rec_call(q, k, v, la, s0):
    n_b, n_t = q.shape[:2]

    def seq(width):
        return pl.BlockSpec((None, GLA_TILE, width), lambda b, t: (b, t, 0))

    state = pl.BlockSpec((None, GLA_HEADS, GLA_DK, GLA_DV), lambda b, t: (b, 0, 0, 0))
    return pl.pallas_call(
        _gla_rec_kernel,
        grid=(n_b, n_t // GLA_TILE),
        in_specs=[seq(GLA_QK), seq(GLA_QK), seq(GLA_V), seq(GLA_QK), state],
        out_specs=[seq(GLA_V), state],
        out_shape=[jax.ShapeDtypeStruct((n_b, n_t, GLA_V), F32),
                   jax.ShapeDtypeStruct((n_b, GLA_HEADS, GLA_DK, GLA_DV), F32)],
        scratch_shapes=[pltpu.VMEM((GLA_HEADS, GLA_DK, GLA_DV), F32),
                        pltpu.VMEM((GLA_CHUNK, GLA_DK), F32),
                        pltpu.VMEM((GLA_CHUNK, GLA_DK), F32)],
        compiler_params=_params("arbitrary", "arbitrary"),
        name="gla_recurrence",
    )(q, k, v, la, s0)


def _gla_step_kernel(qt_ref, kt_ref, lat_ref, v_ref, s0_ref, o_ref, sfin_ref):
    n_t = v_ref.shape[0]
    for hd in range(GLA_HEADS):
        vs = slice(hd * GLA_DV, (hd + 1) * GLA_DV)
        s = s0_ref[hd]
        qt, kt, lat = qt_ref[hd], kt_ref[hd], lat_ref[hd]
        for t in range(n_t):
            s = jnp.exp(lat[:, t:t + 1]) * s + kt[:, t:t + 1] * v_ref[t:t + 1, vs]
            o_ref[t:t + 1, vs] = jnp.sum(qt[:, t:t + 1] * s, axis=0, keepdims=True)
        sfin_ref[hd] = s


def _gla_step_call(qt, kt, lat, v, s0):
    n_b, _, _, n_t = qt.shape
    col = pl.BlockSpec((None, GLA_HEADS, GLA_DK, n_t), lambda b: (b, 0, 0, 0))
    tok = pl.BlockSpec((None, n_t, GLA_V), lambda b: (b, 0, 0))
    state = pl.BlockSpec((None, GLA_HEADS, GLA_DK, GLA_DV), lambda b: (b, 0, 0, 0))
    return pl.pallas_call(
        _gla_step_kernel,
        grid=(n_b,),
        in_specs=[col, col, col, tok, state],
        out_specs=[tok, state],
        out_shape=[jax.ShapeDtypeStruct((n_b, n_t, GLA_V), F32),
                   jax.ShapeDtypeStruct((n_b, GLA_HEADS, GLA_DK, GLA_DV), F32)],
        compiler_params=_params("arbitrary"),
        name="gla_step",
    )(qt, kt, lat, v, s0)


def _gla_out_kernel(o_ref, r_ref, x_ref, gt_ref, gon_ref, wout_ref, g_ref, b_ref, y_ref):
    o = o_ref[...]
    parts = []
    for hd in range(GLA_HEADS):
        oh = o[:, hd * GLA_DV:(hd + 1) * GLA_DV]
        ms = jnp.mean(oh * oh, axis=-1, keepdims=True)
        parts.append(oh * lax.rsqrt(ms + GLA_NORM_EPS) * gon_ref[...])
    gated = (jnp.concatenate(parts, axis=1) * _silu(r_ref[...])).astype(BF16)
    y = _dot(gated, wout_ref[...])
    z = ALPHA * x_ref[...] + (1.0 + gt_ref[...]) * y
    y_ref[...] = _layer_norm(z, g_ref[...], b_ref[...])


def _gla_out_call(rows, o, r, x, layer, g_onorm, w_out, ln_g, ln_b):
    return pl.pallas_call(
        _gla_out_kernel,
        grid=(rows.steps,),
        in_specs=[
            rows.row_spec(GLA_V), rows.row_spec(GLA_V), rows.row_spec(D_MODEL),
            rows.mod_spec(layer, 5),
            _const_spec((1, GLA_DV), (0, 0)),
            _const_spec((GLA_V, D_MODEL), (0, 0)),
        ] + _ln_specs(layer, 1),
        out_specs=rows.row_spec(D_MODEL),
        out_shape=jax.ShapeDtypeStruct((rows.n_rows, D_MODEL), F32),
        compiler_params=_params("arbitrary"),
        name="gla_out",
    )(o, r, x, rows.mods, g_onorm, w_out, ln_g, ln_b)


def _store_by_residue(x, out_refs, scr):
    n_rows, width = x.shape
    n_col = width // LANES
    if any(o.shape[0] > 1 for o in out_refs):
        for c in range(n_col):
            scr[c] = x[:, c * LANES:(c + 1) * LANES]
    for o_ref in out_refs:
        dil = o_ref.shape[0]
        if dil == 1:
            o_ref[0] = x.astype(BF16)
            continue
        for r in range(dil):
            o_ref[r] = jnp.concatenate(
                [scr[c, pl.ds(r, n_rows // dil, stride=dil), :] for c in range(n_col)], axis=1).astype(BF16)


def _load_by_token(x_ref, scr):
    dil, n_per, width = x_ref.shape
    if dil == 1:
        return x_ref[0].astype(F32)
    n_col = width // LANES
    for r in range(dil):
        xr = x_ref[r].astype(F32)
        for c in range(n_col):
            scr[c, pl.ds(r, n_per, stride=dil), :] = xr[:, c * LANES:(c + 1) * LANES]
    if n_col == 1:
        return scr[0]
    return jnp.concatenate([scr[c] for c in range(n_col)], axis=1)


def _kv_kernel(by_residue, x_ref, sh_ref, sc_ref, cos_ref, sin_ref, w_ref, k_ref, v_ref, *rest):
    h = (x_ref[...] * (1.0 + sc_ref[...]) + sh_ref[...]).astype(BF16)
    kv = _dot(h, w_ref[...])
    k = _rope(kv[:, :D_MODEL], cos_ref[...], sin_ref[...])
    v = kv[:, D_MODEL:]
    if by_residue:
        k_ref[...] = jnp.transpose(k)
        v_ref[...] = jnp.transpose(v)
    else:
        k_ref[...] = k.reshape(k_ref.shape)
        v_ref[...] = v.reshape(v_ref.shape)
    if by_residue:
        scr = rest[-1]
        _store_by_residue(k, rest[0:N_GROUPS], scr)
        _store_by_residue(v, rest[N_GROUPS:2 * N_GROUPS], scr)


def _rope_spec(rows):
    if rows.per_row:
        return pl.BlockSpec((rows.tile, LANES), lambda i: (0, 0))
    tps = rows.tiles_per_sample
    return pl.BlockSpec((rows.tile, LANES), lambda i: (i % tps, 0))


def _kv_mod_spec(rows, k):
    if rows.per_row:
        return pl.BlockSpec((None, rows.tile, D_MODEL), lambda i: (0, 0, k))
    tps = rows.tiles_per_sample
    return pl.BlockSpec((None, None, 1, D_MODEL), lambda i: (0, i // tps, 0, k))


def _residue_spec(rows, dil, width):
    tps = rows.tiles_per_sample
    return pl.BlockSpec((None, dil, rows.tile // dil, width), lambda i: (i // tps, 0, i % tps, 0))


def _residue_shapes(rows, width, dtype):
    n_b = rows.n_rows // rows.rows_per_sample
    return [jax.ShapeDtypeStruct((n_b, dil, rows.rows_per_sample // dil, width), dtype) for dil in DIL_DILATIONS]


def _permute_scratch():
    return pltpu.VMEM((D_MODEL // LANES, ROW_TILE, LANES), F32)


def _kv_call(rows, x, cos_t, sin_t, w_kv):
    by_residue = not rows.per_row
    scratch = []
    if by_residue:
        tps = rows.tiles_per_sample
        n_b = rows.n_rows // rows.rows_per_sample
        out_specs = [pl.BlockSpec((None, D_MODEL, rows.tile), lambda i: (i // tps, 0, i % tps))] * 2
        out_shape = [jax.ShapeDtypeStruct((n_b, D_MODEL, rows.rows_per_sample), F32)] * 2
    else:
        out_specs = [pl.BlockSpec((rows.tile, DIL_HEADS, DIL_HD), lambda i: (i, 0, 0))] * 2
        out_shape = [jax.ShapeDtypeStruct((rows.n_rows, DIL_HEADS, DIL_HD), F32)] * 2
    if by_residue:
        out_specs += [_residue_spec(rows, dil, D_MODEL) for dil in DIL_DILATIONS] * 2
        out_shape += _residue_shapes(rows, D_MODEL, BF16) * 2
        scratch = [_permute_scratch()]
    return pl.pallas_call(
        functools.partial(_kv_kernel, by_residue),
        grid=(rows.steps,),
        in_specs=[
            rows.row_spec(D_MODEL),
            _kv_mod_spec(rows, 0), _kv_mod_spec(rows, 1),
            _rope_spec(rows), _rope_spec(rows),
            _const_spec((D_MODEL, 2 * D_MODEL), (0, 0)),
        ],
        out_specs=out_specs,
        out_shape=out_shape,
        scratch_shapes=scratch,
        compiler_params=_params("arbitrary"),
        name="kv_proj",
    )(x, rows.kv_mods, rows.kv_mods, cos_t, sin_t, w_kv)


def _q_kernel(by_residue, x_ref, sh_ref, sc_ref, cos_ref, sin_ref, w_ref, *rest):
    h = (x_ref[...] * (1.0 + sc_ref[...]) + sh_ref[...]).astype(BF16)
    q = _rope(_dot(h, w_ref[...]), cos_ref[...], sin_ref[...]) * DIL_SCALE
    if by_residue:
        for g in range(N_GROUPS):
            _store_by_residue(q[:, g * D_MODEL:(g + 1) * D_MODEL], rest[g:g + 1], rest[-1])
    else:
        rest[0][...] = q.reshape(rest[0].shape)


def _q_call(rows, x, layer, cos_t, sin_t, w_q):
    width = N_GROUPS * D_MODEL
    by_residue = not rows.per_row
    if by_residue:
        out_specs = [_residue_spec(rows, dil, D_MODEL) for dil in DIL_DILATIONS]
        out_shape = _residue_shapes(rows, D_MODEL, BF16)
        scratch = [_permute_scratch()]
    else:
        out_specs = pl.BlockSpec((rows.tile, N_GROUPS * DIL_HEADS, DIL_HD), lambda i: (i, 0, 0))
        out_shape = jax.ShapeDtypeStruct((rows.n_rows, N_GROUPS * DIL_HEADS, DIL_HD), F32)
        scratch = []
    return pl.pallas_call(
        functools.partial(_q_kernel, by_residue),
        grid=(rows.steps,),
        in_specs=[
            rows.row_spec(D_MODEL),
            rows.mod_spec(layer, 3), rows.mod_spec(layer, 4),
            _rope_spec(rows), _rope_spec(rows),
            _const_spec((D_MODEL, width), (0, 0)),
        ],
        out_specs=out_specs,
        out_shape=out_shape,
        scratch_shapes=scratch,
        compiler_params=_params("arbitrary"),
        name="q_proj",
    )(x, rows.mods, rows.mods, cos_t, sin_t, w_q)


def _pair_softmax_pv(scores, v_pair, even_lane):
    half = scores.shape[0] // 2
    m = jnp.max(scores, axis=-1, keepdims=True)
    e = jnp.exp(scores - m)
    den = jnp.sum(e, axis=-1, keepdims=True)
    o2 = _dot(e.astype(BF16), v_pair) / den
    lse2 = m + jnp.log(den)
    o = jnp.where(even_lane, o2[:half], o2[half:])
    return o, lse2[:half], lse2[half:]


def _dil_prompt_kernel(has_prev, q_ref, *refs):
    if has_prev:
        kp_ref, kc_ref, vp_ref, vc_ref, o_ref, lse_ref = refs
    else:
        kc_ref, vc_ref, o_ref, lse_ref = refs
    blk = pl.program_id(2)
    n_keys = (2 if has_prev else 1) * ATT_BLOCK
    lane = lax.broadcasted_iota(jnp.int32, (ATT_BLOCK, LANES), 1)
    even_lane = lane < DIL_HD
    qi = lax.broadcasted_iota(jnp.int32, (2 * ATT_BLOCK, n_keys), 0) % ATT_BLOCK
    kj = lax.broadcasted_iota(jnp.int32, (2 * ATT_BLOCK, n_keys), 1)
    if has_prev:
        dist = qi + ATT_BLOCK - kj
        valid = (dist >= 0) & (dist <= DIL_STEPS) & ((kj >= ATT_BLOCK) | (blk > 0))
    else:
        valid = kj <= qi
    lse_all = jnp.zeros((ATT_BLOCK, LANES), F32)
    for p in range(DIL_HEADS // 2):
        sl = slice(p * LANES, (p + 1) * LANES)
        qp = q_ref[:, sl]
        zero = jnp.zeros_like(qp)
        q2 = jnp.concatenate([jnp.where(even_lane, qp, zero), jnp.where(even_lane, zero, qp)], axis=0)
        if has_prev:
            k_pair = jnp.concatenate([kp_ref[:, sl], kc_ref[:, sl]], axis=0)
            v_pair = jnp.concatenate([vp_ref[:, sl], vc_ref[:, sl]], axis=0)
        else:
            k_pair, v_pair = kc_ref[:, sl], vc_ref[:, sl]
        scores = jnp.where(valid, _dot_nt(q2, k_pair), -jnp.inf)
        o, lse_even, lse_odd = _pair_softmax_pv(scores, v_pair, even_lane)
        o_ref[:, sl] = o.astype(BF16)
        lse_all = jnp.where(lane == 2 * p, lse_even, jnp.where(lane == 2 * p + 1, lse_odd, lse_all))
    lse_ref[...] = lse_all


def _dil_prompt_call(q, kb, vb, group):
    n_b, dil, length, _ = kb.shape
    n_blk = length // ATT_BLOCK
    has_prev = n_blk > 1

    def blk(width, index_map):
        return pl.BlockSpec((None, None, ATT_BLOCK, width), index_map)

    cur = blk(D_MODEL, lambda b, r, i: (b, r, i, 0))
    prev = blk(D_MODEL, lambda b, r, i: (b, r, jnp.maximum(i - 1, 0), 0))
    if has_prev:
        in_specs, operands = [cur, prev, cur, prev, cur], (q, kb, kb, vb, vb)
    else:
        in_specs, operands = [cur, cur, cur], (q, kb, vb)
    return pl.pallas_call(
        functools.partial(_dil_prompt_kernel, has_prev),
        grid=(n_b, dil, n_blk),
        in_specs=in_specs,
        out_specs=[cur, blk(LANES, lambda b, r, i: (b, r, i, 0))],
        out_shape=[jax.ShapeDtypeStruct((n_b, dil, length, D_MODEL), BF16),
                   jax.ShapeDtypeStruct((n_b, dil, length, LANES), F32)],
        compiler_params=_params("arbitrary", "arbitrary", "arbitrary"),
        name="dilated_prompt",
    )(*operands)


SAMPLE_HEAD_CHUNK = 8


def _sample_group_dense(t, q_col, k_win, v_win, k_new, v_new):
    n_t = k_new.shape[-1]
    pos = lax.broadcasted_iota(jnp.int32, (1, 1, k_win.shape[-1]), 2)
    pos_new = lax.broadcasted_iota(jnp.int32, (1, 1, n_t), 2)
    s_c = jnp.where(pos >= t, jnp.sum(k_win * q_col, axis=1, keepdims=True), -jnp.inf)
    s_n = jnp.where(pos_new <= t, jnp.sum(k_new * q_col, axis=1, keepdims=True), -jnp.inf)
    m = jnp.maximum(jnp.max(s_c, axis=2, keepdims=True), jnp.max(s_n, axis=2, keepdims=True))
    e_c, e_n = jnp.exp(s_c - m), jnp.exp(s_n - m)
    den = jnp.sum(e_c, axis=2, keepdims=True) + jnp.sum(e_n, axis=2, keepdims=True)
    o = (jnp.sum(v_win * e_c, axis=2, keepdims=True) + jnp.sum(v_new * e_n, axis=2, keepdims=True)) / den
    return o, m + jnp.log(den)


def _sample_group_dilated(dil, q_g, k_win, v_win, k_new, v_new):
    n_t = q_g.shape[-1]
    width = k_win.shape[-1]
    n_col = width // LANES
    res = lax.broadcasted_iota(jnp.int32, (1, 1, LANES), 2) % dil
    res_w = lax.broadcasted_iota(jnp.int32, (1, 1, width), 2) % dil
    q_pat = jnp.zeros(q_g.shape[:2] + (LANES,), F32)
    for t in range(n_t):
        q_pat = jnp.where(res == t, q_g[:, :, t:t + 1], q_pat)
    s_row = jnp.sum(k_win * jnp.concatenate([q_pat] * n_col, axis=2), axis=1, keepdims=True)
    s_new = jnp.sum(k_new * q_g, axis=1, keepdims=True)
    m_t = [jnp.maximum(jnp.max(jnp.where(res_w == t, s_row, -jnp.inf), axis=2, keepdims=True), s_new[:, :, t:t + 1])
           for t in range(n_t)]
    m_row = jnp.zeros_like(s_row)
    for t in range(n_t):
        m_row = jnp.where(res_w == t, m_t[t], m_row)
    e = jnp.exp(jnp.where(res_w < n_t, s_row, -jnp.inf) - m_row)
    acc = v_win * e
    acc_fold = sum(acc[:, :, c * LANES:(c + 1) * LANES] for c in range(n_col))
    e_fold = sum(e[:, :, c * LANES:(c + 1) * LANES] for c in range(n_col))
    out = []
    for t in range(n_t):
        e_new = jnp.exp(s_new[:, :, t:t + 1] - m_t[t])
        den = jnp.sum(jnp.where(res == t, e_fold, 0.0), axis=2, keepdims=True) + e_new
        o = (jnp.sum(jnp.where(res == t, acc_fold, 0.0), axis=2, keepdims=True) + e_new * v_new[:, :, t:t + 1]) / den
        out.append((o, m_t[t] + jnp.log(den)))
    return out


def _dil_sample_kernel(q_ref, k_ref, v_ref, kn_ref, vn_ref, o_ref):
    n_t = q_ref.shape[-1]
    n_s = k_ref.shape[-1]
    pos_new = lax.broadcasted_iota(jnp.int32, (1, 1, n_t), 2)

    def head_chunk(i, carry):
        hs = pl.ds(pl.multiple_of(i * SAMPLE_HEAD_CHUNK, SAMPLE_HEAD_CHUNK), SAMPLE_HEAD_CHUNK)
        k_new, v_new = kn_ref[hs], vn_ref[hs]
        groups = []
        for g, dil in enumerate(DIL_DILATIONS):
            lo = n_s - DIL_STEPS * dil
            k_win, v_win = k_ref[hs, :, lo:], v_ref[hs, :, lo:]
            if dil == 1:
                groups.append([_sample_group_dense(t, q_ref[g, hs, :, t:t + 1], k_win, v_win, k_new, v_new)
                               for t in range(n_t)])
            else:
                groups.append(_sample_group_dilated(dil, q_ref[g, hs], k_win, v_win, k_new, v_new))
        merged = jnp.zeros(k_new.shape, F32)
        for t in range(n_t):
            parts = [grp[t] for grp in groups]
            m = functools.reduce(jnp.maximum, [lse for _, lse in parts])
            wts = [jnp.exp(lse - m) for _, lse in parts]
            o = sum(wt * o for wt, (o, _) in zip(wts, parts)) / sum(wts)
            merged = jnp.where(pos_new == t, o, merged)
        o_ref[hs] = merged
        return carry

    lax.fori_loop(0, DIL_HEADS // SAMPLE_HEAD_CHUNK, head_chunk, 0)


def _dil_sample_call(q, cache_k, cache_v, k_new, v_new):
    n_b, n_t = q.shape[0], q.shape[-1]
    wb = cache_k.shape[-1]
    assert wb == DIL_STEPS * DIL_DILATIONS[-1] and DIL_DILATIONS[0] == 1 and n_t <= DIL_DILATIONS[1]
    assert all(wb % dil == 0 and LANES % dil == 0 for dil in DIL_DILATIONS)
    cache = pl.BlockSpec((None, DIL_HEADS, DIL_HD, wb), lambda b: (b, 0, 0, 0))
    new = pl.BlockSpec((None, DIL_HEADS, DIL_HD, n_t), lambda b: (b, 0, 0, 0))
    return pl.pallas_call(
        _dil_sample_kernel,
        grid=(n_b,),
        in_specs=[pl.BlockSpec((None, N_GROUPS, DIL_HEADS, DIL_HD, n_t), lambda b: (b, 0, 0, 0, 0)),
                  cache, cache, new, new],
        out_specs=new,
        out_shape=jax.ShapeDtypeStruct((n_b, DIL_HEADS, DIL_HD, n_t), F32),
        compiler_params=_params("arbitrary"),
        name="dilated_sample",
    )(q, cache_k, cache_v, k_new, v_new)


def _att_merge(o_refs, lse_refs, o_scr, lse_scr):
    lses = [_load_by_token(l_ref, lse_scr) for l_ref in lse_refs]
    m = functools.reduce(jnp.maximum, lses)
    e = [jnp.exp(l - m) for l in lses]
    den = sum(e)
    lane = lax.broadcasted_iota(jnp.int32, (lses[0].shape[0], LANES), 1)
    even_lane = lane < DIL_HD
    merged = None
    for o_ref, eg in zip(o_refs, e):
        wt = eg / den
        wide = jnp.concatenate(
            [jnp.where(even_lane, wt[:, 2 * p:2 * p + 1], wt[:, 2 * p + 1:2 * p + 2]) for p in range(DIL_HEADS // 2)],
            axis=1)
        part = wide * _load_by_token(o_ref, o_scr)
        merged = part if merged is None else merged + part
    return merged


def _att_out_kernel(merged_input, *refs):
    if merged_input:
        o_ref, x_ref, gt_ref, wout_ref, g_ref, b_ref, y_ref = refs
        o = o_ref[...]
    else:
        o_refs, lse_refs = refs[0:N_GROUPS], refs[N_GROUPS:2 * N_GROUPS]
        x_ref, gt_ref, wout_ref, g_ref, b_ref, y_ref, o_scr, lse_scr = refs[2 * N_GROUPS:]
        o = _att_merge(o_refs, lse_refs, o_scr, lse_scr)
    y = _dot(o.astype(BF16), wout_ref[...])
    z = ALPHA * x_ref[...] + (1.0 + gt_ref[...]) * y
    y_ref[...] = _layer_norm(z, g_ref[...], b_ref[...])


def _att_out_call(rows, att, x, layer, w_out, ln_g, ln_b):
    merged_input = rows.per_row
    if merged_input:
        att_specs = [rows.row_spec(D_MODEL)]
        operands, scratch = [att], []
    else:
        outs, lses = att
        att_specs = ([_residue_spec(rows, dil, D_MODEL) for dil in DIL_DILATIONS]
                     + [_residue_spec(rows, dil, LANES) for dil in DIL_DILATIONS])
        operands = list(outs) + list(lses)
        scratch = [_permute_scratch(), pltpu.VMEM((1, ROW_TILE, LANES), F32)]
    return pl.pallas_call(
        functools.partial(_att_out_kernel, merged_input),
        grid=(rows.steps,),
        in_specs=att_specs + [
            rows.row_spec(D_MODEL),
            rows.mod_spec(layer, 5),
            _const_spec((D_MODEL, D_MODEL), (0, 0)),
        ] + _ln_specs(layer, 1),
        out_specs=rows.row_spec(D_MODEL),
        out_shape=jax.ShapeDtypeStruct((rows.n_rows, D_MODEL), F32),
        scratch_shapes=scratch,
        compiler_params=_params("arbitrary"),
        name="att_out",
    )(*operands, x, rows.mods, w_out, ln_g, ln_b)


def _rope_tables(positions):
    half = DIL_HD // 2
    inv = ROPE_THETA ** (-jnp.arange(half, dtype=F32) / half)
    ang = positions.astype(F32)[:, None] * inv[None, :]
    cos, sin = jnp.cos(ang), jnp.sin(ang)
    return jnp.concatenate([cos, cos, cos, cos], axis=1), jnp.concatenate([-sin, sin, -sin, sin], axis=1)


def _trunk(rows, x, n_b, n_t, positions, gla_state, cache, w):
    cos_t, sin_t = _rope_tables(positions)
    if rows.per_row:
        cos_t, sin_t = jnp.tile(cos_t, (n_b, 1)), jnp.tile(sin_t, (n_b, 1))
    ffn = functools.partial(_ffn_call, rows)

    x = ffn(x, 0, 0, 0, w['up1'], w['down1'], w['ln_g'], w['ln_b'])
    q, k, v, la, r = _gla_proj_call(rows, x, 0, w['gla_qkv'], w['gla_g'], w['gla_r'], w['gla_g2'], w['gla_bg'])
    if rows.per_row:
        def cols(a):
            return a.reshape(n_b, n_t, GLA_HEADS, GLA_DK).transpose(0, 2, 3, 1)
        o, s_new = _gla_step_call(cols(q), cols(k), cols(la), v.reshape(n_b, n_t, GLA_V), gla_state)
    else:
        o, s_new = _gla_rec_call(q.reshape(n_b, n_t, GLA_QK), k.reshape(n_b, n_t, GLA_QK),
                                 v.reshape(n_b, n_t, GLA_V), la.reshape(n_b, n_t, GLA_QK), gla_state)
    x = _gla_out_call(rows, o.reshape(n_b * n_t, GLA_V), r, x, 0, w['gla_gon'], w['gla_out'], w['ln_g'], w['ln_b'])
    x = ffn(x, 0, 6, 2, w['up2'], w['down2'], w['ln_g'], w['ln_b'])

    k_rows, v_rows, *kv_groups = _kv_call(rows, x, cos_t, sin_t, w['kv'])
    if rows.per_row:
        k_rows, v_rows = (a.reshape(n_b, n_t, DIL_HEADS, DIL_HD) for a in (k_rows, v_rows))
    else:
        k_rows, v_rows = (a.reshape(n_b, DIL_HEADS, DIL_HD, n_t).transpose(0, 3, 1, 2) for a in (k_rows, v_rows))

    x = ffn(x, 1, 0, 0, w['up1'], w['down1'], w['ln_g'], w['ln_b'])
    qd = _q_call(rows, x, 1, cos_t, sin_t, w['q'])
    if rows.per_row:
        cache_k, cache_v = (a.transpose(0, 2, 3, 1) for a in cache)
        q_t = qd.reshape(n_b, n_t, N_GROUPS, DIL_HEADS, DIL_HD).transpose(0, 2, 3, 4, 1)
        att = _dil_sample_call(q_t, cache_k, cache_v, k_rows.transpose(0, 2, 3, 1), v_rows.transpose(0, 2, 3, 1))
        att = att.transpose(0, 3, 1, 2).reshape(n_b * n_t, D_MODEL)
    else:
        att = tuple(zip(*[_dil_prompt_call(qd[g], kv_groups[g], kv_groups[N_GROUPS + g], g) for g in range(N_GROUPS)]))
    x = _att_out_call(rows, att, x, 1, w['att_out'], w['ln_g'], w['ln_b'])
    x = ffn(x, 1, 6, 2, w['up2'], w['down2'], w['ln_g'], w['ln_b'])
    return x, s_new, k_rows, v_rows


def kernel(x_prompt, x_sample, state_gla, cache_k, cache_v, c_prompt, c_sample, w_ada, b_ada, ln_g, ln_b, w_ffn1_up, w_ffn1_down, w_ffn2_up, w_ffn2_down, w_in_a, w_gate2_a, b_gate_a, g_onorm_a, w_out_a, w_ada_kv, b_ada_kv, w_kv, w_q_b, w_out_b):
    n_bp, n_tp = x_prompt.shape[:2]
    n_bs, n_ts = x_sample.shape[:2]
    assert N_A_LAYERS == 1 and DEPTH == 2
    assert n_tp % (ATT_BLOCK * DIL_DILATIONS[-1]) == 0 and n_tp % GLA_TILE == 0 and n_tp <= MAX_WINDOW
    assert n_ts < 8

    c_all = jnp.concatenate([c_prompt, c_sample], axis=0)
    mods = _ada_call(c_all, w_ada, b_ada, 1536)
    kv_mods = _ada_call(c_all, w_ada_kv[None], b_ada_kv[None], 1024)
    rows_p = _Rows(n_bp * n_tp, n_tp, mods[:, :n_bp, None, :], kv_mods[:, :n_bp, None, :])
    rows_s = _Rows(n_bs * n_ts, n_ts, jnp.repeat(mods[:, n_bp:], n_ts, axis=1),
                   jnp.repeat(kv_mods[:, n_bp:], n_ts, axis=1))

    w_in = w_in_a[0]
    gate_lo = 2 * GLA_QK + GLA_V
    w = {
        'up1': w_ffn1_up.astype(BF16), 'down1': w_ffn1_down.astype(BF16),
        'up2': w_ffn2_up.astype(BF16), 'down2': w_ffn2_down.astype(BF16),
        'ln_g': ln_g[:, :, None, :], 'ln_b': ln_b[:, :, None, :],
        'gla_qkv': w_in[:, :gate_lo].astype(BF16),
        'gla_g': jnp.pad(w_in[:, gate_lo:gate_lo + GLA_RANK], ((0, 0), (0, LANES - GLA_RANK))).astype(BF16),
        'gla_r': w_in[:, gate_lo + GLA_RANK:].astype(BF16),
        'gla_g2': jnp.pad(w_gate2_a[0], ((0, LANES - GLA_RANK), (0, 0))).astype(BF16),
        'gla_bg': b_gate_a[0][None, :],
        'gla_gon': g_onorm_a[0][None, :],
        'gla_out': w_out_a[0].astype(BF16),
        'kv': w_kv.astype(BF16),
        'q': w_q_b[0].astype(BF16),
        'att_out': w_out_b[0].astype(BF16),
    }

    zero_state = jnp.zeros((n_bp, GLA_HEADS, GLA_DK, GLA_DV), F32)
    y_p, s_p, k_p, v_p = _trunk(rows_p, x_prompt.reshape(-1, D_MODEL), n_bp, n_tp,
                                jnp.arange(n_tp), zero_state, None, w)
    y_s, s_s, k_s, v_s = _trunk(rows_s, x_sample.reshape(-1, D_MODEL), n_bs, n_ts,
                                PAST_LEN + jnp.arange(n_ts), state_gla[0], (cache_k, cache_v), w)
    return (y_p.reshape(n_bp, n_tp, D_MODEL), y_s.reshape(n_bs, n_ts, D_MODEL),
            s_p[None], s_s[None], k_p, v_p, k_s, v_s)
```

```python
import functools

import jax
import jax.numpy as jnp
from jax import lax
from jax.experimental import pallas as pl
from jax.experimental.pallas import tpu as pltpu

F32 = jnp.float32
BF16 = jnp.bfloat16

D_MODEL = 1024
DEPTH = 2
PAST_LEN = 16384
N_A_LAYERS = DEPTH // 2
GLA_HEADS = 4
GLA_QK = D_MODEL // 2
GLA_V = D_MODEL
GLA_DK = GLA_QK // GLA_HEADS
GLA_DV = GLA_V // GLA_HEADS
GLA_RANK = 16
GLA_TAU = 16.0
GLA_NORM_EPS = 1e-6
DIL_HEADS = 16
DIL_HD = D_MODEL // DIL_HEADS
DIL_WINDOWS = (128, 512, 2048)
DIL_DILATIONS = (1, 4, 16)
N_GROUPS = len(DIL_WINDOWS)
DIL_STEPS = 128
MAX_WINDOW = max(DIL_WINDOWS)
DIL_SCALE = DIL_HD ** -0.5
ROPE_THETA = 10000.0
D_FF = 2816
FFN_RES = 0.5
ALPHA = (2 * DEPTH) ** 0.25
LN_EPS = 1e-5
N_MOD = 9

LANES = 128
ROW_TILE = 512
FF_CHUNK = 256
GLA_CHUNK = 128
GLA_TILE = 512
GLA_FACTOR_BOUND = 80.0
ATT_BLOCK = DIL_STEPS
VMEM_LIMIT = 56 * 1024 * 1024


def _params(*sem):
    return pltpu.CompilerParams(dimension_semantics=sem, vmem_limit_bytes=VMEM_LIMIT)


def _silu(x):
    return x / (1.0 + jnp.exp(-x))


def _layer_norm(z, g, b):
    mu = jnp.mean(z, axis=-1, keepdims=True)
    zc = z - mu
    var = jnp.mean(zc * zc, axis=-1, keepdims=True)
    return zc * lax.rsqrt(var + LN_EPS) * g + b


def _dot(a, b):
    return jnp.dot(a, b, preferred_element_type=F32)


def _dot_nt(a, b):
    return lax.dot_general(a, b, (((1,), (1,)), ((), ())), preferred_element_type=F32)


def _rope(x, cos_t, sin_t):
    width = x.shape[1]
    lane = lax.broadcasted_iota(jnp.int32, x.shape, 1)
    first_half = (lane % DIL_HD) < (DIL_HD // 2)
    partner = jnp.where(first_half, pltpu.roll(x, width - DIL_HD // 2, 1), pltpu.roll(x, DIL_HD // 2, 1))
    reps = width // LANES
    cos_w = jnp.concatenate([cos_t] * reps, axis=1)
    sin_w = jnp.concatenate([sin_t] * reps, axis=1)
    return x * cos_w + partner * sin_w


def _ada_kernel(c_ref, w_ref, b_ref, o_ref):
    s = _silu(c_ref[...]).astype(BF16)
    o_ref[...] = _dot(s, w_ref[...].astype(BF16)) + b_ref[...]


def _ada_call(c_all, w, b, tn):
    n_l, _, n = w.shape
    rows = c_all.shape[0]
    return pl.pallas_call(
        _ada_kernel,
        grid=(n_l, n // tn),
        in_specs=[
            pl.BlockSpec((rows, D_MODEL), lambda l, j: (0, 0)),
            pl.BlockSpec((None, D_MODEL, tn), lambda l, j: (l, 0, j)),
            pl.BlockSpec((None, 1, tn), lambda l, j: (l, 0, j)),
        ],
        out_specs=pl.BlockSpec((None, rows, tn), lambda l, j: (l, 0, j)),
        out_shape=jax.ShapeDtypeStruct((n_l, rows, n), F32),
        compiler_params=_params("arbitrary", "arbitrary"),
        name="ada_mod",
    )(c_all, w, b.reshape(n_l, 1, n))


class _Rows:
    def __init__(self, n_rows, rows_per_sample, mods, kv_mods):
        self.n_rows = n_rows
        self.rows_per_sample = rows_per_sample
        self.per_row = rows_per_sample < 8
        self.tile = n_rows if self.per_row else min(ROW_TILE, rows_per_sample)
        self.steps = n_rows // self.tile
        self.tiles_per_sample = 1 if self.per_row else rows_per_sample // self.tile
        self.mods = mods
        self.kv_mods = kv_mods

    def row_spec(self, width):
        return pl.BlockSpec((self.tile, width), lambda i: (i, 0))

    def mod_spec(self, layer, k):
        if self.per_row:
            return pl.BlockSpec((None, self.tile, D_MODEL), lambda i: (layer, 0, k))
        tps = self.tiles_per_sample
        return pl.BlockSpec((None, None, 1, D_MODEL), lambda i: (layer, i // tps, 0, k))


def _const_spec(shape, index):
    return pl.BlockSpec(shape, lambda i: index, pipeline_mode=pl.Buffered(1))


def _ln_specs(layer, j):
    return [_const_spec((None, None, 1, D_MODEL), (layer, j, 0, 0))] * 2


def _ffn_kernel(x_ref, sh_ref, sc_ref, gt_ref, wup_ref, wdn_ref, g_ref, b_ref, o_ref):
    x = x_ref[...]
    h = (x * (1.0 + sc_ref[...]) + sh_ref[...]).astype(BF16)
    y = None
    for lo in range(0, D_FF, FF_CHUNK):
        a = _dot(h, wup_ref[:, lo:lo + FF_CHUNK])
        u = _dot(h, wup_ref[:, D_FF + lo:D_FF + lo + FF_CHUNK])
        part = _dot((_silu(a) * u).astype(BF16), wdn_ref[lo:lo + FF_CHUNK, :])
        y = part if y is None else y + part
    z = ALPHA * x + (1.0 + gt_ref[...]) * (FFN_RES * y)
    o_ref[...] = _layer_norm(z, g_ref[...], b_ref[...])


def _ffn_call(rows, x, layer, mod0, ln_j, w_up, w_down, ln_g, ln_b):
    return pl.pallas_call(
        _ffn_kernel,
        grid=(rows.steps,),
        in_specs=[
            rows.row_spec(D_MODEL),
            rows.mod_spec(layer, mod0), rows.mod_spec(layer, mod0 + 1), rows.mod_spec(layer, mod0 + 2),
            _const_spec((None, D_MODEL, 2 * D_FF), (layer, 0, 0)),
            _const_spec((None, D_FF, D_MODEL), (layer, 0, 0)),
        ] + _ln_specs(layer, ln_j),
        out_specs=rows.row_spec(D_MODEL),
        out_shape=jax.ShapeDtypeStruct((rows.n_rows, D_MODEL), F32),
        compiler_params=_params("arbitrary"),
        name="ffn",
    )(x, rows.mods, rows.mods, rows.mods, w_up, w_down, ln_g, ln_b)


def _gla_proj_kernel(x_ref, sh_ref, sc_ref, wqkv_ref, wg_ref, wr_ref, wg2_ref, bg_ref,
                     q_ref, k_ref, v_ref, la_ref, r_ref):
    h = (x_ref[...] * (1.0 + sc_ref[...]) + sh_ref[...]).astype(BF16)
    qkv = _dot(h, wqkv_ref[...])
    q_ref[...] = qkv[:, :GLA_QK] * (GLA_DK ** -0.5)
    k_ref[...] = qkv[:, GLA_QK:2 * GLA_QK]
    v_ref[...] = qkv[:, 2 * GLA_QK:].astype(v_ref.dtype)
    g_lr = _dot(h, wg_ref[...]).astype(BF16)
    gate = _dot(g_lr, wg2_ref[...]) + bg_ref[...]
    log_sig = jnp.minimum(gate, 0.0) - jnp.log(1.0 + jnp.exp(-jnp.abs(gate)))
    la_ref[...] = log_sig / GLA_TAU
    r_ref[...] = _silu(_dot(h, wr_ref[...])).astype(r_ref.dtype)


def _gla_proj_call(rows, x, layer, w_qkv, w_g, w_r, w_g2, b_g):
    widths = (GLA_QK, GLA_QK, GLA_V, GLA_QK, GLA_V)
    narrow = F32 if rows.per_row else BF16
    dtypes = (F32, F32, narrow, F32, narrow)
    return pl.pallas_call(
        _gla_proj_kernel,
        grid=(rows.steps,),
        in_specs=[
            rows.row_spec(D_MODEL),
            rows.mod_spec(layer, 3), rows.mod_spec(layer, 4),
            _const_spec((D_MODEL, 2 * GLA_QK + GLA_V), (0, 0)),
            _const_spec((D_MODEL, LANES), (0, 0)),
            _const_spec((D_MODEL, GLA_V), (0, 0)),
            _const_spec((LANES, GLA_QK), (0, 0)),
            _const_spec((1, GLA_QK), (0, 0)),
        ],
        out_specs=[rows.row_spec(w) for w in widths],
        out_shape=[jax.ShapeDtypeStruct((rows.n_rows, w), dt) for w, dt in zip(widths, dtypes)],
        compiler_params=_params("arbitrary"),
        name="gla_proj",
    )(x, rows.mods, rows.mods, w_qkv, w_g, w_r, w_g2, b_g)


def _gla_scores_factored(qh, kh, bh, b_last):
    mid = 0.5 * b_last
    q_t = (qh * jnp.exp(bh - mid)).astype(BF16)
    k_t = (kh * jnp.exp(mid - bh)).astype(BF16)
    return _dot_nt(q_t, k_t)


def _gla_scores_elementwise(qh, kh, bh, k_scr, b_scr):
    k_scr[...] = kh
    b_scr[...] = bh
    col = lax.broadcasted_iota(jnp.int32, (GLA_CHUNK, GLA_CHUNK), 1)

    def body(j, acc):
        kj = k_scr[pl.ds(j, 1), :]
        bj = b_scr[pl.ds(j, 1), :]
        term = qh * kj * jnp.exp(jnp.minimum(bh - bj, 0.0))
        return acc + jnp.where(col == j, jnp.sum(term, axis=1, keepdims=True), 0.0)

    return lax.fori_loop(0, GLA_CHUNK, body, jnp.zeros((GLA_CHUNK, GLA_CHUNK), F32))


def _gla_chunk(rows, b, states, score_fn, q_ref, k_ref, v_ref, o_ref, causal):
    q, k = q_ref[rows, :], k_ref[rows, :]
    b_last = b[GLA_CHUNK - 1:GLA_CHUNK, :]
    q_in = q * jnp.exp(b)
    k_out = k * jnp.exp(b_last - b)
    new_states = []
    for hd, s_prev in enumerate(states):
        ks = slice(hd * GLA_DK, (hd + 1) * GLA_DK)
        vs = slice(hd * GLA_DV, (hd + 1) * GLA_DV)
        vh = v_ref[rows, vs].astype(BF16)
        o_inter = _dot(q_in[:, ks].astype(BF16), s_prev.astype(BF16))
        scores = jnp.where(causal, score_fn(q[:, ks], k[:, ks], b[:, ks], b_last[:, ks]), 0.0)
        o_ref[rows, vs] = o_inter + _dot(scores.astype(BF16), vh)
        decay_col = jnp.transpose(jnp.broadcast_to(jnp.exp(b_last[:, ks]), (GLA_DK, GLA_DK)))[:, :1]
        k_out_t = jnp.transpose(k_out[:, ks]).astype(BF16)
        new_states.append(decay_col * s_prev + _dot(k_out_t, vh))
    return new_states


def _gla_rec_kernel(q_ref, k_ref, v_ref, la_ref, s0_ref, o_ref, sfin_ref, s_scr, k_scr, b_scr):
    t = pl.program_id(1)

    @pl.when(t == 0)
    def _():
        s_scr[...] = s0_ref[...]

    row = lax.broadcasted_iota(jnp.int32, (GLA_CHUNK, GLA_CHUNK), 0)
    col = lax.broadcasted_iota(jnp.int32, (GLA_CHUNK, GLA_CHUNK), 1)
    causal = col <= row
    tri = causal.astype(BF16)
    n_chunks = GLA_TILE // GLA_CHUNK

    def decay_prefix(rows):
        rest = la_ref[rows, :]
        total = None
        for _ in range(3):
            piece = rest.astype(BF16)
            rest = rest - piece.astype(F32)
            part = _dot(tri, piece)
            total = part if total is None else total + part
        return total

    chunk_rows = [slice(c * GLA_CHUNK, (c + 1) * GLA_CHUNK) for c in range(n_chunks)]
    prefixes = [decay_prefix(rows) for rows in chunk_rows]
    total_decay = functools.reduce(jnp.maximum, [-b[GLA_CHUNK - 1:GLA_CHUNK, :] for b in prefixes])
    factorable = jnp.max(total_decay) <= GLA_FACTOR_BOUND

    def refs():
        return q_ref, k_ref, v_ref, o_ref, causal

    @pl.when(factorable)
    def _():
        states = [s_scr[hd] for hd in range(GLA_HEADS)]
        for rows, b in zip(chunk_rows, prefixes):
            states = _gla_chunk(rows, b, states, _gla_scores_factored, *refs())
        for hd in range(GLA_HEADS):
            s_scr[hd] = states[hd]

    @pl.when(jnp.logical_not(factorable))
    def _():
        def chunk(c, carry):
            rows = pl.ds(pl.multiple_of(c * GLA_CHUNK, GLA_CHUNK), GLA_CHUNK)
            states = [s_scr[hd] for hd in range(GLA_HEADS)]
            elementwise = functools.partial(_gla_scores_elementwise, k_scr=k_scr, b_scr=b_scr)
            states = _gla_chunk(rows, decay_prefix(rows), states,
                                lambda qh, kh, bh, _: elementwise(qh, kh, bh), *refs())
            for hd in range(GLA_HEADS):
                s_scr[hd] = states[hd]
            return carry

        lax.fori_loop(0, n_chunks, chunk, 0)

    @pl.when(t == pl.num_programs(1) - 1)
    def _():
        sfin_ref[...] = s_scr[...]


def _gla_rec_call(q, k, v, la, s0):
    n_b, n_t = q.shape[:2]

    def seq(width):
        return pl.BlockSpec((None, GLA_TILE, width), lambda b, t: (b, t, 0))

    state = pl.BlockSpec((None, GLA_HEADS, GLA_DK, GLA_DV), lambda b, t: (b, 0, 0, 0))
    return pl.pallas_call(
        _gla_rec_kernel,
        grid=(n_b, n_t // GLA_TILE),
        in_specs=[seq(GLA_QK), seq(GLA_QK), seq(GLA_V), seq(GLA_QK), state],
        out_specs=[seq(GLA_V), state],
        out_shape=[jax.ShapeDtypeStruct((n_b, n_t, GLA_V), F32),
                   jax.ShapeDtypeStruct((n_b, GLA_HEADS, GLA_DK, GLA_DV), F32)],
        scratch_shapes=[pltpu.VMEM((GLA_HEADS, GLA_DK, GLA_DV), F32),
                        pltpu.VMEM((GLA_CHUNK, GLA_DK), F32),
                        pltpu.VMEM((GLA_CHUNK, GLA_DK), F32)],
        compiler_params=_params("arbitrary", "arbitrary"),
        name="gla_recurrence",
    )(q, k, v, la, s0)


def _gla_step_kernel(qt_ref, kt_ref, lat_ref, v_ref, s0_ref, o_ref, sfin_ref):
    n_t = v_ref.shape[0]
    for hd in range(GLA_HEADS):
        vs = slice(hd * GLA_DV, (hd + 1) * GLA_DV)
        s = s0_ref[hd]
        qt, kt, lat = qt_ref[hd], kt_ref[hd], lat_ref[hd]
        for t in range(n_t):
            s = jnp.exp(lat[:, t:t + 1]) * s + kt[:, t:t + 1] * v_ref[t:t + 1, vs]
            o_ref[t:t + 1, vs] = jnp.sum(qt[:, t:t + 1] * s, axis=0, keepdims=True)
        sfin_ref[hd] = s


def _gla_step_call(qt, kt, lat, v, s0):
    n_b, _, _, n_t = qt.shape
    col = pl.BlockSpec((None, GLA_HEADS, GLA_DK, n_t), lambda b: (b, 0, 0, 0))
    tok = pl.BlockSpec((None, n_t, GLA_V), lambda b: (b, 0, 0))
    state = pl.BlockSpec((None, GLA_HEADS, GLA_DK, GLA_DV), lambda b: (b, 0, 0, 0))
    return pl.pallas_call(
        _gla_step_kernel,
        grid=(n_b,),
        in_specs=[col, col, col, tok, state],
        out_specs=[tok, state],
        out_shape=[jax.ShapeDtypeStruct((n_b, n_t, GLA_V), F32),
                   jax.ShapeDtypeStruct((n_b, GLA_HEADS, GLA_DK, GLA_DV), F32)],
        compiler_params=_params("arbitrary"),
        name="gla_step",
    )(qt, kt, lat, v, s0)


def _gla_out_kernel(o_ref, r_ref, x_ref, gt_ref, gon_ref, wout_ref, g_ref, b_ref, y_ref):
    o = o_ref[...]
    parts = []
    for hd in range(GLA_HEADS):
        oh = o[:, hd * GLA_DV:(hd + 1) * GLA_DV]
        ms = jnp.mean(oh * oh, axis=-1, keepdims=True)
        parts.append(oh * lax.rsqrt(ms + GLA_NORM_EPS) * gon_ref[...])
    gated = (jnp.concatenate(parts, axis=1) * r_ref[...].astype(F32)).astype(BF16)
    y = _dot(gated, wout_ref[...])
    z = ALPHA * x_ref[...] + (1.0 + gt_ref[...]) * y
    y_ref[...] = _layer_norm(z, g_ref[...], b_ref[...])


def _gla_out_call(rows, o, r, x, layer, g_onorm, w_out, ln_g, ln_b):
    return pl.pallas_call(
        _gla_out_kernel,
        grid=(rows.steps,),
        in_specs=[
            rows.row_spec(GLA_V), rows.row_spec(GLA_V), rows.row_spec(D_MODEL),
            rows.mod_spec(layer, 5),
            _const_spec((1, GLA_DV), (0, 0)),
            _const_spec((GLA_V, D_MODEL), (0, 0)),
        ] + _ln_specs(layer, 1),
        out_specs=rows.row_spec(D_MODEL),
        out_shape=jax.ShapeDtypeStruct((rows.n_rows, D_MODEL), F32),
        compiler_params=_params("arbitrary"),
        name="gla_out",
    )(o, r, x, rows.mods, g_onorm, w_out, ln_g, ln_b)


def _gather_stride(scr, start, count, stride):
    return jnp.concatenate([scr[c, pl.ds(start, count, stride=stride), :] for c in range(scr.shape[0])], axis=1)


def _fill_slabs(scr, row0, x):
    for c in range(scr.shape[0]):
        scr[c, row0:row0 + x.shape[0], :] = x[:, c * LANES:(c + 1) * LANES]


def _store_by_residue(x, out_refs, scr_a, scr_b):
    n_rows = x.shape[0]
    s = DIL_DILATIONS[1]
    assert DIL_DILATIONS == (1, s, s * s)
    outs = {o.shape[0]: o for o in out_refs}
    if 1 in outs:
        outs[1][0] = x.astype(BF16)
    if s not in outs and s * s not in outs:
        return
    _fill_slabs(scr_a, 0, x)
    per_class = n_rows // s
    for r in range(s):
        xr = _gather_stride(scr_a, r, per_class, s)
        if s in outs:
            outs[s][r] = xr.astype(BF16)
        if s * s in outs:
            _fill_slabs(scr_b, r * per_class, xr)
    if s * s in outs:
        for r in range(s * s):
            xr = _gather_stride(scr_b, (r % s) * per_class + r // s, per_class // s, s)
            outs[s * s][r] = xr.astype(BF16)


def _kv_kernel(by_residue, x_ref, sh_ref, sc_ref, cos_ref, sin_ref, w_ref, k_ref, v_ref, *rest):
    h = (x_ref[...] * (1.0 + sc_ref[...]) + sh_ref[...]).astype(BF16)
    kv = _dot(h, w_ref[...])
    k = _rope(kv[:, :D_MODEL], cos_ref[...], sin_ref[...])
    v = kv[:, D_MODEL:]
    if by_residue:
        k_ref[...] = jnp.transpose(k)
        v_ref[...] = jnp.transpose(v)
    else:
        k_ref[...] = k.reshape(k_ref.shape)
        v_ref[...] = v.reshape(v_ref.shape)
    if by_residue:
        _store_by_residue(k, rest[0:N_GROUPS], *rest[-2:])
        _store_by_residue(v, rest[N_GROUPS:2 * N_GROUPS], *rest[-2:])


def _rope_spec(rows):
    if rows.per_row:
        return pl.BlockSpec((rows.tile, LANES), lambda i: (0, 0))
    tps = rows.tiles_per_sample
    return pl.BlockSpec((rows.tile, LANES), lambda i: (i % tps, 0))


def _kv_mod_spec(rows, k):
    if rows.per_row:
        return pl.BlockSpec((None, rows.tile, D_MODEL), lambda i: (0, 0, k))
    tps = rows.tiles_per_sample
    return pl.BlockSpec((None, None, 1, D_MODEL), lambda i: (0, i // tps, 0, k))


def _residue_spec(rows, dil, width):
    tps = rows.tiles_per_sample
    return pl.BlockSpec((None, dil, rows.tile // dil, width), lambda i: (i // tps, 0, i % tps, 0))


def _residue_shapes(rows, width, dtype):
    n_b = rows.n_rows // rows.rows_per_sample
    return [jax.ShapeDtypeStruct((n_b, dil, rows.rows_per_sample // dil, width), dtype) for dil in DIL_DILATIONS]


def _permute_scratch():
    return [pltpu.VMEM((D_MODEL // LANES, ROW_TILE, LANES), F32)] * 2


def _kv_call(rows, x, cos_t, sin_t, w_kv):
    by_residue = not rows.per_row
    scratch = []
    if by_residue:
        tps = rows.tiles_per_sample
        n_b = rows.n_rows // rows.rows_per_sample
        out_specs = [pl.BlockSpec((None, D_MODEL, rows.tile), lambda i: (i // tps, 0, i % tps))] * 2
        out_shape = [jax.ShapeDtypeStruct((n_b, D_MODEL, rows.rows_per_sample), F32)] * 2
    else:
        out_specs = [pl.BlockSpec((rows.tile, DIL_HEADS, DIL_HD), lambda i: (i, 0, 0))] * 2
        out_shape = [jax.ShapeDtypeStruct((rows.n_rows, DIL_HEADS, DIL_HD), F32)] * 2
    if by_residue:
        out_specs += [_residue_spec(rows, dil, D_MODEL) for dil in DIL_DILATIONS] * 2
        out_shape += _residue_shapes(rows, D_MODEL, BF16) * 2
        scratch = _permute_scratch()
    return pl.pallas_call(
        functools.partial(_kv_kernel, by_residue),
        grid=(rows.steps,),
        in_specs=[
            rows.row_spec(D_MODEL),
            _kv_mod_spec(rows, 0), _kv_mod_spec(rows, 1),
            _rope_spec(rows), _rope_spec(rows),
            _const_spec((D_MODEL, 2 * D_MODEL), (0, 0)),
        ],
        out_specs=out_specs,
        out_shape=out_shape,
        scratch_shapes=scratch,
        compiler_params=_params("arbitrary"),
        name="kv_proj",
    )(x, rows.kv_mods, rows.kv_mods, cos_t, sin_t, w_kv)


def _q_kernel(by_residue, x_ref, sh_ref, sc_ref, cos_ref, sin_ref, w_ref, *rest):
    h = (x_ref[...] * (1.0 + sc_ref[...]) + sh_ref[...]).astype(BF16)
    q = _rope(_dot(h, w_ref[...]), cos_ref[...], sin_ref[...]) * DIL_SCALE
    if by_residue:
        for g in range(N_GROUPS):
            _store_by_residue(q[:, g * D_MODEL:(g + 1) * D_MODEL], rest[g:g + 1], *rest[-2:])
    else:
        rest[0][...] = q.reshape(rest[0].shape)


def _q_call(rows, x, layer, cos_t, sin_t, w_q):
    width = N_GROUPS * D_MODEL
    by_residue = not rows.per_row
    if by_residue:
        out_specs = [_residue_spec(rows, dil, D_MODEL) for dil in DIL_DILATIONS]
        out_shape = _residue_shapes(rows, D_MODEL, BF16)
        scratch = _permute_scratch()
    else:
        out_specs = pl.BlockSpec((rows.tile, N_GROUPS * DIL_HEADS, DIL_HD), lambda i: (i, 0, 0))
        out_shape = jax.ShapeDtypeStruct((rows.n_rows, N_GROUPS * DIL_HEADS, DIL_HD), F32)
        scratch = []
    return pl.pallas_call(
        functools.partial(_q_kernel, by_residue),
        grid=(rows.steps,),
        in_specs=[
            rows.row_spec(D_MODEL),
            rows.mod_spec(layer, 3), rows.mod_spec(layer, 4),
            _rope_spec(rows), _rope_spec(rows),
            _const_spec((D_MODEL, width), (0, 0)),
        ],
        out_specs=out_specs,
        out_shape=out_shape,
        scratch_shapes=scratch,
        compiler_params=_params("arbitrary"),
        name="q_proj",
    )(x, rows.mods, rows.mods, cos_t, sin_t, w_q)


def _pair_softmax_pv(scores, v_pair, even_lane):
    half = scores.shape[0] // 2
    m = jnp.max(scores, axis=-1, keepdims=True)
    e = jnp.exp(scores - m)
    den = jnp.sum(e, axis=-1, keepdims=True)
    o2 = _dot(e.astype(BF16), v_pair) / den
    lse2 = m + jnp.log(den)
    o = jnp.where(even_lane, o2[:half], o2[half:])
    return o, lse2[:half], lse2[half:]


def _dil_prompt_kernel(has_prev, dil, q_ref, *refs):
    if has_prev:
        kp_ref, kc_ref, vp_ref, vc_ref, o_ref, lse_ref = refs
    else:
        kc_ref, vc_ref, o_ref, lse_ref = refs
    blk = pl.program_id(2)
    first_token = blk * (ATT_BLOCK * dil) + pl.program_id(1)
    if dil == 1:
        token_rows = pl.ds(pl.multiple_of(first_token, ATT_BLOCK), ATT_BLOCK)
    else:
        token_rows = pl.ds(first_token, ATT_BLOCK, stride=dil)
    n_keys = (2 if has_prev else 1) * ATT_BLOCK
    lane = lax.broadcasted_iota(jnp.int32, (ATT_BLOCK, LANES), 1)
    even_lane = lane < DIL_HD
    qi = lax.broadcasted_iota(jnp.int32, (2 * ATT_BLOCK, n_keys), 0) % ATT_BLOCK
    kj = lax.broadcasted_iota(jnp.int32, (2 * ATT_BLOCK, n_keys), 1)
    if has_prev:
        dist = qi + ATT_BLOCK - kj
        valid = (dist >= 0) & (dist <= DIL_STEPS) & ((kj >= ATT_BLOCK) | (blk > 0))
    else:
        valid = kj <= qi
    lse_all = jnp.zeros((ATT_BLOCK, LANES), F32)
    for p in range(DIL_HEADS // 2):
        sl = slice(p * LANES, (p + 1) * LANES)
        qp = q_ref[:, sl]
        zero = jnp.zeros_like(qp)
        q2 = jnp.concatenate([jnp.where(even_lane, qp, zero), jnp.where(even_lane, zero, qp)], axis=0)
        if has_prev:
            k_pair = jnp.concatenate([kp_ref[:, sl], kc_ref[:, sl]], axis=0)
            v_pair = jnp.concatenate([vp_ref[:, sl], vc_ref[:, sl]], axis=0)
        else:
            k_pair, v_pair = kc_ref[:, sl], vc_ref[:, sl]
        scores = jnp.where(valid, _dot_nt(q2, k_pair), -jnp.inf)
        o, lse_even, lse_odd = _pair_softmax_pv(scores, v_pair, even_lane)
        o_ref[p, token_rows, :] = o
        lse_all = jnp.where(lane == 2 * p, lse_even, jnp.where(lane == 2 * p + 1, lse_odd, lse_all))
    lse_ref[token_rows, :] = lse_all


def _dil_prompt_call(q, kb, vb, group):
    n_b, dil, length, _ = kb.shape
    n_t = dil * length
    n_blk = length // ATT_BLOCK
    has_prev = n_blk > 1
    blk_shape = (None, None, ATT_BLOCK, D_MODEL)
    cur = pl.BlockSpec(blk_shape, lambda b, r, i: (b, r, i, 0))
    prev = pl.BlockSpec(blk_shape, lambda b, r, i: (b, r, jnp.maximum(i - 1, 0), 0))
    if has_prev:
        in_specs, operands = [cur, prev, cur, prev, cur], (q, kb, kb, vb, vb)
    else:
        in_specs, operands = [cur, cur, cur], (q, kb, vb)
    return pl.pallas_call(
        functools.partial(_dil_prompt_kernel, has_prev, dil),
        grid=(n_b, dil, n_blk),
        in_specs=in_specs,
        out_specs=[pl.BlockSpec((None, D_MODEL // LANES, n_t, LANES), lambda b, r, i: (b, 0, 0, 0)),
                   pl.BlockSpec((None, n_t, LANES), lambda b, r, i: (b, 0, 0))],
        out_shape=[jax.ShapeDtypeStruct((n_b, D_MODEL // LANES, n_t, LANES), F32),
                   jax.ShapeDtypeStruct((n_b, n_t, LANES), F32)],
        compiler_params=_params("arbitrary", "arbitrary", "arbitrary"),
        name="dilated_prompt",
    )(*operands)


SAMPLE_HEAD_CHUNK = 8


def _sample_group_dense(t, q_col, k_win, v_win, k_new, v_new):
    n_t = k_new.shape[-1]
    pos = lax.broadcasted_iota(jnp.int32, (1, 1, k_win.shape[-1]), 2)
    pos_new = lax.broadcasted_iota(jnp.int32, (1, 1, n_t), 2)
    s_c = jnp.where(pos >= t, jnp.sum(k_win * q_col, axis=1, keepdims=True), -jnp.inf)
    s_n = jnp.where(pos_new <= t, jnp.sum(k_new * q_col, axis=1, keepdims=True), -jnp.inf)
    m = jnp.maximum(jnp.max(s_c, axis=2, keepdims=True), jnp.max(s_n, axis=2, keepdims=True))
    e_c, e_n = jnp.exp(s_c - m), jnp.exp(s_n - m)
    den = jnp.sum(e_c, axis=2, keepdims=True) + jnp.sum(e_n, axis=2, keepdims=True)
    o = (jnp.sum(v_win * e_c, axis=2, keepdims=True) + jnp.sum(v_new * e_n, axis=2, keepdims=True)) / den
    return o, m + jnp.log(den)


def _sample_group_dilated(dil, q_g, k_win, v_win, k_new, v_new):
    n_t = q_g.shape[-1]
    width = k_win.shape[-1]
    n_col = width // LANES
    res = lax.broadcasted_iota(jnp.int32, (1, 1, LANES), 2) % dil
    res_w = lax.broadcasted_iota(jnp.int32, (1, 1, width), 2) % dil
    q_pat = jnp.zeros(q_g.shape[:2] + (LANES,), F32)
    for t in range(n_t):
        q_pat = jnp.where(res == t, q_g[:, :, t:t + 1], q_pat)
    s_row = jnp.sum(k_win * jnp.concatenate([q_pat] * n_col, axis=2), axis=1, keepdims=True)
    s_new = jnp.sum(k_new * q_g, axis=1, keepdims=True)
    m_t = [jnp.maximum(jnp.max(jnp.where(res_w == t, s_row, -jnp.inf), axis=2, keepdims=True), s_new[:, :, t:t + 1])
           for t in range(n_t)]
    m_row = jnp.zeros_like(s_row)
    for t in range(n_t):
        m_row = jnp.where(res_w == t, m_t[t], m_row)
    e = jnp.exp(jnp.where(res_w < n_t, s_row, -jnp.inf) - m_row)
    acc = v_win * e
    acc_fold = sum(acc[:, :, c * LANES:(c + 1) * LANES] for c in range(n_col))
    e_fold = sum(e[:, :, c * LANES:(c + 1) * LANES] for c in range(n_col))
    out = []
    for t in range(n_t):
        e_new = jnp.exp(s_new[:, :, t:t + 1] - m_t[t])
        den = jnp.sum(jnp.where(res == t, e_fold, 0.0), axis=2, keepdims=True) + e_new
        o = (jnp.sum(jnp.where(res == t, acc_fold, 0.0), axis=2, keepdims=True) + e_new * v_new[:, :, t:t + 1]) / den
        out.append((o, m_t[t] + jnp.log(den)))
    return out


def _dil_sample_kernel(q_ref, k_ref, v_ref, kn_ref, vn_ref, o_ref):
    n_t = q_ref.shape[-1]
    n_s = k_ref.shape[-1]
    pos_new = lax.broadcasted_iota(jnp.int32, (1, 1, n_t), 2)

    def head_chunk(i, carry):
        hs = pl.ds(pl.multiple_of(i * SAMPLE_HEAD_CHUNK, SAMPLE_HEAD_CHUNK), SAMPLE_HEAD_CHUNK)
        k_new, v_new = kn_ref[hs], vn_ref[hs]
        groups = []
        for g, dil in enumerate(DIL_DILATIONS):
            lo = n_s - DIL_STEPS * dil
            k_win, v_win = k_ref[hs, :, lo:], v_ref[hs, :, lo:]
            if dil == 1:
                groups.append([_sample_group_dense(t, q_ref[g, hs, :, t:t + 1], k_win, v_win, k_new, v_new)
                               for t in range(n_t)])
            else:
                groups.append(_sample_group_dilated(dil, q_ref[g, hs], k_win, v_win, k_new, v_new))
        merged = jnp.zeros(k_new.shape, F32)
        for t in range(n_t):
            parts = [grp[t] for grp in groups]
            m = functools.reduce(jnp.maximum, [lse for _, lse in parts])
            wts = [jnp.exp(lse - m) for _, lse in parts]
            o = sum(wt * o for wt, (o, _) in zip(wts, parts)) / sum(wts)
            merged = jnp.where(pos_new == t, o, merged)
        o_ref[hs] = merged
        return carry

    lax.fori_loop(0, DIL_HEADS // SAMPLE_HEAD_CHUNK, head_chunk, 0)


def _dil_sample_call(q, cache_k, cache_v, k_new, v_new):
    n_b, n_t = q.shape[0], q.shape[-1]
    wb = cache_k.shape[-1]
    assert wb == DIL_STEPS * DIL_DILATIONS[-1] and DIL_DILATIONS[0] == 1 and n_t <= DIL_DILATIONS[1]
    assert all(wb % dil == 0 and LANES % dil == 0 for dil in DIL_DILATIONS)
    cache = pl.BlockSpec((None, DIL_HEADS, DIL_HD, wb), lambda b: (b, 0, 0, 0))
    new = pl.BlockSpec((None, DIL_HEADS, DIL_HD, n_t), lambda b: (b, 0, 0, 0))
    return pl.pallas_call(
        _dil_sample_kernel,
        grid=(n_b,),
        in_specs=[pl.BlockSpec((None, N_GROUPS, DIL_HEADS, DIL_HD, n_t), lambda b: (b, 0, 0, 0, 0)),
                  cache, cache, new, new],
        out_specs=new,
        out_shape=jax.ShapeDtypeStruct((n_b, DIL_HEADS, DIL_HD, n_t), F32),
        compiler_params=_params("arbitrary"),
        name="dilated_sample",
    )(q, cache_k, cache_v, k_new, v_new)


def _att_merge(o_refs, lse_refs):
    lses = [l_ref[...] for l_ref in lse_refs]
    m = functools.reduce(jnp.maximum, lses)
    e = [jnp.exp(l - m) for l in lses]
    den = sum(e)
    head = lax.broadcasted_iota(jnp.int32, (LANES, D_MODEL), 0)
    col_head = lax.broadcasted_iota(jnp.int32, (LANES, D_MODEL), 1) // DIL_HD
    spread = (head == col_head).astype(BF16)
    merged = None
    for o_ref, eg in zip(o_refs, e):
        wt = eg / den
        wt_hi = wt.astype(BF16)
        wt_lo = (wt - wt_hi.astype(F32)).astype(BF16)
        wide = _dot(wt_hi, spread) + _dot(wt_lo, spread)
        part = wide * jnp.concatenate([o_ref[c] for c in range(o_ref.shape[0])], axis=1)
        merged = part if merged is None else merged + part
    return merged


def _att_out_kernel(merged_input, *refs):
    if merged_input:
        o_ref, x_ref, gt_ref, wout_ref, g_ref, b_ref, y_ref = refs
        o = o_ref[...]
    else:
        o_refs, lse_refs = refs[0:N_GROUPS], refs[N_GROUPS:2 * N_GROUPS]
        x_ref, gt_ref, wout_ref, g_ref, b_ref, y_ref = refs[2 * N_GROUPS:]
        o = _att_merge(o_refs, lse_refs)
    y = _dot(o.astype(BF16), wout_ref[...])
    z = ALPHA * x_ref[...] + (1.0 + gt_ref[...]) * y
    y_ref[...] = _layer_norm(z, g_ref[...], b_ref[...])


def _att_out_call(rows, att, x, layer, w_out, ln_g, ln_b):
    merged_input = rows.per_row
    if merged_input:
        att_specs = [rows.row_spec(D_MODEL)]
        operands = [att]
    else:
        outs, lses = att
        tps = rows.tiles_per_sample
        o_spec = pl.BlockSpec((None, D_MODEL // LANES, rows.tile, LANES), lambda i: (i // tps, 0, i % tps, 0))
        lse_spec = pl.BlockSpec((None, rows.tile, LANES), lambda i: (i // tps, i % tps, 0))
        att_specs = [o_spec] * N_GROUPS + [lse_spec] * N_GROUPS
        operands = list(outs) + list(lses)
    return pl.pallas_call(
        functools.partial(_att_out_kernel, merged_input),
        grid=(rows.steps,),
        in_specs=att_specs + [
            rows.row_spec(D_MODEL),
            rows.mod_spec(layer, 5),
            _const_spec((D_MODEL, D_MODEL), (0, 0)),
        ] + _ln_specs(layer, 1),
        out_specs=rows.row_spec(D_MODEL),
        out_shape=jax.ShapeDtypeStruct((rows.n_rows, D_MODEL), F32),
        compiler_params=_params("arbitrary"),
        name="att_out",
    )(*operands, x, rows.mods, w_out, ln_g, ln_b)


def _rope_tables(positions):
    half = DIL_HD // 2
    inv = ROPE_THETA ** (-jnp.arange(half, dtype=F32) / half)
    ang = positions.astype(F32)[:, None] * inv[None, :]
    cos, sin = jnp.cos(ang), jnp.sin(ang)
    return jnp.concatenate([cos, cos, cos, cos], axis=1), jnp.concatenate([-sin, sin, -sin, sin], axis=1)


def _trunk(rows, x, n_b, n_t, positions, gla_state, cache, w):
    cos_t, sin_t = _rope_tables(positions)
    if rows.per_row:
        cos_t, sin_t = jnp.tile(cos_t, (n_b, 1)), jnp.tile(sin_t, (n_b, 1))
    ffn = functools.partial(_ffn_call, rows)

    x = ffn(x, 0, 0, 0, w['up1'], w['down1'], w['ln_g'], w['ln_b'])
    q, k, v, la, r = _gla_proj_call(rows, x, 0, w['gla_qkv'], w['gla_g'], w['gla_r'], w['gla_g2'], w['gla_bg'])
    if rows.per_row:
        def cols(a):
            return a.reshape(n_b, n_t, GLA_HEADS, GLA_DK).transpose(0, 2, 3, 1)
        o, s_new = _gla_step_call(cols(q), cols(k), cols(la), v.reshape(n_b, n_t, GLA_V), gla_state)
    else:
        o, s_new = _gla_rec_call(q.reshape(n_b, n_t, GLA_QK), k.reshape(n_b, n_t, GLA_QK),
                                 v.reshape(n_b, n_t, GLA_V), la.reshape(n_b, n_t, GLA_QK), gla_state)
    x = _gla_out_call(rows, o.reshape(n_b * n_t, GLA_V), r, x, 0, w['gla_gon'], w['gla_out'], w['ln_g'], w['ln_b'])
    x = ffn(x, 0, 6, 2, w['up2'], w['down2'], w['ln_g'], w['ln_b'])

    k_rows, v_rows, *kv_groups = _kv_call(rows, x, cos_t, sin_t, w['kv'])
    if rows.per_row:
        k_rows, v_rows = (a.reshape(n_b, n_t, DIL_HEADS, DIL_HD) for a in (k_rows, v_rows))
    else:
        k_rows, v_rows = (a.reshape(n_b, DIL_HEADS, DIL_HD, n_t).transpose(0, 3, 1, 2) for a in (k_rows, v_rows))

    x = ffn(x, 1, 0, 0, w['up1'], w['down1'], w['ln_g'], w['ln_b'])
    qd = _q_call(rows, x, 1, cos_t, sin_t, w['q'])
    if rows.per_row:
        cache_k, cache_v = (a.transpose(0, 2, 3, 1) for a in cache)
        q_t = qd.reshape(n_b, n_t, N_GROUPS, DIL_HEADS, DIL_HD).transpose(0, 2, 3, 4, 1)
        att = _dil_sample_call(q_t, cache_k, cache_v, k_rows.transpose(0, 2, 3, 1), v_rows.transpose(0, 2, 3, 1))
        att = att.transpose(0, 3, 1, 2).reshape(n_b * n_t, D_MODEL)
    else:
        att = tuple(zip(*[_dil_prompt_call(qd[g], kv_groups[g], kv_groups[N_GROUPS + g], g) for g in range(N_GROUPS)]))
    x = _att_out_call(rows, att, x, 1, w['att_out'], w['ln_g'], w['ln_b'])
    x = ffn(x, 1, 6, 2, w['up2'], w['down2'], w['ln_g'], w['ln_b'])
    return x, s_new, k_rows, v_rows


def kernel(x_prompt, x_sample, state_gla, cache_k, cache_v, c_prompt, c_sample, w_ada, b_ada, ln_g, ln_b, w_ffn1_up, w_ffn1_down, w_ffn2_up, w_ffn2_down, w_in_a, w_gate2_a, b_gate_a, g_onorm_a, w_out_a, w_ada_kv, b_ada_kv, w_kv, w_q_b, w_out_b):
    n_bp, n_tp = x_prompt.shape[:2]
    n_bs, n_ts = x_sample.shape[:2]
    assert N_A_LAYERS == 1 and DEPTH == 2
    assert n_tp % (ATT_BLOCK * DIL_DILATIONS[-1]) == 0 and n_tp % GLA_TILE == 0 and n_tp <= MAX_WINDOW
    assert n_ts < 8

    c_all = jnp.concatenate([c_prompt, c_sample], axis=0)
    mods = _ada_call(c_all, w_ada, b_ada, 1536)
    kv_mods = _ada_call(c_all, w_ada_kv[None], b_ada_kv[None], 1024)
    rows_p = _Rows(n_bp * n_tp, n_tp, mods[:, :n_bp, None, :], kv_mods[:, :n_bp, None, :])
    rows_s = _Rows(n_bs * n_ts, n_ts, jnp.repeat(mods[:, n_bp:], n_ts, axis=1),
                   jnp.repeat(kv_mods[:, n_bp:], n_ts, axis=1))

    w_in = w_in_a[0]
    gate_lo = 2 * GLA_QK + GLA_V
    w = {
        'up1': w_ffn1_up.astype(BF16), 'down1': w_ffn1_down.astype(BF16),
        'up2': w_ffn2_up.astype(BF16), 'down2': w_ffn2_down.astype(BF16),
        'ln_g': ln_g[:, :, None, :], 'ln_b': ln_b[:, :, None, :],
        'gla_qkv': w_in[:, :gate_lo].astype(BF16),
        'gla_g': jnp.pad(w_in[:, gate_lo:gate_lo + GLA_RANK], ((0, 0), (0, LANES - GLA_RANK))).astype(BF16),
        'gla_r': w_in[:, gate_lo + GLA_RANK:].astype(BF16),
        'gla_g2': jnp.pad(w_gate2_a[0], ((0, LANES - GLA_RANK), (0, 0))).astype(BF16),
        'gla_bg': b_gate_a[0][None, :],
        'gla_gon': g_onorm_a[0][None, :],
        'gla_out': w_out_a[0].astype(BF16),
        'kv': w_kv.astype(BF16),
        'q': w_q_b[0].astype(BF16),
        'att_out': w_out_b[0].astype(BF16),
    }

    zero_state = jnp.zeros((n_bp, GLA_HEADS, GLA_DK, GLA_DV), F32)
    y_p, s_p, k_p, v_p = _trunk(rows_p, x_prompt.reshape(-1, D_MODEL), n_bp, n_tp,
                                jnp.arange(n_tp), zero_state, None, w)
    y_s, s_s, k_s, v_s = _trunk(rows_s, x_sample.reshape(-1, D_MODEL), n_bs, n_ts,
                                PAST_LEN + jnp.arange(n_ts), state_gla[0], (cache_k, cache_v), w)
    return (y_p.reshape(n_bp, n_tp, D_MODEL), y_s.reshape(n_bs, n_ts, D_MODEL),
            s_p[None], s_s[None], k_p, v_p, k_s, v_s)
```

```python
import functools

import jax
import jax.numpy as jnp
from jax import lax
from jax.experimental import pallas as pl
from jax.experimental.pallas import tpu as pltpu

F32 = jnp.float32
BF16 = jnp.bfloat16

D_MODEL = 1024
DEPTH = 2
PAST_LEN = 16384
N_A_LAYERS = DEPTH // 2
GLA_HEADS = 4
GLA_QK = D_MODEL // 2
GLA_V = D_MODEL
GLA_DK = GLA_QK // GLA_HEADS
GLA_DV = GLA_V // GLA_HEADS
GLA_RANK = 16
GLA_TAU = 16.0
GLA_NORM_EPS = 1e-6
DIL_HEADS = 16
DIL_HD = D_MODEL // DIL_HEADS
DIL_WINDOWS = (128, 512, 2048)
DIL_DILATIONS = (1, 4, 16)
N_GROUPS = len(DIL_WINDOWS)
DIL_STEPS = 128
MAX_WINDOW = max(DIL_WINDOWS)
DIL_SCALE = DIL_HD ** -0.5
ROPE_THETA = 10000.0
D_FF = 2816
FFN_RES = 0.5
ALPHA = (2 * DEPTH) ** 0.25
LN_EPS = 1e-5
N_MOD = 9

LANES = 128
ROW_TILE = 512
LOG2_E = 1.4426950408889634
LN_2 = 0.6931471805599453
FF_CHUNK = 256
GLA_CHUNK = 128
GLA_TILE = 512
GLA_FACTOR_BOUND = 80.0
ATT_BLOCK = DIL_STEPS
VMEM_LIMIT = 56 * 1024 * 1024


def _params(*sem):
    return pltpu.CompilerParams(dimension_semantics=sem, vmem_limit_bytes=VMEM_LIMIT)


def _silu(x):
    return x / (1.0 + jnp.exp(-x))


def _layer_norm(z, g, b):
    mu = jnp.mean(z, axis=-1, keepdims=True)
    zc = z - mu
    var = jnp.mean(zc * zc, axis=-1, keepdims=True)
    return zc * lax.rsqrt(var + LN_EPS) * g + b


def _dot(a, b):
    return jnp.dot(a, b, preferred_element_type=F32)


def _dot_nt(a, b):
    return lax.dot_general(a, b, (((1,), (1,)), ((), ())), preferred_element_type=F32)


def _rope(x, cos_t, sin_t):
    width = x.shape[1]
    lane = lax.broadcasted_iota(jnp.int32, x.shape, 1)
    first_half = (lane % DIL_HD) < (DIL_HD // 2)
    partner = jnp.where(first_half, pltpu.roll(x, width - DIL_HD // 2, 1), pltpu.roll(x, DIL_HD // 2, 1))
    reps = width // LANES
    cos_w = jnp.concatenate([cos_t] * reps, axis=1)
    sin_w = jnp.concatenate([sin_t] * reps, axis=1)
    return x * cos_w + partner * sin_w


def _ada_kernel(c_ref, w_ref, b_ref, o_ref):
    s = _silu(c_ref[...]).astype(BF16)
    o_ref[...] = _dot(s, w_ref[...].astype(BF16)) + b_ref[...]


def _ada_call(c_all, w, b, tn):
    n_l, _, n = w.shape
    rows = c_all.shape[0]
    return pl.pallas_call(
        _ada_kernel,
        grid=(n_l, n // tn),
        in_specs=[
            pl.BlockSpec((rows, D_MODEL), lambda l, j: (0, 0)),
            pl.BlockSpec((None, D_MODEL, tn), lambda l, j: (l, 0, j)),
            pl.BlockSpec((None, 1, tn), lambda l, j: (l, 0, j)),
        ],
        out_specs=pl.BlockSpec((None, rows, tn), lambda l, j: (l, 0, j)),
        out_shape=jax.ShapeDtypeStruct((n_l, rows, n), F32),
        compiler_params=_params("arbitrary", "arbitrary"),
        name="ada_mod",
    )(c_all, w, b.reshape(n_l, 1, n))


class _Rows:
    def __init__(self, n_rows, rows_per_sample, mods, kv_mods, tile=ROW_TILE):
        self.n_rows = n_rows
        self.rows_per_sample = rows_per_sample
        self.per_row = rows_per_sample < 8
        self.tile = n_rows if self.per_row else min(tile, rows_per_sample)
        self.steps = n_rows // self.tile
        self.tiles_per_sample = 1 if self.per_row else rows_per_sample // self.tile
        self.mods = mods
        self.kv_mods = kv_mods

    def row_spec(self, width):
        return pl.BlockSpec((self.tile, width), lambda i: (i, 0))

    def mod_spec(self, layer, k):
        if self.per_row:
            return pl.BlockSpec((None, self.tile, D_MODEL), lambda i: (layer, 0, k))
        tps = self.tiles_per_sample
        return pl.BlockSpec((None, None, 1, D_MODEL), lambda i: (layer, i // tps, 0, k))


def _const_spec(shape, index):
    return pl.BlockSpec(shape, lambda i: index, pipeline_mode=pl.Buffered(1))


def _ln_specs(layer, j):
    return [_const_spec((None, None, 1, D_MODEL), (layer, j, 0, 0))] * 2


def _ffn_kernel(has_guest, x_ref, sh_ref, sc_ref, gt_ref, wup_ref, wdn_ref, g_ref, b_ref, *rest):
    if has_guest:
        *guest_in, o_ref, guest_out = rest
        _sample_attention(*guest_in, guest_out)
    else:
        o_ref, = rest
    x = x_ref[...]
    h = (x * (1.0 + sc_ref[...]) + sh_ref[...]).astype(BF16)
    y = None
    for lo in range(0, D_FF, FF_CHUNK):
        a = _dot(h, wup_ref[:, lo:lo + FF_CHUNK])
        u = _dot(h, wup_ref[:, D_FF + lo:D_FF + lo + FF_CHUNK])
        part = _dot((_silu(a) * u).astype(BF16), wdn_ref[lo:lo + FF_CHUNK, :])
        y = part if y is None else y + part
    z = ALPHA * x + (1.0 + gt_ref[...]) * (FFN_RES * y)
    o_ref[...] = _layer_norm(z, g_ref[...], b_ref[...])


def _ffn_call(rows, x, layer, mod0, ln_j, w_up, w_down, ln_g, ln_b, guest=None):
    in_specs = [
        rows.row_spec(D_MODEL),
        rows.mod_spec(layer, mod0), rows.mod_spec(layer, mod0 + 1), rows.mod_spec(layer, mod0 + 2),
        _const_spec((None, D_MODEL, 2 * D_FF), (layer, 0, 0)),
        _const_spec((None, D_FF, D_MODEL), (layer, 0, 0)),
    ] + _ln_specs(layer, ln_j)
    operands = [x, rows.mods, rows.mods, rows.mods, w_up, w_down, ln_g, ln_b]
    out_specs = [rows.row_spec(D_MODEL)]
    out_shape = [jax.ShapeDtypeStruct((rows.n_rows, D_MODEL), F32)]
    if guest is not None:
        assert guest.n_b == rows.steps
        guest_in, guest_out, guest_shape = guest.specs(len(guest.outputs))
        in_specs += guest_in
        operands += guest.operands
        out_specs.append(guest_out)
        out_shape.append(guest_shape)
    outs = pl.pallas_call(
        functools.partial(_ffn_kernel, guest is not None),
        grid=(rows.steps,),
        in_specs=in_specs,
        out_specs=out_specs,
        out_shape=out_shape,
        compiler_params=_params("arbitrary"),
        name="ffn",
    )(*operands)
    if guest is not None:
        guest.outputs.append(outs[1])
    return outs[0]


def _gla_proj_kernel(x_ref, sh_ref, sc_ref, wqkv_ref, wg_ref, wr_ref, wg2_ref, bg_ref,
                     q_ref, k_ref, v_ref, la_ref, r_ref):
    h = (x_ref[...] * (1.0 + sc_ref[...]) + sh_ref[...]).astype(BF16)
    qkv = _dot(h, wqkv_ref[...])
    q_ref[...] = qkv[:, :GLA_QK] * (GLA_DK ** -0.5)
    k_ref[...] = qkv[:, GLA_QK:2 * GLA_QK]
    v_ref[...] = qkv[:, 2 * GLA_QK:].astype(v_ref.dtype)
    g_lr = _dot(h, wg_ref[...]).astype(BF16)
    gate = _dot(g_lr, wg2_ref[...]) + bg_ref[...]
    log_sig = jnp.minimum(gate, 0.0) - jnp.log(1.0 + jnp.exp(-jnp.abs(gate)))
    la_ref[...] = log_sig / GLA_TAU
    r_ref[...] = _silu(_dot(h, wr_ref[...])).astype(r_ref.dtype)


def _gla_proj_call(rows, x, layer, w_qkv, w_g, w_r, w_g2, b_g):
    widths = (GLA_QK, GLA_QK, GLA_V, GLA_QK, GLA_V)
    narrow = F32 if rows.per_row else BF16
    dtypes = (F32, F32, narrow, F32, narrow)
    return pl.pallas_call(
        _gla_proj_kernel,
        grid=(rows.steps,),
        in_specs=[
            rows.row_spec(D_MODEL),
            rows.mod_spec(layer, 3), rows.mod_spec(layer, 4),
            _const_spec((D_MODEL, 2 * GLA_QK + GLA_V), (0, 0)),
            _const_spec((D_MODEL, LANES), (0, 0)),
            _const_spec((D_MODEL, GLA_V), (0, 0)),
            _const_spec((LANES, GLA_QK), (0, 0)),
            _const_spec((1, GLA_QK), (0, 0)),
        ],
        out_specs=[rows.row_spec(w) for w in widths],
        out_shape=[jax.ShapeDtypeStruct((rows.n_rows, w), dt) for w, dt in zip(widths, dtypes)],
        compiler_params=_params("arbitrary"),
        name="gla_proj",
    )(x, rows.mods, rows.mods, w_qkv, w_g, w_r, w_g2, b_g)


def _gla_scores_factored(qh, kh, bh, b_last):
    mid = 0.5 * b_last
    q_t = (qh * jnp.exp(bh - mid)).astype(BF16)
    k_t = (kh * jnp.exp(mid - bh)).astype(BF16)
    return _dot_nt(q_t, k_t)


def _gla_scores_elementwise(qh, kh, bh, k_scr, b_scr):
    k_scr[...] = kh
    b_scr[...] = bh
    col = lax.broadcasted_iota(jnp.int32, (GLA_CHUNK, GLA_CHUNK), 1)

    def body(j, acc):
        kj = k_scr[pl.ds(j, 1), :]
        bj = b_scr[pl.ds(j, 1), :]
        term = qh * kj * jnp.exp(jnp.minimum(bh - bj, 0.0))
        return acc + jnp.where(col == j, jnp.sum(term, axis=1, keepdims=True), 0.0)

    return lax.fori_loop(0, GLA_CHUNK, body, jnp.zeros((GLA_CHUNK, GLA_CHUNK), F32))


def _gla_chunk(rows, b, states, score_fn, q_ref, k_ref, v_ref, o_ref, causal):
    q, k = q_ref[rows, :], k_ref[rows, :]
    b_last = b[GLA_CHUNK - 1:GLA_CHUNK, :]
    q_in = q * jnp.exp(b)
    k_out = k * jnp.exp(b_last - b)
    new_states = []
    for hd, s_prev in enumerate(states):
        ks = slice(hd * GLA_DK, (hd + 1) * GLA_DK)
        vs = slice(hd * GLA_DV, (hd + 1) * GLA_DV)
        vh = v_ref[rows, vs].astype(BF16)
        o_inter = _dot(q_in[:, ks].astype(BF16), s_prev.astype(BF16))
        scores = jnp.where(causal, score_fn(q[:, ks], k[:, ks], b[:, ks], b_last[:, ks]), 0.0)
        o_ref[rows, vs] = o_inter + _dot(scores.astype(BF16), vh)
        decay_col = jnp.transpose(jnp.broadcast_to(jnp.exp(b_last[:, ks]), (GLA_DK, GLA_DK)))[:, :1]
        k_out_t = jnp.transpose(k_out[:, ks]).astype(BF16)
        new_states.append(decay_col * s_prev + _dot(k_out_t, vh))
    return new_states


def _gla_rec_kernel(q_ref, k_ref, v_ref, la_ref, s0_ref, o_ref, sfin_ref, s_scr, k_scr, b_scr):
    t = pl.program_id(1)

    @pl.when(t == 0)
    def _():
        s_scr[...] = s0_ref[...]

    row = lax.broadcasted_iota(jnp.int32, (GLA_CHUNK, GLA_CHUNK), 0)
    col = lax.broadcasted_iota(jnp.int32, (GLA_CHUNK, GLA_CHUNK), 1)
    causal = col <= row
    tri = causal.astype(BF16)
    n_chunks = GLA_TILE // GLA_CHUNK

    def decay_prefix(rows):
        rest = la_ref[rows, :]
        total = None
        for _ in range(3):
            piece = rest.astype(BF16)
            rest = rest - piece.astype(F32)
            part = _dot(tri, piece)
            total = part if total is None else total + part
        return total

    chunk_rows = [slice(c * GLA_CHUNK, (c + 1) * GLA_CHUNK) for c in range(n_chunks)]
    prefixes = [decay_prefix(rows) for rows in chunk_rows]
    total_decay = functools.reduce(jnp.maximum, [-b[GLA_CHUNK - 1:GLA_CHUNK, :] for b in prefixes])
    factorable = jnp.max(total_decay) <= GLA_FACTOR_BOUND

    def refs():
        return q_ref, k_ref, v_ref, o_ref, causal

    @pl.when(factorable)
    def _():
        states = [s_scr[hd] for hd in range(GLA_HEADS)]
        for rows, b in zip(chunk_rows, prefixes):
            states = _gla_chunk(rows, b, states, _gla_scores_factored, *refs())
        for hd in range(GLA_HEADS):
            s_scr[hd] = states[hd]

    @pl.when(jnp.logical_not(factorable))
    def _():
        def chunk(c, carry):
            rows = pl.ds(pl.multiple_of(c * GLA_CHUNK, GLA_CHUNK), GLA_CHUNK)
            states = [s_scr[hd] for hd in range(GLA_HEADS)]
            elementwise = functools.partial(_gla_scores_elementwise, k_scr=k_scr, b_scr=b_scr)
            states = _gla_chunk(rows, decay_prefix(rows), states,
                                lambda qh, kh, bh, _: elementwise(qh, kh, bh), *refs())
            for hd in range(GLA_HEADS):
                s_scr[hd] = states[hd]
            return carry

        lax.fori_loop(0, n_chunks, chunk, 0)

    @pl.when(t == pl.num_programs(1) - 1)
    def _():
        sfin_ref[...] = s_scr[...]


def _gla_rec_call(q, k, v, la, s0):
    n_b, n_t = q.shape[:2]

    def seq(width):
        return pl.BlockSpec((None, GLA_TILE, width), lambda b, t: (b, t, 0))

    state = pl.BlockSpec((None, GLA_HEADS, GLA_DK, GLA_DV), lambda b, t: (b, 0, 0, 0))
    return pl.pallas_call(
        _gla_rec_kernel,
        grid=(n_b, n_t // GLA_TILE),
        in_specs=[seq(GLA_QK), seq(GLA_QK), seq(GLA_V), seq(GLA_QK), state],
        out_specs=[seq(GLA_V), state],
        out_shape=[jax.ShapeDtypeStruct((n_b, n_t, GLA_V), F32),
                   jax.ShapeDtypeStruct((n_b, GLA_HEADS, GLA_DK, GLA_DV), F32)],
        scratch_shapes=[pltpu.VMEM((GLA_HEADS, GLA_DK, GLA_DV), F32),
                        pltpu.VMEM((GLA_CHUNK, GLA_DK), F32),
                        pltpu.VMEM((GLA_CHUNK, GLA_DK), F32)],
        compiler_params=_params("arbitrary", "arbitrary"),
        name="gla_recurrence",
    )(q, k, v, la, s0)


def _gla_step_kernel(qt_ref, kt_ref, lat_ref, v_ref, s0_ref, o_ref, sfin_ref):
    n_t = v_ref.shape[0]
    for hd in range(GLA_HEADS):
        vs = slice(hd * GLA_DV, (hd + 1) * GLA_DV)
        s = s0_ref[hd]
        qt, kt, lat = qt_ref[hd], kt_ref[hd], lat_ref[hd]
        for t in range(n_t):
            s = jnp.exp(lat[:, t:t + 1]) * s + kt[:, t:t + 1] * v_ref[t:t + 1, vs]
            o_ref[t:t + 1, vs] = jnp.sum(qt[:, t:t + 1] * s, axis=0, keepdims=True)
        sfin_ref[hd] = s


def _gla_step_call(qt, kt, lat, v, s0):
    n_b, _, _, n_t = qt.shape
    col = pl.BlockSpec((None, GLA_HEADS, GLA_DK, n_t), lambda b: (b, 0, 0, 0))
    tok = pl.BlockSpec((None, n_t, GLA_V), lambda b: (b, 0, 0))
    state = pl.BlockSpec((None, GLA_HEADS, GLA_DK, GLA_DV), lambda b: (b, 0, 0, 0))
    return pl.pallas_call(
        _gla_step_kernel,
        grid=(n_b,),
        in_specs=[col, col, col, tok, state],
        out_specs=[tok, state],
        out_shape=[jax.ShapeDtypeStruct((n_b, n_t, GLA_V), F32),
                   jax.ShapeDtypeStruct((n_b, GLA_HEADS, GLA_DK, GLA_DV), F32)],
        compiler_params=_params("arbitrary"),
        name="gla_step",
    )(qt, kt, lat, v, s0)


def _gla_out_kernel(o_ref, r_ref, x_ref, gt_ref, gon_ref, wout_ref, g_ref, b_ref, y_ref):
    o = o_ref[...]
    parts = []
    for hd in range(GLA_HEADS):
        oh = o[:, hd * GLA_DV:(hd + 1) * GLA_DV]
        ms = jnp.mean(oh * oh, axis=-1, keepdims=True)
        parts.append(oh * lax.rsqrt(ms + GLA_NORM_EPS) * gon_ref[...])
    gated = (jnp.concatenate(parts, axis=1) * r_ref[...].astype(F32)).astype(BF16)
    y = _dot(gated, wout_ref[...])
    z = ALPHA * x_ref[...] + (1.0 + gt_ref[...]) * y
    y_ref[...] = _layer_norm(z, g_ref[...], b_ref[...])


def _gla_out_call(rows, o, r, x, layer, g_onorm, w_out, ln_g, ln_b):
    return pl.pallas_call(
        _gla_out_kernel,
        grid=(rows.steps,),
        in_specs=[
            rows.row_spec(GLA_V), rows.row_spec(GLA_V), rows.row_spec(D_MODEL),
            rows.mod_spec(layer, 5),
            _const_spec((1, GLA_DV), (0, 0)),
            _const_spec((GLA_V, D_MODEL), (0, 0)),
        ] + _ln_specs(layer, 1),
        out_specs=rows.row_spec(D_MODEL),
        out_shape=jax.ShapeDtypeStruct((rows.n_rows, D_MODEL), F32),
        compiler_params=_params("arbitrary"),
        name="gla_out",
    )(o, r, x, rows.mods, g_onorm, w_out, ln_g, ln_b)


def _gather_stride(scr, start, count, stride):
    return jnp.concatenate([scr[c, pl.ds(start, count, stride=stride), :] for c in range(scr.shape[0])], axis=1)


def _fill_slabs(scr, row0, x):
    for c in range(scr.shape[0]):
        scr[c, row0:row0 + x.shape[0], :] = x[:, c * LANES:(c + 1) * LANES]


def _store_by_residue(x, out_refs, scr_a, scr_b):
    n_rows = x.shape[0]
    s = DIL_DILATIONS[1]
    assert DIL_DILATIONS == (1, s, s * s)
    outs = {o.shape[0]: o for o in out_refs}
    if 1 in outs:
        outs[1][0] = x.astype(BF16)
    if s not in outs and s * s not in outs:
        return
    _fill_slabs(scr_a, 0, x)
    per_class = n_rows // s
    for r in range(s):
        xr = _gather_stride(scr_a, r, per_class, s)
        if s in outs:
            outs[s][r] = xr.astype(BF16)
        if s * s in outs:
            _fill_slabs(scr_b, r * per_class, xr)
    if s * s in outs:
        for r in range(s * s):
            xr = _gather_stride(scr_b, (r % s) * per_class + r // s, per_class // s, s)
            outs[s * s][r] = xr.astype(BF16)


def _kv_kernel(by_residue, x_ref, sh_ref, sc_ref, cos_ref, sin_ref, w_ref, k_ref, v_ref, *rest):
    h = (x_ref[...] * (1.0 + sc_ref[...]) + sh_ref[...]).astype(BF16)
    kv = _dot(h, w_ref[...])
    k = _rope(kv[:, :D_MODEL], cos_ref[...], sin_ref[...])
    v = kv[:, D_MODEL:]
    if by_residue:
        k_ref[...] = jnp.transpose(k)
        v_ref[...] = jnp.transpose(v)
    else:
        k_ref[...] = k.reshape(k_ref.shape)
        v_ref[...] = v.reshape(v_ref.shape)
    if by_residue:
        _store_by_residue(k, rest[0:N_GROUPS], *rest[-2:])
        _store_by_residue(v, rest[N_GROUPS:2 * N_GROUPS], *rest[-2:])


def _rope_spec(rows):
    if rows.per_row:
        return pl.BlockSpec((rows.tile, LANES), lambda i: (0, 0))
    tps = rows.tiles_per_sample
    return pl.BlockSpec((rows.tile, LANES), lambda i: (i % tps, 0))


def _kv_mod_spec(rows, k):
    if rows.per_row:
        return pl.BlockSpec((None, rows.tile, D_MODEL), lambda i: (0, 0, k))
    tps = rows.tiles_per_sample
    return pl.BlockSpec((None, None, 1, D_MODEL), lambda i: (0, i // tps, 0, k))


def _residue_spec(rows, dil, width):
    tps = rows.tiles_per_sample
    return pl.BlockSpec((None, dil, rows.tile // dil, width), lambda i: (i // tps, 0, i % tps, 0))


def _residue_shapes(rows, width, dtype):
    n_b = rows.n_rows // rows.rows_per_sample
    return [jax.ShapeDtypeStruct((n_b, dil, rows.rows_per_sample // dil, width), dtype) for dil in DIL_DILATIONS]


def _permute_scratch():
    return [pltpu.VMEM((D_MODEL // LANES, ROW_TILE, LANES), F32)] * 2


def _kv_call(rows, x, cos_t, sin_t, w_kv):
    by_residue = not rows.per_row
    scratch = []
    if by_residue:
        tps = rows.tiles_per_sample
        n_b = rows.n_rows // rows.rows_per_sample
        out_specs = [pl.BlockSpec((None, D_MODEL, rows.tile), lambda i: (i // tps, 0, i % tps))] * 2
        out_shape = [jax.ShapeDtypeStruct((n_b, D_MODEL, rows.rows_per_sample), F32)] * 2
    else:
        out_specs = [pl.BlockSpec((rows.tile, DIL_HEADS, DIL_HD), lambda i: (i, 0, 0))] * 2
        out_shape = [jax.ShapeDtypeStruct((rows.n_rows, DIL_HEADS, DIL_HD), F32)] * 2
    if by_residue:
        out_specs += [_residue_spec(rows, dil, D_MODEL) for dil in DIL_DILATIONS] * 2
        out_shape += _residue_shapes(rows, D_MODEL, BF16) * 2
        scratch = _permute_scratch()
    return pl.pallas_call(
        functools.partial(_kv_kernel, by_residue),
        grid=(rows.steps,),
        in_specs=[
            rows.row_spec(D_MODEL),
            _kv_mod_spec(rows, 0), _kv_mod_spec(rows, 1),
            _rope_spec(rows), _rope_spec(rows),
            _const_spec((D_MODEL, 2 * D_MODEL), (0, 0)),
        ],
        out_specs=out_specs,
        out_shape=out_shape,
        scratch_shapes=scratch,
        compiler_params=_params("arbitrary"),
        name="kv_proj",
    )(x, rows.kv_mods, rows.kv_mods, cos_t, sin_t, w_kv)


def _q_kernel(by_residue, x_ref, sh_ref, sc_ref, cos_ref, sin_ref, w_ref, *rest):
    h = (x_ref[...] * (1.0 + sc_ref[...]) + sh_ref[...]).astype(BF16)
    q = _rope(_dot(h, w_ref[...]), cos_ref[...], sin_ref[...])
    if by_residue:
        q = q * (DIL_SCALE * LOG2_E)
        for g in range(N_GROUPS):
            _store_by_residue(q[:, g * D_MODEL:(g + 1) * D_MODEL], rest[g:g + 1], *rest[-2:])
    else:
        rest[0][...] = (q * DIL_SCALE).reshape(rest[0].shape)


def _q_call(rows, x, layer, cos_t, sin_t, w_q):
    width = N_GROUPS * D_MODEL
    by_residue = not rows.per_row
    if by_residue:
        out_specs = [_residue_spec(rows, dil, D_MODEL) for dil in DIL_DILATIONS]
        out_shape = _residue_shapes(rows, D_MODEL, BF16)
        scratch = _permute_scratch()
    else:
        out_specs = pl.BlockSpec((rows.tile, N_GROUPS * DIL_HEADS, DIL_HD), lambda i: (i, 0, 0))
        out_shape = jax.ShapeDtypeStruct((rows.n_rows, N_GROUPS * DIL_HEADS, DIL_HD), F32)
        scratch = []
    return pl.pallas_call(
        functools.partial(_q_kernel, by_residue),
        grid=(rows.steps,),
        in_specs=[
            rows.row_spec(D_MODEL),
            rows.mod_spec(layer, 3), rows.mod_spec(layer, 4),
            _rope_spec(rows), _rope_spec(rows),
            _const_spec((D_MODEL, width), (0, 0)),
        ],
        out_specs=out_specs,
        out_shape=out_shape,
        scratch_shapes=scratch,
        compiler_params=_params("arbitrary"),
        name="q_proj",
    )(x, rows.mods, rows.mods, cos_t, sin_t, w_q)


def _pair_softmax_pv(scores, v_pair, even_lane):
    half = scores.shape[0] // 2
    m = jnp.max(scores, axis=-1, keepdims=True)
    e = jnp.exp2(scores - m)
    den = jnp.sum(e, axis=-1, keepdims=True)
    o2 = _dot(e.astype(BF16), v_pair)
    o = jnp.where(even_lane, o2[:half], o2[half:]) / jnp.where(even_lane, den[:half], den[half:])
    lse2 = (m + jnp.log2(den)) * LN_2
    return o, lse2[:half], lse2[half:]


def _dil_prompt_kernel(has_prev, dil, q_ref, *refs):
    if has_prev:
        kp_ref, kc_ref, vp_ref, vc_ref, o_ref, lse_ref = refs
    else:
        kc_ref, vc_ref, o_ref, lse_ref = refs
    blk = pl.program_id(2)
    first_token = blk * (ATT_BLOCK * dil) + pl.program_id(1)
    if dil == 1:
        token_rows = pl.ds(pl.multiple_of(first_token, ATT_BLOCK), ATT_BLOCK)
    else:
        token_rows = pl.ds(first_token, ATT_BLOCK, stride=dil)
    n_keys = (2 if has_prev else 1) * ATT_BLOCK
    lane = lax.broadcasted_iota(jnp.int32, (ATT_BLOCK, LANES), 1)
    even_lane = lane < DIL_HD
    qi = lax.broadcasted_iota(jnp.int32, (2 * ATT_BLOCK, n_keys), 0) % ATT_BLOCK
    kj = lax.broadcasted_iota(jnp.int32, (2 * ATT_BLOCK, n_keys), 1)
    if has_prev:
        dist = qi + ATT_BLOCK - kj
        valid = (dist >= 0) & (dist <= DIL_STEPS) & ((kj >= ATT_BLOCK) | (blk > 0))
    else:
        valid = kj <= qi
    lse_all = jnp.zeros((ATT_BLOCK, LANES), F32)
    for p in range(DIL_HEADS // 2):
        sl = slice(p * LANES, (p + 1) * LANES)
        qp = q_ref[:, sl]
        zero = jnp.zeros_like(qp)
        q2 = jnp.concatenate([jnp.where(even_lane, qp, zero), jnp.where(even_lane, zero, qp)], axis=0)
        if has_prev:
            k_pair = jnp.concatenate([kp_ref[:, sl], kc_ref[:, sl]], axis=0)
            v_pair = jnp.concatenate([vp_ref[:, sl], vc_ref[:, sl]], axis=0)
        else:
            k_pair, v_pair = kc_ref[:, sl], vc_ref[:, sl]
        scores = jnp.where(valid, _dot_nt(q2, k_pair), -jnp.inf)
        o, lse_even, lse_odd = _pair_softmax_pv(scores, v_pair, even_lane)
        o_ref[p, token_rows, :] = o
        lse_all = jnp.where(lane == 2 * p, lse_even, jnp.where(lane == 2 * p + 1, lse_odd, lse_all))
    lse_ref[token_rows, :] = lse_all


def _dil_prompt_call(q, kb, vb, group):
    n_b, dil, length, _ = kb.shape
    n_t = dil * length
    n_blk = length // ATT_BLOCK
    has_prev = n_blk > 1
    blk_shape = (None, None, ATT_BLOCK, D_MODEL)
    cur = pl.BlockSpec(blk_shape, lambda b, r, i: (b, r, i, 0))
    prev = pl.BlockSpec(blk_shape, lambda b, r, i: (b, r, jnp.maximum(i - 1, 0), 0))
    if has_prev:
        in_specs, operands = [cur, prev, cur, prev, cur], (q, kb, kb, vb, vb)
    else:
        in_specs, operands = [cur, cur, cur], (q, kb, vb)
    return pl.pallas_call(
        functools.partial(_dil_prompt_kernel, has_prev, dil),
        grid=(n_b, dil, n_blk),
        in_specs=in_specs,
        out_specs=[pl.BlockSpec((None, D_MODEL // LANES, n_t, LANES), lambda b, r, i: (b, 0, 0, 0)),
                   pl.BlockSpec((None, n_t, LANES), lambda b, r, i: (b, 0, 0))],
        out_shape=[jax.ShapeDtypeStruct((n_b, D_MODEL // LANES, n_t, LANES), F32),
                   jax.ShapeDtypeStruct((n_b, n_t, LANES), F32)],
        compiler_params=_params("arbitrary", "arbitrary", "arbitrary"),
        name="dilated_prompt",
    )(*operands)


SAMPLE_HEAD_CHUNK = 4


def _sample_group_dense(t, q_col, k_win, v_win, k_new, v_new):
    n_t = k_new.shape[-1]
    pos = lax.broadcasted_iota(jnp.int32, (1, 1, k_win.shape[-1]), 2)
    pos_new = lax.broadcasted_iota(jnp.int32, (1, 1, n_t), 2)
    s_c = jnp.where(pos >= t, jnp.sum(k_win * q_col, axis=1, keepdims=True), -jnp.inf)
    s_n = jnp.where(pos_new <= t, jnp.sum(k_new * q_col, axis=1, keepdims=True), -jnp.inf)
    m = jnp.maximum(jnp.max(s_c, axis=2, keepdims=True), jnp.max(s_n, axis=2, keepdims=True))
    e_c, e_n = jnp.exp(s_c - m), jnp.exp(s_n - m)
    den = jnp.sum(e_c, axis=2, keepdims=True) + jnp.sum(e_n, axis=2, keepdims=True)
    o = (jnp.sum(v_win * e_c, axis=2, keepdims=True) + jnp.sum(v_new * e_n, axis=2, keepdims=True)) / den
    return o, m + jnp.log(den)


def _sample_group_dilated(dil, q_g, k_win, v_win, k_new, v_new):
    n_t = q_g.shape[-1]
    width = k_win.shape[-1]
    n_col = width // LANES
    res = lax.broadcasted_iota(jnp.int32, (1, 1, LANES), 2) % dil
    res_w = lax.broadcasted_iota(jnp.int32, (1, 1, width), 2) % dil
    q_pat = jnp.zeros(q_g.shape[:2] + (LANES,), F32)
    for t in range(n_t):
        q_pat = jnp.where(res == t, q_g[:, :, t:t + 1], q_pat)
    s_row = jnp.sum(k_win * jnp.concatenate([q_pat] * n_col, axis=2), axis=1, keepdims=True)
    s_new = jnp.sum(k_new * q_g, axis=1, keepdims=True)
    m_t = [jnp.maximum(jnp.max(jnp.where(res_w == t, s_row, -jnp.inf), axis=2, keepdims=True), s_new[:, :, t:t + 1])
           for t in range(n_t)]
    m_row = jnp.zeros_like(s_row)
    for t in range(n_t):
        m_row = jnp.where(res_w == t, m_t[t], m_row)
    e = jnp.exp(jnp.where(res_w < n_t, s_row, -jnp.inf) - m_row)
    acc = v_win * e
    acc_fold = sum(acc[:, :, c * LANES:(c + 1) * LANES] for c in range(n_col))
    e_fold = sum(e[:, :, c * LANES:(c + 1) * LANES] for c in range(n_col))
    out = []
    for t in range(n_t):
        e_new = jnp.exp(s_new[:, :, t:t + 1] - m_t[t])
        den = jnp.sum(jnp.where(res == t, e_fold, 0.0), axis=2, keepdims=True) + e_new
        o = (jnp.sum(jnp.where(res == t, acc_fold, 0.0), axis=2, keepdims=True) + e_new * v_new[:, :, t:t + 1]) / den
        out.append((o, m_t[t] + jnp.log(den)))
    return out


def _sample_attention(q_ref, k_ref, v_ref, kn_ref, vn_ref, o_ref):
    n_t = q_ref.shape[-1]
    n_s = k_ref.shape[-1]
    pos_new = lax.broadcasted_iota(jnp.int32, (1, 1, n_t), 2)
    k_new, v_new = kn_ref[...], vn_ref[...]
    groups = []
    for g, dil in enumerate(DIL_DILATIONS):
        lo = n_s - DIL_STEPS * dil
        k_win, v_win = k_ref[:, :, lo:], v_ref[:, :, lo:]
        if dil == 1:
            groups.append([_sample_group_dense(t, q_ref[g, :, :, t:t + 1], k_win, v_win, k_new, v_new)
                           for t in range(n_t)])
        else:
            groups.append(_sample_group_dilated(dil, q_ref[g], k_win, v_win, k_new, v_new))
    merged = jnp.zeros(k_new.shape, F32)
    for t in range(n_t):
        parts = [grp[t] for grp in groups]
        m = functools.reduce(jnp.maximum, [lse for _, lse in parts])
        wts = [jnp.exp(lse - m) for _, lse in parts]
        o = sum(wt * o for wt, (o, _) in zip(wts, parts)) / sum(wts)
        merged = jnp.where(pos_new == t, o, merged)
    o_ref[...] = merged


class _SampleAttentionGuest:
    def __init__(self, q, cache_k, cache_v, k_new, v_new):
        self.operands = (q, cache_k, cache_v, k_new, v_new)
        self.n_b, self.n_t = q.shape[0], q.shape[-1]
        wb = cache_k.shape[-1]
        assert wb == DIL_STEPS * DIL_DILATIONS[-1] and DIL_DILATIONS[0] == 1 and self.n_t <= DIL_DILATIONS[1]
        assert all(wb % dil == 0 and LANES % dil == 0 for dil in DIL_DILATIONS)
        self.wb = wb
        self.outputs = []

    def specs(self, chunk):
        hc = SAMPLE_HEAD_CHUNK
        cache = pl.BlockSpec((None, hc, DIL_HD, self.wb), lambda i: (i, chunk, 0, 0))
        new = pl.BlockSpec((None, hc, DIL_HD, self.n_t), lambda i: (i, chunk, 0, 0))
        q = pl.BlockSpec((None, N_GROUPS, hc, DIL_HD, self.n_t), lambda i: (i, 0, chunk, 0, 0))
        out_shape = jax.ShapeDtypeStruct((self.n_b, hc, DIL_HD, self.n_t), F32)
        return [q, cache, cache, new, new], pl.BlockSpec((None, hc, DIL_HD, self.n_t), lambda i: (i, 0, 0, 0)), out_shape

    def result(self):
        assert len(self.outputs) * SAMPLE_HEAD_CHUNK == DIL_HEADS
        return jnp.concatenate(self.outputs, axis=1)


def _att_merge(o_refs, lse_refs):
    lses = [l_ref[...] for l_ref in lse_refs]
    m = functools.reduce(jnp.maximum, lses)
    e = [jnp.exp(l - m) for l in lses]
    den = sum(e)
    head = lax.broadcasted_iota(jnp.int32, (LANES, D_MODEL), 0)
    col_head = lax.broadcasted_iota(jnp.int32, (LANES, D_MODEL), 1) // DIL_HD
    spread = (head == col_head).astype(BF16)
    merged = None
    for o_ref, eg in zip(o_refs, e):
        wt = eg / den
        wt_hi = wt.astype(BF16)
        wt_lo = (wt - wt_hi.astype(F32)).astype(BF16)
        wide = _dot(wt_hi, spread) + _dot(wt_lo, spread)
        part = wide * jnp.concatenate([o_ref[c] for c in range(o_ref.shape[0])], axis=1)
        merged = part if merged is None else merged + part
    return merged


def _att_out_kernel(merged_input, *refs):
    if merged_input:
        o_ref, x_ref, gt_ref, wout_ref, g_ref, b_ref, y_ref = refs
        o = o_ref[...]
    else:
        o_refs, lse_refs = refs[0:N_GROUPS], refs[N_GROUPS:2 * N_GROUPS]
        x_ref, gt_ref, wout_ref, g_ref, b_ref, y_ref = refs[2 * N_GROUPS:]
        o = _att_merge(o_refs, lse_refs)
    y = _dot(o.astype(BF16), wout_ref[...])
    z = ALPHA * x_ref[...] + (1.0 + gt_ref[...]) * y
    y_ref[...] = _layer_norm(z, g_ref[...], b_ref[...])


def _att_out_call(rows, att, x, layer, w_out, ln_g, ln_b):
    merged_input = rows.per_row
    if merged_input:
        att_specs = [rows.row_spec(D_MODEL)]
        operands = [att]
    else:
        outs, lses = att
        tps = rows.tiles_per_sample
        o_spec = pl.BlockSpec((None, D_MODEL // LANES, rows.tile, LANES), lambda i: (i // tps, 0, i % tps, 0))
        lse_spec = pl.BlockSpec((None, rows.tile, LANES), lambda i: (i // tps, i % tps, 0))
        att_specs = [o_spec] * N_GROUPS + [lse_spec] * N_GROUPS
        operands = list(outs) + list(lses)
    return pl.pallas_call(
        functools.partial(_att_out_kernel, merged_input),
        grid=(rows.steps,),
        in_specs=att_specs + [
            rows.row_spec(D_MODEL),
            rows.mod_spec(layer, 5),
            _const_spec((D_MODEL, D_MODEL), (0, 0)),
        ] + _ln_specs(layer, 1),
        out_specs=rows.row_spec(D_MODEL),
        out_shape=jax.ShapeDtypeStruct((rows.n_rows, D_MODEL), F32),
        compiler_params=_params("arbitrary"),
        name="att_out",
    )(*operands, x, rows.mods, w_out, ln_g, ln_b)


def _rope_tables(positions):
    half = DIL_HD // 2
    inv = ROPE_THETA ** (-jnp.arange(half, dtype=F32) / half)
    ang = positions.astype(F32)[:, None] * inv[None, :]
    cos, sin = jnp.cos(ang), jnp.sin(ang)
    return jnp.concatenate([cos, cos, cos, cos], axis=1), jnp.concatenate([-sin, sin, -sin, sin], axis=1)


def _trunk(rows, ffn, x, n_b, n_t, positions, gla_state, w):
    cos_t, sin_t = _rope_tables(positions)
    if rows.per_row:
        cos_t, sin_t = jnp.tile(cos_t, (n_b, 1)), jnp.tile(sin_t, (n_b, 1))

    x = ffn(x, 0, 0, 0, w['up1'], w['down1'], w['ln_g'], w['ln_b'])
    q, k, v, la, r = _gla_proj_call(rows, x, 0, w['gla_qkv'], w['gla_g'], w['gla_r'], w['gla_g2'], w['gla_bg'])
    if rows.per_row:
        def cols(a):
            return a.reshape(n_b, n_t, GLA_HEADS, GLA_DK).transpose(0, 2, 3, 1)
        o, s_new = _gla_step_call(cols(q), cols(k), cols(la), v.reshape(n_b, n_t, GLA_V), gla_state)
    else:
        o, s_new = _gla_rec_call(q.reshape(n_b, n_t, GLA_QK), k.reshape(n_b, n_t, GLA_QK),
                                 v.reshape(n_b, n_t, GLA_V), la.reshape(n_b, n_t, GLA_QK), gla_state)
    x = _gla_out_call(rows, o.reshape(n_b * n_t, GLA_V), r, x, 0, w['gla_gon'], w['gla_out'], w['ln_g'], w['ln_b'])
    x = ffn(x, 0, 6, 2, w['up2'], w['down2'], w['ln_g'], w['ln_b'])

    k_rows, v_rows, *kv_groups = _kv_call(rows, x, cos_t, sin_t, w['kv'])
    if rows.per_row:
        k_rows, v_rows = (a.reshape(n_b, n_t, DIL_HEADS, DIL_HD) for a in (k_rows, v_rows))
    else:
        k_rows, v_rows = (a.reshape(n_b, DIL_HEADS, DIL_HD, n_t).transpose(0, 3, 1, 2) for a in (k_rows, v_rows))

    x = ffn(x, 1, 0, 0, w['up1'], w['down1'], w['ln_g'], w['ln_b'])
    qd = _q_call(rows, x, 1, cos_t, sin_t, w['q'])
    att = yield qd, k_rows, v_rows, kv_groups
    x = _att_out_call(rows, att, x, 1, w['att_out'], w['ln_g'], w['ln_b'])
    x = ffn(x, 1, 6, 2, w['up2'], w['down2'], w['ln_g'], w['ln_b'])
    return x, s_new, k_rows, v_rows


def _finish(trunk, att):
    try:
        trunk.send(att)
    except StopIteration as done:
        return done.value
    raise AssertionError("trunk yielded twice")


def kernel(x_prompt, x_sample, state_gla, cache_k, cache_v, c_prompt, c_sample, w_ada, b_ada, ln_g, ln_b, w_ffn1_up, w_ffn1_down, w_ffn2_up, w_ffn2_down, w_in_a, w_gate2_a, b_gate_a, g_onorm_a, w_out_a, w_ada_kv, b_ada_kv, w_kv, w_q_b, w_out_b):
    n_bp, n_tp = x_prompt.shape[:2]
    n_bs, n_ts = x_sample.shape[:2]
    assert N_A_LAYERS == 1 and DEPTH == 2
    assert n_tp % (ATT_BLOCK * DIL_DILATIONS[-1]) == 0 and n_tp % GLA_TILE == 0 and n_tp <= MAX_WINDOW
    assert n_ts < 8

    c_all = jnp.concatenate([c_prompt, c_sample], axis=0)
    mods = _ada_call(c_all, w_ada, b_ada, 1536)
    kv_mods = _ada_call(c_all, w_ada_kv[None], b_ada_kv[None], 1024)
    rows_p = _Rows(n_bp * n_tp, n_tp, mods[:, :n_bp, None, :], kv_mods[:, :n_bp, None, :])
    rows_s = _Rows(n_bs * n_ts, n_ts, jnp.repeat(mods[:, n_bp:], n_ts, axis=1),
                   jnp.repeat(kv_mods[:, n_bp:], n_ts, axis=1))

    w_in = w_in_a[0]
    gate_lo = 2 * GLA_QK + GLA_V
    w = {
        'up1': w_ffn1_up.astype(BF16), 'down1': w_ffn1_down.astype(BF16),
        'up2': w_ffn2_up.astype(BF16), 'down2': w_ffn2_down.astype(BF16),
        'ln_g': ln_g[:, :, None, :], 'ln_b': ln_b[:, :, None, :],
        'gla_qkv': w_in[:, :gate_lo].astype(BF16),
        'gla_g': jnp.pad(w_in[:, gate_lo:gate_lo + GLA_RANK], ((0, 0), (0, LANES - GLA_RANK))).astype(BF16),
        'gla_r': w_in[:, gate_lo + GLA_RANK:].astype(BF16),
        'gla_g2': jnp.pad(w_gate2_a[0], ((0, LANES - GLA_RANK), (0, 0))).astype(BF16),
        'gla_bg': b_gate_a[0][None, :],
        'gla_gon': g_onorm_a[0][None, :],
        'gla_out': w_out_a[0].astype(BF16),
        'kv': w_kv.astype(BF16),
        'q': w_q_b[0].astype(BF16),
        'att_out': w_out_b[0].astype(BF16),
    }

    trunk_s = _trunk(rows_s, functools.partial(_ffn_call, rows_s), x_sample.reshape(-1, D_MODEL), n_bs, n_ts,
                     PAST_LEN + jnp.arange(n_ts), state_gla[0], w)
    q_s, k_s, v_s, _ = next(trunk_s)
    guest = _SampleAttentionGuest(
        q_s.reshape(n_bs, n_ts, N_GROUPS, DIL_HEADS, DIL_HD).transpose(0, 2, 3, 4, 1),
        cache_k.transpose(0, 2, 3, 1), cache_v.transpose(0, 2, 3, 1),
        k_s.transpose(0, 2, 3, 1), v_s.transpose(0, 2, 3, 1))

    zero_state = jnp.zeros((n_bp, GLA_HEADS, GLA_DK, GLA_DV), F32)
    trunk_p = _trunk(rows_p, functools.partial(_ffn_call, rows_p, guest=guest), x_prompt.reshape(-1, D_MODEL),
                     n_bp, n_tp, jnp.arange(n_tp), zero_state, w)
    q_p, _, _, kv_p = next(trunk_p)
    att_p = tuple(zip(*[_dil_prompt_call(q_p[g], kv_p[g], kv_p[N_GROUPS + g], g) for g in range(N_GROUPS)]))
    y_p, s_p, k_p, v_p = _finish(trunk_p, att_p)
    att_s = guest.result().transpose(0, 3, 1, 2).reshape(n_bs * n_ts, D_MODEL)
    y_s, s_s, k_s, v_s = _finish(trunk_s, att_s)
    return (y_p.reshape(n_bp, n_tp, D_MODEL), y_s.reshape(n_bs, n_ts, D_MODEL),
            s_p[None], s_s[None], k_p, v_p, k_s, v_s)
```

```python
import functools

import jax
import jax.numpy as jnp
from jax import lax
from jax.experimental import pallas as pl
from jax.experimental.pallas import tpu as pltpu

F32 = jnp.float32
BF16 = jnp.bfloat16

D_MODEL = 1024
DEPTH = 2
PAST_LEN = 16384
N_A_LAYERS = DEPTH // 2
GLA_HEADS = 4
GLA_QK = D_MODEL // 2
GLA_V = D_MODEL
GLA_DK = GLA_QK // GLA_HEADS
GLA_DV = GLA_V // GLA_HEADS
GLA_RANK = 16
GLA_TAU = 16.0
GLA_NORM_EPS = 1e-6
DIL_HEADS = 16
DIL_HD = D_MODEL // DIL_HEADS
DIL_WINDOWS = (128, 512, 2048)
DIL_DILATIONS = (1, 4, 16)
N_GROUPS = len(DIL_WINDOWS)
DIL_STEPS = 128
MAX_WINDOW = max(DIL_WINDOWS)
DIL_SCALE = DIL_HD ** -0.5
ROPE_THETA = 10000.0
D_FF = 2816
FFN_RES = 0.5
ALPHA = (2 * DEPTH) ** 0.25
LN_EPS = 1e-5
N_MOD = 9

LANES = 128
ROW_TILE = 512
WIDE_TILE = 1024
LOG2_E = 1.4426950408889634
LN_2 = 0.6931471805599453
FF_CHUNK = 256
GLA_CHUNK = 128
GLA_TILE = 512
GLA_FACTOR_BOUND = 80.0
ATT_BLOCK = DIL_STEPS
VMEM_LIMIT = 56 * 1024 * 1024


def _params(*sem):
    return pltpu.CompilerParams(dimension_semantics=sem, vmem_limit_bytes=VMEM_LIMIT)


def _silu(x):
    return x / (1.0 + jnp.exp(-x))


def _layer_norm(z, g, b):
    mu = jnp.mean(z, axis=-1, keepdims=True)
    zc = z - mu
    var = jnp.mean(zc * zc, axis=-1, keepdims=True)
    return zc * lax.rsqrt(var + LN_EPS) * g + b


def _dot(a, b):
    return jnp.dot(a, b, preferred_element_type=F32)


def _dot_nt(a, b):
    return lax.dot_general(a, b, (((1,), (1,)), ((), ())), preferred_element_type=F32)


def _rope(x, cos_t, sin_t):
    width = x.shape[1]
    lane = lax.broadcasted_iota(jnp.int32, x.shape, 1)
    first_half = (lane % DIL_HD) < (DIL_HD // 2)
    partner = jnp.where(first_half, pltpu.roll(x, width - DIL_HD // 2, 1), pltpu.roll(x, DIL_HD // 2, 1))
    reps = width // LANES
    cos_w = jnp.concatenate([cos_t] * reps, axis=1)
    sin_w = jnp.concatenate([sin_t] * reps, axis=1)
    return x * cos_w + partner * sin_w


def _ada_kernel(c_ref, w_ref, b_ref, o_ref):
    s = _silu(c_ref[...]).astype(BF16)
    o_ref[...] = _dot(s, w_ref[...].astype(BF16)) + b_ref[...]


def _ada_call(c_all, w, b, tn):
    n_l, _, n = w.shape
    rows = c_all.shape[0]
    return pl.pallas_call(
        _ada_kernel,
        grid=(n_l, n // tn),
        in_specs=[
            pl.BlockSpec((rows, D_MODEL), lambda l, j: (0, 0)),
            pl.BlockSpec((None, D_MODEL, tn), lambda l, j: (l, 0, j)),
            pl.BlockSpec((None, 1, tn), lambda l, j: (l, 0, j)),
        ],
        out_specs=pl.BlockSpec((None, rows, tn), lambda l, j: (l, 0, j)),
        out_shape=jax.ShapeDtypeStruct((n_l, rows, n), F32),
        compiler_params=_params("arbitrary", "arbitrary"),
        name="ada_mod",
    )(c_all, w, b.reshape(n_l, 1, n))


class _Rows:
    def __init__(self, n_rows, rows_per_sample, mods, kv_mods, tile=ROW_TILE):
        self.n_rows = n_rows
        self.rows_per_sample = rows_per_sample
        self.per_row = rows_per_sample < 8
        self.tile = n_rows if self.per_row else min(tile, rows_per_sample)
        self.steps = n_rows // self.tile
        self.tiles_per_sample = 1 if self.per_row else rows_per_sample // self.tile
        self.mods = mods
        self.kv_mods = kv_mods

    def wide(self):
        return _Rows(self.n_rows, self.rows_per_sample, self.mods, self.kv_mods, WIDE_TILE)

    def row_spec(self, width):
        return pl.BlockSpec((self.tile, width), lambda i: (i, 0))

    def mod_spec(self, layer, k):
        if self.per_row:
            return pl.BlockSpec((None, self.tile, D_MODEL), lambda i: (layer, 0, k))
        tps = self.tiles_per_sample
        return pl.BlockSpec((None, None, 1, D_MODEL), lambda i: (layer, i // tps, 0, k))


def _const_spec(shape, index):
    return pl.BlockSpec(shape, lambda i: index, pipeline_mode=pl.Buffered(1))


def _ln_specs(layer, j):
    return [_const_spec((None, None, 1, D_MODEL), (layer, j, 0, 0))] * 2


def _ffn_kernel(has_guest, x_ref, sh_ref, sc_ref, gt_ref, wup_ref, wdn_ref, g_ref, b_ref, *rest):
    if has_guest:
        *guest_in, o_ref, guest_out = rest
        _sample_attention(*guest_in, guest_out)
    else:
        o_ref, = rest
    x = x_ref[...]
    h = (x * (1.0 + sc_ref[...]) + sh_ref[...]).astype(BF16)
    y = None
    for lo in range(0, D_FF, FF_CHUNK):
        a = _dot(h, wup_ref[:, lo:lo + FF_CHUNK])
        u = _dot(h, wup_ref[:, D_FF + lo:D_FF + lo + FF_CHUNK])
        part = _dot((_silu(a) * u).astype(BF16), wdn_ref[lo:lo + FF_CHUNK, :])
        y = part if y is None else y + part
    z = ALPHA * x + (1.0 + gt_ref[...]) * (FFN_RES * y)
    o_ref[...] = _layer_norm(z, g_ref[...], b_ref[...])


def _ffn_call(rows, x, layer, mod0, ln_j, w_up, w_down, ln_g, ln_b, guest=None):
    in_specs = [
        rows.row_spec(D_MODEL),
        rows.mod_spec(layer, mod0), rows.mod_spec(layer, mod0 + 1), rows.mod_spec(layer, mod0 + 2),
        _const_spec((None, D_MODEL, 2 * D_FF), (layer, 0, 0)),
        _const_spec((None, D_FF, D_MODEL), (layer, 0, 0)),
    ] + _ln_specs(layer, ln_j)
    operands = [x, rows.mods, rows.mods, rows.mods, w_up, w_down, ln_g, ln_b]
    out_specs = [rows.row_spec(D_MODEL)]
    out_shape = [jax.ShapeDtypeStruct((rows.n_rows, D_MODEL), F32)]
    if guest is not None:
        assert guest.n_b == rows.steps
        guest_in, guest_out, guest_shape = guest.specs(len(guest.outputs))
        in_specs += guest_in
        operands += guest.operands
        out_specs.append(guest_out)
        out_shape.append(guest_shape)
    outs = pl.pallas_call(
        functools.partial(_ffn_kernel, guest is not None),
        grid=(rows.steps,),
        in_specs=in_specs,
        out_specs=out_specs,
        out_shape=out_shape,
        compiler_params=_params("arbitrary"),
        name="ffn",
    )(*operands)
    if guest is not None:
        guest.outputs.append(outs[1])
    return outs[0]


def _gla_proj_kernel(x_ref, sh_ref, sc_ref, wqkv_ref, wg_ref, wr_ref, wg2_ref, bg_ref,
                     q_ref, k_ref, v_ref, la_ref, r_ref):
    h = (x_ref[...] * (1.0 + sc_ref[...]) + sh_ref[...]).astype(BF16)
    qkv = _dot(h, wqkv_ref[...])
    q_ref[...] = qkv[:, :GLA_QK] * (GLA_DK ** -0.5)
    k_ref[...] = qkv[:, GLA_QK:2 * GLA_QK]
    v_ref[...] = qkv[:, 2 * GLA_QK:].astype(v_ref.dtype)
    g_lr = _dot(h, wg_ref[...]).astype(BF16)
    gate = _dot(g_lr, wg2_ref[...]) + bg_ref[...]
    log_sig = jnp.minimum(gate, 0.0) - jnp.log(1.0 + jnp.exp(-jnp.abs(gate)))
    la_ref[...] = log_sig / GLA_TAU
    r_ref[...] = _silu(_dot(h, wr_ref[...])).astype(r_ref.dtype)


def _gla_proj_call(rows, x, layer, w_qkv, w_g, w_r, w_g2, b_g):
    widths = (GLA_QK, GLA_QK, GLA_V, GLA_QK, GLA_V)
    narrow = F32 if rows.per_row else BF16
    dtypes = (F32, F32, narrow, F32, narrow)
    return pl.pallas_call(
        _gla_proj_kernel,
        grid=(rows.steps,),
        in_specs=[
            rows.row_spec(D_MODEL),
            rows.mod_spec(layer, 3), rows.mod_spec(layer, 4),
            _const_spec((D_MODEL, 2 * GLA_QK + GLA_V), (0, 0)),
            _const_spec((D_MODEL, LANES), (0, 0)),
            _const_spec((D_MODEL, GLA_V), (0, 0)),
            _const_spec((LANES, GLA_QK), (0, 0)),
            _const_spec((1, GLA_QK), (0, 0)),
        ],
        out_specs=[rows.row_spec(w) for w in widths],
        out_shape=[jax.ShapeDtypeStruct((rows.n_rows, w), dt) for w, dt in zip(widths, dtypes)],
        compiler_params=_params("arbitrary"),
        name="gla_proj",
    )(x, rows.mods, rows.mods, w_qkv, w_g, w_r, w_g2, b_g)


def _gla_scores_factored(qh, kh, bh, b_last):
    mid = 0.5 * b_last
    q_t = (qh * jnp.exp(bh - mid)).astype(BF16)
    k_t = (kh * jnp.exp(mid - bh)).astype(BF16)
    return _dot_nt(q_t, k_t)


def _gla_scores_elementwise(qh, kh, bh, k_scr, b_scr):
    k_scr[...] = kh
    b_scr[...] = bh
    col = lax.broadcasted_iota(jnp.int32, (GLA_CHUNK, GLA_CHUNK), 1)

    def body(j, acc):
        kj = k_scr[pl.ds(j, 1), :]
        bj = b_scr[pl.ds(j, 1), :]
        term = qh * kj * jnp.exp(jnp.minimum(bh - bj, 0.0))
        return acc + jnp.where(col == j, jnp.sum(term, axis=1, keepdims=True), 0.0)

    return lax.fori_loop(0, GLA_CHUNK, body, jnp.zeros((GLA_CHUNK, GLA_CHUNK), F32))


def _gla_chunk(rows, b, states, score_fn, q_ref, k_ref, v_ref, o_ref, causal):
    q, k = q_ref[rows, :], k_ref[rows, :]
    b_last = b[GLA_CHUNK - 1:GLA_CHUNK, :]
    q_in = q * jnp.exp(b)
    k_out = k * jnp.exp(b_last - b)
    new_states = []
    for hd, s_prev in enumerate(states):
        ks = slice(hd * GLA_DK, (hd + 1) * GLA_DK)
        vs = slice(hd * GLA_DV, (hd + 1) * GLA_DV)
        vh = v_ref[rows, vs].astype(BF16)
        o_inter = _dot(q_in[:, ks].astype(BF16), s_prev.astype(BF16))
        scores = jnp.where(causal, score_fn(q[:, ks], k[:, ks], b[:, ks], b_last[:, ks]), 0.0)
        o_ref[rows, vs] = (o_inter + _dot(scores.astype(BF16), vh)).astype(o_ref.dtype)
        decay_col = jnp.transpose(jnp.broadcast_to(jnp.exp(b_last[:, ks]), (GLA_DK, GLA_DK)))[:, :1]
        k_out_t = jnp.transpose(k_out[:, ks]).astype(BF16)
        new_states.append(decay_col * s_prev + _dot(k_out_t, vh))
    return new_states


def _gla_rec_kernel(q_ref, k_ref, v_ref, la_ref, s0_ref, o_ref, sfin_ref, s_scr, k_scr, b_scr):
    t = pl.program_id(1)

    @pl.when(t == 0)
    def _():
        s_scr[...] = s0_ref[...]

    row = lax.broadcasted_iota(jnp.int32, (GLA_CHUNK, GLA_CHUNK), 0)
    col = lax.broadcasted_iota(jnp.int32, (GLA_CHUNK, GLA_CHUNK), 1)
    causal = col <= row
    tri = causal.astype(BF16)
    n_chunks = GLA_TILE // GLA_CHUNK

    def decay_prefix(rows):
        rest = la_ref[rows, :]
        total = None
        for _ in range(3):
            piece = rest.astype(BF16)
            rest = rest - piece.astype(F32)
            part = _dot(tri, piece)
            total = part if total is None else total + part
        return total

    chunk_rows = [slice(c * GLA_CHUNK, (c + 1) * GLA_CHUNK) for c in range(n_chunks)]
    prefixes = [decay_prefix(rows) for rows in chunk_rows]
    total_decay = functools.reduce(jnp.maximum, [-b[GLA_CHUNK - 1:GLA_CHUNK, :] for b in prefixes])
    factorable = jnp.max(total_decay) <= GLA_FACTOR_BOUND

    def refs():
        return q_ref, k_ref, v_ref, o_ref, causal

    @pl.when(factorable)
    def _():
        states = [s_scr[hd] for hd in range(GLA_HEADS)]
        for rows, b in zip(chunk_rows, prefixes):
            states = _gla_chunk(rows, b, states, _gla_scores_factored, *refs())
        for hd in range(GLA_HEADS):
            s_scr[hd] = states[hd]

    @pl.when(jnp.logical_not(factorable))
    def _():
        def chunk(c, carry):
            rows = pl.ds(pl.multiple_of(c * GLA_CHUNK, GLA_CHUNK), GLA_CHUNK)
            states = [s_scr[hd] for hd in range(GLA_HEADS)]
            elementwise = functools.partial(_gla_scores_elementwise, k_scr=k_scr, b_scr=b_scr)
            states = _gla_chunk(rows, decay_prefix(rows), states,
                                lambda qh, kh, bh, _: elementwise(qh, kh, bh), *refs())
            for hd in range(GLA_HEADS):
                s_scr[hd] = states[hd]
            return carry

        lax.fori_loop(0, n_chunks, chunk, 0)

    @pl.when(t == pl.num_programs(1) - 1)
    def _():
        sfin_ref[...] = s_scr[...]


def _gla_rec_call(q, k, v, la, s0):
    n_b, n_t = q.shape[:2]

    def seq(width):
        return pl.BlockSpec((None, GLA_TILE, width), lambda b, t: (b, t, 0))

    state = pl.BlockSpec((None, GLA_HEADS, GLA_DK, GLA_DV), lambda b, t: (b, 0, 0, 0))
    return pl.pallas_call(
        _gla_rec_kernel,
        grid=(n_b, n_t // GLA_TILE),
        in_specs=[seq(GLA_QK), seq(GLA_QK), seq(GLA_V), seq(GLA_QK), state],
        out_specs=[seq(GLA_V), state],
        out_shape=[jax.ShapeDtypeStruct((n_b, n_t, GLA_V), BF16),
                   jax.ShapeDtypeStruct((n_b, GLA_HEADS, GLA_DK, GLA_DV), F32)],
        scratch_shapes=[pltpu.VMEM((GLA_HEADS, GLA_DK, GLA_DV), F32),
                        pltpu.VMEM((GLA_CHUNK, GLA_DK), F32),
                        pltpu.VMEM((GLA_CHUNK, GLA_DK), F32)],
        compiler_params=_params("arbitrary", "arbitrary"),
        name="gla_recurrence",
    )(q, k, v, la, s0)


def _gla_step_kernel(qt_ref, kt_ref, lat_ref, v_ref, s0_ref, o_ref, sfin_ref):
    n_t = v_ref.shape[0]
    for hd in range(GLA_HEADS):
        vs = slice(hd * GLA_DV, (hd + 1) * GLA_DV)
        s = s0_ref[hd]
        qt, kt, lat = qt_ref[hd], kt_ref[hd], lat_ref[hd]
        for t in range(n_t):
            s = jnp.exp(lat[:, t:t + 1]) * s + kt[:, t:t + 1] * v_ref[t:t + 1, vs]
            o_ref[t:t + 1, vs] = jnp.sum(qt[:, t:t + 1] * s, axis=0, keepdims=True)
        sfin_ref[hd] = s


def _gla_step_call(qt, kt, lat, v, s0):
    n_b, _, _, n_t = qt.shape
    col = pl.BlockSpec((None, GLA_HEADS, GLA_DK, n_t), lambda b: (b, 0, 0, 0))
    tok = pl.BlockSpec((None, n_t, GLA_V), lambda b: (b, 0, 0))
    state = pl.BlockSpec((None, GLA_HEADS, GLA_DK, GLA_DV), lambda b: (b, 0, 0, 0))
    return pl.pallas_call(
        _gla_step_kernel,
        grid=(n_b,),
        in_specs=[col, col, col, tok, state],
        out_specs=[tok, state],
        out_shape=[jax.ShapeDtypeStruct((n_b, n_t, GLA_V), F32),
                   jax.ShapeDtypeStruct((n_b, GLA_HEADS, GLA_DK, GLA_DV), F32)],
        compiler_params=_params("arbitrary"),
        name="gla_step",
    )(qt, kt, lat, v, s0)


def _gla_out_kernel(o_ref, r_ref, x_ref, gt_ref, gon_ref, wout_ref, g_ref, b_ref, y_ref):
    o = o_ref[...].astype(F32)
    parts = []
    for hd in range(GLA_HEADS):
        oh = o[:, hd * GLA_DV:(hd + 1) * GLA_DV]
        ms = jnp.mean(oh * oh, axis=-1, keepdims=True)
        parts.append(oh * lax.rsqrt(ms + GLA_NORM_EPS) * gon_ref[...])
    gated = (jnp.concatenate(parts, axis=1) * r_ref[...].astype(F32)).astype(BF16)
    y = _dot(gated, wout_ref[...])
    z = ALPHA * x_ref[...] + (1.0 + gt_ref[...]) * y
    y_ref[...] = _layer_norm(z, g_ref[...], b_ref[...])


def _gla_out_call(rows, o, r, x, layer, g_onorm, w_out, ln_g, ln_b):
    return pl.pallas_call(
        _gla_out_kernel,
        grid=(rows.steps,),
        in_specs=[
            rows.row_spec(GLA_V), rows.row_spec(GLA_V), rows.row_spec(D_MODEL),
            rows.mod_spec(layer, 5),
            _const_spec((1, GLA_DV), (0, 0)),
            _const_spec((GLA_V, D_MODEL), (0, 0)),
        ] + _ln_specs(layer, 1),
        out_specs=rows.row_spec(D_MODEL),
        out_shape=jax.ShapeDtypeStruct((rows.n_rows, D_MODEL), F32),
        compiler_params=_params("arbitrary"),
        name="gla_out",
    )(o, r, x, rows.mods, g_onorm, w_out, ln_g, ln_b)


def _gather_stride(scr, start, count, stride):
    return jnp.concatenate([scr[c, pl.ds(start, count, stride=stride), :] for c in range(scr.shape[0])], axis=1)


def _fill_slabs(scr, row0, x):
    for c in range(scr.shape[0]):
        scr[c, row0:row0 + x.shape[0], :] = x[:, c * LANES:(c + 1) * LANES]


def _store_by_residue(x, out_refs, scr_a, scr_b):
    n_rows = x.shape[0]
    s = DIL_DILATIONS[1]
    assert DIL_DILATIONS == (1, s, s * s)
    outs = {o.shape[0]: o for o in out_refs}
    if 1 in outs:
        outs[1][0] = x.astype(BF16)
    if s not in outs and s * s not in outs:
        return
    _fill_slabs(scr_a, 0, x)
    per_class = n_rows // s
    for r in range(s):
        xr = _gather_stride(scr_a, r, per_class, s)
        if s in outs:
            outs[s][r] = xr.astype(BF16)
        if s * s in outs:
            _fill_slabs(scr_b, r * per_class, xr)
    if s * s in outs:
        for r in range(s * s):
            xr = _gather_stride(scr_b, (r % s) * per_class + r // s, per_class // s, s)
            outs[s * s][r] = xr.astype(BF16)


def _kv_kernel(by_residue, x_ref, sh_ref, sc_ref, cos_ref, sin_ref, w_ref, k_ref, v_ref, *rest):
    h = (x_ref[...] * (1.0 + sc_ref[...]) + sh_ref[...]).astype(BF16)
    kv = _dot(h, w_ref[...])
    k = _rope(kv[:, :D_MODEL], cos_ref[...], sin_ref[...])
    v = kv[:, D_MODEL:]
    if by_residue:
        k_ref[...] = jnp.transpose(k)
        v_ref[...] = jnp.transpose(v)
    else:
        k_ref[...] = k.reshape(k_ref.shape)
        v_ref[...] = v.reshape(v_ref.shape)
    if by_residue:
        _store_by_residue(k, rest[0:N_GROUPS], *rest[-2:])
        _store_by_residue(v, rest[N_GROUPS:2 * N_GROUPS], *rest[-2:])


def _rope_spec(rows):
    if rows.per_row:
        return pl.BlockSpec((rows.tile, LANES), lambda i: (0, 0))
    tps = rows.tiles_per_sample
    return pl.BlockSpec((rows.tile, LANES), lambda i: (i % tps, 0))


def _kv_mod_spec(rows, k):
    if rows.per_row:
        return pl.BlockSpec((None, rows.tile, D_MODEL), lambda i: (0, 0, k))
    tps = rows.tiles_per_sample
    return pl.BlockSpec((None, None, 1, D_MODEL), lambda i: (0, i // tps, 0, k))


def _residue_spec(rows, dil, width):
    tps = rows.tiles_per_sample
    return pl.BlockSpec((None, dil, rows.tile // dil, width), lambda i: (i // tps, 0, i % tps, 0))


def _residue_shapes(rows, width, dtype):
    n_b = rows.n_rows // rows.rows_per_sample
    return [jax.ShapeDtypeStruct((n_b, dil, rows.rows_per_sample // dil, width), dtype) for dil in DIL_DILATIONS]


def _permute_scratch():
    return [pltpu.VMEM((D_MODEL // LANES, ROW_TILE, LANES), F32)] * 2


def _kv_call(rows, x, cos_t, sin_t, w_kv):
    by_residue = not rows.per_row
    scratch = []
    if by_residue:
        tps = rows.tiles_per_sample
        n_b = rows.n_rows // rows.rows_per_sample
        out_specs = [pl.BlockSpec((None, D_MODEL, rows.tile), lambda i: (i // tps, 0, i % tps))] * 2
        out_shape = [jax.ShapeDtypeStruct((n_b, D_MODEL, rows.rows_per_sample), F32)] * 2
    else:
        out_specs = [pl.BlockSpec((rows.tile, DIL_HEADS, DIL_HD), lambda i: (i, 0, 0))] * 2
        out_shape = [jax.ShapeDtypeStruct((rows.n_rows, DIL_HEADS, DIL_HD), F32)] * 2
    if by_residue:
        out_specs += [_residue_spec(rows, dil, D_MODEL) for dil in DIL_DILATIONS] * 2
        out_shape += _residue_shapes(rows, D_MODEL, BF16) * 2
        scratch = _permute_scratch()
    return pl.pallas_call(
        functools.partial(_kv_kernel, by_residue),
        grid=(rows.steps,),
        in_specs=[
            rows.row_spec(D_MODEL),
            _kv_mod_spec(rows, 0), _kv_mod_spec(rows, 1),
            _rope_spec(rows), _rope_spec(rows),
            _const_spec((D_MODEL, 2 * D_MODEL), (0, 0)),
        ],
        out_specs=out_specs,
        out_shape=out_shape,
        scratch_shapes=scratch,
        compiler_params=_params("arbitrary"),
        name="kv_proj",
    )(x, rows.kv_mods, rows.kv_mods, cos_t, sin_t, w_kv)


def _q_kernel(by_residue, x_ref, sh_ref, sc_ref, cos_ref, sin_ref, w_ref, *rest):
    h = (x_ref[...] * (1.0 + sc_ref[...]) + sh_ref[...]).astype(BF16)
    q = _rope(_dot(h, w_ref[...]), cos_ref[...], sin_ref[...])
    if by_residue:
        q = q * (DIL_SCALE * LOG2_E)
        for g in range(N_GROUPS):
            _store_by_residue(q[:, g * D_MODEL:(g + 1) * D_MODEL], rest[g:g + 1], *rest[-2:])
    else:
        rest[0][...] = (q * DIL_SCALE).reshape(rest[0].shape)


def _q_call(rows, x, layer, cos_t, sin_t, w_q):
    width = N_GROUPS * D_MODEL
    by_residue = not rows.per_row
    if by_residue:
        out_specs = [_residue_spec(rows, dil, D_MODEL) for dil in DIL_DILATIONS]
        out_shape = _residue_shapes(rows, D_MODEL, BF16)
        scratch = _permute_scratch()
    else:
        out_specs = pl.BlockSpec((rows.tile, N_GROUPS * DIL_HEADS, DIL_HD), lambda i: (i, 0, 0))
        out_shape = jax.ShapeDtypeStruct((rows.n_rows, N_GROUPS * DIL_HEADS, DIL_HD), F32)
        scratch = []
    return pl.pallas_call(
        functools.partial(_q_kernel, by_residue),
        grid=(rows.steps,),
        in_specs=[
            rows.row_spec(D_MODEL),
            rows.mod_spec(layer, 3), rows.mod_spec(layer, 4),
            _rope_spec(rows), _rope_spec(rows),
            _const_spec((D_MODEL, width), (0, 0)),
        ],
        out_specs=out_specs,
        out_shape=out_shape,
        scratch_shapes=scratch,
        compiler_params=_params("arbitrary"),
        name="q_proj",
    )(x, rows.mods, rows.mods, cos_t, sin_t, w_q)


def _pair_softmax_pv(scores, v_pair, even_lane):
    half = scores.shape[0] // 2
    m = jnp.max(scores, axis=-1, keepdims=True)
    e = jnp.exp2(scores - m)
    den = jnp.sum(e, axis=-1, keepdims=True)
    o2 = _dot(e.astype(BF16), v_pair)
    o = jnp.where(even_lane, o2[:half], o2[half:]) / jnp.where(even_lane, den[:half], den[half:])
    lse2 = (m + jnp.log2(den)) * LN_2
    return o, lse2[:half], lse2[half:]


def _dil_prompt_kernel(has_prev, dil, q_ref, *refs):
    if has_prev:
        kp_ref, kc_ref, vp_ref, vc_ref, o_ref, lse_ref = refs
    else:
        kc_ref, vc_ref, o_ref, lse_ref = refs
    blk = pl.program_id(2)
    first_token = blk * (ATT_BLOCK * dil) + pl.program_id(1)
    if dil == 1:
        token_rows = pl.ds(pl.multiple_of(first_token, ATT_BLOCK), ATT_BLOCK)
    else:
        token_rows = pl.ds(first_token, ATT_BLOCK, stride=dil)
    n_keys = (2 if has_prev else 1) * ATT_BLOCK
    lane = lax.broadcasted_iota(jnp.int32, (ATT_BLOCK, LANES), 1)
    even_lane = lane < DIL_HD
    qi = lax.broadcasted_iota(jnp.int32, (2 * ATT_BLOCK, n_keys), 0) % ATT_BLOCK
    kj = lax.broadcasted_iota(jnp.int32, (2 * ATT_BLOCK, n_keys), 1)
    if has_prev:
        dist = qi + ATT_BLOCK - kj
        valid = (dist >= 0) & (dist <= DIL_STEPS) & ((kj >= ATT_BLOCK) | (blk > 0))
    else:
        valid = kj <= qi
    lse_all = jnp.zeros((ATT_BLOCK, LANES), F32)
    for p in range(DIL_HEADS // 2):
        sl = slice(p * LANES, (p + 1) * LANES)
        qp = q_ref[:, sl]
        zero = jnp.zeros_like(qp)
        q2 = jnp.concatenate([jnp.where(even_lane, qp, zero), jnp.where(even_lane, zero, qp)], axis=0)
        if has_prev:
            k_pair = jnp.concatenate([kp_ref[:, sl], kc_ref[:, sl]], axis=0)
            v_pair = jnp.concatenate([vp_ref[:, sl], vc_ref[:, sl]], axis=0)
        else:
            k_pair, v_pair = kc_ref[:, sl], vc_ref[:, sl]
        scores = jnp.where(valid, _dot_nt(q2, k_pair), -jnp.inf)
        o, lse_even, lse_odd = _pair_softmax_pv(scores, v_pair, even_lane)
        o_ref[p, token_rows, :] = o
        lse_all = jnp.where(lane == 2 * p, lse_even, jnp.where(lane == 2 * p + 1, lse_odd, lse_all))
    lse_ref[token_rows, :] = lse_all


def _dil_prompt_call(q, kb, vb, group):
    n_b, dil, length, _ = kb.shape
    n_t = dil * length
    n_blk = length // ATT_BLOCK
    has_prev = n_blk > 1
    blk_shape = (None, None, ATT_BLOCK, D_MODEL)
    cur = pl.BlockSpec(blk_shape, lambda b, r, i: (b, r, i, 0))
    prev = pl.BlockSpec(blk_shape, lambda b, r, i: (b, r, jnp.maximum(i - 1, 0), 0))
    if has_prev:
        in_specs, operands = [cur, prev, cur, prev, cur], (q, kb, kb, vb, vb)
    else:
        in_specs, operands = [cur, cur, cur], (q, kb, vb)
    return pl.pallas_call(
        functools.partial(_dil_prompt_kernel, has_prev, dil),
        grid=(n_b, dil, n_blk),
        in_specs=in_specs,
        out_specs=[pl.BlockSpec((None, D_MODEL // LANES, n_t, LANES), lambda b, r, i: (b, 0, 0, 0)),
                   pl.BlockSpec((None, n_t, LANES), lambda b, r, i: (b, 0, 0))],
        out_shape=[jax.ShapeDtypeStruct((n_b, D_MODEL // LANES, n_t, LANES), F32),
                   jax.ShapeDtypeStruct((n_b, n_t, LANES), F32)],
        compiler_params=_params("arbitrary", "arbitrary", "arbitrary"),
        name="dilated_prompt",
    )(*operands)


SAMPLE_HEAD_CHUNK = 4


def _sample_group_dense(t, q_col, k_win, v_win, k_new, v_new):
    n_t = k_new.shape[-1]
    pos = lax.broadcasted_iota(jnp.int32, (1, 1, k_win.shape[-1]), 2)
    pos_new = lax.broadcasted_iota(jnp.int32, (1, 1, n_t), 2)
    s_c = jnp.where(pos >= t, jnp.sum(k_win * q_col, axis=1, keepdims=True), -jnp.inf)
    s_n = jnp.where(pos_new <= t, jnp.sum(k_new * q_col, axis=1, keepdims=True), -jnp.inf)
    m = jnp.maximum(jnp.max(s_c, axis=2, keepdims=True), jnp.max(s_n, axis=2, keepdims=True))
    e_c, e_n = jnp.exp(s_c - m), jnp.exp(s_n - m)
    den = jnp.sum(e_c, axis=2, keepdims=True) + jnp.sum(e_n, axis=2, keepdims=True)
    o = (jnp.sum(v_win * e_c, axis=2, keepdims=True) + jnp.sum(v_new * e_n, axis=2, keepdims=True)) / den
    return o, m + jnp.log(den)


def _sample_group_dilated(dil, q_g, k_win, v_win, k_new, v_new):
    n_t = q_g.shape[-1]
    width = k_win.shape[-1]
    n_col = width // LANES
    res = lax.broadcasted_iota(jnp.int32, (1, 1, LANES), 2) % dil
    res_w = lax.broadcasted_iota(jnp.int32, (1, 1, width), 2) % dil
    q_pat = jnp.zeros(q_g.shape[:2] + (LANES,), F32)
    for t in range(n_t):
        q_pat = jnp.where(res == t, q_g[:, :, t:t + 1], q_pat)
    s_row = jnp.sum(k_win * jnp.concatenate([q_pat] * n_col, axis=2), axis=1, keepdims=True)
    s_new = jnp.sum(k_new * q_g, axis=1, keepdims=True)
    m_t = [jnp.maximum(jnp.max(jnp.where(res_w == t, s_row, -jnp.inf), axis=2, keepdims=True), s_new[:, :, t:t + 1])
           for t in range(n_t)]
    m_row = jnp.zeros_like(s_row)
    for t in range(n_t):
        m_row = jnp.where(res_w == t, m_t[t], m_row)
    e = jnp.exp(jnp.where(res_w < n_t, s_row, -jnp.inf) - m_row)
    acc = v_win * e
    acc_fold = sum(acc[:, :, c * LANES:(c + 1) * LANES] for c in range(n_col))
    e_fold = sum(e[:, :, c * LANES:(c + 1) * LANES] for c in range(n_col))
    out = []
    for t in range(n_t):
        e_new = jnp.exp(s_new[:, :, t:t + 1] - m_t[t])
        den = jnp.sum(jnp.where(res == t, e_fold, 0.0), axis=2, keepdims=True) + e_new
        o = (jnp.sum(jnp.where(res == t, acc_fold, 0.0), axis=2, keepdims=True) + e_new * v_new[:, :, t:t + 1]) / den
        out.append((o, m_t[t] + jnp.log(den)))
    return out


def _sample_attention(q_ref, k_ref, v_ref, kn_ref, vn_ref, o_ref):
    n_t = q_ref.shape[-1]
    n_s = k_ref.shape[-1]
    pos_new = lax.broadcasted_iota(jnp.int32, (1, 1, n_t), 2)
    k_new, v_new = kn_ref[...], vn_ref[...]
    groups = []
    for g, dil in enumerate(DIL_DILATIONS):
        lo = n_s - DIL_STEPS * dil
        k_win, v_win = k_ref[:, :, lo:], v_ref[:, :, lo:]
        if dil == 1:
            groups.append([_sample_group_dense(t, q_ref[g, :, :, t:t + 1], k_win, v_win, k_new, v_new)
                           for t in range(n_t)])
        else:
            groups.append(_sample_group_dilated(dil, q_ref[g], k_win, v_win, k_new, v_new))
    merged = jnp.zeros(k_new.shape, F32)
    for t in range(n_t):
        parts = [grp[t] for grp in groups]
        m = functools.reduce(jnp.maximum, [lse for _, lse in parts])
        wts = [jnp.exp(lse - m) for _, lse in parts]
        o = sum(wt * o for wt, (o, _) in zip(wts, parts)) / sum(wts)
        merged = jnp.where(pos_new == t, o, merged)
    o_ref[...] = merged


class _SampleAttentionGuest:
    def __init__(self, q, cache_k, cache_v, k_new, v_new):
        self.operands = (q, cache_k, cache_v, k_new, v_new)
        self.n_b, self.n_t = q.shape[0], q.shape[-1]
        wb = cache_k.shape[-1]
        assert wb == DIL_STEPS * DIL_DILATIONS[-1] and DIL_DILATIONS[0] == 1 and self.n_t <= DIL_DILATIONS[1]
        assert all(wb % dil == 0 and LANES % dil == 0 for dil in DIL_DILATIONS)
        self.wb = wb
        self.outputs = []

    def specs(self, chunk):
        hc = SAMPLE_HEAD_CHUNK
        cache = pl.BlockSpec((None, hc, DIL_HD, self.wb), lambda i: (i, chunk, 0, 0))
        new = pl.BlockSpec((None, hc, DIL_HD, self.n_t), lambda i: (i, chunk, 0, 0))
        q = pl.BlockSpec((None, N_GROUPS, hc, DIL_HD, self.n_t), lambda i: (i, 0, chunk, 0, 0))
        out_shape = jax.ShapeDtypeStruct((self.n_b, hc, DIL_HD, self.n_t), F32)
        return [q, cache, cache, new, new], pl.BlockSpec((None, hc, DIL_HD, self.n_t), lambda i: (i, 0, 0, 0)), out_shape

    def result(self):
        assert len(self.outputs) * SAMPLE_HEAD_CHUNK == DIL_HEADS
        return jnp.concatenate(self.outputs, axis=1)


def _att_merge(o_refs, lse_refs):
    lses = [l_ref[...] for l_ref in lse_refs]
    m = functools.reduce(jnp.maximum, lses)
    e = [jnp.exp(l - m) for l in lses]
    den = sum(e)
    head = lax.broadcasted_iota(jnp.int32, (LANES, D_MODEL), 0)
    col_head = lax.broadcasted_iota(jnp.int32, (LANES, D_MODEL), 1) // DIL_HD
    spread = (head == col_head).astype(BF16)
    merged = None
    for o_ref, eg in zip(o_refs, e):
        wt = eg / den
        wt_hi = wt.astype(BF16)
        wt_lo = (wt - wt_hi.astype(F32)).astype(BF16)
        wide = _dot(wt_hi, spread) + _dot(wt_lo, spread)
        part = wide * jnp.concatenate([o_ref[c] for c in range(o_ref.shape[0])], axis=1)
        merged = part if merged is None else merged + part
    return merged


def _att_out_kernel(merged_input, *refs):
    if merged_input:
        o_ref, x_ref, gt_ref, wout_ref, g_ref, b_ref, y_ref = refs
        o = o_ref[...]
    else:
        o_refs, lse_refs = refs[0:N_GROUPS], refs[N_GROUPS:2 * N_GROUPS]
        x_ref, gt_ref, wout_ref, g_ref, b_ref, y_ref = refs[2 * N_GROUPS:]
        o = _att_merge(o_refs, lse_refs)
    y = _dot(o.astype(BF16), wout_ref[...])
    z = ALPHA * x_ref[...] + (1.0 + gt_ref[...]) * y
    y_ref[...] = _layer_norm(z, g_ref[...], b_ref[...])


def _att_out_call(rows, att, x, layer, w_out, ln_g, ln_b):
    merged_input = rows.per_row
    if merged_input:
        att_specs = [rows.row_spec(D_MODEL)]
        operands = [att]
    else:
        outs, lses = att
        tps = rows.tiles_per_sample
        o_spec = pl.BlockSpec((None, D_MODEL // LANES, rows.tile, LANES), lambda i: (i // tps, 0, i % tps, 0))
        lse_spec = pl.BlockSpec((None, rows.tile, LANES), lambda i: (i // tps, i % tps, 0))
        att_specs = [o_spec] * N_GROUPS + [lse_spec] * N_GROUPS
        operands = list(outs) + list(lses)
    return pl.pallas_call(
        functools.partial(_att_out_kernel, merged_input),
        grid=(rows.steps,),
        in_specs=att_specs + [
            rows.row_spec(D_MODEL),
            rows.mod_spec(layer, 5),
            _const_spec((D_MODEL, D_MODEL), (0, 0)),
        ] + _ln_specs(layer, 1),
        out_specs=rows.row_spec(D_MODEL),
        out_shape=jax.ShapeDtypeStruct((rows.n_rows, D_MODEL), F32),
        compiler_params=_params("arbitrary"),
        name="att_out",
    )(*operands, x, rows.mods, w_out, ln_g, ln_b)


def _rope_tables(positions):
    half = DIL_HD // 2
    inv = ROPE_THETA ** (-jnp.arange(half, dtype=F32) / half)
    ang = positions.astype(F32)[:, None] * inv[None, :]
    cos, sin = jnp.cos(ang), jnp.sin(ang)
    return jnp.concatenate([cos, cos, cos, cos], axis=1), jnp.concatenate([-sin, sin, -sin, sin], axis=1)


def _trunk(rows, ffn, x, n_b, n_t, positions, gla_state, w):
    cos_t, sin_t = _rope_tables(positions)
    if rows.per_row:
        cos_t, sin_t = jnp.tile(cos_t, (n_b, 1)), jnp.tile(sin_t, (n_b, 1))

    x = ffn(x, 0, 0, 0, w['up1'], w['down1'], w['ln_g'], w['ln_b'])
    q, k, v, la, r = _gla_proj_call(rows.wide(), x, 0, w['gla_qkv'], w['gla_g'], w['gla_r'], w['gla_g2'], w['gla_bg'])
    if rows.per_row:
        def cols(a):
            return a.reshape(n_b, n_t, GLA_HEADS, GLA_DK).transpose(0, 2, 3, 1)
        o, s_new = _gla_step_call(cols(q), cols(k), cols(la), v.reshape(n_b, n_t, GLA_V), gla_state)
    else:
        o, s_new = _gla_rec_call(q.reshape(n_b, n_t, GLA_QK), k.reshape(n_b, n_t, GLA_QK),
                                 v.reshape(n_b, n_t, GLA_V), la.reshape(n_b, n_t, GLA_QK), gla_state)
    x = _gla_out_call(rows.wide(), o.reshape(n_b * n_t, GLA_V), r, x, 0, w['gla_gon'], w['gla_out'], w['ln_g'], w['ln_b'])
    x = ffn(x, 0, 6, 2, w['up2'], w['down2'], w['ln_g'], w['ln_b'])

    k_rows, v_rows, *kv_groups = _kv_call(rows, x, cos_t, sin_t, w['kv'])
    if rows.per_row:
        k_rows, v_rows = (a.reshape(n_b, n_t, DIL_HEADS, DIL_HD) for a in (k_rows, v_rows))
    else:
        k_rows, v_rows = (a.reshape(n_b, DIL_HEADS, DIL_HD, n_t).transpose(0, 3, 1, 2) for a in (k_rows, v_rows))

    x = ffn(x, 1, 0, 0, w['up1'], w['down1'], w['ln_g'], w['ln_b'])
    qd = _q_call(rows, x, 1, cos_t, sin_t, w['q'])
    att = yield qd, k_rows, v_rows, kv_groups
    x = _att_out_call(rows, att, x, 1, w['att_out'], w['ln_g'], w['ln_b'])
    x = ffn(x, 1, 6, 2, w['up2'], w['down2'], w['ln_g'], w['ln_b'])
    return x, s_new, k_rows, v_rows


def _finish(trunk, att):
    try:
        trunk.send(att)
    except StopIteration as done:
        return done.value
    raise AssertionError("trunk yielded twice")


def kernel(x_prompt, x_sample, state_gla, cache_k, cache_v, c_prompt, c_sample, w_ada, b_ada, ln_g, ln_b, w_ffn1_up, w_ffn1_down, w_ffn2_up, w_ffn2_down, w_in_a, w_gate2_a, b_gate_a, g_onorm_a, w_out_a, w_ada_kv, b_ada_kv, w_kv, w_q_b, w_out_b):
    n_bp, n_tp = x_prompt.shape[:2]
    n_bs, n_ts = x_sample.shape[:2]
    assert N_A_LAYERS == 1 and DEPTH == 2
    assert n_tp % (ATT_BLOCK * DIL_DILATIONS[-1]) == 0 and n_tp % GLA_TILE == 0 and n_tp <= MAX_WINDOW
    assert n_ts < 8

    c_all = jnp.concatenate([c_prompt, c_sample], axis=0)
    mods = _ada_call(c_all, w_ada, b_ada, 1536)
    kv_mods = _ada_call(c_all, w_ada_kv[None], b_ada_kv[None], 1024)
    rows_p = _Rows(n_bp * n_tp, n_tp, mods[:, :n_bp, None, :], kv_mods[:, :n_bp, None, :])
    rows_s = _Rows(n_bs * n_ts, n_ts, jnp.repeat(mods[:, n_bp:], n_ts, axis=1),
                   jnp.repeat(kv_mods[:, n_bp:], n_ts, axis=1))

    w_in = w_in_a[0]
    gate_lo = 2 * GLA_QK + GLA_V
    w = {
        'up1': w_ffn1_up.astype(BF16), 'down1': w_ffn1_down.astype(BF16),
        'up2': w_ffn2_up.astype(BF16), 'down2': w_ffn2_down.astype(BF16),
        'ln_g': ln_g[:, :, None, :], 'ln_b': ln_b[:, :, None, :],
        'gla_qkv': w_in[:, :gate_lo].astype(BF16),
        'gla_g': jnp.pad(w_in[:, gate_lo:gate_lo + GLA_RANK], ((0, 0), (0, LANES - GLA_RANK))).astype(BF16),
        'gla_r': w_in[:, gate_lo + GLA_RANK:].astype(BF16),
        'gla_g2': jnp.pad(w_gate2_a[0], ((0, LANES - GLA_RANK), (0, 0))).astype(BF16),
        'gla_bg': b_gate_a[0][None, :],
        'gla_gon': g_onorm_a[0][None, :],
        'gla_out': w_out_a[0].astype(BF16),
        'kv': w_kv.astype(BF16),
        'q': w_q_b[0].astype(BF16),
        'att_out': w_out_b[0].astype(BF16),
    }

    trunk_s = _trunk(rows_s, functools.partial(_ffn_call, rows_s), x_sample.reshape(-1, D_MODEL), n_bs, n_ts,
                     PAST_LEN + jnp.arange(n_ts), state_gla[0], w)
    q_s, k_s, v_s, _ = next(trunk_s)
    guest = _SampleAttentionGuest(
        q_s.reshape(n_bs, n_ts, N_GROUPS, DIL_HEADS, DIL_HD).transpose(0, 2, 3, 4, 1),
        cache_k.transpose(0, 2, 3, 1), cache_v.transpose(0, 2, 3, 1),
        k_s.transpose(0, 2, 3, 1), v_s.transpose(0, 2, 3, 1))

    zero_state = jnp.zeros((n_bp, GLA_HEADS, GLA_DK, GLA_DV), F32)
    trunk_p = _trunk(rows_p, functools.partial(_ffn_call, rows_p, guest=guest), x_prompt.reshape(-1, D_MODEL),
                     n_bp, n_tp, jnp.arange(n_tp), zero_state, w)
    q_p, _, _, kv_p = next(trunk_p)
    att_p = tuple(zip(*[_dil_prompt_call(q_p[g], kv_p[g], kv_p[N_GROUPS + g], g) for g in range(N_GROUPS)]))
    y_p, s_p, k_p, v_p = _finish(trunk_p, att_p)
    att_s = guest.result().transpose(0, 3, 1, 2).reshape(n_bs * n_ts, D_MODEL)
    y_s, s_s, k_s, v_s = _finish(trunk_s, att_s)
    return (y_p.reshape(n_bp, n_tp, D_MODEL), y_s.reshape(n_bs, n_ts, D_MODEL),
            s_p[None], s_s[None], k_p, v_p, k_s, v_s)
```

```python
import functools

import jax
import jax.numpy as jnp
from jax import lax
from jax.experimental import pallas as pl
from jax.experimental.pallas import tpu as pltpu

F32 = jnp.float32
BF16 = jnp.bfloat16

D_MODEL = 1024
DEPTH = 2
PAST_LEN = 16384
N_A_LAYERS = DEPTH // 2
GLA_HEADS = 4
GLA_QK = D_MODEL // 2
GLA_V = D_MODEL
GLA_DK = GLA_QK // GLA_HEADS
GLA_DV = GLA_V // GLA_HEADS
GLA_RANK = 16
GLA_TAU = 16.0
GLA_NORM_EPS = 1e-6
DIL_HEADS = 16
DIL_HD = D_MODEL // DIL_HEADS
DIL_WINDOWS = (128, 512, 2048)
DIL_DILATIONS = (1, 4, 16)
N_GROUPS = len(DIL_WINDOWS)
DIL_STEPS = 128
MAX_WINDOW = max(DIL_WINDOWS)
DIL_SCALE = DIL_HD ** -0.5
ROPE_THETA = 10000.0
D_FF = 2816
FFN_RES = 0.5
ALPHA = (2 * DEPTH) ** 0.25
LN_EPS = 1e-5
N_MOD = 9

LANES = 128
ROW_TILE = 512
WIDE_TILE = 1024
LOG2_E = 1.4426950408889634
LN_2 = 0.6931471805599453
FF_CHUNK = 256
GLA_CHUNK = 128
GLA_TILE = 512
GLA_FACTOR_BOUND = 80.0
ATT_BLOCK = DIL_STEPS
VMEM_LIMIT = 56 * 1024 * 1024


def _params(*sem):
    return pltpu.CompilerParams(dimension_semantics=sem, vmem_limit_bytes=VMEM_LIMIT)


def _silu(x):
    return x / (1.0 + jnp.exp(-x))


def _layer_norm(z, g, b):
    mu = jnp.mean(z, axis=-1, keepdims=True)
    zc = z - mu
    var = jnp.mean(zc * zc, axis=-1, keepdims=True)
    return zc * lax.rsqrt(var + LN_EPS) * g + b


def _dot(a, b):
    return jnp.dot(a, b, preferred_element_type=F32)


def _dot_nt(a, b):
    return lax.dot_general(a, b, (((1,), (1,)), ((), ())), preferred_element_type=F32)


def _rope(x, cos_t, sin_t):
    width = x.shape[1]
    lane = lax.broadcasted_iota(jnp.int32, x.shape, 1)
    first_half = (lane % DIL_HD) < (DIL_HD // 2)
    partner = jnp.where(first_half, pltpu.roll(x, width - DIL_HD // 2, 1), pltpu.roll(x, DIL_HD // 2, 1))
    reps = width // LANES
    cos_w = jnp.concatenate([cos_t] * reps, axis=1)
    sin_w = jnp.concatenate([sin_t] * reps, axis=1)
    return x * cos_w + partner * sin_w


def _ada_kernel(c_ref, w_ref, b_ref, o_ref):
    s = _silu(c_ref[...]).astype(BF16)
    o_ref[...] = _dot(s, w_ref[...].astype(BF16)) + b_ref[...]


def _ada_call(c_all, w, b, tn):
    n_l, _, n = w.shape
    rows = c_all.shape[0]
    return pl.pallas_call(
        _ada_kernel,
        grid=(n_l, n // tn),
        in_specs=[
            pl.BlockSpec((rows, D_MODEL), lambda l, j: (0, 0)),
            pl.BlockSpec((None, D_MODEL, tn), lambda l, j: (l, 0, j)),
            pl.BlockSpec((None, 1, tn), lambda l, j: (l, 0, j)),
        ],
        out_specs=pl.BlockSpec((None, rows, tn), lambda l, j: (l, 0, j)),
        out_shape=jax.ShapeDtypeStruct((n_l, rows, n), F32),
        compiler_params=_params("arbitrary", "arbitrary"),
        name="ada_mod",
    )(c_all, w, b.reshape(n_l, 1, n))


class _Rows:
    def __init__(self, n_rows, rows_per_sample, mods, kv_mods, tile=ROW_TILE):
        self.n_rows = n_rows
        self.rows_per_sample = rows_per_sample
        self.per_row = rows_per_sample < 8
        self.tile = n_rows if self.per_row else min(tile, rows_per_sample)
        self.steps = n_rows // self.tile
        self.tiles_per_sample = 1 if self.per_row else rows_per_sample // self.tile
        self.mods = mods
        self.kv_mods = kv_mods

    def wide(self):
        return _Rows(self.n_rows, self.rows_per_sample, self.mods, self.kv_mods, WIDE_TILE)

    def row_spec(self, width):
        return pl.BlockSpec((self.tile, width), lambda i: (i, 0))

    def mod_spec(self, layer, k):
        if self.per_row:
            return pl.BlockSpec((None, self.tile, D_MODEL), lambda i: (layer, 0, k))
        tps = self.tiles_per_sample
        return pl.BlockSpec((None, None, 1, D_MODEL), lambda i: (layer, i // tps, 0, k))


def _const_spec(shape, index):
    return pl.BlockSpec(shape, lambda i: index, pipeline_mode=pl.Buffered(1))


def _ln_specs(layer, j):
    return [_const_spec((None, None, 1, D_MODEL), (layer, j, 0, 0))] * 2


def _ffn_kernel(has_guest, x_ref, sh_ref, sc_ref, gt_ref, wa_ref, wu_ref, wdn_ref, g_ref, b_ref, *rest):
    if has_guest:
        *guest_in, o_ref, guest_out = rest
        _sample_attention(*guest_in, guest_out)
    else:
        o_ref, = rest
    x = x_ref[...]
    h = (x * (1.0 + sc_ref[...]) + sh_ref[...]).astype(BF16)
    y = None
    for lo in range(0, D_FF, FF_CHUNK):
        a = _dot(h, wa_ref[:, lo:lo + FF_CHUNK])
        u = _dot(h, wu_ref[:, lo:lo + FF_CHUNK])
        part = _dot((_silu(a) * u).astype(BF16), wdn_ref[lo:lo + FF_CHUNK, :])
        y = part if y is None else y + part
    z = ALPHA * x + (1.0 + gt_ref[...]) * (FFN_RES * y)
    o_ref[...] = _layer_norm(z, g_ref[...], b_ref[...])


def _ffn_cast_kernel(x_ref, sh_ref, sc_ref, gt_ref, wa_ref, wu_ref, wdn_ref, g_ref, b_ref,
                     o_ref, wa_out, wu_out, wdn_out, y_scr):
    c = pl.program_id(0)
    wa, wu, wdn = wa_ref[...].astype(BF16), wu_ref[...].astype(BF16), wdn_ref[...].astype(BF16)
    wa_out[...], wu_out[...], wdn_out[...] = wa, wu, wdn
    x = x_ref[...]
    h = (x * (1.0 + sc_ref[...]) + sh_ref[...]).astype(BF16)
    part = _dot((_silu(_dot(h, wa)) * _dot(h, wu)).astype(BF16), wdn)

    @pl.when(c == 0)
    def _():
        y_scr[...] = part

    @pl.when(c > 0)
    def _():
        y_scr[...] += part

    @pl.when(c == pl.num_programs(0) - 1)
    def _():
        z = ALPHA * x + (1.0 + gt_ref[...]) * (FFN_RES * y_scr[...])
        o_ref[...] = _layer_norm(z, g_ref[...], b_ref[...])


def _ffn_cast_call(rows, x, layer, mod0, ln_j, w_up, w_down, ln_g, ln_b):
    assert rows.steps == 1
    n_chunks = D_FF // FF_CHUNK

    def whole(spec):
        return pl.BlockSpec(spec.block_shape, lambda c, index_map=spec.index_map: index_map(0))

    return pl.pallas_call(
        _ffn_cast_kernel,
        grid=(n_chunks,),
        in_specs=[whole(s) for s in (rows.row_spec(D_MODEL), rows.mod_spec(layer, mod0),
                                     rows.mod_spec(layer, mod0 + 1), rows.mod_spec(layer, mod0 + 2))] + [
            pl.BlockSpec((None, D_MODEL, FF_CHUNK), lambda c: (layer, 0, c)),
            pl.BlockSpec((None, D_MODEL, FF_CHUNK), lambda c: (layer, 0, n_chunks + c)),
            pl.BlockSpec((None, FF_CHUNK, D_MODEL), lambda c: (layer, c, 0)),
        ] + [whole(s) for s in _ln_specs(layer, ln_j)],
        out_specs=[whole(rows.row_spec(D_MODEL)),
                   pl.BlockSpec((D_MODEL, FF_CHUNK), lambda c: (0, c)),
                   pl.BlockSpec((D_MODEL, FF_CHUNK), lambda c: (0, c)),
                   pl.BlockSpec((FF_CHUNK, D_MODEL), lambda c: (c, 0))],
        out_shape=[jax.ShapeDtypeStruct((rows.n_rows, D_MODEL), F32),
                   jax.ShapeDtypeStruct((D_MODEL, D_FF), BF16),
                   jax.ShapeDtypeStruct((D_MODEL, D_FF), BF16),
                   jax.ShapeDtypeStruct((D_FF, D_MODEL), BF16)],
        scratch_shapes=[pltpu.VMEM((rows.n_rows, D_MODEL), F32)],
        compiler_params=_params("arbitrary"),
        name="ffn_cast",
    )(x, rows.mods, rows.mods, rows.mods, w_up, w_up, w_down, ln_g, ln_b)


def _ffn_call(rows, x, layer, mod0, ln_j, w_gate, w_up, w_down, ln_g, ln_b, guest=None):
    in_specs = [
        rows.row_spec(D_MODEL),
        rows.mod_spec(layer, mod0), rows.mod_spec(layer, mod0 + 1), rows.mod_spec(layer, mod0 + 2),
        _const_spec((D_MODEL, D_FF), (0, 0)),
        _const_spec((D_MODEL, D_FF), (0, 0)),
        _const_spec((D_FF, D_MODEL), (0, 0)),
    ] + _ln_specs(layer, ln_j)
    operands = [x, rows.mods, rows.mods, rows.mods, w_gate, w_up, w_down, ln_g, ln_b]
    out_specs = [rows.row_spec(D_MODEL)]
    out_shape = [jax.ShapeDtypeStruct((rows.n_rows, D_MODEL), F32)]
    if guest is not None:
        assert guest.n_b == rows.steps
        guest_in, guest_out, guest_shape = guest.specs(len(guest.outputs))
        in_specs += guest_in
        operands += guest.operands
        out_specs.append(guest_out)
        out_shape.append(guest_shape)
    outs = pl.pallas_call(
        functools.partial(_ffn_kernel, guest is not None),
        grid=(rows.steps,),
        in_specs=in_specs,
        out_specs=out_specs,
        out_shape=out_shape,
        compiler_params=_params("arbitrary"),
        name="ffn",
    )(*operands)
    if guest is not None:
        guest.outputs.append(outs[1])
    return outs[0]


def _gla_proj_kernel(x_ref, sh_ref, sc_ref, wqkv_ref, wg_ref, wr_ref, wg2_ref, bg_ref,
                     q_ref, k_ref, v_ref, la_ref, r_ref):
    h = (x_ref[...] * (1.0 + sc_ref[...]) + sh_ref[...]).astype(BF16)
    qkv = _dot(h, wqkv_ref[...])
    q_ref[...] = qkv[:, :GLA_QK] * (GLA_DK ** -0.5)
    k_ref[...] = qkv[:, GLA_QK:2 * GLA_QK]
    v_ref[...] = qkv[:, 2 * GLA_QK:].astype(v_ref.dtype)
    g_lr = _dot(h, wg_ref[...]).astype(BF16)
    gate = _dot(g_lr, wg2_ref[...]) + bg_ref[...]
    log_sig = jnp.minimum(gate, 0.0) - jnp.log(1.0 + jnp.exp(-jnp.abs(gate)))
    la_ref[...] = log_sig / GLA_TAU
    r_ref[...] = _silu(_dot(h, wr_ref[...])).astype(r_ref.dtype)


def _gla_proj_call(rows, x, layer, w_qkv, w_g, w_r, w_g2, b_g):
    widths = (GLA_QK, GLA_QK, GLA_V, GLA_QK, GLA_V)
    narrow = F32 if rows.per_row else BF16
    dtypes = (F32, F32, narrow, F32, narrow)
    return pl.pallas_call(
        _gla_proj_kernel,
        grid=(rows.steps,),
        in_specs=[
            rows.row_spec(D_MODEL),
            rows.mod_spec(layer, 3), rows.mod_spec(layer, 4),
            _const_spec((D_MODEL, 2 * GLA_QK + GLA_V), (0, 0)),
            _const_spec((D_MODEL, LANES), (0, 0)),
            _const_spec((D_MODEL, GLA_V), (0, 0)),
            _const_spec((LANES, GLA_QK), (0, 0)),
            _const_spec((1, GLA_QK), (0, 0)),
        ],
        out_specs=[rows.row_spec(w) for w in widths],
        out_shape=[jax.ShapeDtypeStruct((rows.n_rows, w), dt) for w, dt in zip(widths, dtypes)],
        compiler_params=_params("arbitrary"),
        name="gla_proj",
    )(x, rows.mods, rows.mods, w_qkv, w_g, w_r, w_g2, b_g)


def _gla_scores_factored(qh, kh, bh, b_last):
    mid = 0.5 * b_last
    q_t = (qh * jnp.exp(bh - mid)).astype(BF16)
    k_t = (kh * jnp.exp(mid - bh)).astype(BF16)
    return _dot_nt(q_t, k_t)


def _gla_scores_elementwise(qh, kh, bh, k_scr, b_scr):
    k_scr[...] = kh
    b_scr[...] = bh
    col = lax.broadcasted_iota(jnp.int32, (GLA_CHUNK, GLA_CHUNK), 1)

    def body(j, acc):
        kj = k_scr[pl.ds(j, 1), :]
        bj = b_scr[pl.ds(j, 1), :]
        term = qh * kj * jnp.exp(jnp.minimum(bh - bj, 0.0))
        return acc + jnp.where(col == j, jnp.sum(term, axis=1, keepdims=True), 0.0)

    return lax.fori_loop(0, GLA_CHUNK, body, jnp.zeros((GLA_CHUNK, GLA_CHUNK), F32))


def _gla_chunk(rows, b, states, score_fn, q_ref, k_ref, v_ref, o_ref, causal):
    q, k = q_ref[rows, :], k_ref[rows, :]
    b_last = b[GLA_CHUNK - 1:GLA_CHUNK, :]
    q_in = q * jnp.exp(b)
    k_out = k * jnp.exp(b_last - b)
    new_states = []
    for hd, s_prev in enumerate(states):
        ks = slice(hd * GLA_DK, (hd + 1) * GLA_DK)
        vs = slice(hd * GLA_DV, (hd + 1) * GLA_DV)
        vh = v_ref[rows, vs].astype(BF16)
        o_inter = _dot(q_in[:, ks].astype(BF16), s_prev.astype(BF16))
        scores = jnp.where(causal, score_fn(q[:, ks], k[:, ks], b[:, ks], b_last[:, ks]), 0.0)
        o_ref[rows, vs] = (o_inter + _dot(scores.astype(BF16), vh)).astype(o_ref.dtype)
        decay_col = jnp.transpose(jnp.broadcast_to(jnp.exp(b_last[:, ks]), (GLA_DK, GLA_DK)))[:, :1]
        k_out_t = jnp.transpose(k_out[:, ks]).astype(BF16)
        new_states.append(decay_col * s_prev + _dot(k_out_t, vh))
    return new_states


def _gla_rec_kernel(q_ref, k_ref, v_ref, la_ref, s0_ref, o_ref, sfin_ref, s_scr, k_scr, b_scr):
    t = pl.program_id(1)

    @pl.when(t == 0)
    def _():
        s_scr[...] = s0_ref[...]

    row = lax.broadcasted_iota(jnp.int32, (GLA_CHUNK, GLA_CHUNK), 0)
    col = lax.broadcasted_iota(jnp.int32, (GLA_CHUNK, GLA_CHUNK), 1)
    causal = col <= row
    tri = causal.astype(BF16)
    n_chunks = GLA_TILE // GLA_CHUNK

    def decay_prefix(rows):
        rest = la_ref[rows, :]
        total = None
        for _ in range(3):
            piece = rest.astype(BF16)
            rest = rest - piece.astype(F32)
            part = _dot(tri, piece)
            total = part if total is None else total + part
        return total

    chunk_rows = [slice(c * GLA_CHUNK, (c + 1) * GLA_CHUNK) for c in range(n_chunks)]
    prefixes = [decay_prefix(rows) for rows in chunk_rows]
    total_decay = functools.reduce(jnp.maximum, [-b[GLA_CHUNK - 1:GLA_CHUNK, :] for b in prefixes])
    factorable = jnp.max(total_decay) <= GLA_FACTOR_BOUND

    def refs():
        return q_ref, k_ref, v_ref, o_ref, causal

    @pl.when(factorable)
    def _():
        states = [s_scr[hd] for hd in range(GLA_HEADS)]
        for rows, b in zip(chunk_rows, prefixes):
            states = _gla_chunk(rows, b, states, _gla_scores_factored, *refs())
        for hd in range(GLA_HEADS):
            s_scr[hd] = states[hd]

    @pl.when(jnp.logical_not(factorable))
    def _():
        def chunk(c, carry):
            rows = pl.ds(pl.multiple_of(c * GLA_CHUNK, GLA_CHUNK), GLA_CHUNK)
            states = [s_scr[hd] for hd in range(GLA_HEADS)]
            elementwise = functools.partial(_gla_scores_elementwise, k_scr=k_scr, b_scr=b_scr)
            states = _gla_chunk(rows, decay_prefix(rows), states,
                                lambda qh, kh, bh, _: elementwise(qh, kh, bh), *refs())
            for hd in range(GLA_HEADS):
                s_scr[hd] = states[hd]
            return carry

        lax.fori_loop(0, n_chunks, chunk, 0)

    @pl.when(t == pl.num_programs(1) - 1)
    def _():
        sfin_ref[...] = s_scr[...]


def _gla_rec_call(q, k, v, la, s0):
    n_b, n_t = q.shape[:2]

    def seq(width):
        return pl.BlockSpec((None, GLA_TILE, width), lambda b, t: (b, t, 0))

    state = pl.BlockSpec((None, GLA_HEADS, GLA_DK, GLA_DV), lambda b, t: (b, 0, 0, 0))
    return pl.pallas_call(
        _gla_rec_kernel,
        grid=(n_b, n_t // GLA_TILE),
        in_specs=[seq(GLA_QK), seq(GLA_QK), seq(GLA_V), seq(GLA_QK), state],
        out_specs=[seq(GLA_V), state],
        out_shape=[jax.ShapeDtypeStruct((n_b, n_t, GLA_V), BF16),
                   jax.ShapeDtypeStruct((n_b, GLA_HEADS, GLA_DK, GLA_DV), F32)],
        scratch_shapes=[pltpu.VMEM((GLA_HEADS, GLA_DK, GLA_DV), F32),
                        pltpu.VMEM((GLA_CHUNK, GLA_DK), F32),
                        pltpu.VMEM((GLA_CHUNK, GLA_DK), F32)],
        compiler_params=_params("arbitrary", "arbitrary"),
        name="gla_recurrence",
    )(q, k, v, la, s0)


def _gla_step_kernel(qt_ref, kt_ref, lat_ref, v_ref, s0_ref, o_ref, sfin_ref):
    n_t = v_ref.shape[0]
    for hd in range(GLA_HEADS):
        vs = slice(hd * GLA_DV, (hd + 1) * GLA_DV)
        s = s0_ref[hd]
        qt, kt, lat = qt_ref[hd], kt_ref[hd], lat_ref[hd]
        for t in range(n_t):
            s = jnp.exp(lat[:, t:t + 1]) * s + kt[:, t:t + 1] * v_ref[t:t + 1, vs]
            o_ref[t:t + 1, vs] = jnp.sum(qt[:, t:t + 1] * s, axis=0, keepdims=True)
        sfin_ref[hd] = s


def _gla_step_call(qt, kt, lat, v, s0):
    n_b, _, _, n_t = qt.shape
    col = pl.BlockSpec((None, GLA_HEADS, GLA_DK, n_t), lambda b: (b, 0, 0, 0))
    tok = pl.BlockSpec((None, n_t, GLA_V), lambda b: (b, 0, 0))
    state = pl.BlockSpec((None, GLA_HEADS, GLA_DK, GLA_DV), lambda b: (b, 0, 0, 0))
    return pl.pallas_call(
        _gla_step_kernel,
        grid=(n_b,),
        in_specs=[col, col, col, tok, state],
        out_specs=[tok, state],
        out_shape=[jax.ShapeDtypeStruct((n_b, n_t, GLA_V), F32),
                   jax.ShapeDtypeStruct((n_b, GLA_HEADS, GLA_DK, GLA_DV), F32)],
        compiler_params=_params("arbitrary"),
        name="gla_step",
    )(qt, kt, lat, v, s0)


def _gla_out_kernel(o_ref, r_ref, x_ref, gt_ref, gon_ref, wout_ref, g_ref, b_ref, y_ref):
    o = o_ref[...].astype(F32)
    parts = []
    for hd in range(GLA_HEADS):
        oh = o[:, hd * GLA_DV:(hd + 1) * GLA_DV]
        ms = jnp.mean(oh * oh, axis=-1, keepdims=True)
        parts.append(oh * lax.rsqrt(ms + GLA_NORM_EPS) * gon_ref[...])
    gated = (jnp.concatenate(parts, axis=1) * r_ref[...].astype(F32)).astype(BF16)
    y = _dot(gated, wout_ref[...])
    z = ALPHA * x_ref[...] + (1.0 + gt_ref[...]) * y
    y_ref[...] = _layer_norm(z, g_ref[...], b_ref[...])


def _gla_out_call(rows, o, r, x, layer, g_onorm, w_out, ln_g, ln_b):
    return pl.pallas_call(
        _gla_out_kernel,
        grid=(rows.steps,),
        in_specs=[
            rows.row_spec(GLA_V), rows.row_spec(GLA_V), rows.row_spec(D_MODEL),
            rows.mod_spec(layer, 5),
            _const_spec((1, GLA_DV), (0, 0)),
            _const_spec((GLA_V, D_MODEL), (0, 0)),
        ] + _ln_specs(layer, 1),
        out_specs=rows.row_spec(D_MODEL),
        out_shape=jax.ShapeDtypeStruct((rows.n_rows, D_MODEL), F32),
        compiler_params=_params("arbitrary"),
        name="gla_out",
    )(o, r, x, rows.mods, g_onorm, w_out, ln_g, ln_b)


def _gather_stride(scr, start, count, stride):
    return jnp.concatenate([scr[c, pl.ds(start, count, stride=stride), :] for c in range(scr.shape[0])], axis=1)


def _fill_slabs(scr, row0, x):
    for c in range(scr.shape[0]):
        scr[c, row0:row0 + x.shape[0], :] = x[:, c * LANES:(c + 1) * LANES]


def _store_by_residue(x, out_refs, scr_a, scr_b):
    n_rows = x.shape[0]
    s = DIL_DILATIONS[1]
    assert DIL_DILATIONS == (1, s, s * s)
    outs = {o.shape[0]: o for o in out_refs}
    if 1 in outs:
        outs[1][0] = x.astype(BF16)
    if s not in outs and s * s not in outs:
        return
    _fill_slabs(scr_a, 0, x)
    per_class = n_rows // s
    for r in range(s):
        xr = _gather_stride(scr_a, r, per_class, s)
        if s in outs:
            outs[s][r] = xr.astype(BF16)
        if s * s in outs:
            _fill_slabs(scr_b, r * per_class, xr)
    if s * s in outs:
        for r in range(s * s):
            xr = _gather_stride(scr_b, (r % s) * per_class + r // s, per_class // s, s)
            outs[s * s][r] = xr.astype(BF16)


def _kv_kernel(by_residue, x_ref, sh_ref, sc_ref, cos_ref, sin_ref, w_ref, k_ref, v_ref, *rest):
    h = (x_ref[...] * (1.0 + sc_ref[...]) + sh_ref[...]).astype(BF16)
    kv = _dot(h, w_ref[...])
    k = _rope(kv[:, :D_MODEL], cos_ref[...], sin_ref[...])
    v = kv[:, D_MODEL:]
    if by_residue:
        k_ref[...] = jnp.transpose(k)
        v_ref[...] = jnp.transpose(v)
    else:
        k_ref[...] = k.reshape(k_ref.shape)
        v_ref[...] = v.reshape(v_ref.shape)
    if by_residue:
        _store_by_residue(k, rest[0:N_GROUPS], *rest[-2:])
        _store_by_residue(v, rest[N_GROUPS:2 * N_GROUPS], *rest[-2:])


def _rope_spec(rows):
    if rows.per_row:
        return pl.BlockSpec((rows.tile, LANES), lambda i: (0, 0))
    tps = rows.tiles_per_sample
    return pl.BlockSpec((rows.tile, LANES), lambda i: (i % tps, 0))


def _kv_mod_spec(rows, k):
    if rows.per_row:
        return pl.BlockSpec((None, rows.tile, D_MODEL), lambda i: (0, 0, k))
    tps = rows.tiles_per_sample
    return pl.BlockSpec((None, None, 1, D_MODEL), lambda i: (0, i // tps, 0, k))


def _residue_spec(rows, dil, width):
    tps = rows.tiles_per_sample
    return pl.BlockSpec((None, dil, rows.tile // dil, width), lambda i: (i // tps, 0, i % tps, 0))


def _residue_shapes(rows, width, dtype):
    n_b = rows.n_rows // rows.rows_per_sample
    return [jax.ShapeDtypeStruct((n_b, dil, rows.rows_per_sample // dil, width), dtype) for dil in DIL_DILATIONS]


def _permute_scratch():
    return [pltpu.VMEM((D_MODEL // LANES, ROW_TILE, LANES), F32)] * 2


def _kv_call(rows, x, cos_t, sin_t, w_kv):
    by_residue = not rows.per_row
    scratch = []
    if by_residue:
        tps = rows.tiles_per_sample
        n_b = rows.n_rows // rows.rows_per_sample
        out_specs = [pl.BlockSpec((None, D_MODEL, rows.tile), lambda i: (i // tps, 0, i % tps))] * 2
        out_shape = [jax.ShapeDtypeStruct((n_b, D_MODEL, rows.rows_per_sample), F32)] * 2
    else:
        out_specs = [pl.BlockSpec((rows.tile, DIL_HEADS, DIL_HD), lambda i: (i, 0, 0))] * 2
        out_shape = [jax.ShapeDtypeStruct((rows.n_rows, DIL_HEADS, DIL_HD), F32)] * 2
    if by_residue:
        out_specs += [_residue_spec(rows, dil, D_MODEL) for dil in DIL_DILATIONS] * 2
        out_shape += _residue_shapes(rows, D_MODEL, BF16) * 2
        scratch = _permute_scratch()
    return pl.pallas_call(
        functools.partial(_kv_kernel, by_residue),
        grid=(rows.steps,),
        in_specs=[
            rows.row_spec(D_MODEL),
            _kv_mod_spec(rows, 0), _kv_mod_spec(rows, 1),
            _rope_spec(rows), _rope_spec(rows),
            _const_spec((D_MODEL, 2 * D_MODEL), (0, 0)),
        ],
        out_specs=out_specs,
        out_shape=out_shape,
        scratch_shapes=scratch,
        compiler_params=_params("arbitrary"),
        name="kv_proj",
    )(x, rows.kv_mods, rows.kv_mods, cos_t, sin_t, w_kv)


def _q_kernel(by_residue, x_ref, sh_ref, sc_ref, cos_ref, sin_ref, w_ref, *rest):
    h = (x_ref[...] * (1.0 + sc_ref[...]) + sh_ref[...]).astype(BF16)
    q = _rope(_dot(h, w_ref[...]), cos_ref[...], sin_ref[...])
    if by_residue:
        q = q * (DIL_SCALE * LOG2_E)
        for g in range(N_GROUPS):
            _store_by_residue(q[:, g * D_MODEL:(g + 1) * D_MODEL], rest[g:g + 1], *rest[-2:])
    else:
        rest[0][...] = (q * DIL_SCALE).reshape(rest[0].shape)


def _q_call(rows, x, layer, cos_t, sin_t, w_q):
    width = N_GROUPS * D_MODEL
    by_residue = not rows.per_row
    if by_residue:
        out_specs = [_residue_spec(rows, dil, D_MODEL) for dil in DIL_DILATIONS]
        out_shape = _residue_shapes(rows, D_MODEL, BF16)
        scratch = _permute_scratch()
    else:
        out_specs = pl.BlockSpec((rows.tile, N_GROUPS * DIL_HEADS, DIL_HD), lambda i: (i, 0, 0))
        out_shape = jax.ShapeDtypeStruct((rows.n_rows, N_GROUPS * DIL_HEADS, DIL_HD), F32)
        scratch = []
    return pl.pallas_call(
        functools.partial(_q_kernel, by_residue),
        grid=(rows.steps,),
        in_specs=[
            rows.row_spec(D_MODEL),
            rows.mod_spec(layer, 3), rows.mod_spec(layer, 4),
            _rope_spec(rows), _rope_spec(rows),
            _const_spec((D_MODEL, width), (0, 0)),
        ],
        out_specs=out_specs,
        out_shape=out_shape,
        scratch_shapes=scratch,
        compiler_params=_params("arbitrary"),
        name="q_proj",
    )(x, rows.mods, rows.mods, cos_t, sin_t, w_q)


def _pair_softmax_pv(scores, v_pair, even_lane):
    half = scores.shape[0] // 2
    m = jnp.max(scores, axis=-1, keepdims=True)
    e = jnp.exp2(scores - m)
    den = jnp.sum(e, axis=-1, keepdims=True)
    o2 = _dot(e.astype(BF16), v_pair)
    o = jnp.where(even_lane, o2[:half], o2[half:]) / jnp.where(even_lane, den[:half], den[half:])
    lse2 = (m + jnp.log2(den)) * LN_2
    return o, lse2[:half], lse2[half:]


def _dil_prompt_kernel(has_prev, dil, q_ref, *refs):
    if has_prev:
        kp_ref, kc_ref, vp_ref, vc_ref, o_ref, lse_ref = refs
    else:
        kc_ref, vc_ref, o_ref, lse_ref = refs
    blk = pl.program_id(2)
    first_token = blk * (ATT_BLOCK * dil) + pl.program_id(1)
    if dil == 1:
        token_rows = pl.ds(pl.multiple_of(first_token, ATT_BLOCK), ATT_BLOCK)
    else:
        token_rows = pl.ds(first_token, ATT_BLOCK, stride=dil)
    n_keys = (2 if has_prev else 1) * ATT_BLOCK
    lane = lax.broadcasted_iota(jnp.int32, (ATT_BLOCK, LANES), 1)
    even_lane = lane < DIL_HD
    qi = lax.broadcasted_iota(jnp.int32, (2 * ATT_BLOCK, n_keys), 0) % ATT_BLOCK
    kj = lax.broadcasted_iota(jnp.int32, (2 * ATT_BLOCK, n_keys), 1)
    if has_prev:
        dist = qi + ATT_BLOCK - kj
        valid = (dist >= 0) & (dist <= DIL_STEPS) & ((kj >= ATT_BLOCK) | (blk > 0))
    else:
        valid = kj <= qi
    lse_all = jnp.zeros((ATT_BLOCK, LANES), F32)
    for p in range(DIL_HEADS // 2):
        sl = slice(p * LANES, (p + 1) * LANES)
        qp = q_ref[:, sl]
        zero = jnp.zeros_like(qp)
        q2 = jnp.concatenate([jnp.where(even_lane, qp, zero), jnp.where(even_lane, zero, qp)], axis=0)
        if has_prev:
            k_pair = jnp.concatenate([kp_ref[:, sl], kc_ref[:, sl]], axis=0)
            v_pair = jnp.concatenate([vp_ref[:, sl], vc_ref[:, sl]], axis=0)
        else:
            k_pair, v_pair = kc_ref[:, sl], vc_ref[:, sl]
        scores = jnp.where(valid, _dot_nt(q2, k_pair), -jnp.inf)
        o, lse_even, lse_odd = _pair_softmax_pv(scores, v_pair, even_lane)
        o_ref[p, token_rows, :] = o
        lse_all = jnp.where(lane == 2 * p, lse_even, jnp.where(lane == 2 * p + 1, lse_odd, lse_all))
    lse_ref[token_rows, :] = lse_all


def _dil_prompt_call(q, kb, vb, group):
    n_b, dil, length, _ = kb.shape
    n_t = dil * length
    n_blk = length // ATT_BLOCK
    has_prev = n_blk > 1
    blk_shape = (None, None, ATT_BLOCK, D_MODEL)
    cur = pl.BlockSpec(blk_shape, lambda b, r, i: (b, r, i, 0))
    prev = pl.BlockSpec(blk_shape, lambda b, r, i: (b, r, jnp.maximum(i - 1, 0), 0))
    if has_prev:
        in_specs, operands = [cur, prev, cur, prev, cur], (q, kb, kb, vb, vb)
    else:
        in_specs, operands = [cur, cur, cur], (q, kb, vb)
    return pl.pallas_call(
        functools.partial(_dil_prompt_kernel, has_prev, dil),
        grid=(n_b, dil, n_blk),
        in_specs=in_specs,
        out_specs=[pl.BlockSpec((None, D_MODEL // LANES, n_t, LANES), lambda b, r, i: (b, 0, 0, 0)),
                   pl.BlockSpec((None, n_t, LANES), lambda b, r, i: (b, 0, 0))],
        out_shape=[jax.ShapeDtypeStruct((n_b, D_MODEL // LANES, n_t, LANES), F32),
                   jax.ShapeDtypeStruct((n_b, n_t, LANES), F32)],
        compiler_params=_params("arbitrary", "arbitrary", "arbitrary"),
        name="dilated_prompt",
    )(*operands)


SAMPLE_HEAD_CHUNK = 4


def _sample_group_dense(t, q_col, k_win, v_win, k_new, v_new):
    n_t = k_new.shape[-1]
    pos = lax.broadcasted_iota(jnp.int32, (1, 1, k_win.shape[-1]), 2)
    pos_new = lax.broadcasted_iota(jnp.int32, (1, 1, n_t), 2)
    s_c = jnp.where(pos >= t, jnp.sum(k_win * q_col, axis=1, keepdims=True), -jnp.inf)
    s_n = jnp.where(pos_new <= t, jnp.sum(k_new * q_col, axis=1, keepdims=True), -jnp.inf)
    m = jnp.maximum(jnp.max(s_c, axis=2, keepdims=True), jnp.max(s_n, axis=2, keepdims=True))
    e_c, e_n = jnp.exp(s_c - m), jnp.exp(s_n - m)
    den = jnp.sum(e_c, axis=2, keepdims=True) + jnp.sum(e_n, axis=2, keepdims=True)
    o = (jnp.sum(v_win * e_c, axis=2, keepdims=True) + jnp.sum(v_new * e_n, axis=2, keepdims=True)) / den
    return o, m + jnp.log(den)


def _sample_group_dilated(dil, q_g, k_win, v_win, k_new, v_new):
    n_t = q_g.shape[-1]
    width = k_win.shape[-1]
    n_col = width // LANES
    res = lax.broadcasted_iota(jnp.int32, (1, 1, LANES), 2) % dil
    res_w = lax.broadcasted_iota(jnp.int32, (1, 1, width), 2) % dil
    q_pat = jnp.zeros(q_g.shape[:2] + (LANES,), F32)
    for t in range(n_t):
        q_pat = jnp.where(res == t, q_g[:, :, t:t + 1], q_pat)
    s_row = jnp.sum(k_win * jnp.concatenate([q_pat] * n_col, axis=2), axis=1, keepdims=True)
    s_new = jnp.sum(k_new * q_g, axis=1, keepdims=True)
    m_t = [jnp.maximum(jnp.max(jnp.where(res_w == t, s_row, -jnp.inf), axis=2, keepdims=True), s_new[:, :, t:t + 1])
           for t in range(n_t)]
    m_row = jnp.zeros_like(s_row)
    for t in range(n_t):
        m_row = jnp.where(res_w == t, m_t[t], m_row)
    e = jnp.exp(jnp.where(res_w < n_t, s_row, -jnp.inf) - m_row)
    acc = v_win * e
    acc_fold = sum(acc[:, :, c * LANES:(c + 1) * LANES] for c in range(n_col))
    e_fold = sum(e[:, :, c * LANES:(c + 1) * LANES] for c in range(n_col))
    out = []
    for t in range(n_t):
        e_new = jnp.exp(s_new[:, :, t:t + 1] - m_t[t])
        den = jnp.sum(jnp.where(res == t, e_fold, 0.0), axis=2, keepdims=True) + e_new
        o = (jnp.sum(jnp.where(res == t, acc_fold, 0.0), axis=2, keepdims=True) + e_new * v_new[:, :, t:t + 1]) / den
        out.append((o, m_t[t] + jnp.log(den)))
    return out


def _sample_attention(q_ref, k_ref, v_ref, kn_ref, vn_ref, o_ref):
    n_t = q_ref.shape[-1]
    n_s = k_ref.shape[-1]
    pos_new = lax.broadcasted_iota(jnp.int32, (1, 1, n_t), 2)
    k_new, v_new = kn_ref[...], vn_ref[...]
    groups = []
    for g, dil in enumerate(DIL_DILATIONS):
        lo = n_s - DIL_STEPS * dil
        k_win, v_win = k_ref[:, :, lo:], v_ref[:, :, lo:]
        if dil == 1:
            groups.append([_sample_group_dense(t, q_ref[g, :, :, t:t + 1], k_win, v_win, k_new, v_new)
                           for t in range(n_t)])
        else:
            groups.append(_sample_group_dilated(dil, q_ref[g], k_win, v_win, k_new, v_new))
    merged = jnp.zeros(k_new.shape, F32)
    for t in range(n_t):
        parts = [grp[t] for grp in groups]
        m = functools.reduce(jnp.maximum, [lse for _, lse in parts])
        wts = [jnp.exp(lse - m) for _, lse in parts]
        o = sum(wt * o for wt, (o, _) in zip(wts, parts)) / sum(wts)
        merged = jnp.where(pos_new == t, o, merged)
    o_ref[...] = merged


class _SampleAttentionGuest:
    def __init__(self, q, cache_k, cache_v, k_new, v_new):
        self.operands = (q, cache_k, cache_v, k_new, v_new)
        self.n_b, self.n_t = q.shape[0], q.shape[-1]
        wb = cache_k.shape[-1]
        assert wb == DIL_STEPS * DIL_DILATIONS[-1] and DIL_DILATIONS[0] == 1 and self.n_t <= DIL_DILATIONS[1]
        assert all(wb % dil == 0 and LANES % dil == 0 for dil in DIL_DILATIONS)
        self.wb = wb
        self.outputs = []

    def specs(self, chunk):
        hc = SAMPLE_HEAD_CHUNK
        cache = pl.BlockSpec((None, hc, DIL_HD, self.wb), lambda i: (i, chunk, 0, 0))
        new = pl.BlockSpec((None, hc, DIL_HD, self.n_t), lambda i: (i, chunk, 0, 0))
        q = pl.BlockSpec((None, N_GROUPS, hc, DIL_HD, self.n_t), lambda i: (i, 0, chunk, 0, 0))
        out_shape = jax.ShapeDtypeStruct((self.n_b, hc, DIL_HD, self.n_t), F32)
        return [q, cache, cache, new, new], pl.BlockSpec((None, hc, DIL_HD, self.n_t), lambda i: (i, 0, 0, 0)), out_shape

    def result(self):
        assert len(self.outputs) * SAMPLE_HEAD_CHUNK == DIL_HEADS
        return jnp.concatenate(self.outputs, axis=1)


def _att_merge(o_refs, lse_refs):
    lses = [l_ref[...] for l_ref in lse_refs]
    m = functools.reduce(jnp.maximum, lses)
    e = [jnp.exp(l - m) for l in lses]
    den = sum(e)
    head = lax.broadcasted_iota(jnp.int32, (LANES, D_MODEL), 0)
    col_head = lax.broadcasted_iota(jnp.int32, (LANES, D_MODEL), 1) // DIL_HD
    spread = (head == col_head).astype(BF16)
    merged = None
    for o_ref, eg in zip(o_refs, e):
        wt = eg / den
        wt_hi = wt.astype(BF16)
        wt_lo = (wt - wt_hi.astype(F32)).astype(BF16)
        wide = _dot(wt_hi, spread) + _dot(wt_lo, spread)
        part = wide * jnp.concatenate([o_ref[c] for c in range(o_ref.shape[0])], axis=1)
        merged = part if merged is None else merged + part
    return merged


def _att_out_kernel(merged_input, *refs):
    if merged_input:
        o_ref, x_ref, gt_ref, wout_ref, g_ref, b_ref, y_ref = refs
        o = o_ref[...]
    else:
        o_refs, lse_refs = refs[0:N_GROUPS], refs[N_GROUPS:2 * N_GROUPS]
        x_ref, gt_ref, wout_ref, g_ref, b_ref, y_ref = refs[2 * N_GROUPS:]
        o = _att_merge(o_refs, lse_refs)
    y = _dot(o.astype(BF16), wout_ref[...])
    z = ALPHA * x_ref[...] + (1.0 + gt_ref[...]) * y
    y_ref[...] = _layer_norm(z, g_ref[...], b_ref[...])


def _att_out_call(rows, att, x, layer, w_out, ln_g, ln_b):
    merged_input = rows.per_row
    if merged_input:
        att_specs = [rows.row_spec(D_MODEL)]
        operands = [att]
    else:
        outs, lses = att
        tps = rows.tiles_per_sample
        o_spec = pl.BlockSpec((None, D_MODEL // LANES, rows.tile, LANES), lambda i: (i // tps, 0, i % tps, 0))
        lse_spec = pl.BlockSpec((None, rows.tile, LANES), lambda i: (i // tps, i % tps, 0))
        att_specs = [o_spec] * N_GROUPS + [lse_spec] * N_GROUPS
        operands = list(outs) + list(lses)
    return pl.pallas_call(
        functools.partial(_att_out_kernel, merged_input),
        grid=(rows.steps,),
        in_specs=att_specs + [
            rows.row_spec(D_MODEL),
            rows.mod_spec(layer, 5),
            _const_spec((D_MODEL, D_MODEL), (0, 0)),
        ] + _ln_specs(layer, 1),
        out_specs=rows.row_spec(D_MODEL),
        out_shape=jax.ShapeDtypeStruct((rows.n_rows, D_MODEL), F32),
        compiler_params=_params("arbitrary"),
        name="att_out",
    )(*operands, x, rows.mods, w_out, ln_g, ln_b)


def _rope_tables(positions):
    half = DIL_HD // 2
    inv = ROPE_THETA ** (-jnp.arange(half, dtype=F32) / half)
    ang = positions.astype(F32)[:, None] * inv[None, :]
    cos, sin = jnp.cos(ang), jnp.sin(ang)
    return jnp.concatenate([cos, cos, cos, cos], axis=1), jnp.concatenate([-sin, sin, -sin, sin], axis=1)


def _trunk(rows, ffn, x, n_b, n_t, positions, gla_state, w):
    cos_t, sin_t = _rope_tables(positions)
    if rows.per_row:
        cos_t, sin_t = jnp.tile(cos_t, (n_b, 1)), jnp.tile(sin_t, (n_b, 1))

    x = ffn(x, 0, 1)
    q, k, v, la, r = _gla_proj_call(rows.wide(), x, 0, w['gla_qkv'], w['gla_g'], w['gla_r'], w['gla_g2'], w['gla_bg'])
    if rows.per_row:
        def cols(a):
            return a.reshape(n_b, n_t, GLA_HEADS, GLA_DK).transpose(0, 2, 3, 1)
        o, s_new = _gla_step_call(cols(q), cols(k), cols(la), v.reshape(n_b, n_t, GLA_V), gla_state)
    else:
        o, s_new = _gla_rec_call(q.reshape(n_b, n_t, GLA_QK), k.reshape(n_b, n_t, GLA_QK),
                                 v.reshape(n_b, n_t, GLA_V), la.reshape(n_b, n_t, GLA_QK), gla_state)
    x = _gla_out_call(rows.wide(), o.reshape(n_b * n_t, GLA_V), r, x, 0, w['gla_gon'], w['gla_out'], w['ln_g'], w['ln_b'])
    x = ffn(x, 0, 2)

    k_rows, v_rows, *kv_groups = _kv_call(rows, x, cos_t, sin_t, w['kv'])
    if rows.per_row:
        k_rows, v_rows = (a.reshape(n_b, n_t, DIL_HEADS, DIL_HD) for a in (k_rows, v_rows))
    else:
        k_rows, v_rows = (a.reshape(n_b, DIL_HEADS, DIL_HD, n_t).transpose(0, 3, 1, 2) for a in (k_rows, v_rows))

    x = ffn(x, 1, 1)
    qd = _q_call(rows, x, 1, cos_t, sin_t, w['q'])
    att = yield qd, k_rows, v_rows, kv_groups
    x = _att_out_call(rows, att, x, 1, w['att_out'], w['ln_g'], w['ln_b'])
    x = ffn(x, 1, 2)
    return x, s_new, k_rows, v_rows


def _finish(trunk, att):
    try:
        trunk.send(att)
    except StopIteration as done:
        return done.value
    raise AssertionError("trunk yielded twice")


def kernel(x_prompt, x_sample, state_gla, cache_k, cache_v, c_prompt, c_sample, w_ada, b_ada, ln_g, ln_b, w_ffn1_up, w_ffn1_down, w_ffn2_up, w_ffn2_down, w_in_a, w_gate2_a, b_gate_a, g_onorm_a, w_out_a, w_ada_kv, b_ada_kv, w_kv, w_q_b, w_out_b):
    n_bp, n_tp = x_prompt.shape[:2]
    n_bs, n_ts = x_sample.shape[:2]
    assert N_A_LAYERS == 1 and DEPTH == 2
    assert n_tp % (ATT_BLOCK * DIL_DILATIONS[-1]) == 0 and n_tp % GLA_TILE == 0 and n_tp <= MAX_WINDOW
    assert n_ts < 8

    n_rs = n_bs * n_ts
    c_all = jnp.concatenate([jnp.repeat(c_sample, n_ts, axis=0), c_prompt], axis=0)
    mods = _ada_call(c_all, w_ada, b_ada, 1536)
    kv_mods = _ada_call(c_all, w_ada_kv[None], b_ada_kv[None], 1024)
    rows_p = _Rows(n_bp * n_tp, n_tp, mods[:, n_rs:, None, :], kv_mods[:, n_rs:, None, :])
    rows_s = _Rows(n_rs, n_ts, mods, kv_mods)

    w_in = w_in_a[0]
    gate_lo = 2 * GLA_QK + GLA_V
    w = {
        'ln_g': ln_g[:, :, None, :], 'ln_b': ln_b[:, :, None, :],
        'gla_qkv': w_in[:, :gate_lo].astype(BF16),
        'gla_g': jnp.pad(w_in[:, gate_lo:gate_lo + GLA_RANK], ((0, 0), (0, LANES - GLA_RANK))).astype(BF16),
        'gla_r': w_in[:, gate_lo + GLA_RANK:].astype(BF16),
        'gla_g2': jnp.pad(w_gate2_a[0], ((0, LANES - GLA_RANK), (0, 0))).astype(BF16),
        'gla_bg': b_gate_a[0][None, :],
        'gla_gon': g_onorm_a[0][None, :],
        'gla_out': w_out_a[0].astype(BF16),
        'kv': w_kv.astype(BF16),
        'q': w_q_b[0].astype(BF16),
        'att_out': w_out_b[0].astype(BF16),
    }

    ffn_f32 = {1: (w_ffn1_up, w_ffn1_down), 2: (w_ffn2_up, w_ffn2_down)}
    ffn_args = {1: (0, 0), 2: (6, 2)}
    last = (DEPTH - 1, 2)
    up_last, down_last = ffn_f32[2][0][DEPTH - 1], ffn_f32[2][1][DEPTH - 1]
    ffn_bf16 = {last: (up_last[:, :D_FF].astype(BF16), up_last[:, D_FF:].astype(BF16), down_last.astype(BF16))}

    def ffn_sample(x, layer, which):
        if (layer, which) in ffn_bf16:
            return _ffn_call(rows_s, x, layer, *ffn_args[which], *ffn_bf16[layer, which], w['ln_g'], w['ln_b'])
        y, *ffn_bf16[layer, which] = _ffn_cast_call(rows_s, x, layer, *ffn_args[which], *ffn_f32[which],
                                                    w['ln_g'], w['ln_b'])
        return y

    def ffn_prompt(x, layer, which):
        return _ffn_call(rows_p, x, layer, *ffn_args[which], *ffn_bf16[layer, which], w['ln_g'], w['ln_b'],
                         guest=guest)

    trunk_s = _trunk(rows_s, ffn_sample, x_sample.reshape(-1, D_MODEL), n_bs, n_ts,
                     PAST_LEN + jnp.arange(n_ts), state_gla[0], w)
    q_s, k_s, v_s, _ = next(trunk_s)
    guest = _SampleAttentionGuest(
        q_s.reshape(n_bs, n_ts, N_GROUPS, DIL_HEADS, DIL_HD).transpose(0, 2, 3, 4, 1),
        cache_k.transpose(0, 2, 3, 1), cache_v.transpose(0, 2, 3, 1),
        k_s.transpose(0, 2, 3, 1), v_s.transpose(0, 2, 3, 1))

    zero_state = jnp.zeros((n_bp, GLA_HEADS, GLA_DK, GLA_DV), F32)
    trunk_p = _trunk(rows_p, ffn_prompt, x_prompt.reshape(-1, D_MODEL),
                     n_bp, n_tp, jnp.arange(n_tp), zero_state, w)
    q_p, _, _, kv_p = next(trunk_p)
    att_p = tuple(zip(*[_dil_prompt_call(q_p[g], kv_p[g], kv_p[N_GROUPS + g], g) for g in range(N_GROUPS)]))
    y_p, s_p, k_p, v_p = _finish(trunk_p, att_p)
    att_s = guest.result().transpose(0, 3, 1, 2).reshape(n_bs * n_ts, D_MODEL)
    y_s, s_s, k_s, v_s = _finish(trunk_s, att_s)
    return (y_p.reshape(n_bp, n_tp, D_MODEL), y_s.reshape(n_bs, n_ts, D_MODEL),
            s_p[None], s_s[None], k_p, v_p, k_s, v_s)
```

```python
import functools

import jax
import jax.numpy as jnp
from jax import lax
from jax.experimental import pallas as pl
from jax.experimental.pallas import tpu as pltpu

F32 = jnp.float32
BF16 = jnp.bfloat16

D_MODEL = 1024
DEPTH = 2
PAST_LEN = 16384
N_A_LAYERS = DEPTH // 2
GLA_HEADS = 4
GLA_QK = D_MODEL // 2
GLA_V = D_MODEL
GLA_DK = GLA_QK // GLA_HEADS
GLA_DV = GLA_V // GLA_HEADS
GLA_RANK = 16
GLA_TAU = 16.0
GLA_NORM_EPS = 1e-6
DIL_HEADS = 16
DIL_HD = D_MODEL // DIL_HEADS
DIL_WINDOWS = (128, 512, 2048)
DIL_DILATIONS = (1, 4, 16)
N_GROUPS = len(DIL_WINDOWS)
DIL_STEPS = 128
MAX_WINDOW = max(DIL_WINDOWS)
DIL_SCALE = DIL_HD ** -0.5
ROPE_THETA = 10000.0
D_FF = 2816
FFN_RES = 0.5
ALPHA = (2 * DEPTH) ** 0.25
LN_EPS = 1e-5
N_MOD = 9

LANES = 128
ROW_TILE = 512
WIDE_TILE = 1024
LOG2_E = 1.4426950408889634
LN_2 = 0.6931471805599453
FF_CHUNK = 256
GLA_CHUNK = 128
GLA_TILE = 512
GLA_FACTOR_BOUND = 80.0
ATT_BLOCK = DIL_STEPS
VMEM_LIMIT = 56 * 1024 * 1024


def _params(*sem):
    return pltpu.CompilerParams(dimension_semantics=sem, vmem_limit_bytes=VMEM_LIMIT)


def _silu(x):
    return x / (1.0 + jnp.exp(-x))


def _layer_norm(z, g, b):
    mu = jnp.mean(z, axis=-1, keepdims=True)
    zc = z - mu
    var = jnp.mean(zc * zc, axis=-1, keepdims=True)
    return zc * lax.rsqrt(var + LN_EPS) * g + b


def _dot(a, b):
    return jnp.dot(a, b, preferred_element_type=F32)


def _dot_nt(a, b):
    return lax.dot_general(a, b, (((1,), (1,)), ((), ())), preferred_element_type=F32)


def _rope(x, cos_t, sin_t):
    width = x.shape[1]
    lane = lax.broadcasted_iota(jnp.int32, x.shape, 1)
    first_half = (lane % DIL_HD) < (DIL_HD // 2)
    partner = jnp.where(first_half, pltpu.roll(x, width - DIL_HD // 2, 1), pltpu.roll(x, DIL_HD // 2, 1))
    reps = width // LANES
    cos_w = jnp.concatenate([cos_t] * reps, axis=1)
    sin_w = jnp.concatenate([sin_t] * reps, axis=1)
    return x * cos_w + partner * sin_w


def _ada_kernel(c_ref, w_ref, b_ref, o_ref):
    s = _silu(c_ref[...]).astype(BF16)
    o_ref[...] = _dot(s, w_ref[...].astype(BF16)) + b_ref[...]


def _ada_call(c_all, w, b, tn):
    n_l, _, n = w.shape
    rows = c_all.shape[0]
    return pl.pallas_call(
        _ada_kernel,
        grid=(n_l, n // tn),
        in_specs=[
            pl.BlockSpec((rows, D_MODEL), lambda l, j: (0, 0)),
            pl.BlockSpec((None, D_MODEL, tn), lambda l, j: (l, 0, j)),
            pl.BlockSpec((None, 1, tn), lambda l, j: (l, 0, j)),
        ],
        out_specs=pl.BlockSpec((None, rows, tn), lambda l, j: (l, 0, j)),
        out_shape=jax.ShapeDtypeStruct((n_l, rows, n), F32),
        compiler_params=_params("arbitrary", "arbitrary"),
        name="ada_mod",
    )(c_all, w, b.reshape(n_l, 1, n))


class _Rows:
    def __init__(self, n_rows, rows_per_sample, mods, kv_mods, tile=ROW_TILE):
        self.n_rows = n_rows
        self.rows_per_sample = rows_per_sample
        self.per_row = rows_per_sample < 8
        self.tile = n_rows if self.per_row else min(tile, rows_per_sample)
        self.steps = n_rows // self.tile
        self.tiles_per_sample = 1 if self.per_row else rows_per_sample // self.tile
        self.mods = mods
        self.kv_mods = kv_mods

    def wide(self):
        return _Rows(self.n_rows, self.rows_per_sample, self.mods, self.kv_mods, WIDE_TILE)

    def row_spec(self, width):
        return pl.BlockSpec((self.tile, width), lambda i: (i, 0))

    def mod_spec(self, layer, k):
        if self.per_row:
            return pl.BlockSpec((None, self.tile, D_MODEL), lambda i: (layer, 0, k))
        tps = self.tiles_per_sample
        return pl.BlockSpec((None, None, 1, D_MODEL), lambda i: (layer, i // tps, 0, k))


def _const_spec(shape, index):
    return pl.BlockSpec(shape, lambda i: index, pipeline_mode=pl.Buffered(1))


def _ln_specs(layer, j):
    return [_const_spec((None, None, 1, D_MODEL), (layer, j, 0, 0))] * 2


def _ffn_kernel(has_guest, x_ref, sh_ref, sc_ref, gt_ref, wa_ref, wu_ref, wdn_ref, g_ref, b_ref, *rest):
    if has_guest:
        *guest_in, o_ref, guest_out = rest
        _sample_attention(*guest_in, guest_out)
    else:
        o_ref, = rest
    x = x_ref[...]
    h = (x * (1.0 + sc_ref[...]) + sh_ref[...]).astype(BF16)
    y = None
    for lo in range(0, D_FF, FF_CHUNK):
        a = _dot(h, wa_ref[:, lo:lo + FF_CHUNK])
        u = _dot(h, wu_ref[:, lo:lo + FF_CHUNK])
        part = _dot((_silu(a) * u).astype(BF16), wdn_ref[lo:lo + FF_CHUNK, :])
        y = part if y is None else y + part
    z = ALPHA * x + (1.0 + gt_ref[...]) * (FFN_RES * y)
    o_ref[...] = _layer_norm(z, g_ref[...], b_ref[...])


def _ffn_cast_kernel(n_cast_only, x_ref, sh_ref, sc_ref, gt_ref, wa_ref, wu_ref, wdn_ref, *rest):
    extra_in, (g_ref, b_ref, o_ref, wa_out, wu_out, wdn_out), rest = rest[:n_cast_only], rest[n_cast_only:n_cast_only + 6], rest[n_cast_only + 6:]
    extra_out, y_scr = rest[:n_cast_only], rest[n_cast_only]
    for src, dst in zip(extra_in, extra_out):
        dst[...] = src[...].astype(BF16)
    c = pl.program_id(0)
    wa, wu, wdn = wa_ref[...].astype(BF16), wu_ref[...].astype(BF16), wdn_ref[...].astype(BF16)
    wa_out[...], wu_out[...], wdn_out[...] = wa, wu, wdn
    x = x_ref[...]
    h = (x * (1.0 + sc_ref[...]) + sh_ref[...]).astype(BF16)
    part = _dot((_silu(_dot(h, wa)) * _dot(h, wu)).astype(BF16), wdn)

    @pl.when(c == 0)
    def _():
        y_scr[...] = part

    @pl.when(c > 0)
    def _():
        y_scr[...] += part

    @pl.when(c == pl.num_programs(0) - 1)
    def _():
        z = ALPHA * x + (1.0 + gt_ref[...]) * (FFN_RES * y_scr[...])
        o_ref[...] = _layer_norm(z, g_ref[...], b_ref[...])


def _ffn_cast_call(rows, x, layer, mod0, ln_j, w_up, w_down, ln_g, ln_b, cast_only=()):
    assert rows.steps == 1
    n_chunks = D_FF // FF_CHUNK

    def whole(spec):
        return pl.BlockSpec(spec.block_shape, lambda c, index_map=spec.index_map: index_map(0))

    f32_specs = [pl.BlockSpec((None, D_MODEL, FF_CHUNK), lambda c: (layer, 0, c)),
                 pl.BlockSpec((None, D_MODEL, FF_CHUNK), lambda c: (layer, 0, n_chunks + c)),
                 pl.BlockSpec((None, FF_CHUNK, D_MODEL), lambda c: (layer, c, 0))]
    bf16_specs = [pl.BlockSpec((D_MODEL, FF_CHUNK), lambda c: (0, c)),
                  pl.BlockSpec((D_MODEL, FF_CHUNK), lambda c: (0, c)),
                  pl.BlockSpec((FF_CHUNK, D_MODEL), lambda c: (c, 0))]
    bf16_shapes = [jax.ShapeDtypeStruct((D_MODEL, D_FF), BF16), jax.ShapeDtypeStruct((D_MODEL, D_FF), BF16),
                   jax.ShapeDtypeStruct((D_FF, D_MODEL), BF16)]
    extra = [w for up, down in cast_only for w in (up, up, down)]
    return pl.pallas_call(
        functools.partial(_ffn_cast_kernel, len(extra)),
        grid=(n_chunks,),
        in_specs=[whole(s) for s in (rows.row_spec(D_MODEL), rows.mod_spec(layer, mod0),
                                     rows.mod_spec(layer, mod0 + 1), rows.mod_spec(layer, mod0 + 2))]
        + f32_specs + f32_specs * len(cast_only) + [whole(s) for s in _ln_specs(layer, ln_j)],
        out_specs=[whole(rows.row_spec(D_MODEL))] + bf16_specs * (1 + len(cast_only)),
        out_shape=[jax.ShapeDtypeStruct((rows.n_rows, D_MODEL), F32)] + bf16_shapes * (1 + len(cast_only)),
        scratch_shapes=[pltpu.VMEM((rows.n_rows, D_MODEL), F32)],
        compiler_params=_params("arbitrary"),
        name="ffn_cast",
    )(x, rows.mods, rows.mods, rows.mods, w_up, w_up, w_down, *extra, ln_g, ln_b)


def _ffn_call(rows, x, layer, mod0, ln_j, w_gate, w_up, w_down, ln_g, ln_b, guest=None):
    in_specs = [
        rows.row_spec(D_MODEL),
        rows.mod_spec(layer, mod0), rows.mod_spec(layer, mod0 + 1), rows.mod_spec(layer, mod0 + 2),
        _const_spec((D_MODEL, D_FF), (0, 0)),
        _const_spec((D_MODEL, D_FF), (0, 0)),
        _const_spec((D_FF, D_MODEL), (0, 0)),
    ] + _ln_specs(layer, ln_j)
    operands = [x, rows.mods, rows.mods, rows.mods, w_gate, w_up, w_down, ln_g, ln_b]
    out_specs = [rows.row_spec(D_MODEL)]
    out_shape = [jax.ShapeDtypeStruct((rows.n_rows, D_MODEL), F32)]
    if guest is not None:
        assert guest.n_b == rows.steps
        guest_in, guest_out, guest_shape = guest.specs(len(guest.outputs))
        in_specs += guest_in
        operands += guest.operands
        out_specs.append(guest_out)
        out_shape.append(guest_shape)
    outs = pl.pallas_call(
        functools.partial(_ffn_kernel, guest is not None),
        grid=(rows.steps,),
        in_specs=in_specs,
        out_specs=out_specs,
        out_shape=out_shape,
        compiler_params=_params("arbitrary"),
        name="ffn",
    )(*operands)
    if guest is not None:
        guest.outputs.append(outs[1])
    return outs[0]


def _gla_proj_kernel(x_ref, sh_ref, sc_ref, wqkv_ref, wg_ref, wr_ref, wg2_ref, bg_ref,
                     q_ref, k_ref, v_ref, la_ref, r_ref):
    h = (x_ref[...] * (1.0 + sc_ref[...]) + sh_ref[...]).astype(BF16)
    qkv = _dot(h, wqkv_ref[...])
    q_ref[...] = qkv[:, :GLA_QK] * (GLA_DK ** -0.5)
    k_ref[...] = qkv[:, GLA_QK:2 * GLA_QK]
    v_ref[...] = qkv[:, 2 * GLA_QK:].astype(v_ref.dtype)
    g_lr = _dot(h, wg_ref[...]).astype(BF16)
    gate = _dot(g_lr, wg2_ref[...]) + bg_ref[...]
    log_sig = jnp.minimum(gate, 0.0) - jnp.log(1.0 + jnp.exp(-jnp.abs(gate)))
    la_ref[...] = log_sig / GLA_TAU
    r_ref[...] = _silu(_dot(h, wr_ref[...])).astype(r_ref.dtype)


def _gla_proj_call(rows, x, layer, w_qkv, w_g, w_r, w_g2, b_g):
    widths = (GLA_QK, GLA_QK, GLA_V, GLA_QK, GLA_V)
    narrow = F32 if rows.per_row else BF16
    dtypes = (F32, F32, narrow, F32, narrow)
    return pl.pallas_call(
        _gla_proj_kernel,
        grid=(rows.steps,),
        in_specs=[
            rows.row_spec(D_MODEL),
            rows.mod_spec(layer, 3), rows.mod_spec(layer, 4),
            _const_spec((D_MODEL, 2 * GLA_QK + GLA_V), (0, 0)),
            _const_spec((D_MODEL, LANES), (0, 0)),
            _const_spec((D_MODEL, GLA_V), (0, 0)),
            _const_spec((LANES, GLA_QK), (0, 0)),
            _const_spec((1, GLA_QK), (0, 0)),
        ],
        out_specs=[rows.row_spec(w) for w in widths],
        out_shape=[jax.ShapeDtypeStruct((rows.n_rows, w), dt) for w, dt in zip(widths, dtypes)],
        compiler_params=_params("arbitrary"),
        name="gla_proj",
    )(x, rows.mods, rows.mods, w_qkv, w_g, w_r, w_g2, b_g)


def _prefix_sum(tri, x):
    rest, total = x, None
    for _ in range(3):
        piece = rest.astype(BF16)
        rest = rest - piece.astype(F32)
        part = _dot(tri, piece)
        total = part if total is None else total + part
    return total


def _gla_scores_factored(qh, kh, bh, b_last):
    mid = 0.5 * b_last
    q_t = (qh * jnp.exp(bh - mid)).astype(BF16)
    k_t = (kh * jnp.exp(mid - bh)).astype(BF16)
    return _dot_nt(q_t, k_t)


def _gla_scores_elementwise(qh, kh, bh, k_scr, b_scr):
    k_scr[...] = kh
    b_scr[...] = bh
    col = lax.broadcasted_iota(jnp.int32, (GLA_CHUNK, GLA_CHUNK), 1)

    def body(j, acc):
        kj = k_scr[pl.ds(j, 1), :]
        bj = b_scr[pl.ds(j, 1), :]
        term = qh * kj * jnp.exp(jnp.minimum(bh - bj, 0.0))
        return acc + jnp.where(col == j, jnp.sum(term, axis=1, keepdims=True), 0.0)

    return lax.fori_loop(0, GLA_CHUNK, body, jnp.zeros((GLA_CHUNK, GLA_CHUNK), F32))


def _gla_chunk(rows, b, states, score_fn, q_ref, k_ref, v_ref, o_ref, causal):
    q, k = q_ref[rows, :], k_ref[rows, :]
    b_last = b[GLA_CHUNK - 1:GLA_CHUNK, :]
    q_in = q * jnp.exp(b)
    k_out = k * jnp.exp(b_last - b)
    new_states = []
    for hd, s_prev in enumerate(states):
        ks = slice(hd * GLA_DK, (hd + 1) * GLA_DK)
        vs = slice(hd * GLA_DV, (hd + 1) * GLA_DV)
        vh = v_ref[rows, vs].astype(BF16)
        o_inter = _dot(q_in[:, ks].astype(BF16), s_prev.astype(BF16))
        scores = jnp.where(causal, score_fn(q[:, ks], k[:, ks], b[:, ks], b_last[:, ks]), 0.0)
        o_ref[rows, vs] = (o_inter + _dot(scores.astype(BF16), vh)).astype(o_ref.dtype)
        decay_col = jnp.transpose(jnp.broadcast_to(jnp.exp(b_last[:, ks]), (GLA_DK, GLA_DK)))[:, :1]
        k_out_t = jnp.transpose(k_out[:, ks]).astype(BF16)
        new_states.append(decay_col * s_prev + _dot(k_out_t, vh))
    return new_states


def _gla_rec_kernel(q_ref, k_ref, v_ref, la_ref, s0_ref, o_ref, sfin_ref, s_scr, k_scr, b_scr):
    t = pl.program_id(1)

    @pl.when(t == 0)
    def _():
        s_scr[...] = s0_ref[...]

    row = lax.broadcasted_iota(jnp.int32, (GLA_CHUNK, GLA_CHUNK), 0)
    col = lax.broadcasted_iota(jnp.int32, (GLA_CHUNK, GLA_CHUNK), 1)
    causal = col <= row
    tri = causal.astype(BF16)
    n_chunks = GLA_TILE // GLA_CHUNK

    def decay_prefix(rows):
        return _prefix_sum(tri, la_ref[rows, :])

    chunk_rows = [slice(c * GLA_CHUNK, (c + 1) * GLA_CHUNK) for c in range(n_chunks)]
    prefixes = [decay_prefix(rows) for rows in chunk_rows]
    total_decay = functools.reduce(jnp.maximum, [-b[GLA_CHUNK - 1:GLA_CHUNK, :] for b in prefixes])
    factorable = jnp.max(total_decay) <= GLA_FACTOR_BOUND

    def refs():
        return q_ref, k_ref, v_ref, o_ref, causal

    @pl.when(factorable)
    def _():
        states = [s_scr[hd] for hd in range(GLA_HEADS)]
        for rows, b in zip(chunk_rows, prefixes):
            states = _gla_chunk(rows, b, states, _gla_scores_factored, *refs())
        for hd in range(GLA_HEADS):
            s_scr[hd] = states[hd]

    @pl.when(jnp.logical_not(factorable))
    def _():
        def chunk(c, carry):
            rows = pl.ds(pl.multiple_of(c * GLA_CHUNK, GLA_CHUNK), GLA_CHUNK)
            states = [s_scr[hd] for hd in range(GLA_HEADS)]
            elementwise = functools.partial(_gla_scores_elementwise, k_scr=k_scr, b_scr=b_scr)
            states = _gla_chunk(rows, decay_prefix(rows), states,
                                lambda qh, kh, bh, _: elementwise(qh, kh, bh), *refs())
            for hd in range(GLA_HEADS):
                s_scr[hd] = states[hd]
            return carry

        lax.fori_loop(0, n_chunks, chunk, 0)

    @pl.when(t == pl.num_programs(1) - 1)
    def _():
        sfin_ref[...] = s_scr[...]


def _gla_rec_call(q, k, la, v, s0):
    n_b, n_t = q.shape[:2]

    def seq(width):
        return pl.BlockSpec((None, GLA_TILE, width), lambda b, t: (b, t, 0))

    state = pl.BlockSpec((None, GLA_HEADS, GLA_DK, GLA_DV), lambda b, t: (b, 0, 0, 0))
    return pl.pallas_call(
        _gla_rec_kernel,
        grid=(n_b, n_t // GLA_TILE),
        in_specs=[seq(GLA_QK), seq(GLA_QK), seq(GLA_V), seq(GLA_QK), state],
        out_specs=[seq(GLA_V), state],
        out_shape=[jax.ShapeDtypeStruct((n_b, n_t, GLA_V), BF16),
                   jax.ShapeDtypeStruct((n_b, GLA_HEADS, GLA_DK, GLA_DV), F32)],
        scratch_shapes=[pltpu.VMEM((GLA_HEADS, GLA_DK, GLA_DV), F32),
                        pltpu.VMEM((GLA_CHUNK, GLA_DK), F32),
                        pltpu.VMEM((GLA_CHUNK, GLA_DK), F32)],
        compiler_params=_params("arbitrary", "arbitrary"),
        name="gla_recurrence",
    )(q, k, v, la, s0)


def _gla_step_kernel(q_ref, k_ref, la_ref, v_ref, s0_ref, o_ref, sfin_ref):
    n_t = v_ref.shape[0]
    q, k = q_ref[...], k_ref[...]
    row = lax.broadcasted_iota(jnp.int32, (n_t, n_t), 0)
    col = lax.broadcasted_iota(jnp.int32, (n_t, n_t), 1)
    b = _prefix_sum((col <= row).astype(BF16), la_ref[...])
    b_last = b[n_t - 1:n_t, :]
    q_in = q * jnp.exp(b)
    k_out = k * jnp.exp(b_last - b)
    for hd in range(GLA_HEADS):
        ks = slice(hd * GLA_DK, (hd + 1) * GLA_DK)
        vs = slice(hd * GLA_DV, (hd + 1) * GLA_DV)
        vh = v_ref[:, vs]
        s_prev = s0_ref[hd]
        o = _dot(q_in[:, ks].astype(BF16), s_prev.astype(BF16))
        for u in range(n_t):
            decay = jnp.exp(jnp.minimum(b[:, ks] - b[u:u + 1, ks], 0.0))
            score = jnp.sum(q[:, ks] * k[u:u + 1, ks] * decay, axis=1, keepdims=True)
            o = o + jnp.where(row[:, :1] >= u, score, 0.0) * vh[u:u + 1, :]
        o_ref[:, vs] = o
        decay_col = jnp.transpose(jnp.broadcast_to(jnp.exp(b_last[:, ks]), (8, GLA_DK)))[:, :1]
        sfin_ref[hd] = decay_col * s_prev + lax.dot_general(
            k_out[:, ks].astype(BF16), vh.astype(BF16), (((0,), (0,)), ((), ())), preferred_element_type=F32)


def _gla_step_call(q, k, la, v, s0):
    n_b, n_t = q.shape[:2]

    def tok(width):
        return pl.BlockSpec((None, n_t, width), lambda b: (b, 0, 0))

    state = pl.BlockSpec((None, GLA_HEADS, GLA_DK, GLA_DV), lambda b: (b, 0, 0, 0))
    return pl.pallas_call(
        _gla_step_kernel,
        grid=(n_b,),
        in_specs=[tok(GLA_QK), tok(GLA_QK), tok(GLA_QK), tok(GLA_V), state],
        out_specs=[tok(GLA_V), state],
        out_shape=[jax.ShapeDtypeStruct((n_b, n_t, GLA_V), F32),
                   jax.ShapeDtypeStruct((n_b, GLA_HEADS, GLA_DK, GLA_DV), F32)],
        compiler_params=_params("arbitrary"),
        name="gla_step",
    )(q, k, la, v, s0)


def _gla_out_kernel(o_ref, r_ref, x_ref, gt_ref, gon_ref, wout_ref, g_ref, b_ref, y_ref):
    o = o_ref[...].astype(F32)
    parts = []
    for hd in range(GLA_HEADS):
        oh = o[:, hd * GLA_DV:(hd + 1) * GLA_DV]
        ms = jnp.mean(oh * oh, axis=-1, keepdims=True)
        parts.append(oh * lax.rsqrt(ms + GLA_NORM_EPS) * gon_ref[...])
    gated = (jnp.concatenate(parts, axis=1) * r_ref[...].astype(F32)).astype(BF16)
    y = _dot(gated, wout_ref[...])
    z = ALPHA * x_ref[...] + (1.0 + gt_ref[...]) * y
    y_ref[...] = _layer_norm(z, g_ref[...], b_ref[...])


def _gla_out_call(rows, o, r, x, layer, g_onorm, w_out, ln_g, ln_b):
    return pl.pallas_call(
        _gla_out_kernel,
        grid=(rows.steps,),
        in_specs=[
            rows.row_spec(GLA_V), rows.row_spec(GLA_V), rows.row_spec(D_MODEL),
            rows.mod_spec(layer, 5),
            _const_spec((1, GLA_DV), (0, 0)),
            _const_spec((GLA_V, D_MODEL), (0, 0)),
        ] + _ln_specs(layer, 1),
        out_specs=rows.row_spec(D_MODEL),
        out_shape=jax.ShapeDtypeStruct((rows.n_rows, D_MODEL), F32),
        compiler_params=_params("arbitrary"),
        name="gla_out",
    )(o, r, x, rows.mods, g_onorm, w_out, ln_g, ln_b)


def _gather_stride(scr, start, count, stride):
    return jnp.concatenate([scr[c, pl.ds(start, count, stride=stride), :] for c in range(scr.shape[0])], axis=1)


def _fill_slabs(scr, row0, x):
    for c in range(scr.shape[0]):
        scr[c, row0:row0 + x.shape[0], :] = x[:, c * LANES:(c + 1) * LANES]


def _store_by_residue(x, out_refs, scr_a, scr_b):
    n_rows = x.shape[0]
    s = DIL_DILATIONS[1]
    assert DIL_DILATIONS == (1, s, s * s)
    outs = {o.shape[0]: o for o in out_refs}
    if 1 in outs:
        outs[1][0] = x.astype(BF16)
    if s not in outs and s * s not in outs:
        return
    _fill_slabs(scr_a, 0, x)
    per_class = n_rows // s
    for r in range(s):
        xr = _gather_stride(scr_a, r, per_class, s)
        if s in outs:
            outs[s][r] = xr.astype(BF16)
        if s * s in outs:
            _fill_slabs(scr_b, r * per_class, xr)
    if s * s in outs:
        for r in range(s * s):
            xr = _gather_stride(scr_b, (r % s) * per_class + r // s, per_class // s, s)
            outs[s * s][r] = xr.astype(BF16)


def _kv_kernel(by_residue, x_ref, sh_ref, sc_ref, cos_ref, sin_ref, w_ref, k_ref, v_ref, *rest):
    h = (x_ref[...] * (1.0 + sc_ref[...]) + sh_ref[...]).astype(BF16)
    kv = _dot(h, w_ref[...])
    k = _rope(kv[:, :D_MODEL], cos_ref[...], sin_ref[...])
    v = kv[:, D_MODEL:]
    if by_residue:
        k_ref[...] = jnp.transpose(k)
        v_ref[...] = jnp.transpose(v)
    else:
        k_ref[...] = k.reshape(k_ref.shape)
        v_ref[...] = v.reshape(v_ref.shape)
        rest[0][...] = k
        rest[1][...] = v
    if by_residue:
        _store_by_residue(k, rest[0:N_GROUPS], *rest[-2:])
        _store_by_residue(v, rest[N_GROUPS:2 * N_GROUPS], *rest[-2:])


def _rope_spec(rows):
    if rows.per_row:
        return pl.BlockSpec((rows.tile, LANES), lambda i: (0, 0))
    tps = rows.tiles_per_sample
    return pl.BlockSpec((rows.tile, LANES), lambda i: (i % tps, 0))


def _kv_mod_spec(rows, k):
    if rows.per_row:
        return pl.BlockSpec((None, rows.tile, D_MODEL), lambda i: (0, 0, k))
    tps = rows.tiles_per_sample
    return pl.BlockSpec((None, None, 1, D_MODEL), lambda i: (0, i // tps, 0, k))


def _residue_spec(rows, dil, width):
    tps = rows.tiles_per_sample
    return pl.BlockSpec((None, dil, rows.tile // dil, width), lambda i: (i // tps, 0, i % tps, 0))


def _residue_shapes(rows, width, dtype):
    n_b = rows.n_rows // rows.rows_per_sample
    return [jax.ShapeDtypeStruct((n_b, dil, rows.rows_per_sample // dil, width), dtype) for dil in DIL_DILATIONS]


def _permute_scratch():
    return [pltpu.VMEM((D_MODEL // LANES, ROW_TILE, LANES), F32)] * 2


def _kv_call(rows, x, cos_t, sin_t, w_kv):
    by_residue = not rows.per_row
    scratch = []
    if by_residue:
        tps = rows.tiles_per_sample
        n_b = rows.n_rows // rows.rows_per_sample
        out_specs = [pl.BlockSpec((None, D_MODEL, rows.tile), lambda i: (i // tps, 0, i % tps))] * 2
        out_shape = [jax.ShapeDtypeStruct((n_b, D_MODEL, rows.rows_per_sample), F32)] * 2
    else:
        out_specs = [pl.BlockSpec((rows.tile, DIL_HEADS, DIL_HD), lambda i: (i, 0, 0))] * 2 + [rows.row_spec(D_MODEL)] * 2
        out_shape = ([jax.ShapeDtypeStruct((rows.n_rows, DIL_HEADS, DIL_HD), F32)] * 2
                     + [jax.ShapeDtypeStruct((rows.n_rows, D_MODEL), F32)] * 2)
    if by_residue:
        out_specs += [_residue_spec(rows, dil, D_MODEL) for dil in DIL_DILATIONS] * 2
        out_shape += _residue_shapes(rows, D_MODEL, BF16) * 2
        scratch = _permute_scratch()
    return pl.pallas_call(
        functools.partial(_kv_kernel, by_residue),
        grid=(rows.steps,),
        in_specs=[
            rows.row_spec(D_MODEL),
            _kv_mod_spec(rows, 0), _kv_mod_spec(rows, 1),
            _rope_spec(rows), _rope_spec(rows),
            _const_spec((D_MODEL, 2 * D_MODEL), (0, 0)),
        ],
        out_specs=out_specs,
        out_shape=out_shape,
        scratch_shapes=scratch,
        compiler_params=_params("arbitrary"),
        name="kv_proj",
    )(x, rows.kv_mods, rows.kv_mods, cos_t, sin_t, w_kv)


def _q_kernel(by_residue, x_ref, sh_ref, sc_ref, cos_ref, sin_ref, w_ref, *rest):
    h = (x_ref[...] * (1.0 + sc_ref[...]) + sh_ref[...]).astype(BF16)
    q = _rope(_dot(h, w_ref[...]), cos_ref[...], sin_ref[...])
    if by_residue:
        q = q * (DIL_SCALE * LOG2_E)
        for g in range(N_GROUPS):
            _store_by_residue(q[:, g * D_MODEL:(g + 1) * D_MODEL], rest[g:g + 1], *rest[-2:])
    else:
        rest[0][...] = q * DIL_SCALE


def _q_call(rows, x, layer, cos_t, sin_t, w_q):
    width = N_GROUPS * D_MODEL
    by_residue = not rows.per_row
    if by_residue:
        out_specs = [_residue_spec(rows, dil, D_MODEL) for dil in DIL_DILATIONS]
        out_shape = _residue_shapes(rows, D_MODEL, BF16)
        scratch = _permute_scratch()
    else:
        out_specs = rows.row_spec(width)
        out_shape = jax.ShapeDtypeStruct((rows.n_rows, width), F32)
        scratch = []
    return pl.pallas_call(
        functools.partial(_q_kernel, by_residue),
        grid=(rows.steps,),
        in_specs=[
            rows.row_spec(D_MODEL),
            rows.mod_spec(layer, 3), rows.mod_spec(layer, 4),
            _rope_spec(rows), _rope_spec(rows),
            _const_spec((D_MODEL, width), (0, 0)),
        ],
        out_specs=out_specs,
        out_shape=out_shape,
        scratch_shapes=scratch,
        compiler_params=_params("arbitrary"),
        name="q_proj",
    )(x, rows.mods, rows.mods, cos_t, sin_t, w_q)


def _pair_softmax_pv(scores, v_pair, even_lane):
    half = scores.shape[0] // 2
    m = jnp.max(scores, axis=-1, keepdims=True)
    e = jnp.exp2(scores - m)
    den = jnp.sum(e, axis=-1, keepdims=True)
    o2 = _dot(e.astype(BF16), v_pair)
    o = jnp.where(even_lane, o2[:half], o2[half:]) / jnp.where(even_lane, den[:half], den[half:])
    lse2 = (m + jnp.log2(den)) * LN_2
    return o, lse2[:half], lse2[half:]


def _dil_prompt_kernel(has_prev, dil, q_ref, *refs):
    if has_prev:
        kp_ref, kc_ref, vp_ref, vc_ref, o_ref, lse_ref = refs
    else:
        kc_ref, vc_ref, o_ref, lse_ref = refs
    blk = pl.program_id(2)
    first_token = blk * (ATT_BLOCK * dil) + pl.program_id(1)
    if dil == 1:
        token_rows = pl.ds(pl.multiple_of(first_token, ATT_BLOCK), ATT_BLOCK)
    else:
        token_rows = pl.ds(first_token, ATT_BLOCK, stride=dil)
    n_keys = (2 if has_prev else 1) * ATT_BLOCK
    lane = lax.broadcasted_iota(jnp.int32, (ATT_BLOCK, LANES), 1)
    even_lane = lane < DIL_HD
    qi = lax.broadcasted_iota(jnp.int32, (2 * ATT_BLOCK, n_keys), 0) % ATT_BLOCK
    kj = lax.broadcasted_iota(jnp.int32, (2 * ATT_BLOCK, n_keys), 1)
    if has_prev:
        dist = qi + ATT_BLOCK - kj
        valid = (dist >= 0) & (dist <= DIL_STEPS) & ((kj >= ATT_BLOCK) | (blk > 0))
    else:
        valid = kj <= qi
    lse_all = jnp.zeros((ATT_BLOCK, LANES), F32)
    for p in range(DIL_HEADS // 2):
        sl = slice(p * LANES, (p + 1) * LANES)
        qp = q_ref[:, sl]
        zero = jnp.zeros_like(qp)
        q2 = jnp.concatenate([jnp.where(even_lane, qp, zero), jnp.where(even_lane, zero, qp)], axis=0)
        if has_prev:
            k_pair = jnp.concatenate([kp_ref[:, sl], kc_ref[:, sl]], axis=0)
            v_pair = jnp.concatenate([vp_ref[:, sl], vc_ref[:, sl]], axis=0)
        else:
            k_pair, v_pair = kc_ref[:, sl], vc_ref[:, sl]
        scores = jnp.where(valid, _dot_nt(q2, k_pair), -jnp.inf)
        o, lse_even, lse_odd = _pair_softmax_pv(scores, v_pair, even_lane)
        o_ref[p, token_rows, :] = o
        lse_all = jnp.where(lane == 2 * p, lse_even, jnp.where(lane == 2 * p + 1, lse_odd, lse_all))
    lse_ref[token_rows, :] = lse_all


def _dil_prompt_call(q, kb, vb, group):
    n_b, dil, length, _ = kb.shape
    n_t = dil * length
    n_blk = length // ATT_BLOCK
    has_prev = n_blk > 1
    blk_shape = (None, None, ATT_BLOCK, D_MODEL)
    cur = pl.BlockSpec(blk_shape, lambda b, r, i: (b, r, i, 0))
    prev = pl.BlockSpec(blk_shape, lambda b, r, i: (b, r, jnp.maximum(i - 1, 0), 0))
    if has_prev:
        in_specs, operands = [cur, prev, cur, prev, cur], (q, kb, kb, vb, vb)
    else:
        in_specs, operands = [cur, cur, cur], (q, kb, vb)
    return pl.pallas_call(
        functools.partial(_dil_prompt_kernel, has_prev, dil),
        grid=(n_b, dil, n_blk),
        in_specs=in_specs,
        out_specs=[pl.BlockSpec((None, D_MODEL // LANES, n_t, LANES), lambda b, r, i: (b, 0, 0, 0)),
                   pl.BlockSpec((None, n_t, LANES), lambda b, r, i: (b, 0, 0))],
        out_shape=[jax.ShapeDtypeStruct((n_b, D_MODEL // LANES, n_t, LANES), F32),
                   jax.ShapeDtypeStruct((n_b, n_t, LANES), F32)],
        compiler_params=_params("arbitrary", "arbitrary", "arbitrary"),
        name="dilated_prompt",
    )(*operands)


SAMPLE_HEAD_CHUNK = 4


def _sample_group_dense(t, q_col, k_win, v_win, k_new, v_new):
    n_t = k_new.shape[-1]
    pos = lax.broadcasted_iota(jnp.int32, (1, 1, k_win.shape[-1]), 2)
    pos_new = lax.broadcasted_iota(jnp.int32, (1, 1, n_t), 2)
    s_c = jnp.where(pos >= t, jnp.sum(k_win * q_col, axis=1, keepdims=True), -jnp.inf)
    s_n = jnp.where(pos_new <= t, jnp.sum(k_new * q_col, axis=1, keepdims=True), -jnp.inf)
    m = jnp.maximum(jnp.max(s_c, axis=2, keepdims=True), jnp.max(s_n, axis=2, keepdims=True))
    e_c, e_n = jnp.exp(s_c - m), jnp.exp(s_n - m)
    den = jnp.sum(e_c, axis=2, keepdims=True) + jnp.sum(e_n, axis=2, keepdims=True)
    o = (jnp.sum(v_win * e_c, axis=2, keepdims=True) + jnp.sum(v_new * e_n, axis=2, keepdims=True)) / den
    return o, m + jnp.log(den)


def _sample_group_dilated(dil, q_g, k_win, v_win, k_new, v_new):
    n_t = q_g.shape[-1]
    width = k_win.shape[-1]
    n_col = width // LANES
    res = lax.broadcasted_iota(jnp.int32, (1, 1, LANES), 2) % dil
    res_w = lax.broadcasted_iota(jnp.int32, (1, 1, width), 2) % dil
    q_pat = jnp.zeros(q_g.shape[:2] + (LANES,), F32)
    for t in range(n_t):
        q_pat = jnp.where(res == t, q_g[:, :, t:t + 1], q_pat)
    s_row = jnp.sum(k_win * jnp.concatenate([q_pat] * n_col, axis=2), axis=1, keepdims=True)
    s_new = jnp.sum(k_new * q_g, axis=1, keepdims=True)
    m_t = [jnp.maximum(jnp.max(jnp.where(res_w == t, s_row, -jnp.inf), axis=2, keepdims=True), s_new[:, :, t:t + 1])
           for t in range(n_t)]
    m_row = jnp.zeros_like(s_row)
    for t in range(n_t):
        m_row = jnp.where(res_w == t, m_t[t], m_row)
    e = jnp.exp(jnp.where(res_w < n_t, s_row, -jnp.inf) - m_row)
    acc = v_win * e
    acc_fold = sum(acc[:, :, c * LANES:(c + 1) * LANES] for c in range(n_col))
    e_fold = sum(e[:, :, c * LANES:(c + 1) * LANES] for c in range(n_col))
    out = []
    for t in range(n_t):
        e_new = jnp.exp(s_new[:, :, t:t + 1] - m_t[t])
        den = jnp.sum(jnp.where(res == t, e_fold, 0.0), axis=2, keepdims=True) + e_new
        o = (jnp.sum(jnp.where(res == t, acc_fold, 0.0), axis=2, keepdims=True) + e_new * v_new[:, :, t:t + 1]) / den
        out.append((o, m_t[t] + jnp.log(den)))
    return out


def _sample_attention(*refs):
    q_refs, (k_ref, v_ref, kn_ref, vn_ref, o_ref) = refs[:N_GROUPS], refs[N_GROUPS:]
    n_t = o_ref.shape[0]
    n_s = k_ref.shape[-1]
    heads = k_ref.shape[0]

    def token_minor(ref):
        return jnp.transpose(ref[...]).reshape(heads, DIL_HD, n_t)

    pos_new = lax.broadcasted_iota(jnp.int32, (1, 1, n_t), 2)
    k_new, v_new = token_minor(kn_ref), token_minor(vn_ref)
    groups = []
    for q_ref, dil in zip(q_refs, DIL_DILATIONS):
        lo = n_s - DIL_STEPS * dil
        k_win, v_win = k_ref[:, :, lo:], v_ref[:, :, lo:]
        q_g = token_minor(q_ref)
        if dil == 1:
            groups.append([_sample_group_dense(t, q_g[:, :, t:t + 1], k_win, v_win, k_new, v_new)
                           for t in range(n_t)])
        else:
            groups.append(_sample_group_dilated(dil, q_g, k_win, v_win, k_new, v_new))
    merged = jnp.zeros(k_new.shape, F32)
    for t in range(n_t):
        parts = [grp[t] for grp in groups]
        m = functools.reduce(jnp.maximum, [lse for _, lse in parts])
        wts = [jnp.exp(lse - m) for _, lse in parts]
        o = sum(wt * o for wt, (o, _) in zip(wts, parts)) / sum(wts)
        merged = jnp.where(pos_new == t, o, merged)
    o_ref[...] = jnp.transpose(merged.reshape(heads * DIL_HD, n_t))


class _SampleAttentionGuest:
    def __init__(self, q, cache_k, cache_v, k_new, v_new):
        self.operands = (q,) * N_GROUPS + (cache_k, cache_v, k_new, v_new)
        self.n_b, self.n_t = q.shape[:2]
        wb = cache_k.shape[-1]
        assert wb == DIL_STEPS * DIL_DILATIONS[-1] and DIL_DILATIONS[0] == 1 and self.n_t <= DIL_DILATIONS[1]
        assert all(wb % dil == 0 and LANES % dil == 0 for dil in DIL_DILATIONS)
        self.wb = wb
        self.outputs = []

    def specs(self, chunk):
        hc = SAMPLE_HEAD_CHUNK
        width = hc * DIL_HD
        chunks = DIL_HEADS // hc

        def cols(block):
            return pl.BlockSpec((None, self.n_t, width), lambda i: (i, 0, block))

        cache = pl.BlockSpec((None, hc, DIL_HD, self.wb), lambda i: (i, chunk, 0, 0))
        in_specs = [cols(g * chunks + chunk) for g in range(N_GROUPS)] + [cache, cache, cols(chunk), cols(chunk)]
        return in_specs, cols(0), jax.ShapeDtypeStruct((self.n_b, self.n_t, width), F32)

    def result(self):
        assert len(self.outputs) * SAMPLE_HEAD_CHUNK == DIL_HEADS
        return jnp.concatenate(self.outputs, axis=-1).reshape(self.n_b * self.n_t, D_MODEL)


def _att_merge(o_refs, lse_refs):
    lses = [l_ref[...] for l_ref in lse_refs]
    m = functools.reduce(jnp.maximum, lses)
    e = [jnp.exp(l - m) for l in lses]
    den = sum(e)
    head = lax.broadcasted_iota(jnp.int32, (LANES, D_MODEL), 0)
    col_head = lax.broadcasted_iota(jnp.int32, (LANES, D_MODEL), 1) // DIL_HD
    spread = (head == col_head).astype(BF16)
    merged = None
    for o_ref, eg in zip(o_refs, e):
        wt = eg / den
        wt_hi = wt.astype(BF16)
        wt_lo = (wt - wt_hi.astype(F32)).astype(BF16)
        wide = _dot(wt_hi, spread) + _dot(wt_lo, spread)
        part = wide * jnp.concatenate([o_ref[c] for c in range(o_ref.shape[0])], axis=1)
        merged = part if merged is None else merged + part
    return merged


def _att_out_kernel(merged_input, *refs):
    if merged_input:
        o_ref, x_ref, gt_ref, wout_ref, g_ref, b_ref, y_ref = refs
        o = o_ref[...]
    else:
        o_refs, lse_refs = refs[0:N_GROUPS], refs[N_GROUPS:2 * N_GROUPS]
        x_ref, gt_ref, wout_ref, g_ref, b_ref, y_ref = refs[2 * N_GROUPS:]
        o = _att_merge(o_refs, lse_refs)
    y = _dot(o.astype(BF16), wout_ref[...])
    z = ALPHA * x_ref[...] + (1.0 + gt_ref[...]) * y
    y_ref[...] = _layer_norm(z, g_ref[...], b_ref[...])


def _att_out_call(rows, att, x, layer, w_out, ln_g, ln_b):
    merged_input = rows.per_row
    if merged_input:
        att_specs = [rows.row_spec(D_MODEL)]
        operands = [att]
    else:
        outs, lses = att
        tps = rows.tiles_per_sample
        o_spec = pl.BlockSpec((None, D_MODEL // LANES, rows.tile, LANES), lambda i: (i // tps, 0, i % tps, 0))
        lse_spec = pl.BlockSpec((None, rows.tile, LANES), lambda i: (i // tps, i % tps, 0))
        att_specs = [o_spec] * N_GROUPS + [lse_spec] * N_GROUPS
        operands = list(outs) + list(lses)
    return pl.pallas_call(
        functools.partial(_att_out_kernel, merged_input),
        grid=(rows.steps,),
        in_specs=att_specs + [
            rows.row_spec(D_MODEL),
            rows.mod_spec(layer, 5),
            _const_spec((D_MODEL, D_MODEL), (0, 0)),
        ] + _ln_specs(layer, 1),
        out_specs=rows.row_spec(D_MODEL),
        out_shape=jax.ShapeDtypeStruct((rows.n_rows, D_MODEL), F32),
        compiler_params=_params("arbitrary"),
        name="att_out",
    )(*operands, x, rows.mods, w_out, ln_g, ln_b)


def _rope_tables(positions):
    half = DIL_HD // 2
    inv = ROPE_THETA ** (-jnp.arange(half, dtype=F32) / half)
    ang = positions.astype(F32)[:, None] * inv[None, :]
    cos, sin = jnp.cos(ang), jnp.sin(ang)
    return jnp.concatenate([cos, cos, cos, cos], axis=1), jnp.concatenate([-sin, sin, -sin, sin], axis=1)


def _trunk(rows, ffn, x, n_b, n_t, positions, gla_state, w):
    cos_t, sin_t = _rope_tables(positions)
    if rows.per_row:
        cos_t, sin_t = jnp.tile(cos_t, (n_b, 1)), jnp.tile(sin_t, (n_b, 1))

    x = ffn(x, 0, 1)
    q, k, v, la, r = _gla_proj_call(rows.wide(), x, 0, w['gla_qkv'], w['gla_g'], w['gla_r'], w['gla_g2'], w['gla_bg'])
    recurrence = _gla_step_call if rows.per_row else _gla_rec_call
    o, s_new = recurrence(q.reshape(n_b, n_t, GLA_QK), k.reshape(n_b, n_t, GLA_QK), la.reshape(n_b, n_t, GLA_QK),
                          v.reshape(n_b, n_t, GLA_V), gla_state)
    x = _gla_out_call(rows.wide(), o.reshape(n_b * n_t, GLA_V), r, x, 0, w['gla_gon'], w['gla_out'], w['ln_g'], w['ln_b'])
    x = ffn(x, 0, 2)

    k_rows, v_rows, *kv_groups = _kv_call(rows, x, cos_t, sin_t, w['kv'])
    if rows.per_row:
        k_rows, v_rows = (a.reshape(n_b, n_t, DIL_HEADS, DIL_HD) for a in (k_rows, v_rows))
    else:
        k_rows, v_rows = (a.reshape(n_b, DIL_HEADS, DIL_HD, n_t).transpose(0, 3, 1, 2) for a in (k_rows, v_rows))

    x = ffn(x, 1, 1)
    qd = _q_call(rows, x, 1, cos_t, sin_t, w['q'])
    att = yield qd, k_rows, v_rows, kv_groups
    x = _att_out_call(rows, att, x, 1, w['att_out'], w['ln_g'], w['ln_b'])
    x = ffn(x, 1, 2)
    return x, s_new, k_rows, v_rows


def _finish(trunk, att):
    try:
        trunk.send(att)
    except StopIteration as done:
        return done.value
    raise AssertionError("trunk yielded twice")


def kernel(x_prompt, x_sample, state_gla, cache_k, cache_v, c_prompt, c_sample, w_ada, b_ada, ln_g, ln_b, w_ffn1_up, w_ffn1_down, w_ffn2_up, w_ffn2_down, w_in_a, w_gate2_a, b_gate_a, g_onorm_a, w_out_a, w_ada_kv, b_ada_kv, w_kv, w_q_b, w_out_b):
    n_bp, n_tp = x_prompt.shape[:2]
    n_bs, n_ts = x_sample.shape[:2]
    assert N_A_LAYERS == 1 and DEPTH == 2
    assert n_tp % (ATT_BLOCK * DIL_DILATIONS[-1]) == 0 and n_tp % GLA_TILE == 0 and n_tp <= MAX_WINDOW
    assert n_ts < 8

    n_rs = n_bs * n_ts
    c_all = jnp.concatenate([jnp.repeat(c_sample, n_ts, axis=0), c_prompt], axis=0)
    mods = _ada_call(c_all, w_ada, b_ada, 1536)
    kv_mods = _ada_call(c_all, w_ada_kv[None], b_ada_kv[None], 1024)
    rows_p = _Rows(n_bp * n_tp, n_tp, mods[:, n_rs:, None, :], kv_mods[:, n_rs:, None, :])
    rows_s = _Rows(n_rs, n_ts, mods, kv_mods)

    w_in = w_in_a[0]
    gate_lo = 2 * GLA_QK + GLA_V
    w = {
        'ln_g': ln_g[:, :, None, :], 'ln_b': ln_b[:, :, None, :],
        'gla_qkv': w_in[:, :gate_lo].astype(BF16),
        'gla_g': jnp.pad(w_in[:, gate_lo:gate_lo + GLA_RANK], ((0, 0), (0, LANES - GLA_RANK))).astype(BF16),
        'gla_r': w_in[:, gate_lo + GLA_RANK:].astype(BF16),
        'gla_g2': jnp.pad(w_gate2_a[0], ((0, LANES - GLA_RANK), (0, 0))).astype(BF16),
        'gla_bg': b_gate_a[0][None, :],
        'gla_gon': g_onorm_a[0][None, :],
        'gla_out': w_out_a[0].astype(BF16),
        'kv': w_kv.astype(BF16),
        'q': w_q_b[0].astype(BF16),
        'att_out': w_out_b[0].astype(BF16),
    }

    ffn_f32 = {1: (w_ffn1_up, w_ffn1_down), 2: (w_ffn2_up, w_ffn2_down)}
    ffn_args = {1: (0, 0), 2: (6, 2)}
    ffn_bf16 = {}

    def ffn_sample(x, layer, which):
        if (layer, which) in ffn_bf16:
            return _ffn_call(rows_s, x, layer, *ffn_args[which], *ffn_bf16[layer, which], w['ln_g'], w['ln_b'])
        ahead = (layer, which) == (DEPTH - 1, 1)
        y, *casts = _ffn_cast_call(rows_s, x, layer, *ffn_args[which], *ffn_f32[which], w['ln_g'], w['ln_b'],
                                   cast_only=(ffn_f32[2],) if ahead else ())
        ffn_bf16[layer, which] = casts[:3]
        if ahead:
            ffn_bf16[layer, 2] = casts[3:]
        return y

    def ffn_prompt(x, layer, which):
        return _ffn_call(rows_p, x, layer, *ffn_args[which], *ffn_bf16[layer, which], w['ln_g'], w['ln_b'],
                         guest=guest)

    trunk_s = _trunk(rows_s, ffn_sample, x_sample.reshape(-1, D_MODEL), n_bs, n_ts,
                     PAST_LEN + jnp.arange(n_ts), state_gla[0], w)
    q_s, _, _, (k_flat, v_flat) = next(trunk_s)
    guest = _SampleAttentionGuest(
        q_s.reshape(n_bs, n_ts, -1), cache_k.transpose(0, 2, 3, 1), cache_v.transpose(0, 2, 3, 1),
        k_flat.reshape(n_bs, n_ts, D_MODEL), v_flat.reshape(n_bs, n_ts, D_MODEL))

    zero_state = jnp.zeros((n_bp, GLA_HEADS, GLA_DK, GLA_DV), F32)
    trunk_p = _trunk(rows_p, ffn_prompt, x_prompt.reshape(-1, D_MODEL),
                     n_bp, n_tp, jnp.arange(n_tp), zero_state, w)
    q_p, _, _, kv_p = next(trunk_p)
    att_p = tuple(zip(*[_dil_prompt_call(q_p[g], kv_p[g], kv_p[N_GROUPS + g], g) for g in range(N_GROUPS)]))
    y_p, s_p, k_p, v_p = _finish(trunk_p, att_p)
    y_s, s_s, k_s, v_s = _finish(trunk_s, guest.result())
    return (y_p.reshape(n_bp, n_tp, D_MODEL), y_s.reshape(n_bs, n_ts, D_MODEL),
            s_p[None], s_s[None], k_p, v_p, k_s, v_s)
```

```python
import functools

import jax
import jax.numpy as jnp
from jax import lax
from jax.experimental import pallas as pl
from jax.experimental.pallas import tpu as pltpu

F32 = jnp.float32
BF16 = jnp.bfloat16

D_MODEL = 1024
DEPTH = 2
PAST_LEN = 16384
N_A_LAYERS = DEPTH // 2
GLA_HEADS = 4
GLA_QK = D_MODEL // 2
GLA_V = D_MODEL
GLA_DK = GLA_QK // GLA_HEADS
GLA_DV = GLA_V // GLA_HEADS
GLA_RANK = 16
GLA_TAU = 16.0
GLA_NORM_EPS = 1e-6
DIL_HEADS = 16
DIL_HD = D_MODEL // DIL_HEADS
DIL_WINDOWS = (128, 512, 2048)
DIL_DILATIONS = (1, 4, 16)
N_GROUPS = len(DIL_WINDOWS)
DIL_STEPS = 128
MAX_WINDOW = max(DIL_WINDOWS)
DIL_SCALE = DIL_HD ** -0.5
ROPE_THETA = 10000.0
D_FF = 2816
FFN_RES = 0.5
ALPHA = (2 * DEPTH) ** 0.25
LN_EPS = 1e-5
N_MOD = 9

LANES = 128
ROW_TILE = 512
WIDE_TILE = 1024
LOG2_E = 1.4426950408889634
LN_2 = 0.6931471805599453
FF_CHUNK = 256
GLA_CHUNK = 128
GLA_TILE = 512
GLA_FACTOR_BOUND = 80.0
ATT_BLOCK = DIL_STEPS
VMEM_LIMIT = 56 * 1024 * 1024


def _params(*sem):
    return pltpu.CompilerParams(dimension_semantics=sem, vmem_limit_bytes=VMEM_LIMIT)


def _silu(x):
    return x / (1.0 + jnp.exp(-x))


def _layer_norm(z, g, b):
    mu = jnp.mean(z, axis=-1, keepdims=True)
    zc = z - mu
    var = jnp.mean(zc * zc, axis=-1, keepdims=True)
    return zc * lax.rsqrt(var + LN_EPS) * g + b


def _dot(a, b):
    return jnp.dot(a, b, preferred_element_type=F32)


def _dot_nt(a, b):
    return lax.dot_general(a, b, (((1,), (1,)), ((), ())), preferred_element_type=F32)


def _rope(x, cos_t, sin_t):
    width = x.shape[1]
    lane = lax.broadcasted_iota(jnp.int32, x.shape, 1)
    first_half = (lane % DIL_HD) < (DIL_HD // 2)
    partner = jnp.where(first_half, pltpu.roll(x, width - DIL_HD // 2, 1), pltpu.roll(x, DIL_HD // 2, 1))
    reps = width // LANES
    cos_w = jnp.concatenate([cos_t] * reps, axis=1)
    sin_w = jnp.concatenate([sin_t] * reps, axis=1)
    return x * cos_w + partner * sin_w


def _ada_kernel(c_ref, w_ref, b_ref, o_ref):
    s = _silu(c_ref[...]).astype(BF16)
    o_ref[...] = _dot(s, w_ref[...].astype(BF16)) + b_ref[...]


def _ada_call(c_all, w, b, tn):
    n_l, _, n = w.shape
    rows = c_all.shape[0]
    return pl.pallas_call(
        _ada_kernel,
        grid=(n_l, n // tn),
        in_specs=[
            pl.BlockSpec((rows, D_MODEL), lambda l, j: (0, 0)),
            pl.BlockSpec((None, D_MODEL, tn), lambda l, j: (l, 0, j)),
            pl.BlockSpec((None, 1, tn), lambda l, j: (l, 0, j)),
        ],
        out_specs=pl.BlockSpec((None, rows, tn), lambda l, j: (l, 0, j)),
        out_shape=jax.ShapeDtypeStruct((n_l, rows, n), F32),
        compiler_params=_params("arbitrary", "arbitrary"),
        name="ada_mod",
    )(c_all, w, b.reshape(n_l, 1, n))


class _Rows:
    def __init__(self, n_rows, rows_per_sample, mods, kv_mods, tile=ROW_TILE):
        self.n_rows = n_rows
        self.rows_per_sample = rows_per_sample
        self.per_row = rows_per_sample < 8
        self.tile = n_rows if self.per_row else min(tile, rows_per_sample)
        self.steps = n_rows // self.tile
        self.tiles_per_sample = 1 if self.per_row else rows_per_sample // self.tile
        self.mods = mods
        self.kv_mods = kv_mods

    def wide(self):
        return _Rows(self.n_rows, self.rows_per_sample, self.mods, self.kv_mods, WIDE_TILE)

    def row_spec(self, width):
        return pl.BlockSpec((self.tile, width), lambda i: (i, 0))

    def mod_spec(self, layer, k):
        if self.per_row:
            return pl.BlockSpec((None, self.tile, D_MODEL), lambda i: (layer, 0, k))
        tps = self.tiles_per_sample
        return pl.BlockSpec((None, None, 1, D_MODEL), lambda i: (layer, i // tps, 0, k))


def _const_spec(shape, index):
    return pl.BlockSpec(shape, lambda i: index, pipeline_mode=pl.Buffered(1))


def _ln_specs(layer, j):
    return [_const_spec((None, None, 1, D_MODEL), (layer, j, 0, 0))] * 2


def _ffn_kernel(has_guest, x_ref, sh_ref, sc_ref, gt_ref, wa_ref, wu_ref, wdn_ref, g_ref, b_ref, *rest):
    if has_guest:
        *guest_in, o_ref, guest_out = rest
        _sample_attention(*guest_in, guest_out)
    else:
        o_ref, = rest
    x = x_ref[...]
    h = (x * (1.0 + sc_ref[...]) + sh_ref[...]).astype(BF16)
    y = None
    for lo in range(0, D_FF, FF_CHUNK):
        a = _dot(h, wa_ref[:, lo:lo + FF_CHUNK])
        u = _dot(h, wu_ref[:, lo:lo + FF_CHUNK])
        part = _dot((_silu(a) * u).astype(BF16), wdn_ref[lo:lo + FF_CHUNK, :])
        y = part if y is None else y + part
    z = ALPHA * x + (1.0 + gt_ref[...]) * (FFN_RES * y)
    o_ref[...] = _layer_norm(z, g_ref[...], b_ref[...])


def _ffn_cast_kernel(n_cast_only, x_ref, sh_ref, sc_ref, gt_ref, wa_ref, wu_ref, wdn_ref, *rest):
    extra_in, (g_ref, b_ref, o_ref, wa_out, wu_out, wdn_out), rest = rest[:n_cast_only], rest[n_cast_only:n_cast_only + 6], rest[n_cast_only + 6:]
    extra_out, y_scr = rest[:n_cast_only], rest[n_cast_only]
    for src, dst in zip(extra_in, extra_out):
        dst[...] = src[...].astype(BF16)
    c = pl.program_id(0)
    wa, wu, wdn = wa_ref[...].astype(BF16), wu_ref[...].astype(BF16), wdn_ref[...].astype(BF16)
    wa_out[...], wu_out[...], wdn_out[...] = wa, wu, wdn
    x = x_ref[...]
    h = (x * (1.0 + sc_ref[...]) + sh_ref[...]).astype(BF16)
    part = _dot((_silu(_dot(h, wa)) * _dot(h, wu)).astype(BF16), wdn)

    @pl.when(c == 0)
    def _():
        y_scr[...] = part

    @pl.when(c > 0)
    def _():
        y_scr[...] += part

    @pl.when(c == pl.num_programs(0) - 1)
    def _():
        z = ALPHA * x + (1.0 + gt_ref[...]) * (FFN_RES * y_scr[...])
        o_ref[...] = _layer_norm(z, g_ref[...], b_ref[...])


def _ffn_cast_call(rows, x, layer, mod0, ln_j, w_up, w_down, ln_g, ln_b, cast_only=()):
    assert rows.steps == 1
    n_chunks = D_FF // FF_CHUNK

    def whole(spec):
        return pl.BlockSpec(spec.block_shape, lambda c, index_map=spec.index_map: index_map(0))

    f32_specs = [pl.BlockSpec((None, D_MODEL, FF_CHUNK), lambda c: (layer, 0, c)),
                 pl.BlockSpec((None, D_MODEL, FF_CHUNK), lambda c: (layer, 0, n_chunks + c)),
                 pl.BlockSpec((None, FF_CHUNK, D_MODEL), lambda c: (layer, c, 0))]
    bf16_specs = [pl.BlockSpec((D_MODEL, FF_CHUNK), lambda c: (0, c)),
                  pl.BlockSpec((D_MODEL, FF_CHUNK), lambda c: (0, c)),
                  pl.BlockSpec((FF_CHUNK, D_MODEL), lambda c: (c, 0))]
    bf16_shapes = [jax.ShapeDtypeStruct((D_MODEL, D_FF), BF16), jax.ShapeDtypeStruct((D_MODEL, D_FF), BF16),
                   jax.ShapeDtypeStruct((D_FF, D_MODEL), BF16)]
    extra = [w for up, down in cast_only for w in (up, up, down)]
    return pl.pallas_call(
        functools.partial(_ffn_cast_kernel, len(extra)),
        grid=(n_chunks,),
        in_specs=[whole(s) for s in (rows.row_spec(D_MODEL), rows.mod_spec(layer, mod0),
                                     rows.mod_spec(layer, mod0 + 1), rows.mod_spec(layer, mod0 + 2))]
        + f32_specs + f32_specs * len(cast_only) + [whole(s) for s in _ln_specs(layer, ln_j)],
        out_specs=[whole(rows.row_spec(D_MODEL))] + bf16_specs * (1 + len(cast_only)),
        out_shape=[jax.ShapeDtypeStruct((rows.n_rows, D_MODEL), F32)] + bf16_shapes * (1 + len(cast_only)),
        scratch_shapes=[pltpu.VMEM((rows.n_rows, D_MODEL), F32)],
        compiler_params=_params("arbitrary"),
        name="ffn_cast",
    )(x, rows.mods, rows.mods, rows.mods, w_up, w_up, w_down, *extra, ln_g, ln_b)


def _ffn_call(rows, x, layer, mod0, ln_j, w_gate, w_up, w_down, ln_g, ln_b, guest=None):
    in_specs = [
        rows.row_spec(D_MODEL),
        rows.mod_spec(layer, mod0), rows.mod_spec(layer, mod0 + 1), rows.mod_spec(layer, mod0 + 2),
        _const_spec((D_MODEL, D_FF), (0, 0)),
        _const_spec((D_MODEL, D_FF), (0, 0)),
        _const_spec((D_FF, D_MODEL), (0, 0)),
    ] + _ln_specs(layer, ln_j)
    operands = [x, rows.mods, rows.mods, rows.mods, w_gate, w_up, w_down, ln_g, ln_b]
    out_specs = [rows.row_spec(D_MODEL)]
    out_shape = [jax.ShapeDtypeStruct((rows.n_rows, D_MODEL), F32)]
    if guest is not None:
        assert guest.n_b == rows.steps
        guest_in, guest_out, guest_shape = guest.specs(len(guest.outputs))
        in_specs += guest_in
        operands += guest.operands
        out_specs.append(guest_out)
        out_shape.append(guest_shape)
    outs = pl.pallas_call(
        functools.partial(_ffn_kernel, guest is not None),
        grid=(rows.steps,),
        in_specs=in_specs,
        out_specs=out_specs,
        out_shape=out_shape,
        compiler_params=_params("arbitrary"),
        name="ffn",
    )(*operands)
    if guest is not None:
        guest.outputs.append(outs[1])
    return outs[0]


def _gla_proj_kernel(x_ref, sh_ref, sc_ref, wqkv_ref, wg_ref, wr_ref, wg2_ref, bg_ref,
                     q_ref, k_ref, v_ref, la_ref, r_ref):
    h = (x_ref[...] * (1.0 + sc_ref[...]) + sh_ref[...]).astype(BF16)
    qkv = _dot(h, wqkv_ref[...])
    q_ref[...] = qkv[:, :GLA_QK] * (GLA_DK ** -0.5)
    k_ref[...] = qkv[:, GLA_QK:2 * GLA_QK]
    v_ref[...] = qkv[:, 2 * GLA_QK:].astype(v_ref.dtype)
    g_lr = _dot(h, wg_ref[...]).astype(BF16)
    gate = _dot(g_lr, wg2_ref[...]) + bg_ref[...]
    log_sig = jnp.minimum(gate, 0.0) - jnp.log(1.0 + jnp.exp(-jnp.abs(gate)))
    la_ref[...] = log_sig / GLA_TAU
    r_ref[...] = _silu(_dot(h, wr_ref[...])).astype(r_ref.dtype)


def _gla_proj_call(rows, x, layer, w_qkv, w_g, w_r, w_g2, b_g):
    widths = (GLA_QK, GLA_QK, GLA_V, GLA_QK, GLA_V)
    narrow = F32 if rows.per_row else BF16
    dtypes = (F32, F32, narrow, F32, narrow)
    return pl.pallas_call(
        _gla_proj_kernel,
        grid=(rows.steps,),
        in_specs=[
            rows.row_spec(D_MODEL),
            rows.mod_spec(layer, 3), rows.mod_spec(layer, 4),
            _const_spec((D_MODEL, 2 * GLA_QK + GLA_V), (0, 0)),
            _const_spec((D_MODEL, LANES), (0, 0)),
            _const_spec((D_MODEL, GLA_V), (0, 0)),
            _const_spec((LANES, GLA_QK), (0, 0)),
            _const_spec((1, GLA_QK), (0, 0)),
        ],
        out_specs=[rows.row_spec(w) for w in widths],
        out_shape=[jax.ShapeDtypeStruct((rows.n_rows, w), dt) for w, dt in zip(widths, dtypes)],
        compiler_params=_params("arbitrary"),
        name="gla_proj",
    )(x, rows.mods, rows.mods, w_qkv, w_g, w_r, w_g2, b_g)


def _prefix_sum(tri, x):
    rest, total = x, None
    for _ in range(3):
        piece = rest.astype(BF16)
        rest = rest - piece.astype(F32)
        part = _dot(tri, piece)
        total = part if total is None else total + part
    return total


def _gla_scores_factored(qh, kh, bh, b_last):
    mid = 0.5 * b_last
    q_t = (qh * jnp.exp(bh - mid)).astype(BF16)
    k_t = (kh * jnp.exp(mid - bh)).astype(BF16)
    return _dot_nt(q_t, k_t)


def _gla_scores_elementwise(qh, kh, bh, k_scr, b_scr):
    k_scr[...] = kh
    b_scr[...] = bh
    col = lax.broadcasted_iota(jnp.int32, (GLA_CHUNK, GLA_CHUNK), 1)

    def body(j, acc):
        kj = k_scr[pl.ds(j, 1), :]
        bj = b_scr[pl.ds(j, 1), :]
        term = qh * kj * jnp.exp(jnp.minimum(bh - bj, 0.0))
        return acc + jnp.where(col == j, jnp.sum(term, axis=1, keepdims=True), 0.0)

    return lax.fori_loop(0, GLA_CHUNK, body, jnp.zeros((GLA_CHUNK, GLA_CHUNK), F32))


def _gla_chunk(rows, b, states, score_fn, q_ref, k_ref, v_ref, o_ref, causal):
    q, k = q_ref[rows, :], k_ref[rows, :]
    b_last = b[GLA_CHUNK - 1:GLA_CHUNK, :]
    q_in = q * jnp.exp(b)
    k_out = k * jnp.exp(b_last - b)
    new_states = []
    for hd, s_prev in enumerate(states):
        ks = slice(hd * GLA_DK, (hd + 1) * GLA_DK)
        vs = slice(hd * GLA_DV, (hd + 1) * GLA_DV)
        vh = v_ref[rows, vs].astype(BF16)
        o_inter = _dot(q_in[:, ks].astype(BF16), s_prev.astype(BF16))
        scores = jnp.where(causal, score_fn(q[:, ks], k[:, ks], b[:, ks], b_last[:, ks]), 0.0)
        o_ref[rows, vs] = (o_inter + _dot(scores.astype(BF16), vh)).astype(o_ref.dtype)
        decay_col = jnp.transpose(jnp.broadcast_to(jnp.exp(b_last[:, ks]), (GLA_DK, GLA_DK)))[:, :1]
        k_out_t = jnp.transpose(k_out[:, ks]).astype(BF16)
        new_states.append(decay_col * s_prev + _dot(k_out_t, vh))
    return new_states


def _gla_rec_kernel(q_ref, k_ref, v_ref, la_ref, s0_ref, o_ref, sfin_ref, s_scr, k_scr, b_scr):
    t = pl.program_id(1)

    @pl.when(t == 0)
    def _():
        s_scr[...] = s0_ref[...]

    row = lax.broadcasted_iota(jnp.int32, (GLA_CHUNK, GLA_CHUNK), 0)
    col = lax.broadcasted_iota(jnp.int32, (GLA_CHUNK, GLA_CHUNK), 1)
    causal = col <= row
    tri = causal.astype(BF16)
    n_chunks = GLA_TILE // GLA_CHUNK

    def decay_prefix(rows):
        return _prefix_sum(tri, la_ref[rows, :])

    chunk_rows = [slice(c * GLA_CHUNK, (c + 1) * GLA_CHUNK) for c in range(n_chunks)]
    prefixes = [decay_prefix(rows) for rows in chunk_rows]
    total_decay = functools.reduce(jnp.maximum, [-b[GLA_CHUNK - 1:GLA_CHUNK, :] for b in prefixes])
    factorable = jnp.max(total_decay) <= GLA_FACTOR_BOUND

    def refs():
        return q_ref, k_ref, v_ref, o_ref, causal

    @pl.when(factorable)
    def _():
        states = [s_scr[hd] for hd in range(GLA_HEADS)]
        for rows, b in zip(chunk_rows, prefixes):
            states = _gla_chunk(rows, b, states, _gla_scores_factored, *refs())
        for hd in range(GLA_HEADS):
            s_scr[hd] = states[hd]

    @pl.when(jnp.logical_not(factorable))
    def _():
        def chunk(c, carry):
            rows = pl.ds(pl.multiple_of(c * GLA_CHUNK, GLA_CHUNK), GLA_CHUNK)
            states = [s_scr[hd] for hd in range(GLA_HEADS)]
            elementwise = functools.partial(_gla_scores_elementwise, k_scr=k_scr, b_scr=b_scr)
            states = _gla_chunk(rows, decay_prefix(rows), states,
                                lambda qh, kh, bh, _: elementwise(qh, kh, bh), *refs())
            for hd in range(GLA_HEADS):
                s_scr[hd] = states[hd]
            return carry

        lax.fori_loop(0, n_chunks, chunk, 0)

    @pl.when(t == pl.num_programs(1) - 1)
    def _():
        sfin_ref[...] = s_scr[...]


def _gla_rec_call(q, k, la, v, s0):
    n_b, n_t = q.shape[:2]

    def seq(width):
        return pl.BlockSpec((None, GLA_TILE, width), lambda b, t: (b, t, 0))

    state = pl.BlockSpec((None, GLA_HEADS, GLA_DK, GLA_DV), lambda b, t: (b, 0, 0, 0))
    return pl.pallas_call(
        _gla_rec_kernel,
        grid=(n_b, n_t // GLA_TILE),
        in_specs=[seq(GLA_QK), seq(GLA_QK), seq(GLA_V), seq(GLA_QK), state],
        out_specs=[seq(GLA_V), state],
        out_shape=[jax.ShapeDtypeStruct((n_b, n_t, GLA_V), BF16),
                   jax.ShapeDtypeStruct((n_b, GLA_HEADS, GLA_DK, GLA_DV), F32)],
        scratch_shapes=[pltpu.VMEM((GLA_HEADS, GLA_DK, GLA_DV), F32),
                        pltpu.VMEM((GLA_CHUNK, GLA_DK), F32),
                        pltpu.VMEM((GLA_CHUNK, GLA_DK), F32)],
        compiler_params=_params("arbitrary", "arbitrary"),
        name="gla_recurrence",
    )(q, k, v, la, s0)


def _gla_step_kernel(q_ref, k_ref, la_ref, v_ref, s0_ref, o_ref, sfin_ref):
    n_t = v_ref.shape[0]
    q, k = q_ref[...], k_ref[...]
    row = lax.broadcasted_iota(jnp.int32, (n_t, n_t), 0)
    col = lax.broadcasted_iota(jnp.int32, (n_t, n_t), 1)
    b = _prefix_sum((col <= row).astype(BF16), la_ref[...])
    b_last = b[n_t - 1:n_t, :]
    q_in = q * jnp.exp(b)
    k_out = k * jnp.exp(b_last - b)
    for hd in range(GLA_HEADS):
        ks = slice(hd * GLA_DK, (hd + 1) * GLA_DK)
        vs = slice(hd * GLA_DV, (hd + 1) * GLA_DV)
        vh = v_ref[:, vs]
        s_prev = s0_ref[hd]
        o = _dot(q_in[:, ks].astype(BF16), s_prev.astype(BF16))
        for u in range(n_t):
            decay = jnp.exp(jnp.minimum(b[:, ks] - b[u:u + 1, ks], 0.0))
            score = jnp.sum(q[:, ks] * k[u:u + 1, ks] * decay, axis=1, keepdims=True)
            o = o + jnp.where(row[:, :1] >= u, score, 0.0) * vh[u:u + 1, :]
        o_ref[:, vs] = o
        decay_col = jnp.transpose(jnp.broadcast_to(jnp.exp(b_last[:, ks]), (8, GLA_DK)))[:, :1]
        sfin_ref[hd] = decay_col * s_prev + lax.dot_general(
            k_out[:, ks].astype(BF16), vh.astype(BF16), (((0,), (0,)), ((), ())), preferred_element_type=F32)


def _gla_step_call(q, k, la, v, s0):
    n_b, n_t = q.shape[:2]

    def tok(width):
        return pl.BlockSpec((None, n_t, width), lambda b: (b, 0, 0))

    state = pl.BlockSpec((None, GLA_HEADS, GLA_DK, GLA_DV), lambda b: (b, 0, 0, 0))
    return pl.pallas_call(
        _gla_step_kernel,
        grid=(n_b,),
        in_specs=[tok(GLA_QK), tok(GLA_QK), tok(GLA_QK), tok(GLA_V), state],
        out_specs=[tok(GLA_V), state],
        out_shape=[jax.ShapeDtypeStruct((n_b, n_t, GLA_V), F32),
                   jax.ShapeDtypeStruct((n_b, GLA_HEADS, GLA_DK, GLA_DV), F32)],
        compiler_params=_params("arbitrary"),
        name="gla_step",
    )(q, k, la, v, s0)


def _gla_out_kernel(o_ref, r_ref, x_ref, gt_ref, gon_ref, wout_ref, g_ref, b_ref, y_ref):
    o = o_ref[...].astype(F32)
    parts = []
    for hd in range(GLA_HEADS):
        oh = o[:, hd * GLA_DV:(hd + 1) * GLA_DV]
        ms = jnp.mean(oh * oh, axis=-1, keepdims=True)
        parts.append(oh * lax.rsqrt(ms + GLA_NORM_EPS) * gon_ref[...])
    gated = (jnp.concatenate(parts, axis=1) * r_ref[...].astype(F32)).astype(BF16)
    y = _dot(gated, wout_ref[...])
    z = ALPHA * x_ref[...] + (1.0 + gt_ref[...]) * y
    y_ref[...] = _layer_norm(z, g_ref[...], b_ref[...])


def _gla_out_call(rows, o, r, x, layer, g_onorm, w_out, ln_g, ln_b):
    return pl.pallas_call(
        _gla_out_kernel,
        grid=(rows.steps,),
        in_specs=[
            rows.row_spec(GLA_V), rows.row_spec(GLA_V), rows.row_spec(D_MODEL),
            rows.mod_spec(layer, 5),
            _const_spec((1, GLA_DV), (0, 0)),
            _const_spec((GLA_V, D_MODEL), (0, 0)),
        ] + _ln_specs(layer, 1),
        out_specs=rows.row_spec(D_MODEL),
        out_shape=jax.ShapeDtypeStruct((rows.n_rows, D_MODEL), F32),
        compiler_params=_params("arbitrary"),
        name="gla_out",
    )(o, r, x, rows.mods, g_onorm, w_out, ln_g, ln_b)


def _gather_stride(scr, start, count, stride):
    return jnp.concatenate([scr[c, pl.ds(start, count, stride=stride), :] for c in range(scr.shape[0])], axis=1)


def _fill_slabs(scr, row0, x):
    for c in range(scr.shape[0]):
        scr[c, row0:row0 + x.shape[0], :] = x[:, c * LANES:(c + 1) * LANES]


def _store_by_residue(x, out_refs, scr_a, scr_b):
    n_rows = x.shape[0]
    s = DIL_DILATIONS[1]
    assert DIL_DILATIONS == (1, s, s * s)
    outs = {o.shape[0]: o for o in out_refs}
    if 1 in outs:
        outs[1][0] = x.astype(BF16)
    if s not in outs and s * s not in outs:
        return
    _fill_slabs(scr_a, 0, x)
    per_class = n_rows // s
    for r in range(s):
        xr = _gather_stride(scr_a, r, per_class, s)
        if s in outs:
            outs[s][r] = xr.astype(BF16)
        if s * s in outs:
            _fill_slabs(scr_b, r * per_class, xr)
    if s * s in outs:
        for r in range(s * s):
            xr = _gather_stride(scr_b, (r % s) * per_class + r // s, per_class // s, s)
            outs[s * s][r] = xr.astype(BF16)


def _kv_kernel(by_residue, x_ref, sh_ref, sc_ref, cos_ref, sin_ref, w_ref, k_ref, v_ref, *rest):
    h = (x_ref[...] * (1.0 + sc_ref[...]) + sh_ref[...]).astype(BF16)
    kv = _dot(h, w_ref[...])
    k = _rope(kv[:, :D_MODEL], cos_ref[...], sin_ref[...])
    v = kv[:, D_MODEL:]
    if by_residue:
        k_ref[...] = jnp.transpose(k)
        v_ref[...] = jnp.transpose(v)
    else:
        k_ref[...] = k.reshape(k_ref.shape)
        v_ref[...] = v.reshape(v_ref.shape)
        rest[0][...] = k
        rest[1][...] = v
    if by_residue:
        _store_by_residue(jnp.concatenate([k, v], axis=1), rest[0:N_GROUPS], *rest[-2:])


def _rope_spec(rows):
    if rows.per_row:
        return pl.BlockSpec((rows.tile, LANES), lambda i: (0, 0))
    tps = rows.tiles_per_sample
    return pl.BlockSpec((rows.tile, LANES), lambda i: (i % tps, 0))


def _kv_mod_spec(rows, k):
    if rows.per_row:
        return pl.BlockSpec((None, rows.tile, D_MODEL), lambda i: (0, 0, k))
    tps = rows.tiles_per_sample
    return pl.BlockSpec((None, None, 1, D_MODEL), lambda i: (0, i // tps, 0, k))


def _residue_spec(rows, dil, width):
    tps = rows.tiles_per_sample
    return pl.BlockSpec((None, dil, rows.tile // dil, width), lambda i: (i // tps, 0, i % tps, 0))


def _residue_shapes(rows, width, dtype):
    n_b = rows.n_rows // rows.rows_per_sample
    return [jax.ShapeDtypeStruct((n_b, dil, rows.rows_per_sample // dil, width), dtype) for dil in DIL_DILATIONS]


def _permute_scratch(width=D_MODEL):
    return [pltpu.VMEM((width // LANES, ROW_TILE, LANES), F32)] * 2


def _kv_call(rows, x, cos_t, sin_t, w_kv):
    by_residue = not rows.per_row
    scratch = []
    if by_residue:
        tps = rows.tiles_per_sample
        n_b = rows.n_rows // rows.rows_per_sample
        out_specs = [pl.BlockSpec((None, D_MODEL, rows.tile), lambda i: (i // tps, 0, i % tps))] * 2
        out_shape = [jax.ShapeDtypeStruct((n_b, D_MODEL, rows.rows_per_sample), F32)] * 2
    else:
        out_specs = [pl.BlockSpec((rows.tile, DIL_HEADS, DIL_HD), lambda i: (i, 0, 0))] * 2 + [rows.row_spec(D_MODEL)] * 2
        out_shape = ([jax.ShapeDtypeStruct((rows.n_rows, DIL_HEADS, DIL_HD), F32)] * 2
                     + [jax.ShapeDtypeStruct((rows.n_rows, D_MODEL), F32)] * 2)
    if by_residue:
        out_specs += [_residue_spec(rows, dil, 2 * D_MODEL) for dil in DIL_DILATIONS]
        out_shape += _residue_shapes(rows, 2 * D_MODEL, BF16)
        scratch = _permute_scratch(2 * D_MODEL)
    return pl.pallas_call(
        functools.partial(_kv_kernel, by_residue),
        grid=(rows.steps,),
        in_specs=[
            rows.row_spec(D_MODEL),
            _kv_mod_spec(rows, 0), _kv_mod_spec(rows, 1),
            _rope_spec(rows), _rope_spec(rows),
            _const_spec((D_MODEL, 2 * D_MODEL), (0, 0)),
        ],
        out_specs=out_specs,
        out_shape=out_shape,
        scratch_shapes=scratch,
        compiler_params=_params("arbitrary"),
        name="kv_proj",
    )(x, rows.kv_mods, rows.kv_mods, cos_t, sin_t, w_kv)


def _q_kernel(by_residue, x_ref, sh_ref, sc_ref, cos_ref, sin_ref, w_ref, *rest):
    h = (x_ref[...] * (1.0 + sc_ref[...]) + sh_ref[...]).astype(BF16)
    q = _rope(_dot(h, w_ref[...]), cos_ref[...], sin_ref[...])
    if by_residue:
        q = q * (DIL_SCALE * LOG2_E)
        for g in range(N_GROUPS):
            _store_by_residue(q[:, g * D_MODEL:(g + 1) * D_MODEL], rest[g:g + 1], *rest[-2:])
    else:
        rest[0][...] = q * DIL_SCALE


def _q_call(rows, x, layer, cos_t, sin_t, w_q):
    width = N_GROUPS * D_MODEL
    by_residue = not rows.per_row
    if by_residue:
        out_specs = [_residue_spec(rows, dil, D_MODEL) for dil in DIL_DILATIONS]
        out_shape = _residue_shapes(rows, D_MODEL, BF16)
        scratch = _permute_scratch()
    else:
        out_specs = rows.row_spec(width)
        out_shape = jax.ShapeDtypeStruct((rows.n_rows, width), F32)
        scratch = []
    return pl.pallas_call(
        functools.partial(_q_kernel, by_residue),
        grid=(rows.steps,),
        in_specs=[
            rows.row_spec(D_MODEL),
            rows.mod_spec(layer, 3), rows.mod_spec(layer, 4),
            _rope_spec(rows), _rope_spec(rows),
            _const_spec((D_MODEL, width), (0, 0)),
        ],
        out_specs=out_specs,
        out_shape=out_shape,
        scratch_shapes=scratch,
        compiler_params=_params("arbitrary"),
        name="q_proj",
    )(x, rows.mods, rows.mods, cos_t, sin_t, w_q)


def _pair_softmax_pv(scores, v_pair, even_lane):
    half = scores.shape[0] // 2
    m = jnp.max(scores, axis=-1, keepdims=True)
    e = jnp.exp2(scores - m)
    den = jnp.sum(e, axis=-1, keepdims=True)
    o2 = _dot(e.astype(BF16), v_pair)
    o = jnp.where(even_lane, o2[:half], o2[half:]) / jnp.where(even_lane, den[:half], den[half:])
    lse2 = (m + jnp.log2(den)) * LN_2
    return o, lse2[:half], lse2[half:]


def _dil_prompt_kernel(has_prev, dil, q_ref, *refs):
    if has_prev:
        kvp_ref, kvc_ref, o_ref, lse_ref = refs
    else:
        kvc_ref, o_ref, lse_ref = refs
    blk = pl.program_id(2)
    first_token = blk * (ATT_BLOCK * dil) + pl.program_id(1)
    if dil == 1:
        token_rows = pl.ds(pl.multiple_of(first_token, ATT_BLOCK), ATT_BLOCK)
    else:
        token_rows = pl.ds(first_token, ATT_BLOCK, stride=dil)
    n_keys = (2 if has_prev else 1) * ATT_BLOCK
    lane = lax.broadcasted_iota(jnp.int32, (ATT_BLOCK, LANES), 1)
    even_lane = lane < DIL_HD
    qi = lax.broadcasted_iota(jnp.int32, (2 * ATT_BLOCK, n_keys), 0) % ATT_BLOCK
    kj = lax.broadcasted_iota(jnp.int32, (2 * ATT_BLOCK, n_keys), 1)
    if has_prev:
        dist = qi + ATT_BLOCK - kj
        valid = (dist >= 0) & (dist <= DIL_STEPS) & ((kj >= ATT_BLOCK) | (blk > 0))
    else:
        valid = kj <= qi
    lse_all = jnp.zeros((ATT_BLOCK, LANES), F32)
    for p in range(DIL_HEADS // 2):
        sl = slice(p * LANES, (p + 1) * LANES)
        vsl = slice(D_MODEL + p * LANES, D_MODEL + (p + 1) * LANES)
        qp = q_ref[:, sl]
        zero = jnp.zeros_like(qp)
        q2 = jnp.concatenate([jnp.where(even_lane, qp, zero), jnp.where(even_lane, zero, qp)], axis=0)
        if has_prev:
            k_pair = jnp.concatenate([kvp_ref[:, sl], kvc_ref[:, sl]], axis=0)
            v_pair = jnp.concatenate([kvp_ref[:, vsl], kvc_ref[:, vsl]], axis=0)
        else:
            k_pair, v_pair = kvc_ref[:, sl], kvc_ref[:, vsl]
        scores = jnp.where(valid, _dot_nt(q2, k_pair), -jnp.inf)
        o, lse_even, lse_odd = _pair_softmax_pv(scores, v_pair, even_lane)
        o_ref[p, token_rows, :] = o
        lse_all = jnp.where(lane == 2 * p, lse_even, jnp.where(lane == 2 * p + 1, lse_odd, lse_all))
    lse_ref[token_rows, :] = lse_all


def _dil_prompt_call(q, kvb):
    n_b, dil, length, _ = q.shape
    n_t = dil * length
    n_blk = length // ATT_BLOCK
    has_prev = n_blk > 1
    cur = pl.BlockSpec((None, None, ATT_BLOCK, D_MODEL), lambda b, r, i: (b, r, i, 0))
    kv_cur = pl.BlockSpec((None, None, ATT_BLOCK, 2 * D_MODEL), lambda b, r, i: (b, r, i, 0))
    kv_prev = pl.BlockSpec((None, None, ATT_BLOCK, 2 * D_MODEL), lambda b, r, i: (b, r, jnp.maximum(i - 1, 0), 0))
    if has_prev:
        in_specs, operands = [cur, kv_prev, kv_cur], (q, kvb, kvb)
    else:
        in_specs, operands = [cur, kv_cur], (q, kvb)
    return pl.pallas_call(
        functools.partial(_dil_prompt_kernel, has_prev, dil),
        grid=(n_b, dil, n_blk),
        in_specs=in_specs,
        out_specs=[pl.BlockSpec((None, D_MODEL // LANES, n_t, LANES), lambda b, r, i: (b, 0, 0, 0)),
                   pl.BlockSpec((None, n_t, LANES), lambda b, r, i: (b, 0, 0))],
        out_shape=[jax.ShapeDtypeStruct((n_b, D_MODEL // LANES, n_t, LANES), F32),
                   jax.ShapeDtypeStruct((n_b, n_t, LANES), F32)],
        compiler_params=_params("arbitrary", "arbitrary", "arbitrary"),
        name="dilated_prompt",
    )(*operands)


SAMPLE_HEAD_CHUNK = 4


def _sample_group_dense(t, q_col, k_win, v_win, k_new, v_new):
    n_t = k_new.shape[-1]
    pos = lax.broadcasted_iota(jnp.int32, (1, 1, k_win.shape[-1]), 2)
    pos_new = lax.broadcasted_iota(jnp.int32, (1, 1, n_t), 2)
    s_c = jnp.where(pos >= t, jnp.sum(k_win * q_col, axis=1, keepdims=True), -jnp.inf)
    s_n = jnp.where(pos_new <= t, jnp.sum(k_new * q_col, axis=1, keepdims=True), -jnp.inf)
    m = jnp.maximum(jnp.max(s_c, axis=2, keepdims=True), jnp.max(s_n, axis=2, keepdims=True))
    e_c, e_n = jnp.exp(s_c - m), jnp.exp(s_n - m)
    den = jnp.sum(e_c, axis=2, keepdims=True) + jnp.sum(e_n, axis=2, keepdims=True)
    o = (jnp.sum(v_win * e_c, axis=2, keepdims=True) + jnp.sum(v_new * e_n, axis=2, keepdims=True)) / den
    return o, m + jnp.log(den)


def _sample_group_dilated(dil, q_g, k_win, v_win, k_new, v_new):
    n_t = q_g.shape[-1]
    width = k_win.shape[-1]
    n_col = width // LANES
    res = lax.broadcasted_iota(jnp.int32, (1, 1, LANES), 2) % dil
    res_w = lax.broadcasted_iota(jnp.int32, (1, 1, width), 2) % dil
    q_pat = jnp.zeros(q_g.shape[:2] + (LANES,), F32)
    for t in range(n_t):
        q_pat = jnp.where(res == t, q_g[:, :, t:t + 1], q_pat)
    s_row = jnp.sum(k_win * jnp.concatenate([q_pat] * n_col, axis=2), axis=1, keepdims=True)
    s_new = jnp.sum(k_new * q_g, axis=1, keepdims=True)
    m_t = [jnp.maximum(jnp.max(jnp.where(res_w == t, s_row, -jnp.inf), axis=2, keepdims=True), s_new[:, :, t:t + 1])
           for t in range(n_t)]
    m_row = jnp.zeros_like(s_row)
    for t in range(n_t):
        m_row = jnp.where(res_w == t, m_t[t], m_row)
    e = jnp.exp(jnp.where(res_w < n_t, s_row, -jnp.inf) - m_row)
    acc = v_win * e
    acc_fold = sum(acc[:, :, c * LANES:(c + 1) * LANES] for c in range(n_col))
    e_fold = sum(e[:, :, c * LANES:(c + 1) * LANES] for c in range(n_col))
    out = []
    for t in range(n_t):
        e_new = jnp.exp(s_new[:, :, t:t + 1] - m_t[t])
        den = jnp.sum(jnp.where(res == t, e_fold, 0.0), axis=2, keepdims=True) + e_new
        o = (jnp.sum(jnp.where(res == t, acc_fold, 0.0), axis=2, keepdims=True) + e_new * v_new[:, :, t:t + 1]) / den
        out.append((o, m_t[t] + jnp.log(den)))
    return out


def _sample_attention(new_ref, k_ref, v_ref, o_ref):
    n_t = o_ref.shape[0]
    n_s = k_ref.shape[-1]
    heads = k_ref.shape[0]
    new = jnp.transpose(new_ref[...]).reshape(N_GROUPS + 2, heads, DIL_HD, n_t)
    pos_new = lax.broadcasted_iota(jnp.int32, (1, 1, n_t), 2)
    k_new, v_new = new[N_GROUPS], new[N_GROUPS + 1]
    groups = []
    for g, dil in enumerate(DIL_DILATIONS):
        lo = n_s - DIL_STEPS * dil
        k_win, v_win = k_ref[:, :, lo:], v_ref[:, :, lo:]
        q_g = new[g]
        if dil == 1:
            groups.append([_sample_group_dense(t, q_g[:, :, t:t + 1], k_win, v_win, k_new, v_new)
                           for t in range(n_t)])
        else:
            groups.append(_sample_group_dilated(dil, q_g, k_win, v_win, k_new, v_new))
    merged = jnp.zeros(k_new.shape, F32)
    for t in range(n_t):
        parts = [grp[t] for grp in groups]
        m = functools.reduce(jnp.maximum, [lse for _, lse in parts])
        wts = [jnp.exp(lse - m) for _, lse in parts]
        o = sum(wt * o for wt, (o, _) in zip(wts, parts)) / sum(wts)
        merged = jnp.where(pos_new == t, o, merged)
    o_ref[...] = jnp.transpose(merged.reshape(heads * DIL_HD, n_t))


class _SampleAttentionGuest:
    def __init__(self, q, cache_k, cache_v, k_new, v_new):
        self.n_b, self.n_t = q.shape[:2]
        chunks = DIL_HEADS // SAMPLE_HEAD_CHUNK
        parts = [a.reshape(self.n_b, self.n_t, -1, chunks, SAMPLE_HEAD_CHUNK * DIL_HD) for a in (q, k_new, v_new)]
        new = jnp.concatenate(parts, axis=2).transpose(0, 1, 3, 2, 4).reshape(self.n_b, self.n_t, -1)
        self.operands = (new, cache_k, cache_v)
        wb = cache_k.shape[-1]
        assert wb == DIL_STEPS * DIL_DILATIONS[-1] and DIL_DILATIONS[0] == 1 and self.n_t <= DIL_DILATIONS[1]
        assert all(wb % dil == 0 and LANES % dil == 0 for dil in DIL_DILATIONS)
        self.wb = wb
        self.outputs = []

    def specs(self, chunk):
        hc = SAMPLE_HEAD_CHUNK
        width = hc * DIL_HD
        new = pl.BlockSpec((None, self.n_t, (N_GROUPS + 2) * width), lambda i: (i, 0, chunk))
        cache = pl.BlockSpec((None, hc, DIL_HD, self.wb), lambda i: (i, chunk, 0, 0))
        out = pl.BlockSpec((None, self.n_t, width), lambda i: (i, 0, 0))
        return [new, cache, cache], out, jax.ShapeDtypeStruct((self.n_b, self.n_t, width), F32)

    def result(self):
        assert len(self.outputs) * SAMPLE_HEAD_CHUNK == DIL_HEADS
        return jnp.concatenate(self.outputs, axis=-1).reshape(self.n_b * self.n_t, D_MODEL)


def _att_merge(o_refs, lse_refs):
    lses = [l_ref[...] for l_ref in lse_refs]
    m = functools.reduce(jnp.maximum, lses)
    e = [jnp.exp(l - m) for l in lses]
    den = sum(e)
    head = lax.broadcasted_iota(jnp.int32, (LANES, D_MODEL), 0)
    col_head = lax.broadcasted_iota(jnp.int32, (LANES, D_MODEL), 1) // DIL_HD
    spread = (head == col_head).astype(BF16)
    merged = None
    for o_ref, eg in zip(o_refs, e):
        wt = eg / den
        wt_hi = wt.astype(BF16)
        wt_lo = (wt - wt_hi.astype(F32)).astype(BF16)
        wide = _dot(wt_hi, spread) + _dot(wt_lo, spread)
        part = wide * jnp.concatenate([o_ref[c] for c in range(o_ref.shape[0])], axis=1)
        merged = part if merged is None else merged + part
    return merged


def _att_out_kernel(merged_input, *refs):
    if merged_input:
        o_ref, x_ref, gt_ref, wout_ref, g_ref, b_ref, y_ref = refs
        o = o_ref[...]
    else:
        o_refs, lse_refs = refs[0:N_GROUPS], refs[N_GROUPS:2 * N_GROUPS]
        x_ref, gt_ref, wout_ref, g_ref, b_ref, y_ref = refs[2 * N_GROUPS:]
        o = _att_merge(o_refs, lse_refs)
    y = _dot(o.astype(BF16), wout_ref[...])
    z = ALPHA * x_ref[...] + (1.0 + gt_ref[...]) * y
    y_ref[...] = _layer_norm(z, g_ref[...], b_ref[...])


def _att_out_call(rows, att, x, layer, w_out, ln_g, ln_b):
    merged_input = rows.per_row
    if merged_input:
        att_specs = [rows.row_spec(D_MODEL)]
        operands = [att]
    else:
        outs, lses = att
        tps = rows.tiles_per_sample
        o_spec = pl.BlockSpec((None, D_MODEL // LANES, rows.tile, LANES), lambda i: (i // tps, 0, i % tps, 0))
        lse_spec = pl.BlockSpec((None, rows.tile, LANES), lambda i: (i // tps, i % tps, 0))
        att_specs = [o_spec] * N_GROUPS + [lse_spec] * N_GROUPS
        operands = list(outs) + list(lses)
    return pl.pallas_call(
        functools.partial(_att_out_kernel, merged_input),
        grid=(rows.steps,),
        in_specs=att_specs + [
            rows.row_spec(D_MODEL),
            rows.mod_spec(layer, 5),
            _const_spec((D_MODEL, D_MODEL), (0, 0)),
        ] + _ln_specs(layer, 1),
        out_specs=rows.row_spec(D_MODEL),
        out_shape=jax.ShapeDtypeStruct((rows.n_rows, D_MODEL), F32),
        compiler_params=_params("arbitrary"),
        name="att_out",
    )(*operands, x, rows.mods, w_out, ln_g, ln_b)


def _rope_tables(positions):
    half = DIL_HD // 2
    inv = ROPE_THETA ** (-jnp.arange(half, dtype=F32) / half)
    ang = positions.astype(F32)[:, None] * inv[None, :]
    cos, sin = jnp.cos(ang), jnp.sin(ang)
    return jnp.concatenate([cos, cos, cos, cos], axis=1), jnp.concatenate([-sin, sin, -sin, sin], axis=1)


def _trunk(rows, ffn, x, n_b, n_t, positions, gla_state, w):
    cos_t, sin_t = _rope_tables(positions)
    if rows.per_row:
        cos_t, sin_t = jnp.tile(cos_t, (n_b, 1)), jnp.tile(sin_t, (n_b, 1))

    x = ffn(x, 0, 1)
    q, k, v, la, r = _gla_proj_call(rows.wide(), x, 0, w['gla_qkv'], w['gla_g'], w['gla_r'], w['gla_g2'], w['gla_bg'])
    recurrence = _gla_step_call if rows.per_row else _gla_rec_call
    o, s_new = recurrence(q.reshape(n_b, n_t, GLA_QK), k.reshape(n_b, n_t, GLA_QK), la.reshape(n_b, n_t, GLA_QK),
                          v.reshape(n_b, n_t, GLA_V), gla_state)
    x = _gla_out_call(rows.wide(), o.reshape(n_b * n_t, GLA_V), r, x, 0, w['gla_gon'], w['gla_out'], w['ln_g'], w['ln_b'])
    x = ffn(x, 0, 2)

    k_rows, v_rows, *kv_groups = _kv_call(rows, x, cos_t, sin_t, w['kv'])
    if rows.per_row:
        k_rows, v_rows = (a.reshape(n_b, n_t, DIL_HEADS, DIL_HD) for a in (k_rows, v_rows))
    else:
        k_rows, v_rows = (a.reshape(n_b, DIL_HEADS, DIL_HD, n_t).transpose(0, 3, 1, 2) for a in (k_rows, v_rows))

    x = ffn(x, 1, 1)
    qd = _q_call(rows, x, 1, cos_t, sin_t, w['q'])
    att = yield qd, k_rows, v_rows, kv_groups
    x = _att_out_call(rows, att, x, 1, w['att_out'], w['ln_g'], w['ln_b'])
    x = ffn(x, 1, 2)
    return x, s_new, k_rows, v_rows


def _finish(trunk, att):
    try:
        trunk.send(att)
    except StopIteration as done:
        return done.value
    raise AssertionError("trunk yielded twice")


def kernel(x_prompt, x_sample, state_gla, cache_k, cache_v, c_prompt, c_sample, w_ada, b_ada, ln_g, ln_b, w_ffn1_up, w_ffn1_down, w_ffn2_up, w_ffn2_down, w_in_a, w_gate2_a, b_gate_a, g_onorm_a, w_out_a, w_ada_kv, b_ada_kv, w_kv, w_q_b, w_out_b):
    n_bp, n_tp = x_prompt.shape[:2]
    n_bs, n_ts = x_sample.shape[:2]
    assert N_A_LAYERS == 1 and DEPTH == 2
    assert n_tp % (ATT_BLOCK * DIL_DILATIONS[-1]) == 0 and n_tp % GLA_TILE == 0 and n_tp <= MAX_WINDOW
    assert n_ts < 8

    n_rs = n_bs * n_ts
    c_all = jnp.concatenate([jnp.repeat(c_sample, n_ts, axis=0), c_prompt], axis=0)
    mods = _ada_call(c_all, w_ada, b_ada, 1536)
    kv_mods = _ada_call(c_all, w_ada_kv[None], b_ada_kv[None], 1024)
    rows_p = _Rows(n_bp * n_tp, n_tp, mods[:, n_rs:, None, :], kv_mods[:, n_rs:, None, :])
    rows_s = _Rows(n_rs, n_ts, mods, kv_mods)

    w_in = w_in_a[0]
    gate_lo = 2 * GLA_QK + GLA_V
    w = {
        'ln_g': ln_g[:, :, None, :], 'ln_b': ln_b[:, :, None, :],
        'gla_qkv': w_in[:, :gate_lo].astype(BF16),
        'gla_g': jnp.pad(w_in[:, gate_lo:gate_lo + GLA_RANK], ((0, 0), (0, LANES - GLA_RANK))).astype(BF16),
        'gla_r': w_in[:, gate_lo + GLA_RANK:].astype(BF16),
        'gla_g2': jnp.pad(w_gate2_a[0], ((0, LANES - GLA_RANK), (0, 0))).astype(BF16),
        'gla_bg': b_gate_a[0][None, :],
        'gla_gon': g_onorm_a[0][None, :],
        'gla_out': w_out_a[0].astype(BF16),
        'kv': w_kv.astype(BF16),
        'q': w_q_b[0].astype(BF16),
        'att_out': w_out_b[0].astype(BF16),
    }

    ffn_f32 = {1: (w_ffn1_up, w_ffn1_down), 2: (w_ffn2_up, w_ffn2_down)}
    ffn_args = {1: (0, 0), 2: (6, 2)}
    ffn_bf16 = {}

    def ffn_sample(x, layer, which):
        if (layer, which) in ffn_bf16:
            return _ffn_call(rows_s, x, layer, *ffn_args[which], *ffn_bf16[layer, which], w['ln_g'], w['ln_b'])
        ahead = (layer, which) == (DEPTH - 1, 1)
        y, *casts = _ffn_cast_call(rows_s, x, layer, *ffn_args[which], *ffn_f32[which], w['ln_g'], w['ln_b'],
                                   cast_only=(ffn_f32[2],) if ahead else ())
        ffn_bf16[layer, which] = casts[:3]
        if ahead:
            ffn_bf16[layer, 2] = casts[3:]
        return y

    def ffn_prompt(x, layer, which):
        return _ffn_call(rows_p, x, layer, *ffn_args[which], *ffn_bf16[layer, which], w['ln_g'], w['ln_b'],
                         guest=guest)

    trunk_s = _trunk(rows_s, ffn_sample, x_sample.reshape(-1, D_MODEL), n_bs, n_ts,
                     PAST_LEN + jnp.arange(n_ts), state_gla[0], w)
    q_s, _, _, (k_flat, v_flat) = next(trunk_s)
    guest = _SampleAttentionGuest(
        q_s.reshape(n_bs, n_ts, -1), cache_k.transpose(0, 2, 3, 1), cache_v.transpose(0, 2, 3, 1),
        k_flat.reshape(n_bs, n_ts, D_MODEL), v_flat.reshape(n_bs, n_ts, D_MODEL))

    zero_state = jnp.zeros((n_bp, GLA_HEADS, GLA_DK, GLA_DV), F32)
    trunk_p = _trunk(rows_p, ffn_prompt, x_prompt.reshape(-1, D_MODEL),
                     n_bp, n_tp, jnp.arange(n_tp), zero_state, w)
    q_p, _, _, kv_p = next(trunk_p)
    att_p = tuple(zip(*[_dil_prompt_call(q_p[g], kv_p[g]) for g in range(N_GROUPS)]))
    y_p, s_p, k_p, v_p = _finish(trunk_p, att_p)
    y_s, s_s, k_s, v_s = _finish(trunk_s, guest.result())
    return (y_p.reshape(n_bp, n_tp, D_MODEL), y_s.reshape(n_bs, n_ts, D_MODEL),
            s_p[None], s_s[None], k_p, v_p, k_s, v_s)
```

```python
import functools

import jax
import jax.numpy as jnp
from jax import lax
from jax.experimental import pallas as pl
from jax.experimental.pallas import tpu as pltpu

F32 = jnp.float32
BF16 = jnp.bfloat16

D_MODEL = 1024
DEPTH = 2
PAST_LEN = 16384
N_A_LAYERS = DEPTH // 2
GLA_HEADS = 4
GLA_QK = D_MODEL // 2
GLA_V = D_MODEL
GLA_DK = GLA_QK // GLA_HEADS
GLA_DV = GLA_V // GLA_HEADS
GLA_RANK = 16
GLA_TAU = 16.0
GLA_NORM_EPS = 1e-6
DIL_HEADS = 16
DIL_HD = D_MODEL // DIL_HEADS
DIL_WINDOWS = (128, 512, 2048)
DIL_DILATIONS = (1, 4, 16)
N_GROUPS = len(DIL_WINDOWS)
DIL_STEPS = 128
MAX_WINDOW = max(DIL_WINDOWS)
DIL_SCALE = DIL_HD ** -0.5
ROPE_THETA = 10000.0
D_FF = 2816
FFN_RES = 0.5
ALPHA = (2 * DEPTH) ** 0.25
LN_EPS = 1e-5
N_MOD = 9

LANES = 128
ROW_TILE = 512
WIDE_TILE = 1024
LOG2_E = 1.4426950408889634
LN_2 = 0.6931471805599453
FF_CHUNK = 256
GLA_CHUNK = 128
GLA_TILE = 512
GLA_FACTOR_BOUND = 80.0
ATT_BLOCK = DIL_STEPS
VMEM_LIMIT = 56 * 1024 * 1024


def _params(*sem):
    return pltpu.CompilerParams(dimension_semantics=sem, vmem_limit_bytes=VMEM_LIMIT)


def _silu(x):
    return x / (1.0 + jnp.exp(-x))


def _layer_norm(z, g, b):
    mu = jnp.mean(z, axis=-1, keepdims=True)
    zc = z - mu
    var = jnp.mean(zc * zc, axis=-1, keepdims=True)
    return zc * lax.rsqrt(var + LN_EPS) * g + b


def _dot(a, b):
    return jnp.dot(a, b, preferred_element_type=F32)


def _dot_nt(a, b):
    return lax.dot_general(a, b, (((1,), (1,)), ((), ())), preferred_element_type=F32)


def _by_pair_block(fn, x):
    return jnp.concatenate([fn(x[:, c * LANES:(c + 1) * LANES]) for c in range(x.shape[1] // LANES)], axis=1)


def _rope(x, cos_t, sin_t):
    return _by_pair_block(lambda blk: blk * cos_t + pltpu.roll(blk, LANES // 2, 1) * sin_t, x)


def _pairs_to_heads(x):
    quarter = lax.broadcasted_iota(jnp.int32, (x.shape[0], LANES), 1) // (LANES // 4)

    def swap(blk):
        return jnp.where(quarter == 1, pltpu.roll(blk, 3 * LANES // 4, 1),
                         jnp.where(quarter == 2, pltpu.roll(blk, LANES // 4, 1), blk))

    return _by_pair_block(swap, x)


def _paired_columns(w):
    lead = w.shape[:-1]
    half = DIL_HD // 2
    return w.reshape(*lead, -1, 2, 2, half).swapaxes(-3, -2).reshape(*lead, -1)


def _ada_kernel(c_ref, w_ref, b_ref, o_ref):
    s = _silu(c_ref[...]).astype(BF16)
    o_ref[...] = _dot(s, w_ref[...].astype(BF16)) + b_ref[...]


def _ada_call(c_all, w, b, tn):
    n_l, _, n = w.shape
    rows = c_all.shape[0]
    return pl.pallas_call(
        _ada_kernel,
        grid=(n_l, n // tn),
        in_specs=[
            pl.BlockSpec((rows, D_MODEL), lambda l, j: (0, 0)),
            pl.BlockSpec((None, D_MODEL, tn), lambda l, j: (l, 0, j)),
            pl.BlockSpec((None, 1, tn), lambda l, j: (l, 0, j)),
        ],
        out_specs=pl.BlockSpec((None, rows, tn), lambda l, j: (l, 0, j)),
        out_shape=jax.ShapeDtypeStruct((n_l, rows, n), F32),
        compiler_params=_params("arbitrary", "arbitrary"),
        name="ada_mod",
    )(c_all, w, b.reshape(n_l, 1, n))


class _Rows:
    def __init__(self, n_rows, rows_per_sample, mods, kv_mods, tile=ROW_TILE):
        self.n_rows = n_rows
        self.rows_per_sample = rows_per_sample
        self.per_row = rows_per_sample < 8
        self.tile = n_rows if self.per_row else min(tile, rows_per_sample)
        self.steps = n_rows // self.tile
        self.tiles_per_sample = 1 if self.per_row else rows_per_sample // self.tile
        self.mods = mods
        self.kv_mods = kv_mods

    def wide(self):
        return _Rows(self.n_rows, self.rows_per_sample, self.mods, self.kv_mods, WIDE_TILE)

    def row_spec(self, width):
        return pl.BlockSpec((self.tile, width), lambda i: (i, 0))

    def mod_spec(self, layer, k):
        if self.per_row:
            return pl.BlockSpec((None, self.tile, D_MODEL), lambda i: (layer, 0, k))
        tps = self.tiles_per_sample
        return pl.BlockSpec((None, None, 1, D_MODEL), lambda i: (layer, i // tps, 0, k))


def _const_spec(shape, index):
    return pl.BlockSpec(shape, lambda i: index, pipeline_mode=pl.Buffered(1))


def _ln_specs(layer, j):
    return [_const_spec((None, None, 1, D_MODEL), (layer, j, 0, 0))] * 2


def _ffn_kernel(has_guest, x_ref, sh_ref, sc_ref, gt_ref, wa_ref, wu_ref, wdn_ref, g_ref, b_ref, *rest):
    if has_guest:
        *guest_in, o_ref, guest_out = rest
        _sample_attention(pl.program_id(0), *guest_in, guest_out)
    else:
        o_ref, = rest
    x = x_ref[...]
    h = (x * (1.0 + sc_ref[...]) + sh_ref[...]).astype(BF16)
    y = None
    for lo in range(0, D_FF, FF_CHUNK):
        a = _dot(h, wa_ref[:, lo:lo + FF_CHUNK])
        u = _dot(h, wu_ref[:, lo:lo + FF_CHUNK])
        part = _dot((_silu(a) * u).astype(BF16), wdn_ref[lo:lo + FF_CHUNK, :])
        y = part if y is None else y + part
    z = ALPHA * x + (1.0 + gt_ref[...]) * (FFN_RES * y)
    o_ref[...] = _layer_norm(z, g_ref[...], b_ref[...])


def _ffn_cast_kernel(n_cast_only, x_ref, sh_ref, sc_ref, gt_ref, wa_ref, wu_ref, wdn_ref, *rest):
    extra_in, (g_ref, b_ref, o_ref, wa_out, wu_out, wdn_out), rest = rest[:n_cast_only], rest[n_cast_only:n_cast_only + 6], rest[n_cast_only + 6:]
    extra_out, y_scr = rest[:n_cast_only], rest[n_cast_only]
    for src, dst in zip(extra_in, extra_out):
        dst[...] = src[...].astype(BF16)
    c = pl.program_id(0)
    wa, wu, wdn = wa_ref[...].astype(BF16), wu_ref[...].astype(BF16), wdn_ref[...].astype(BF16)
    wa_out[...], wu_out[...], wdn_out[...] = wa, wu, wdn
    x = x_ref[...]
    h = (x * (1.0 + sc_ref[...]) + sh_ref[...]).astype(BF16)
    part = _dot((_silu(_dot(h, wa)) * _dot(h, wu)).astype(BF16), wdn)

    @pl.when(c == 0)
    def _():
        y_scr[...] = part

    @pl.when(c > 0)
    def _():
        y_scr[...] += part

    @pl.when(c == pl.num_programs(0) - 1)
    def _():
        z = ALPHA * x + (1.0 + gt_ref[...]) * (FFN_RES * y_scr[...])
        o_ref[...] = _layer_norm(z, g_ref[...], b_ref[...])


def _ffn_cast_call(rows, x, layer, mod0, ln_j, w_up, w_down, ln_g, ln_b, cast_only=()):
    assert rows.steps == 1
    n_chunks = D_FF // FF_CHUNK

    def whole(spec):
        return pl.BlockSpec(spec.block_shape, lambda c, index_map=spec.index_map: index_map(0))

    f32_specs = [pl.BlockSpec((None, D_MODEL, FF_CHUNK), lambda c: (layer, 0, c)),
                 pl.BlockSpec((None, D_MODEL, FF_CHUNK), lambda c: (layer, 0, n_chunks + c)),
                 pl.BlockSpec((None, FF_CHUNK, D_MODEL), lambda c: (layer, c, 0))]
    bf16_specs = [pl.BlockSpec((D_MODEL, FF_CHUNK), lambda c: (0, c)),
                  pl.BlockSpec((D_MODEL, FF_CHUNK), lambda c: (0, c)),
                  pl.BlockSpec((FF_CHUNK, D_MODEL), lambda c: (c, 0))]
    bf16_shapes = [jax.ShapeDtypeStruct((D_MODEL, D_FF), BF16), jax.ShapeDtypeStruct((D_MODEL, D_FF), BF16),
                   jax.ShapeDtypeStruct((D_FF, D_MODEL), BF16)]
    extra = [w for up, down in cast_only for w in (up, up, down)]
    return pl.pallas_call(
        functools.partial(_ffn_cast_kernel, len(extra)),
        grid=(n_chunks,),
        in_specs=[whole(s) for s in (rows.row_spec(D_MODEL), rows.mod_spec(layer, mod0),
                                     rows.mod_spec(layer, mod0 + 1), rows.mod_spec(layer, mod0 + 2))]
        + f32_specs + f32_specs * len(cast_only) + [whole(s) for s in _ln_specs(layer, ln_j)],
        out_specs=[whole(rows.row_spec(D_MODEL))] + bf16_specs * (1 + len(cast_only)),
        out_shape=[jax.ShapeDtypeStruct((rows.n_rows, D_MODEL), F32)] + bf16_shapes * (1 + len(cast_only)),
        scratch_shapes=[pltpu.VMEM((rows.n_rows, D_MODEL), F32)],
        compiler_params=_params("arbitrary"),
        name="ffn_cast",
    )(x, rows.mods, rows.mods, rows.mods, w_up, w_up, w_down, *extra, ln_g, ln_b)


def _ffn_call(rows, x, layer, mod0, ln_j, w_gate, w_up, w_down, ln_g, ln_b, guest=None):
    in_specs = [
        rows.row_spec(D_MODEL),
        rows.mod_spec(layer, mod0), rows.mod_spec(layer, mod0 + 1), rows.mod_spec(layer, mod0 + 2),
        _const_spec((D_MODEL, D_FF), (0, 0)),
        _const_spec((D_MODEL, D_FF), (0, 0)),
        _const_spec((D_FF, D_MODEL), (0, 0)),
    ] + _ln_specs(layer, ln_j)
    operands = [x, rows.mods, rows.mods, rows.mods, w_gate, w_up, w_down, ln_g, ln_b]
    out_specs = [rows.row_spec(D_MODEL)]
    out_shape = [jax.ShapeDtypeStruct((rows.n_rows, D_MODEL), F32)]
    if guest is not None:
        assert guest.n_b == rows.steps
        guest_in, guest_out, guest_shape = guest.specs(len(guest.outputs))
        in_specs += guest_in
        operands += guest.operands
        out_specs.append(guest_out)
        out_shape.append(guest_shape)
    outs = pl.pallas_call(
        functools.partial(_ffn_kernel, guest is not None),
        grid=(rows.steps,),
        in_specs=in_specs,
        out_specs=out_specs,
        out_shape=out_shape,
        compiler_params=_params("arbitrary"),
        name="ffn",
    )(*operands)
    if guest is not None:
        guest.outputs.append(outs[1])
    return outs[0]


def _gla_proj_kernel(x_ref, sh_ref, sc_ref, wqkv_ref, wg_ref, wr_ref, wg2_ref, bg_ref,
                     q_ref, k_ref, v_ref, la_ref, r_ref):
    h = (x_ref[...] * (1.0 + sc_ref[...]) + sh_ref[...]).astype(BF16)
    qkv = _dot(h, wqkv_ref[...])
    q_ref[...] = qkv[:, :GLA_QK] * (GLA_DK ** -0.5)
    k_ref[...] = qkv[:, GLA_QK:2 * GLA_QK]
    v_ref[...] = qkv[:, 2 * GLA_QK:].astype(v_ref.dtype)
    g_lr = _dot(h, wg_ref[...]).astype(BF16)
    gate = _dot(g_lr, wg2_ref[...]) + bg_ref[...]
    log_sig = jnp.minimum(gate, 0.0) - jnp.log(1.0 + jnp.exp(-jnp.abs(gate)))
    la_ref[...] = log_sig / GLA_TAU
    r_ref[...] = _silu(_dot(h, wr_ref[...])).astype(r_ref.dtype)


def _gla_proj_call(rows, x, layer, w_qkv, w_g, w_r, w_g2, b_g):
    widths = (GLA_QK, GLA_QK, GLA_V, GLA_QK, GLA_V)
    narrow = F32 if rows.per_row else BF16
    dtypes = (F32, F32, narrow, F32, narrow)
    return pl.pallas_call(
        _gla_proj_kernel,
        grid=(rows.steps,),
        in_specs=[
            rows.row_spec(D_MODEL),
            rows.mod_spec(layer, 3), rows.mod_spec(layer, 4),
            _const_spec((D_MODEL, 2 * GLA_QK + GLA_V), (0, 0)),
            _const_spec((D_MODEL, LANES), (0, 0)),
            _const_spec((D_MODEL, GLA_V), (0, 0)),
            _const_spec((LANES, GLA_QK), (0, 0)),
            _const_spec((1, GLA_QK), (0, 0)),
        ],
        out_specs=[rows.row_spec(w) for w in widths],
        out_shape=[jax.ShapeDtypeStruct((rows.n_rows, w), dt) for w, dt in zip(widths, dtypes)],
        compiler_params=_params("arbitrary"),
        name="gla_proj",
    )(x, rows.mods, rows.mods, w_qkv, w_g, w_r, w_g2, b_g)


def _prefix_sum(tri, x):
    rest, total = x, None
    for _ in range(3):
        piece = rest.astype(BF16)
        rest = rest - piece.astype(F32)
        part = _dot(tri, piece)
        total = part if total is None else total + part
    return total


def _gla_scores_factored(qh, kh, bh, b_last):
    mid = 0.5 * b_last
    q_t = (qh * jnp.exp(bh - mid)).astype(BF16)
    k_t = (kh * jnp.exp(mid - bh)).astype(BF16)
    return _dot_nt(q_t, k_t)


def _gla_scores_elementwise(qh, kh, bh, k_scr, b_scr):
    k_scr[...] = kh
    b_scr[...] = bh
    col = lax.broadcasted_iota(jnp.int32, (GLA_CHUNK, GLA_CHUNK), 1)

    def body(j, acc):
        kj = k_scr[pl.ds(j, 1), :]
        bj = b_scr[pl.ds(j, 1), :]
        term = qh * kj * jnp.exp(jnp.minimum(bh - bj, 0.0))
        return acc + jnp.where(col == j, jnp.sum(term, axis=1, keepdims=True), 0.0)

    return lax.fori_loop(0, GLA_CHUNK, body, jnp.zeros((GLA_CHUNK, GLA_CHUNK), F32))


def _gla_chunk(rows, b, states, score_fn, q_ref, k_ref, v_ref, o_ref, causal):
    q, k = q_ref[rows, :], k_ref[rows, :]
    b_last = b[GLA_CHUNK - 1:GLA_CHUNK, :]
    q_in = q * jnp.exp(b)
    k_out = k * jnp.exp(b_last - b)
    new_states = []
    for hd, s_prev in enumerate(states):
        ks = slice(hd * GLA_DK, (hd + 1) * GLA_DK)
        vs = slice(hd * GLA_DV, (hd + 1) * GLA_DV)
        vh = v_ref[rows, vs].astype(BF16)
        o_inter = _dot(q_in[:, ks].astype(BF16), s_prev.astype(BF16))
        scores = jnp.where(causal, score_fn(q[:, ks], k[:, ks], b[:, ks], b_last[:, ks]), 0.0)
        o_ref[rows, vs] = (o_inter + _dot(scores.astype(BF16), vh)).astype(o_ref.dtype)
        decay_col = jnp.transpose(jnp.broadcast_to(jnp.exp(b_last[:, ks]), (GLA_DK, GLA_DK)))[:, :1]
        k_out_t = jnp.transpose(k_out[:, ks]).astype(BF16)
        new_states.append(decay_col * s_prev + _dot(k_out_t, vh))
    return new_states


def _gla_rec_kernel(q_ref, k_ref, v_ref, la_ref, s0_ref, o_ref, sfin_ref, s_scr, k_scr, b_scr):
    t = pl.program_id(1)

    @pl.when(t == 0)
    def _():
        s_scr[...] = s0_ref[...]

    row = lax.broadcasted_iota(jnp.int32, (GLA_CHUNK, GLA_CHUNK), 0)
    col = lax.broadcasted_iota(jnp.int32, (GLA_CHUNK, GLA_CHUNK), 1)
    causal = col <= row
    tri = causal.astype(BF16)
    n_chunks = GLA_TILE // GLA_CHUNK

    def decay_prefix(rows):
        return _prefix_sum(tri, la_ref[rows, :])

    chunk_rows = [slice(c * GLA_CHUNK, (c + 1) * GLA_CHUNK) for c in range(n_chunks)]
    prefixes = [decay_prefix(rows) for rows in chunk_rows]
    total_decay = functools.reduce(jnp.maximum, [-b[GLA_CHUNK - 1:GLA_CHUNK, :] for b in prefixes])
    factorable = jnp.max(total_decay) <= GLA_FACTOR_BOUND

    def refs():
        return q_ref, k_ref, v_ref, o_ref, causal

    @pl.when(factorable)
    def _():
        states = [s_scr[hd] for hd in range(GLA_HEADS)]
        for rows, b in zip(chunk_rows, prefixes):
            states = _gla_chunk(rows, b, states, _gla_scores_factored, *refs())
        for hd in range(GLA_HEADS):
            s_scr[hd] = states[hd]

    @pl.when(jnp.logical_not(factorable))
    def _():
        def chunk(c, carry):
            rows = pl.ds(pl.multiple_of(c * GLA_CHUNK, GLA_CHUNK), GLA_CHUNK)
            states = [s_scr[hd] for hd in range(GLA_HEADS)]
            elementwise = functools.partial(_gla_scores_elementwise, k_scr=k_scr, b_scr=b_scr)
            states = _gla_chunk(rows, decay_prefix(rows), states,
                                lambda qh, kh, bh, _: elementwise(qh, kh, bh), *refs())
            for hd in range(GLA_HEADS):
                s_scr[hd] = states[hd]
            return carry

        lax.fori_loop(0, n_chunks, chunk, 0)

    @pl.when(t == pl.num_programs(1) - 1)
    def _():
        sfin_ref[...] = s_scr[...]


def _gla_rec_call(q, k, la, v, s0):
    n_b, n_t = q.shape[:2]

    def seq(width):
        return pl.BlockSpec((None, GLA_TILE, width), lambda b, t: (b, t, 0))

    state = pl.BlockSpec((None, GLA_HEADS, GLA_DK, GLA_DV), lambda b, t: (b, 0, 0, 0))
    return pl.pallas_call(
        _gla_rec_kernel,
        grid=(n_b, n_t // GLA_TILE),
        in_specs=[seq(GLA_QK), seq(GLA_QK), seq(GLA_V), seq(GLA_QK), state],
        out_specs=[seq(GLA_V), state],
        out_shape=[jax.ShapeDtypeStruct((n_b, n_t, GLA_V), BF16),
                   jax.ShapeDtypeStruct((n_b, GLA_HEADS, GLA_DK, GLA_DV), F32)],
        scratch_shapes=[pltpu.VMEM((GLA_HEADS, GLA_DK, GLA_DV), F32),
                        pltpu.VMEM((GLA_CHUNK, GLA_DK), F32),
                        pltpu.VMEM((GLA_CHUNK, GLA_DK), F32)],
        compiler_params=_params("arbitrary", "arbitrary"),
        name="gla_recurrence",
    )(q, k, v, la, s0)


def _gla_step_kernel(q_ref, k_ref, la_ref, v_ref, s0_ref, o_ref, sfin_ref):
    n_t = v_ref.shape[0]
    q, k = q_ref[...], k_ref[...]
    row = lax.broadcasted_iota(jnp.int32, (n_t, n_t), 0)
    col = lax.broadcasted_iota(jnp.int32, (n_t, n_t), 1)
    b = _prefix_sum((col <= row).astype(BF16), la_ref[...])
    b_last = b[n_t - 1:n_t, :]
    q_in = q * jnp.exp(b)
    k_out = k * jnp.exp(b_last - b)
    for hd in range(GLA_HEADS):
        ks = slice(hd * GLA_DK, (hd + 1) * GLA_DK)
        vs = slice(hd * GLA_DV, (hd + 1) * GLA_DV)
        vh = v_ref[:, vs]
        s_prev = s0_ref[hd]
        o = _dot(q_in[:, ks].astype(BF16), s_prev.astype(BF16))
        for u in range(n_t):
            decay = jnp.exp(jnp.minimum(b[:, ks] - b[u:u + 1, ks], 0.0))
            score = jnp.sum(q[:, ks] * k[u:u + 1, ks] * decay, axis=1, keepdims=True)
            o = o + jnp.where(row[:, :1] >= u, score, 0.0) * vh[u:u + 1, :]
        o_ref[:, vs] = o
        decay_col = jnp.transpose(jnp.broadcast_to(jnp.exp(b_last[:, ks]), (8, GLA_DK)))[:, :1]
        sfin_ref[hd] = decay_col * s_prev + lax.dot_general(
            k_out[:, ks].astype(BF16), vh.astype(BF16), (((0,), (0,)), ((), ())), preferred_element_type=F32)


def _gla_step_call(q, k, la, v, s0):
    n_b, n_t = q.shape[:2]

    def tok(width):
        return pl.BlockSpec((None, n_t, width), lambda b: (b, 0, 0))

    state = pl.BlockSpec((None, GLA_HEADS, GLA_DK, GLA_DV), lambda b: (b, 0, 0, 0))
    return pl.pallas_call(
        _gla_step_kernel,
        grid=(n_b,),
        in_specs=[tok(GLA_QK), tok(GLA_QK), tok(GLA_QK), tok(GLA_V), state],
        out_specs=[tok(GLA_V), state],
        out_shape=[jax.ShapeDtypeStruct((n_b, n_t, GLA_V), F32),
                   jax.ShapeDtypeStruct((n_b, GLA_HEADS, GLA_DK, GLA_DV), F32)],
        compiler_params=_params("arbitrary"),
        name="gla_step",
    )(q, k, la, v, s0)


def _gla_out_kernel(o_ref, r_ref, x_ref, gt_ref, gon_ref, wout_ref, g_ref, b_ref, y_ref):
    o = o_ref[...].astype(F32)
    parts = []
    for hd in range(GLA_HEADS):
        oh = o[:, hd * GLA_DV:(hd + 1) * GLA_DV]
        ms = jnp.mean(oh * oh, axis=-1, keepdims=True)
        parts.append(oh * lax.rsqrt(ms + GLA_NORM_EPS) * gon_ref[...])
    gated = (jnp.concatenate(parts, axis=1) * r_ref[...].astype(F32)).astype(BF16)
    y = _dot(gated, wout_ref[...])
    z = ALPHA * x_ref[...] + (1.0 + gt_ref[...]) * y
    y_ref[...] = _layer_norm(z, g_ref[...], b_ref[...])


def _gla_out_call(rows, o, r, x, layer, g_onorm, w_out, ln_g, ln_b):
    return pl.pallas_call(
        _gla_out_kernel,
        grid=(rows.steps,),
        in_specs=[
            rows.row_spec(GLA_V), rows.row_spec(GLA_V), rows.row_spec(D_MODEL),
            rows.mod_spec(layer, 5),
            _const_spec((1, GLA_DV), (0, 0)),
            _const_spec((GLA_V, D_MODEL), (0, 0)),
        ] + _ln_specs(layer, 1),
        out_specs=rows.row_spec(D_MODEL),
        out_shape=jax.ShapeDtypeStruct((rows.n_rows, D_MODEL), F32),
        compiler_params=_params("arbitrary"),
        name="gla_out",
    )(o, r, x, rows.mods, g_onorm, w_out, ln_g, ln_b)


def _gather_stride(scr, start, count, stride):
    return jnp.concatenate([scr[c, pl.ds(start, count, stride=stride), :] for c in range(scr.shape[0])], axis=1)


def _fill_slabs(scr, row0, x):
    for c in range(scr.shape[0]):
        scr[c, row0:row0 + x.shape[0], :] = x[:, c * LANES:(c + 1) * LANES]


def _store_by_residue(x, out_refs, scr_a, scr_b):
    n_rows = x.shape[0]
    s = DIL_DILATIONS[1]
    assert DIL_DILATIONS == (1, s, s * s)
    outs = {o.shape[0]: o for o in out_refs}
    if 1 in outs:
        outs[1][0] = x.astype(BF16)
    if s not in outs and s * s not in outs:
        return
    _fill_slabs(scr_a, 0, x)
    per_class = n_rows // s
    for r in range(s):
        xr = _gather_stride(scr_a, r, per_class, s)
        if s in outs:
            outs[s][r] = xr.astype(BF16)
        if s * s in outs:
            _fill_slabs(scr_b, r * per_class, xr)
    if s * s in outs:
        for r in range(s * s):
            xr = _gather_stride(scr_b, (r % s) * per_class + r // s, per_class // s, s)
            outs[s * s][r] = xr.astype(BF16)


def _kv_kernel(by_residue, x_ref, sh_ref, sc_ref, cos_ref, sin_ref, w_ref, k_ref, v_ref, *rest):
    h = (x_ref[...] * (1.0 + sc_ref[...]) + sh_ref[...]).astype(BF16)
    kv = _dot(h, w_ref[...])
    k = _rope(kv[:, :D_MODEL], cos_ref[...], sin_ref[...])
    v = kv[:, D_MODEL:]
    if by_residue:
        k_t = jnp.transpose(k)
        quarter = LANES // 4
        k_ref[...] = jnp.concatenate(
            [k_t[blk * LANES + j * quarter:blk * LANES + (j + 1) * quarter]
             for blk in range(D_MODEL // LANES) for j in (0, 2, 1, 3)], axis=0)
        v_ref[...] = jnp.transpose(v)
        _store_by_residue(jnp.concatenate([k, v], axis=1), rest[0:N_GROUPS], *rest[-2:])
    else:
        k = _pairs_to_heads(k)
        k_ref[...] = k.reshape(k_ref.shape)
        v_ref[...] = v.reshape(v_ref.shape)
        rest[0][...] = k
        rest[1][...] = v


def _rope_spec(rows):
    if rows.per_row:
        return pl.BlockSpec((rows.tile, LANES), lambda i: (0, 0))
    tps = rows.tiles_per_sample
    return pl.BlockSpec((rows.tile, LANES), lambda i: (i % tps, 0))


def _kv_mod_spec(rows, k):
    if rows.per_row:
        return pl.BlockSpec((None, rows.tile, D_MODEL), lambda i: (0, 0, k))
    tps = rows.tiles_per_sample
    return pl.BlockSpec((None, None, 1, D_MODEL), lambda i: (0, i // tps, 0, k))


def _residue_spec(rows, dil, width):
    tps = rows.tiles_per_sample
    return pl.BlockSpec((None, dil, rows.tile // dil, width), lambda i: (i // tps, 0, i % tps, 0))


def _residue_shapes(rows, width, dtype):
    n_b = rows.n_rows // rows.rows_per_sample
    return [jax.ShapeDtypeStruct((n_b, dil, rows.rows_per_sample // dil, width), dtype) for dil in DIL_DILATIONS]


def _permute_scratch(width=D_MODEL):
    return [pltpu.VMEM((width // LANES, ROW_TILE, LANES), F32)] * 2


def _kv_call(rows, x, cos_t, sin_t, w_kv):
    by_residue = not rows.per_row
    scratch = []
    if by_residue:
        tps = rows.tiles_per_sample
        n_b = rows.n_rows // rows.rows_per_sample
        out_specs = [pl.BlockSpec((None, D_MODEL, rows.tile), lambda i: (i // tps, 0, i % tps))] * 2
        out_shape = [jax.ShapeDtypeStruct((n_b, D_MODEL, rows.rows_per_sample), F32)] * 2
    else:
        out_specs = [pl.BlockSpec((rows.tile, DIL_HEADS, DIL_HD), lambda i: (i, 0, 0))] * 2 + [rows.row_spec(D_MODEL)] * 2
        out_shape = ([jax.ShapeDtypeStruct((rows.n_rows, DIL_HEADS, DIL_HD), F32)] * 2
                     + [jax.ShapeDtypeStruct((rows.n_rows, D_MODEL), F32)] * 2)
    if by_residue:
        out_specs += [_residue_spec(rows, dil, 2 * D_MODEL) for dil in DIL_DILATIONS]
        out_shape += _residue_shapes(rows, 2 * D_MODEL, BF16)
        scratch = _permute_scratch(2 * D_MODEL)
    return pl.pallas_call(
        functools.partial(_kv_kernel, by_residue),
        grid=(rows.steps,),
        in_specs=[
            rows.row_spec(D_MODEL),
            _kv_mod_spec(rows, 0), _kv_mod_spec(rows, 1),
            _rope_spec(rows), _rope_spec(rows),
            _const_spec((D_MODEL, 2 * D_MODEL), (0, 0)),
        ],
        out_specs=out_specs,
        out_shape=out_shape,
        scratch_shapes=scratch,
        compiler_params=_params("arbitrary"),
        name="kv_proj",
    )(x, rows.kv_mods, rows.kv_mods, cos_t, sin_t, w_kv)


def _q_kernel(by_residue, x_ref, sh_ref, sc_ref, cos_ref, sin_ref, w_ref, *rest):
    h = (x_ref[...] * (1.0 + sc_ref[...]) + sh_ref[...]).astype(BF16)
    scale = DIL_SCALE * LOG2_E if by_residue else DIL_SCALE
    q = _rope(_dot(h, w_ref[...]), cos_ref[...] * scale, sin_ref[...] * scale)
    if by_residue:
        for g in range(N_GROUPS):
            _store_by_residue(q[:, g * D_MODEL:(g + 1) * D_MODEL], rest[g:g + 1], *rest[-2:])
    else:
        rest[0][...] = _pairs_to_heads(q)


def _q_call(rows, x, layer, cos_t, sin_t, w_q):
    width = N_GROUPS * D_MODEL
    by_residue = not rows.per_row
    if by_residue:
        out_specs = [_residue_spec(rows, dil, D_MODEL) for dil in DIL_DILATIONS]
        out_shape = _residue_shapes(rows, D_MODEL, BF16)
        scratch = _permute_scratch()
    else:
        out_specs = rows.row_spec(width)
        out_shape = jax.ShapeDtypeStruct((rows.n_rows, width), F32)
        scratch = []
    return pl.pallas_call(
        functools.partial(_q_kernel, by_residue),
        grid=(rows.steps,),
        in_specs=[
            rows.row_spec(D_MODEL),
            rows.mod_spec(layer, 3), rows.mod_spec(layer, 4),
            _rope_spec(rows), _rope_spec(rows),
            _const_spec((D_MODEL, width), (0, 0)),
        ],
        out_specs=out_specs,
        out_shape=out_shape,
        scratch_shapes=scratch,
        compiler_params=_params("arbitrary"),
        name="q_proj",
    )(x, rows.mods, rows.mods, cos_t, sin_t, w_q)


def _pair_softmax_pv(scores, v_pair, even_lane):
    half = scores.shape[0] // 2
    m = jnp.max(scores, axis=-1, keepdims=True)
    e = jnp.exp2(scores - m)
    den = jnp.sum(e, axis=-1, keepdims=True)
    o2 = _dot(e.astype(BF16), v_pair)
    o = jnp.where(even_lane, o2[:half], o2[half:]) / jnp.where(even_lane, den[:half], den[half:])
    lse2 = (m + jnp.log2(den)) * LN_2
    return o, lse2[:half], lse2[half:]


def _dil_prompt_kernel(has_prev, dil, q_ref, *refs):
    if has_prev:
        kvp_ref, kvc_ref, o_ref, lse_ref = refs
    else:
        kvc_ref, o_ref, lse_ref = refs
    blk = pl.program_id(2)
    first_token = blk * (ATT_BLOCK * dil) + pl.program_id(1)
    if dil == 1:
        token_rows = pl.ds(pl.multiple_of(first_token, ATT_BLOCK), ATT_BLOCK)
    else:
        token_rows = pl.ds(first_token, ATT_BLOCK, stride=dil)
    n_keys = (2 if has_prev else 1) * ATT_BLOCK
    lane = lax.broadcasted_iota(jnp.int32, (ATT_BLOCK, LANES), 1)
    even_lane = lane < DIL_HD
    even_head_qk = (lane % DIL_HD) < DIL_HD // 2
    qi = lax.broadcasted_iota(jnp.int32, (2 * ATT_BLOCK, n_keys), 0) % ATT_BLOCK
    kj = lax.broadcasted_iota(jnp.int32, (2 * ATT_BLOCK, n_keys), 1)
    if has_prev:
        dist = qi + ATT_BLOCK - kj
        valid = (dist >= 0) & (dist <= DIL_STEPS) & ((kj >= ATT_BLOCK) | (blk > 0))
    else:
        valid = kj <= qi
    lse_all = jnp.zeros((ATT_BLOCK, LANES), F32)
    for p in range(DIL_HEADS // 2):
        sl = slice(p * LANES, (p + 1) * LANES)
        vsl = slice(D_MODEL + p * LANES, D_MODEL + (p + 1) * LANES)
        qp = q_ref[:, sl]
        zero = jnp.zeros_like(qp)
        q2 = jnp.concatenate([jnp.where(even_head_qk, qp, zero), jnp.where(even_head_qk, zero, qp)], axis=0)
        if has_prev:
            k_pair = jnp.concatenate([kvp_ref[:, sl], kvc_ref[:, sl]], axis=0)
            v_pair = jnp.concatenate([kvp_ref[:, vsl], kvc_ref[:, vsl]], axis=0)
        else:
            k_pair, v_pair = kvc_ref[:, sl], kvc_ref[:, vsl]
        scores = jnp.where(valid, _dot_nt(q2, k_pair), -jnp.inf)
        o, lse_even, lse_odd = _pair_softmax_pv(scores, v_pair, even_lane)
        o_ref[p, token_rows, :] = o
        lse_all = jnp.where(lane == 2 * p, lse_even, jnp.where(lane == 2 * p + 1, lse_odd, lse_all))
    lse_ref[token_rows, :] = lse_all


def _dil_prompt_call(q, kvb):
    n_b, dil, length, _ = q.shape
    n_t = dil * length
    n_blk = length // ATT_BLOCK
    has_prev = n_blk > 1
    cur = pl.BlockSpec((None, None, ATT_BLOCK, D_MODEL), lambda b, r, i: (b, r, i, 0))
    kv_cur = pl.BlockSpec((None, None, ATT_BLOCK, 2 * D_MODEL), lambda b, r, i: (b, r, i, 0))
    kv_prev = pl.BlockSpec((None, None, ATT_BLOCK, 2 * D_MODEL), lambda b, r, i: (b, r, jnp.maximum(i - 1, 0), 0))
    if has_prev:
        in_specs, operands = [cur, kv_prev, kv_cur], (q, kvb, kvb)
    else:
        in_specs, operands = [cur, kv_cur], (q, kvb)
    return pl.pallas_call(
        functools.partial(_dil_prompt_kernel, has_prev, dil),
        grid=(n_b, dil, n_blk),
        in_specs=in_specs,
        out_specs=[pl.BlockSpec((None, D_MODEL // LANES, n_t, LANES), lambda b, r, i: (b, 0, 0, 0)),
                   pl.BlockSpec((None, n_t, LANES), lambda b, r, i: (b, 0, 0))],
        out_shape=[jax.ShapeDtypeStruct((n_b, D_MODEL // LANES, n_t, LANES), F32),
                   jax.ShapeDtypeStruct((n_b, n_t, LANES), F32)],
        compiler_params=_params("arbitrary", "arbitrary", "arbitrary"),
        name="dilated_prompt",
    )(*operands)


SAMPLE_HEAD_CHUNK = 4


def _sample_group_dense(t, q_col, k_win, v_win, k_new, v_new):
    n_t = k_new.shape[-1]
    pos = lax.broadcasted_iota(jnp.int32, (1, 1, k_win.shape[-1]), 2)
    pos_new = lax.broadcasted_iota(jnp.int32, (1, 1, n_t), 2)
    s_c = jnp.where(pos >= t, jnp.sum(k_win * q_col, axis=1, keepdims=True), -jnp.inf)
    s_n = jnp.where(pos_new <= t, jnp.sum(k_new * q_col, axis=1, keepdims=True), -jnp.inf)
    m = jnp.maximum(jnp.max(s_c, axis=2, keepdims=True), jnp.max(s_n, axis=2, keepdims=True))
    e_c, e_n = jnp.exp(s_c - m), jnp.exp(s_n - m)
    den = jnp.sum(e_c, axis=2, keepdims=True) + jnp.sum(e_n, axis=2, keepdims=True)
    o = (jnp.sum(v_win * e_c, axis=2, keepdims=True) + jnp.sum(v_new * e_n, axis=2, keepdims=True)) / den
    return o, m + jnp.log(den)


def _sample_group_dilated(dil, q_g, k_win, v_win, k_new, v_new):
    n_t = q_g.shape[-1]
    width = k_win.shape[-1]
    n_col = width // LANES
    res = lax.broadcasted_iota(jnp.int32, (1, 1, LANES), 2) % dil

    def columns(x):
        return [x[:, :, c * LANES:(c + 1) * LANES] for c in range(n_col)]

    q_pat = jnp.zeros(q_g.shape[:2] + (LANES,), F32)
    for t in range(n_t):
        q_pat = jnp.where(res == t, q_g[:, :, t:t + 1], q_pat)
    s_row = jnp.sum(k_win * jnp.concatenate([q_pat] * n_col, axis=2), axis=1, keepdims=True)
    s_new = jnp.sum(k_new * q_g, axis=1, keepdims=True)
    s_max = functools.reduce(jnp.maximum, columns(s_row))
    m_t = [jnp.maximum(jnp.max(jnp.where(res == t, s_max, -jnp.inf), axis=2, keepdims=True), s_new[:, :, t:t + 1])
           for t in range(n_t)]
    m_pat = jnp.full(s_max.shape, jnp.inf, F32)
    for t in range(n_t):
        m_pat = jnp.where(res == t, m_t[t], m_pat)
    e = jnp.exp(s_row - jnp.concatenate([m_pat] * n_col, axis=2))
    acc_fold = sum(columns(v_win * e))
    e_fold = sum(columns(e))
    out = []
    for t in range(n_t):
        e_new = jnp.exp(s_new[:, :, t:t + 1] - m_t[t])
        den = jnp.sum(jnp.where(res == t, e_fold, 0.0), axis=2, keepdims=True) + e_new
        o = (jnp.sum(jnp.where(res == t, acc_fold, 0.0), axis=2, keepdims=True) + e_new * v_new[:, :, t:t + 1]) / den
        out.append((o, m_t[t] + jnp.log(den)))
    return out


def _sample_attention(sample, new_ref, k_ref, v_ref, o_ref):
    n_t = o_ref.shape[1]
    n_s = k_ref.shape[-1]
    heads = k_ref.shape[0]
    new = jnp.transpose(new_ref[sample]).reshape(N_GROUPS + 2, heads, DIL_HD, n_t)
    pos_new = lax.broadcasted_iota(jnp.int32, (1, 1, n_t), 2)
    k_new, v_new = new[N_GROUPS], new[N_GROUPS + 1]
    groups = []
    for g, dil in enumerate(DIL_DILATIONS):
        lo = n_s - DIL_STEPS * dil
        k_win, v_win = k_ref[:, :, lo:], v_ref[:, :, lo:]
        q_g = new[g]
        if dil == 1:
            groups.append([_sample_group_dense(t, q_g[:, :, t:t + 1], k_win, v_win, k_new, v_new)
                           for t in range(n_t)])
        else:
            groups.append(_sample_group_dilated(dil, q_g, k_win, v_win, k_new, v_new))
    merged = jnp.zeros(k_new.shape, F32)
    for t in range(n_t):
        parts = [grp[t] for grp in groups]
        m = functools.reduce(jnp.maximum, [lse for _, lse in parts])
        wts = [jnp.exp(lse - m) for _, lse in parts]
        o = sum(wt * o for wt, (o, _) in zip(wts, parts)) / sum(wts)
        merged = jnp.where(pos_new == t, o, merged)
    o_ref[sample] = jnp.transpose(merged.reshape(heads * DIL_HD, n_t))


class _SampleAttentionGuest:
    def __init__(self, q, cache_k, cache_v, k_new, v_new):
        self.n_b, self.n_t = q.shape[:2]
        chunks = DIL_HEADS // SAMPLE_HEAD_CHUNK
        parts = [a.reshape(self.n_b, self.n_t, -1, chunks, SAMPLE_HEAD_CHUNK * DIL_HD) for a in (q, k_new, v_new)]
        new = jnp.concatenate(parts, axis=2).transpose(0, 1, 3, 2, 4).reshape(self.n_b, self.n_t, -1)
        self.operands = (new, cache_k, cache_v)
        wb = cache_k.shape[-1]
        assert wb == DIL_STEPS * DIL_DILATIONS[-1] and DIL_DILATIONS[0] == 1 and self.n_t <= DIL_DILATIONS[1]
        assert all(wb % dil == 0 and LANES % dil == 0 for dil in DIL_DILATIONS)
        self.wb = wb
        self.outputs = []

    def specs(self, chunk):
        hc = SAMPLE_HEAD_CHUNK
        width = hc * DIL_HD
        new = pl.BlockSpec((self.n_b, self.n_t, (N_GROUPS + 2) * width), lambda i: (0, 0, chunk))
        cache = pl.BlockSpec((None, hc, DIL_HD, self.wb), lambda i: (i, chunk, 0, 0))
        out = pl.BlockSpec((self.n_b, self.n_t, width), lambda i: (0, 0, 0))
        return [new, cache, cache], out, jax.ShapeDtypeStruct((self.n_b, self.n_t, width), F32)

    def result(self):
        assert len(self.outputs) * SAMPLE_HEAD_CHUNK == DIL_HEADS
        return jnp.concatenate(self.outputs, axis=-1).reshape(self.n_b * self.n_t, D_MODEL)


def _att_merge(o_refs, lse_refs):
    lses = [l_ref[...] for l_ref in lse_refs]
    m = functools.reduce(jnp.maximum, lses)
    e = [jnp.exp(l - m) for l in lses]
    den = sum(e)
    head = lax.broadcasted_iota(jnp.int32, (LANES, D_MODEL), 0)
    col_head = lax.broadcasted_iota(jnp.int32, (LANES, D_MODEL), 1) // DIL_HD
    spread = (head == col_head).astype(BF16)
    merged = None
    for o_ref, eg in zip(o_refs, e):
        wt = eg / den
        wt_hi = wt.astype(BF16)
        wt_lo = (wt - wt_hi.astype(F32)).astype(BF16)
        wide = _dot(wt_hi, spread) + _dot(wt_lo, spread)
        part = wide * jnp.concatenate([o_ref[c] for c in range(o_ref.shape[0])], axis=1)
        merged = part if merged is None else merged + part
    return merged


def _att_out_kernel(merged_input, *refs):
    if merged_input:
        o_ref, x_ref, gt_ref, wout_ref, g_ref, b_ref, y_ref = refs
        o = o_ref[...]
    else:
        o_refs, lse_refs = refs[0:N_GROUPS], refs[N_GROUPS:2 * N_GROUPS]
        x_ref, gt_ref, wout_ref, g_ref, b_ref, y_ref = refs[2 * N_GROUPS:]
        o = _att_merge(o_refs, lse_refs)
    y = _dot(o.astype(BF16), wout_ref[...])
    z = ALPHA * x_ref[...] + (1.0 + gt_ref[...]) * y
    y_ref[...] = _layer_norm(z, g_ref[...], b_ref[...])


def _att_out_call(rows, att, x, layer, w_out, ln_g, ln_b):
    merged_input = rows.per_row
    if merged_input:
        att_specs = [rows.row_spec(D_MODEL)]
        operands = [att]
    else:
        outs, lses = att
        tps = rows.tiles_per_sample
        o_spec = pl.BlockSpec((None, D_MODEL // LANES, rows.tile, LANES), lambda i: (i // tps, 0, i % tps, 0))
        lse_spec = pl.BlockSpec((None, rows.tile, LANES), lambda i: (i // tps, i % tps, 0))
        att_specs = [o_spec] * N_GROUPS + [lse_spec] * N_GROUPS
        operands = list(outs) + list(lses)
    return pl.pallas_call(
        functools.partial(_att_out_kernel, merged_input),
        grid=(rows.steps,),
        in_specs=att_specs + [
            rows.row_spec(D_MODEL),
            rows.mod_spec(layer, 5),
            _const_spec((D_MODEL, D_MODEL), (0, 0)),
        ] + _ln_specs(layer, 1),
        out_specs=rows.row_spec(D_MODEL),
        out_shape=jax.ShapeDtypeStruct((rows.n_rows, D_MODEL), F32),
        compiler_params=_params("arbitrary"),
        name="att_out",
    )(*operands, x, rows.mods, w_out, ln_g, ln_b)


def _rope_tables(positions):
    half = DIL_HD // 2
    inv = ROPE_THETA ** (-jnp.arange(half, dtype=F32) / half)
    ang = positions.astype(F32)[:, None] * inv[None, :]
    cos, sin = jnp.cos(ang), jnp.sin(ang)
    return jnp.concatenate([cos, cos, cos, cos], axis=1), jnp.concatenate([-sin, -sin, sin, sin], axis=1)


def _trunk(rows, ffn, x, n_b, n_t, positions, gla_state, w):
    cos_t, sin_t = _rope_tables(positions)
    if rows.per_row:
        cos_t, sin_t = jnp.tile(cos_t, (n_b, 1)), jnp.tile(sin_t, (n_b, 1))

    x = ffn(x, 0, 1)
    q, k, v, la, r = _gla_proj_call(rows.wide(), x, 0, w['gla_qkv'], w['gla_g'], w['gla_r'], w['gla_g2'], w['gla_bg'])
    recurrence = _gla_step_call if rows.per_row else _gla_rec_call
    o, s_new = recurrence(q.reshape(n_b, n_t, GLA_QK), k.reshape(n_b, n_t, GLA_QK), la.reshape(n_b, n_t, GLA_QK),
                          v.reshape(n_b, n_t, GLA_V), gla_state)
    x = _gla_out_call(rows.wide(), o.reshape(n_b * n_t, GLA_V), r, x, 0, w['gla_gon'], w['gla_out'], w['ln_g'], w['ln_b'])
    x = ffn(x, 0, 2)

    k_rows, v_rows, *kv_groups = _kv_call(rows, x, cos_t, sin_t, w['kv'])
    if rows.per_row:
        k_rows, v_rows = (a.reshape(n_b, n_t, DIL_HEADS, DIL_HD) for a in (k_rows, v_rows))
    else:
        k_rows, v_rows = (a.reshape(n_b, DIL_HEADS, DIL_HD, n_t).transpose(0, 3, 1, 2) for a in (k_rows, v_rows))

    x = ffn(x, 1, 1)
    qd = _q_call(rows, x, 1, cos_t, sin_t, w['q'])
    att = yield qd, k_rows, v_rows, kv_groups
    x = _att_out_call(rows, att, x, 1, w['att_out'], w['ln_g'], w['ln_b'])
    x = ffn(x, 1, 2)
    return x, s_new, k_rows, v_rows


def _finish(trunk, att):
    try:
        trunk.send(att)
    except StopIteration as done:
        return done.value
    raise AssertionError("trunk yielded twice")


def kernel(x_prompt, x_sample, state_gla, cache_k, cache_v, c_prompt, c_sample, w_ada, b_ada, ln_g, ln_b, w_ffn1_up, w_ffn1_down, w_ffn2_up, w_ffn2_down, w_in_a, w_gate2_a, b_gate_a, g_onorm_a, w_out_a, w_ada_kv, b_ada_kv, w_kv, w_q_b, w_out_b):
    n_bp, n_tp = x_prompt.shape[:2]
    n_bs, n_ts = x_sample.shape[:2]
    assert N_A_LAYERS == 1 and DEPTH == 2
    assert n_tp % (ATT_BLOCK * DIL_DILATIONS[-1]) == 0 and n_tp % GLA_TILE == 0 and n_tp <= MAX_WINDOW
    assert n_ts < 8

    n_rs = n_bs * n_ts
    c_all = jnp.concatenate([jnp.repeat(c_sample, n_ts, axis=0), c_prompt], axis=0)
    mods = _ada_call(c_all, w_ada, b_ada, 1536)
    kv_mods = _ada_call(c_all, w_ada_kv[None], b_ada_kv[None], 1024)
    rows_p = _Rows(n_bp * n_tp, n_tp, mods[:, n_rs:, None, :], kv_mods[:, n_rs:, None, :])
    rows_s = _Rows(n_rs, n_ts, mods, kv_mods)

    w_in = w_in_a[0]
    gate_lo = 2 * GLA_QK + GLA_V
    w = {
        'ln_g': ln_g[:, :, None, :], 'ln_b': ln_b[:, :, None, :],
        'gla_qkv': w_in[:, :gate_lo].astype(BF16),
        'gla_g': jnp.pad(w_in[:, gate_lo:gate_lo + GLA_RANK], ((0, 0), (0, LANES - GLA_RANK))).astype(BF16),
        'gla_r': w_in[:, gate_lo + GLA_RANK:].astype(BF16),
        'gla_g2': jnp.pad(w_gate2_a[0], ((0, LANES - GLA_RANK), (0, 0))).astype(BF16),
        'gla_bg': b_gate_a[0][None, :],
        'gla_gon': g_onorm_a[0][None, :],
        'gla_out': w_out_a[0].astype(BF16),
        'kv': jnp.concatenate([_paired_columns(w_kv[:, :D_MODEL]), w_kv[:, D_MODEL:]], axis=1).astype(BF16),
        'q': _paired_columns(w_q_b[0]).astype(BF16),
        'att_out': w_out_b[0].astype(BF16),
    }

    ffn_f32 = {1: (w_ffn1_up, w_ffn1_down), 2: (w_ffn2_up, w_ffn2_down)}
    ffn_args = {1: (0, 0), 2: (6, 2)}
    ffn_bf16 = {}

    def ffn_sample(x, layer, which):
        if (layer, which) in ffn_bf16:
            return _ffn_call(rows_s, x, layer, *ffn_args[which], *ffn_bf16[layer, which], w['ln_g'], w['ln_b'])
        ahead = (layer, which) == (DEPTH - 1, 1)
        y, *casts = _ffn_cast_call(rows_s, x, layer, *ffn_args[which], *ffn_f32[which], w['ln_g'], w['ln_b'],
                                   cast_only=(ffn_f32[2],) if ahead else ())
        ffn_bf16[layer, which] = casts[:3]
        if ahead:
            ffn_bf16[layer, 2] = casts[3:]
        return y

    def ffn_prompt(x, layer, which):
        return _ffn_call(rows_p, x, layer, *ffn_args[which], *ffn_bf16[layer, which], w['ln_g'], w['ln_b'],
                         guest=guest)

    trunk_s = _trunk(rows_s, ffn_sample, x_sample.reshape(-1, D_MODEL), n_bs, n_ts,
                     PAST_LEN + jnp.arange(n_ts), state_gla[0], w)
    q_s, _, _, (k_flat, v_flat) = next(trunk_s)
    guest = _SampleAttentionGuest(
        q_s.reshape(n_bs, n_ts, -1), cache_k.transpose(0, 2, 3, 1), cache_v.transpose(0, 2, 3, 1),
        k_flat.reshape(n_bs, n_ts, D_MODEL), v_flat.reshape(n_bs, n_ts, D_MODEL))

    zero_state = jnp.zeros((n_bp, GLA_HEADS, GLA_DK, GLA_DV), F32)
    trunk_p = _trunk(rows_p, ffn_prompt, x_prompt.reshape(-1, D_MODEL),
                     n_bp, n_tp, jnp.arange(n_tp), zero_state, w)
    q_p, _, _, kv_p = next(trunk_p)
    att_p = tuple(zip(*[_dil_prompt_call(q_p[g], kv_p[g]) for g in range(N_GROUPS)]))
    y_p, s_p, k_p, v_p = _finish(trunk_p, att_p)
    y_s, s_s, k_s, v_s = _finish(trunk_s, guest.result())
    return (y_p.reshape(n_bp, n_tp, D_MODEL), y_s.reshape(n_bs, n_ts, D_MODEL),
            s_p[None], s_s[None], k_p, v_p, k_s, v_s)
```

```python
import functools

import jax
import jax.numpy as jnp
from jax import lax
from jax.experimental import pallas as pl
from jax.experimental.pallas import tpu as pltpu

F32 = jnp.float32
BF16 = jnp.bfloat16

D_MODEL = 1024
DEPTH = 2
PAST_LEN = 16384
N_A_LAYERS = DEPTH // 2
GLA_HEADS = 4
GLA_QK = D_MODEL // 2
GLA_V = D_MODEL
GLA_DK = GLA_QK // GLA_HEADS
GLA_DV = GLA_V // GLA_HEADS
GLA_RANK = 16
GLA_TAU = 16.0
GLA_NORM_EPS = 1e-6
DIL_HEADS = 16
DIL_HD = D_MODEL // DIL_HEADS
DIL_WINDOWS = (128, 512, 2048)
DIL_DILATIONS = (1, 4, 16)
N_GROUPS = len(DIL_WINDOWS)
DIL_STEPS = 128
MAX_WINDOW = max(DIL_WINDOWS)
DIL_SCALE = DIL_HD ** -0.5
ROPE_THETA = 10000.0
D_FF = 2816
FFN_RES = 0.5
ALPHA = (2 * DEPTH) ** 0.25
LN_EPS = 1e-5
N_MOD = 9

LANES = 128
ROW_TILE = 512
WIDE_TILE = 1024
LOG2_E = 1.4426950408889634
LN_2 = 0.6931471805599453
FF_CHUNK = 256
GLA_CHUNK = 128
GLA_TILE = 512
GLA_FACTOR_BOUND = 80.0
ATT_BLOCK = DIL_STEPS
VMEM_LIMIT = 56 * 1024 * 1024


def _params(*sem):
    return pltpu.CompilerParams(dimension_semantics=sem, vmem_limit_bytes=VMEM_LIMIT)


def _silu(x):
    return x / (1.0 + jnp.exp(-x))


def _layer_norm(z, g, b):
    mu = jnp.mean(z, axis=-1, keepdims=True)
    zc = z - mu
    var = jnp.mean(zc * zc, axis=-1, keepdims=True)
    return zc * lax.rsqrt(var + LN_EPS) * g + b


def _dot(a, b):
    return jnp.dot(a, b, preferred_element_type=F32)


def _dot_nt(a, b):
    return lax.dot_general(a, b, (((1,), (1,)), ((), ())), preferred_element_type=F32)


def _by_pair_block(fn, x):
    return jnp.concatenate([fn(x[:, c * LANES:(c + 1) * LANES]) for c in range(x.shape[1] // LANES)], axis=1)


def _rope(x, cos_t, sin_t):
    return _by_pair_block(lambda blk: blk * cos_t + pltpu.roll(blk, LANES // 2, 1) * sin_t, x)


def _pairs_to_heads(x):
    quarter = lax.broadcasted_iota(jnp.int32, (x.shape[0], LANES), 1) // (LANES // 4)

    def swap(blk):
        return jnp.where(quarter == 1, pltpu.roll(blk, 3 * LANES // 4, 1),
                         jnp.where(quarter == 2, pltpu.roll(blk, LANES // 4, 1), blk))

    return _by_pair_block(swap, x)


def _ada_kernel(c_ref, w_ref, b_ref, o_ref):
    s = _silu(c_ref[...]).astype(BF16)
    o_ref[...] = _dot(s, w_ref[...].astype(BF16)) + b_ref[...]


def _ada_call(c_all, w, b, tn):
    n_l, _, n = w.shape
    rows = c_all.shape[0]
    return pl.pallas_call(
        _ada_kernel,
        grid=(n_l, n // tn),
        in_specs=[
            pl.BlockSpec((rows, D_MODEL), lambda l, j: (0, 0)),
            pl.BlockSpec((None, D_MODEL, tn), lambda l, j: (l, 0, j)),
            pl.BlockSpec((None, 1, tn), lambda l, j: (l, 0, j)),
        ],
        out_specs=pl.BlockSpec((None, rows, tn), lambda l, j: (l, 0, j)),
        out_shape=jax.ShapeDtypeStruct((n_l, rows, n), F32),
        compiler_params=_params("arbitrary", "arbitrary"),
        name="ada_mod",
    )(c_all, w, b.reshape(n_l, 1, n))


class _Rows:
    def __init__(self, n_rows, rows_per_sample, mods, kv_mods, tile=ROW_TILE):
        self.n_rows = n_rows
        self.rows_per_sample = rows_per_sample
        self.per_row = rows_per_sample < 8
        self.tile = n_rows if self.per_row else min(tile, rows_per_sample)
        self.steps = n_rows // self.tile
        self.tiles_per_sample = 1 if self.per_row else rows_per_sample // self.tile
        self.mods = mods
        self.kv_mods = kv_mods

    def wide(self):
        return _Rows(self.n_rows, self.rows_per_sample, self.mods, self.kv_mods, WIDE_TILE)

    def row_spec(self, width):
        return pl.BlockSpec((self.tile, width), lambda i: (i, 0))

    def mod_spec(self, layer, k):
        if self.per_row:
            return pl.BlockSpec((None, self.tile, D_MODEL), lambda i: (layer, 0, k))
        tps = self.tiles_per_sample
        return pl.BlockSpec((None, None, 1, D_MODEL), lambda i: (layer, i // tps, 0, k))


def _const_spec(shape, index):
    return pl.BlockSpec(shape, lambda i: index, pipeline_mode=pl.Buffered(1))


def _ln_specs(layer, j):
    return [_const_spec((None, None, 1, D_MODEL), (layer, j, 0, 0))] * 2


def _ffn_kernel(has_guest, x_ref, sh_ref, sc_ref, gt_ref, wa_ref, wu_ref, wdn_ref, g_ref, b_ref, *rest):
    if has_guest:
        *guest_in, o_ref, guest_out = rest
        _sample_attention(pl.program_id(0), *guest_in, guest_out)
    else:
        o_ref, = rest
    x = x_ref[...]
    h = (x * (1.0 + sc_ref[...]) + sh_ref[...]).astype(BF16)
    y = None
    for lo in range(0, D_FF, FF_CHUNK):
        a = _dot(h, wa_ref[:, lo:lo + FF_CHUNK])
        u = _dot(h, wu_ref[:, lo:lo + FF_CHUNK])
        part = _dot((_silu(a) * u).astype(BF16), wdn_ref[lo:lo + FF_CHUNK, :])
        y = part if y is None else y + part
    z = ALPHA * x + (1.0 + gt_ref[...]) * (FFN_RES * y)
    o_ref[...] = _layer_norm(z, g_ref[...], b_ref[...])


def _ffn_cast_kernel(n_cast_only, x_ref, sh_ref, sc_ref, gt_ref, wa_ref, wu_ref, wdn_ref, *rest):
    extra_in, (g_ref, b_ref, o_ref, wa_out, wu_out, wdn_out), rest = rest[:n_cast_only], rest[n_cast_only:n_cast_only + 6], rest[n_cast_only + 6:]
    extra_out, y_scr = rest[:n_cast_only], rest[n_cast_only]
    for src, dst in zip(extra_in, extra_out):
        dst[...] = src[...].astype(BF16)
    c = pl.program_id(0)
    wa, wu, wdn = wa_ref[...].astype(BF16), wu_ref[...].astype(BF16), wdn_ref[...].astype(BF16)
    wa_out[...], wu_out[...], wdn_out[...] = wa, wu, wdn
    x = x_ref[...]
    h = (x * (1.0 + sc_ref[...]) + sh_ref[...]).astype(BF16)
    part = _dot((_silu(_dot(h, wa)) * _dot(h, wu)).astype(BF16), wdn)

    @pl.when(c == 0)
    def _():
        y_scr[...] = part

    @pl.when(c > 0)
    def _():
        y_scr[...] += part

    @pl.when(c == pl.num_programs(0) - 1)
    def _():
        z = ALPHA * x + (1.0 + gt_ref[...]) * (FFN_RES * y_scr[...])
        o_ref[...] = _layer_norm(z, g_ref[...], b_ref[...])


def _ffn_cast_call(rows, x, layer, mod0, ln_j, w_up, w_down, ln_g, ln_b, cast_only=()):
    assert rows.steps == 1
    n_chunks = D_FF // FF_CHUNK

    def whole(spec):
        return pl.BlockSpec(spec.block_shape, lambda c, index_map=spec.index_map: index_map(0))

    f32_specs = [pl.BlockSpec((None, D_MODEL, FF_CHUNK), lambda c: (layer, 0, c)),
                 pl.BlockSpec((None, D_MODEL, FF_CHUNK), lambda c: (layer, 0, n_chunks + c)),
                 pl.BlockSpec((None, FF_CHUNK, D_MODEL), lambda c: (layer, c, 0))]
    bf16_specs = [pl.BlockSpec((D_MODEL, FF_CHUNK), lambda c: (0, c)),
                  pl.BlockSpec((D_MODEL, FF_CHUNK), lambda c: (0, c)),
                  pl.BlockSpec((FF_CHUNK, D_MODEL), lambda c: (c, 0))]
    bf16_shapes = [jax.ShapeDtypeStruct((D_MODEL, D_FF), BF16), jax.ShapeDtypeStruct((D_MODEL, D_FF), BF16),
                   jax.ShapeDtypeStruct((D_FF, D_MODEL), BF16)]
    extra = [w for up, down in cast_only for w in (up, up, down)]
    return pl.pallas_call(
        functools.partial(_ffn_cast_kernel, len(extra)),
        grid=(n_chunks,),
        in_specs=[whole(s) for s in (rows.row_spec(D_MODEL), rows.mod_spec(layer, mod0),
                                     rows.mod_spec(layer, mod0 + 1), rows.mod_spec(layer, mod0 + 2))]
        + f32_specs + f32_specs * len(cast_only) + [whole(s) for s in _ln_specs(layer, ln_j)],
        out_specs=[whole(rows.row_spec(D_MODEL))] + bf16_specs * (1 + len(cast_only)),
        out_shape=[jax.ShapeDtypeStruct((rows.n_rows, D_MODEL), F32)] + bf16_shapes * (1 + len(cast_only)),
        scratch_shapes=[pltpu.VMEM((rows.n_rows, D_MODEL), F32)],
        compiler_params=_params("arbitrary"),
        name="ffn_cast",
    )(x, rows.mods, rows.mods, rows.mods, w_up, w_up, w_down, *extra, ln_g, ln_b)


def _ffn_call(rows, x, layer, mod0, ln_j, w_gate, w_up, w_down, ln_g, ln_b, guest=None):
    in_specs = [
        rows.row_spec(D_MODEL),
        rows.mod_spec(layer, mod0), rows.mod_spec(layer, mod0 + 1), rows.mod_spec(layer, mod0 + 2),
        _const_spec((D_MODEL, D_FF), (0, 0)),
        _const_spec((D_MODEL, D_FF), (0, 0)),
        _const_spec((D_FF, D_MODEL), (0, 0)),
    ] + _ln_specs(layer, ln_j)
    operands = [x, rows.mods, rows.mods, rows.mods, w_gate, w_up, w_down, ln_g, ln_b]
    out_specs = [rows.row_spec(D_MODEL)]
    out_shape = [jax.ShapeDtypeStruct((rows.n_rows, D_MODEL), F32)]
    if guest is not None:
        assert guest.n_b == rows.steps
        guest_in, guest_out, guest_shape = guest.specs(len(guest.outputs))
        in_specs += guest_in
        operands += guest.operands
        out_specs.append(guest_out)
        out_shape.append(guest_shape)
    outs = pl.pallas_call(
        functools.partial(_ffn_kernel, guest is not None),
        grid=(rows.steps,),
        in_specs=in_specs,
        out_specs=out_specs,
        out_shape=out_shape,
        compiler_params=_params("arbitrary"),
        name="ffn",
    )(*operands)
    if guest is not None:
        guest.outputs.append(outs[1])
    return outs[0]


def _gla_proj_kernel(x_ref, sh_ref, sc_ref, wqkv_ref, wg_ref, wr_ref, wg2_ref, bg_ref,
                     q_ref, k_ref, v_ref, la_ref, r_ref):
    h = (x_ref[...] * (1.0 + sc_ref[...]) + sh_ref[...]).astype(BF16)
    qkv = _dot(h, wqkv_ref[...])
    q_ref[...] = qkv[:, :GLA_QK] * (GLA_DK ** -0.5)
    k_ref[...] = qkv[:, GLA_QK:2 * GLA_QK]
    v_ref[...] = qkv[:, 2 * GLA_QK:].astype(v_ref.dtype)
    g_lr = _dot(h, wg_ref[...]).astype(BF16)
    gate = _dot(g_lr, wg2_ref[...]) + bg_ref[...]
    log_sig = jnp.minimum(gate, 0.0) - jnp.log(1.0 + jnp.exp(-jnp.abs(gate)))
    la_ref[...] = log_sig / GLA_TAU
    r_ref[...] = _silu(_dot(h, wr_ref[...])).astype(r_ref.dtype)


def _gla_proj_call(rows, x, layer, w_qkv, w_g, w_r, w_g2, b_g):
    widths = (GLA_QK, GLA_QK, GLA_V, GLA_QK, GLA_V)
    narrow = F32 if rows.per_row else BF16
    dtypes = (F32, F32, narrow, F32, narrow)
    return pl.pallas_call(
        _gla_proj_kernel,
        grid=(rows.steps,),
        in_specs=[
            rows.row_spec(D_MODEL),
            rows.mod_spec(layer, 3), rows.mod_spec(layer, 4),
            _const_spec((D_MODEL, 2 * GLA_QK + GLA_V), (0, 0)),
            _const_spec((D_MODEL, LANES), (0, 0)),
            _const_spec((D_MODEL, GLA_V), (0, 0)),
            _const_spec((LANES, GLA_QK), (0, 0)),
            _const_spec((1, GLA_QK), (0, 0)),
        ],
        out_specs=[rows.row_spec(w) for w in widths],
        out_shape=[jax.ShapeDtypeStruct((rows.n_rows, w), dt) for w, dt in zip(widths, dtypes)],
        compiler_params=_params("arbitrary"),
        name="gla_proj",
    )(x, rows.mods, rows.mods, w_qkv, w_g, w_r, w_g2, b_g)


def _prefix_sum(tri, x):
    rest, total = x, None
    for _ in range(3):
        piece = rest.astype(BF16)
        rest = rest - piece.astype(F32)
        part = _dot(tri, piece)
        total = part if total is None else total + part
    return total


def _gla_scores_factored(qh, kh, bh, b_last):
    mid = 0.5 * b_last
    q_t = (qh * jnp.exp(bh - mid)).astype(BF16)
    k_t = (kh * jnp.exp(mid - bh)).astype(BF16)
    return _dot_nt(q_t, k_t)


def _gla_scores_elementwise(qh, kh, bh, k_scr, b_scr):
    k_scr[...] = kh
    b_scr[...] = bh
    col = lax.broadcasted_iota(jnp.int32, (GLA_CHUNK, GLA_CHUNK), 1)

    def body(j, acc):
        kj = k_scr[pl.ds(j, 1), :]
        bj = b_scr[pl.ds(j, 1), :]
        term = qh * kj * jnp.exp(jnp.minimum(bh - bj, 0.0))
        return acc + jnp.where(col == j, jnp.sum(term, axis=1, keepdims=True), 0.0)

    return lax.fori_loop(0, GLA_CHUNK, body, jnp.zeros((GLA_CHUNK, GLA_CHUNK), F32))


def _gla_chunk(rows, b, states, score_fn, q_ref, k_ref, v_ref, o_ref, causal):
    q, k = q_ref[rows, :], k_ref[rows, :]
    b_last = b[GLA_CHUNK - 1:GLA_CHUNK, :]
    q_in = q * jnp.exp(b)
    k_out = k * jnp.exp(b_last - b)
    new_states = []
    for hd, s_prev in enumerate(states):
        ks = slice(hd * GLA_DK, (hd + 1) * GLA_DK)
        vs = slice(hd * GLA_DV, (hd + 1) * GLA_DV)
        vh = v_ref[rows, vs].astype(BF16)
        o_inter = _dot(q_in[:, ks].astype(BF16), s_prev.astype(BF16))
        scores = jnp.where(causal, score_fn(q[:, ks], k[:, ks], b[:, ks], b_last[:, ks]), 0.0)
        o_ref[rows, vs] = (o_inter + _dot(scores.astype(BF16), vh)).astype(o_ref.dtype)
        decay_col = jnp.transpose(jnp.broadcast_to(jnp.exp(b_last[:, ks]), (GLA_DK, GLA_DK)))[:, :1]
        k_out_t = jnp.transpose(k_out[:, ks]).astype(BF16)
        new_states.append(decay_col * s_prev + _dot(k_out_t, vh))
    return new_states


def _gla_rec_kernel(q_ref, k_ref, v_ref, la_ref, s0_ref, o_ref, sfin_ref, s_scr, k_scr, b_scr):
    t = pl.program_id(1)

    @pl.when(t == 0)
    def _():
        s_scr[...] = s0_ref[...]

    row = lax.broadcasted_iota(jnp.int32, (GLA_CHUNK, GLA_CHUNK), 0)
    col = lax.broadcasted_iota(jnp.int32, (GLA_CHUNK, GLA_CHUNK), 1)
    causal = col <= row
    tri = causal.astype(BF16)
    n_chunks = GLA_TILE // GLA_CHUNK

    def decay_prefix(rows):
        return _prefix_sum(tri, la_ref[rows, :])

    chunk_rows = [slice(c * GLA_CHUNK, (c + 1) * GLA_CHUNK) for c in range(n_chunks)]
    prefixes = [decay_prefix(rows) for rows in chunk_rows]
    total_decay = functools.reduce(jnp.maximum, [-b[GLA_CHUNK - 1:GLA_CHUNK, :] for b in prefixes])
    factorable = jnp.max(total_decay) <= GLA_FACTOR_BOUND

    def refs():
        return q_ref, k_ref, v_ref, o_ref, causal

    @pl.when(factorable)
    def _():
        states = [s_scr[hd] for hd in range(GLA_HEADS)]
        for rows, b in zip(chunk_rows, prefixes):
            states = _gla_chunk(rows, b, states, _gla_scores_factored, *refs())
        for hd in range(GLA_HEADS):
            s_scr[hd] = states[hd]

    @pl.when(jnp.logical_not(factorable))
    def _():
        def chunk(c, carry):
            rows = pl.ds(pl.multiple_of(c * GLA_CHUNK, GLA_CHUNK), GLA_CHUNK)
            states = [s_scr[hd] for hd in range(GLA_HEADS)]
            elementwise = functools.partial(_gla_scores_elementwise, k_scr=k_scr, b_scr=b_scr)
            states = _gla_chunk(rows, decay_prefix(rows), states,
                                lambda qh, kh, bh, _: elementwise(qh, kh, bh), *refs())
            for hd in range(GLA_HEADS):
                s_scr[hd] = states[hd]
            return carry

        lax.fori_loop(0, n_chunks, chunk, 0)

    @pl.when(t == pl.num_programs(1) - 1)
    def _():
        sfin_ref[...] = s_scr[...]


def _gla_rec_call(q, k, la, v, s0):
    n_b, n_t = q.shape[:2]

    def seq(width):
        return pl.BlockSpec((None, GLA_TILE, width), lambda b, t: (b, t, 0))

    state = pl.BlockSpec((None, GLA_HEADS, GLA_DK, GLA_DV), lambda b, t: (b, 0, 0, 0))
    return pl.pallas_call(
        _gla_rec_kernel,
        grid=(n_b, n_t // GLA_TILE),
        in_specs=[seq(GLA_QK), seq(GLA_QK), seq(GLA_V), seq(GLA_QK), state],
        out_specs=[seq(GLA_V), state],
        out_shape=[jax.ShapeDtypeStruct((n_b, n_t, GLA_V), BF16),
                   jax.ShapeDtypeStruct((n_b, GLA_HEADS, GLA_DK, GLA_DV), F32)],
        scratch_shapes=[pltpu.VMEM((GLA_HEADS, GLA_DK, GLA_DV), F32),
                        pltpu.VMEM((GLA_CHUNK, GLA_DK), F32),
                        pltpu.VMEM((GLA_CHUNK, GLA_DK), F32)],
        compiler_params=_params("arbitrary", "arbitrary"),
        name="gla_recurrence",
    )(q, k, v, la, s0)


def _gla_step_kernel(q_ref, k_ref, la_ref, v_ref, s0_ref, o_ref, sfin_ref):
    n_t = v_ref.shape[0]
    q, k = q_ref[...], k_ref[...]
    row = lax.broadcasted_iota(jnp.int32, (n_t, n_t), 0)
    col = lax.broadcasted_iota(jnp.int32, (n_t, n_t), 1)
    b = _prefix_sum((col <= row).astype(BF16), la_ref[...])
    b_last = b[n_t - 1:n_t, :]
    q_in = q * jnp.exp(b)
    k_out = k * jnp.exp(b_last - b)
    for hd in range(GLA_HEADS):
        ks = slice(hd * GLA_DK, (hd + 1) * GLA_DK)
        vs = slice(hd * GLA_DV, (hd + 1) * GLA_DV)
        vh = v_ref[:, vs]
        s_prev = s0_ref[hd]
        o = _dot(q_in[:, ks].astype(BF16), s_prev.astype(BF16))
        for u in range(n_t):
            decay = jnp.exp(jnp.minimum(b[:, ks] - b[u:u + 1, ks], 0.0))
            score = jnp.sum(q[:, ks] * k[u:u + 1, ks] * decay, axis=1, keepdims=True)
            o = o + jnp.where(row[:, :1] >= u, score, 0.0) * vh[u:u + 1, :]
        o_ref[:, vs] = o
        decay_col = jnp.transpose(jnp.broadcast_to(jnp.exp(b_last[:, ks]), (8, GLA_DK)))[:, :1]
        sfin_ref[hd] = decay_col * s_prev + lax.dot_general(
            k_out[:, ks].astype(BF16), vh.astype(BF16), (((0,), (0,)), ((), ())), preferred_element_type=F32)


def _gla_step_call(q, k, la, v, s0):
    n_b, n_t = q.shape[:2]

    def tok(width):
        return pl.BlockSpec((None, n_t, width), lambda b: (b, 0, 0))

    state = pl.BlockSpec((None, GLA_HEADS, GLA_DK, GLA_DV), lambda b: (b, 0, 0, 0))
    return pl.pallas_call(
        _gla_step_kernel,
        grid=(n_b,),
        in_specs=[tok(GLA_QK), tok(GLA_QK), tok(GLA_QK), tok(GLA_V), state],
        out_specs=[tok(GLA_V), state],
        out_shape=[jax.ShapeDtypeStruct((n_b, n_t, GLA_V), F32),
                   jax.ShapeDtypeStruct((n_b, GLA_HEADS, GLA_DK, GLA_DV), F32)],
        compiler_params=_params("arbitrary"),
        name="gla_step",
    )(q, k, la, v, s0)


def _gla_out_kernel(o_ref, r_ref, x_ref, gt_ref, gon_ref, wout_ref, g_ref, b_ref, y_ref):
    o = o_ref[...].astype(F32)
    parts = []
    for hd in range(GLA_HEADS):
        oh = o[:, hd * GLA_DV:(hd + 1) * GLA_DV]
        ms = jnp.mean(oh * oh, axis=-1, keepdims=True)
        parts.append(oh * lax.rsqrt(ms + GLA_NORM_EPS) * gon_ref[...])
    gated = (jnp.concatenate(parts, axis=1) * r_ref[...].astype(F32)).astype(BF16)
    y = _dot(gated, wout_ref[...])
    z = ALPHA * x_ref[...] + (1.0 + gt_ref[...]) * y
    y_ref[...] = _layer_norm(z, g_ref[...], b_ref[...])


def _gla_out_call(rows, o, r, x, layer, g_onorm, w_out, ln_g, ln_b):
    return pl.pallas_call(
        _gla_out_kernel,
        grid=(rows.steps,),
        in_specs=[
            rows.row_spec(GLA_V), rows.row_spec(GLA_V), rows.row_spec(D_MODEL),
            rows.mod_spec(layer, 5),
            _const_spec((1, GLA_DV), (0, 0)),
            _const_spec((GLA_V, D_MODEL), (0, 0)),
        ] + _ln_specs(layer, 1),
        out_specs=rows.row_spec(D_MODEL),
        out_shape=jax.ShapeDtypeStruct((rows.n_rows, D_MODEL), F32),
        compiler_params=_params("arbitrary"),
        name="gla_out",
    )(o, r, x, rows.mods, g_onorm, w_out, ln_g, ln_b)


def _gather_stride(scr, start, count, stride):
    return jnp.concatenate([scr[c, pl.ds(start, count, stride=stride), :] for c in range(scr.shape[0])], axis=1)


def _fill_slabs(scr, row0, x):
    for c in range(scr.shape[0]):
        scr[c, row0:row0 + x.shape[0], :] = x[:, c * LANES:(c + 1) * LANES]


def _store_by_residue(x, out_refs, scr_a, scr_b):
    n_rows = x.shape[0]
    s = DIL_DILATIONS[1]
    assert DIL_DILATIONS == (1, s, s * s)
    outs = {o.shape[0]: o for o in out_refs}
    if 1 in outs:
        outs[1][0] = x.astype(BF16)
    if s not in outs and s * s not in outs:
        return
    _fill_slabs(scr_a, 0, x)
    per_class = n_rows // s
    for r in range(s):
        xr = _gather_stride(scr_a, r, per_class, s)
        if s in outs:
            outs[s][r] = xr.astype(BF16)
        if s * s in outs:
            _fill_slabs(scr_b, r * per_class, xr)
    if s * s in outs:
        for r in range(s * s):
            xr = _gather_stride(scr_b, (r % s) * per_class + r // s, per_class // s, s)
            outs[s * s][r] = xr.astype(BF16)


def _kv_kernel(by_residue, x_ref, sh_ref, sc_ref, cos_ref, sin_ref, w_ref, k_ref, v_ref, *rest):
    h = (x_ref[...] * (1.0 + sc_ref[...]) + sh_ref[...]).astype(BF16)
    if by_residue:
        w = w_ref[...]
    else:
        w = jnp.concatenate([_pairs_to_heads(w_ref[:, :D_MODEL]), w_ref[:, D_MODEL:]], axis=1).astype(BF16)
        rest[2][...] = w
    kv = _dot(h, w)
    k = _rope(kv[:, :D_MODEL], cos_ref[...], sin_ref[...])
    v = kv[:, D_MODEL:]
    if by_residue:
        k_t = jnp.transpose(k)
        quarter = LANES // 4
        k_ref[...] = jnp.concatenate(
            [k_t[blk * LANES + j * quarter:blk * LANES + (j + 1) * quarter]
             for blk in range(D_MODEL // LANES) for j in (0, 2, 1, 3)], axis=0)
        v_ref[...] = jnp.transpose(v)
        _store_by_residue(jnp.concatenate([k, v], axis=1), rest[0:N_GROUPS], *rest[-2:])
    else:
        k = _pairs_to_heads(k)
        k_ref[...] = k.reshape(k_ref.shape)
        v_ref[...] = v.reshape(v_ref.shape)
        rest[0][...] = k
        rest[1][...] = v


def _rope_spec(rows):
    if rows.per_row:
        return pl.BlockSpec((rows.tile, LANES), lambda i: (0, 0))
    tps = rows.tiles_per_sample
    return pl.BlockSpec((rows.tile, LANES), lambda i: (i % tps, 0))


def _kv_mod_spec(rows, k):
    if rows.per_row:
        return pl.BlockSpec((None, rows.tile, D_MODEL), lambda i: (0, 0, k))
    tps = rows.tiles_per_sample
    return pl.BlockSpec((None, None, 1, D_MODEL), lambda i: (0, i // tps, 0, k))


def _residue_spec(rows, dil, width):
    tps = rows.tiles_per_sample
    return pl.BlockSpec((None, dil, rows.tile // dil, width), lambda i: (i // tps, 0, i % tps, 0))


def _residue_shapes(rows, width, dtype):
    n_b = rows.n_rows // rows.rows_per_sample
    return [jax.ShapeDtypeStruct((n_b, dil, rows.rows_per_sample // dil, width), dtype) for dil in DIL_DILATIONS]


def _permute_scratch(width=D_MODEL):
    return [pltpu.VMEM((width // LANES, ROW_TILE, LANES), F32)] * 2


def _kv_call(rows, x, cos_t, sin_t, w_kv):
    by_residue = not rows.per_row
    scratch = []
    if by_residue:
        tps = rows.tiles_per_sample
        n_b = rows.n_rows // rows.rows_per_sample
        out_specs = [pl.BlockSpec((None, D_MODEL, rows.tile), lambda i: (i // tps, 0, i % tps))] * 2
        out_shape = [jax.ShapeDtypeStruct((n_b, D_MODEL, rows.rows_per_sample), F32)] * 2
    else:
        assert rows.steps == 1
        out_specs = ([pl.BlockSpec((rows.tile, DIL_HEADS, DIL_HD), lambda i: (i, 0, 0))] * 2
                     + [rows.row_spec(D_MODEL)] * 2 + [pl.BlockSpec(w_kv.shape, lambda i: (0, 0))])
        out_shape = ([jax.ShapeDtypeStruct((rows.n_rows, DIL_HEADS, DIL_HD), F32)] * 2
                     + [jax.ShapeDtypeStruct((rows.n_rows, D_MODEL), F32)] * 2
                     + [jax.ShapeDtypeStruct(w_kv.shape, BF16)])
    if by_residue:
        out_specs += [_residue_spec(rows, dil, 2 * D_MODEL) for dil in DIL_DILATIONS]
        out_shape += _residue_shapes(rows, 2 * D_MODEL, BF16)
        scratch = _permute_scratch(2 * D_MODEL)
    return pl.pallas_call(
        functools.partial(_kv_kernel, by_residue),
        grid=(rows.steps,),
        in_specs=[
            rows.row_spec(D_MODEL),
            _kv_mod_spec(rows, 0), _kv_mod_spec(rows, 1),
            _rope_spec(rows), _rope_spec(rows),
            _const_spec((D_MODEL, 2 * D_MODEL), (0, 0)),
        ],
        out_specs=out_specs,
        out_shape=out_shape,
        scratch_shapes=scratch,
        compiler_params=_params("arbitrary"),
        name="kv_proj",
    )(x, rows.kv_mods, rows.kv_mods, cos_t, sin_t, w_kv)


def _q_kernel(by_residue, x_ref, sh_ref, sc_ref, cos_ref, sin_ref, w_ref, *rest):
    h = (x_ref[...] * (1.0 + sc_ref[...]) + sh_ref[...]).astype(BF16)
    scale = DIL_SCALE * LOG2_E if by_residue else DIL_SCALE
    if by_residue:
        w = w_ref[...]
    else:
        w = _pairs_to_heads(w_ref[...]).astype(BF16)
        rest[1][...] = w
    q = _rope(_dot(h, w), cos_ref[...] * scale, sin_ref[...] * scale)
    if by_residue:
        for g in range(N_GROUPS):
            _store_by_residue(q[:, g * D_MODEL:(g + 1) * D_MODEL], rest[g:g + 1], *rest[-2:])
    else:
        rest[0][...] = _pairs_to_heads(q)


def _q_call(rows, x, layer, cos_t, sin_t, w_q):
    width = N_GROUPS * D_MODEL
    by_residue = not rows.per_row
    if by_residue:
        out_specs = [_residue_spec(rows, dil, D_MODEL) for dil in DIL_DILATIONS]
        out_shape = _residue_shapes(rows, D_MODEL, BF16)
        scratch = _permute_scratch()
    else:
        assert rows.steps == 1
        out_specs = [rows.row_spec(width), pl.BlockSpec(w_q.shape, lambda i: (0, 0))]
        out_shape = [jax.ShapeDtypeStruct((rows.n_rows, width), F32), jax.ShapeDtypeStruct(w_q.shape, BF16)]
        scratch = []
    return pl.pallas_call(
        functools.partial(_q_kernel, by_residue),
        grid=(rows.steps,),
        in_specs=[
            rows.row_spec(D_MODEL),
            rows.mod_spec(layer, 3), rows.mod_spec(layer, 4),
            _rope_spec(rows), _rope_spec(rows),
            _const_spec((D_MODEL, width), (0, 0)),
        ],
        out_specs=out_specs,
        out_shape=out_shape,
        scratch_shapes=scratch,
        compiler_params=_params("arbitrary"),
        name="q_proj",
    )(x, rows.mods, rows.mods, cos_t, sin_t, w_q)


def _pair_softmax_pv(scores, v_pair, even_lane):
    half = scores.shape[0] // 2
    m = jnp.max(scores, axis=-1, keepdims=True)
    e = jnp.exp2(scores - m)
    den = jnp.sum(e, axis=-1, keepdims=True)
    o2 = _dot(e.astype(BF16), v_pair)
    o = jnp.where(even_lane, o2[:half], o2[half:]) / jnp.where(even_lane, den[:half], den[half:])
    lse2 = (m + jnp.log2(den)) * LN_2
    return o, lse2[:half], lse2[half:]


def _dil_prompt_kernel(has_prev, dil, q_ref, *refs):
    if has_prev:
        kvp_ref, kvc_ref, o_ref, lse_ref = refs
    else:
        kvc_ref, o_ref, lse_ref = refs
    blk = pl.program_id(2)
    first_token = blk * (ATT_BLOCK * dil) + pl.program_id(1)
    if dil == 1:
        token_rows = pl.ds(pl.multiple_of(first_token, ATT_BLOCK), ATT_BLOCK)
    else:
        token_rows = pl.ds(first_token, ATT_BLOCK, stride=dil)
    n_keys = (2 if has_prev else 1) * ATT_BLOCK
    lane = lax.broadcasted_iota(jnp.int32, (ATT_BLOCK, LANES), 1)
    even_lane = lane < DIL_HD
    even_head_qk = (lane % DIL_HD) < DIL_HD // 2
    qi = lax.broadcasted_iota(jnp.int32, (2 * ATT_BLOCK, n_keys), 0) % ATT_BLOCK
    kj = lax.broadcasted_iota(jnp.int32, (2 * ATT_BLOCK, n_keys), 1)
    if has_prev:
        dist = qi + ATT_BLOCK - kj
        valid = (dist >= 0) & (dist <= DIL_STEPS) & ((kj >= ATT_BLOCK) | (blk > 0))
    else:
        valid = kj <= qi
    lse_all = jnp.zeros((ATT_BLOCK, LANES), F32)
    for p in range(DIL_HEADS // 2):
        sl = slice(p * LANES, (p + 1) * LANES)
        vsl = slice(D_MODEL + p * LANES, D_MODEL + (p + 1) * LANES)
        qp = q_ref[:, sl]
        zero = jnp.zeros_like(qp)
        q2 = jnp.concatenate([jnp.where(even_head_qk, qp, zero), jnp.where(even_head_qk, zero, qp)], axis=0)
        if has_prev:
            k_pair = jnp.concatenate([kvp_ref[:, sl], kvc_ref[:, sl]], axis=0)
            v_pair = jnp.concatenate([kvp_ref[:, vsl], kvc_ref[:, vsl]], axis=0)
        else:
            k_pair, v_pair = kvc_ref[:, sl], kvc_ref[:, vsl]
        scores = jnp.where(valid, _dot_nt(q2, k_pair), -jnp.inf)
        o, lse_even, lse_odd = _pair_softmax_pv(scores, v_pair, even_lane)
        o_ref[p, token_rows, :] = o
        lse_all = jnp.where(lane == 2 * p, lse_even, jnp.where(lane == 2 * p + 1, lse_odd, lse_all))
    lse_ref[token_rows, :] = lse_all


def _dil_prompt_call(q, kvb):
    n_b, dil, length, _ = q.shape
    n_t = dil * length
    n_blk = length // ATT_BLOCK
    has_prev = n_blk > 1
    cur = pl.BlockSpec((None, None, ATT_BLOCK, D_MODEL), lambda b, r, i: (b, r, i, 0))
    kv_cur = pl.BlockSpec((None, None, ATT_BLOCK, 2 * D_MODEL), lambda b, r, i: (b, r, i, 0))
    kv_prev = pl.BlockSpec((None, None, ATT_BLOCK, 2 * D_MODEL), lambda b, r, i: (b, r, jnp.maximum(i - 1, 0), 0))
    if has_prev:
        in_specs, operands = [cur, kv_prev, kv_cur], (q, kvb, kvb)
    else:
        in_specs, operands = [cur, kv_cur], (q, kvb)
    return pl.pallas_call(
        functools.partial(_dil_prompt_kernel, has_prev, dil),
        grid=(n_b, dil, n_blk),
        in_specs=in_specs,
        out_specs=[pl.BlockSpec((None, D_MODEL // LANES, n_t, LANES), lambda b, r, i: (b, 0, 0, 0)),
                   pl.BlockSpec((None, n_t, LANES), lambda b, r, i: (b, 0, 0))],
        out_shape=[jax.ShapeDtypeStruct((n_b, D_MODEL // LANES, n_t, LANES), F32),
                   jax.ShapeDtypeStruct((n_b, n_t, LANES), F32)],
        compiler_params=_params("arbitrary", "arbitrary", "arbitrary"),
        name="dilated_prompt",
    )(*operands)


SAMPLE_HEAD_CHUNK = 4


def _sample_group_dense(t, q_col, k_win, v_win, k_new, v_new):
    n_t = k_new.shape[-1]
    pos = lax.broadcasted_iota(jnp.int32, (1, 1, k_win.shape[-1]), 2)
    pos_new = lax.broadcasted_iota(jnp.int32, (1, 1, n_t), 2)
    s_c = jnp.where(pos >= t, jnp.sum(k_win * q_col, axis=1, keepdims=True), -jnp.inf)
    s_n = jnp.where(pos_new <= t, jnp.sum(k_new * q_col, axis=1, keepdims=True), -jnp.inf)
    m = jnp.maximum(jnp.max(s_c, axis=2, keepdims=True), jnp.max(s_n, axis=2, keepdims=True))
    e_c, e_n = jnp.exp(s_c - m), jnp.exp(s_n - m)
    den = jnp.sum(e_c, axis=2, keepdims=True) + jnp.sum(e_n, axis=2, keepdims=True)
    o = (jnp.sum(v_win * e_c, axis=2, keepdims=True) + jnp.sum(v_new * e_n, axis=2, keepdims=True)) / den
    return o, m + jnp.log(den)


def _sample_group_dilated(dil, q_g, k_win, v_win, k_new, v_new):
    n_t = q_g.shape[-1]
    width = k_win.shape[-1]
    n_col = width // LANES
    res = lax.broadcasted_iota(jnp.int32, (1, 1, LANES), 2) % dil

    def columns(x):
        return [x[:, :, c * LANES:(c + 1) * LANES] for c in range(n_col)]

    q_pat = jnp.zeros(q_g.shape[:2] + (LANES,), F32)
    for t in range(n_t):
        q_pat = jnp.where(res == t, q_g[:, :, t:t + 1], q_pat)
    s_row = jnp.sum(k_win * jnp.concatenate([q_pat] * n_col, axis=2), axis=1, keepdims=True)
    s_new = jnp.sum(k_new * q_g, axis=1, keepdims=True)
    s_max = functools.reduce(jnp.maximum, columns(s_row))
    m_t = [jnp.maximum(jnp.max(jnp.where(res == t, s_max, -jnp.inf), axis=2, keepdims=True), s_new[:, :, t:t + 1])
           for t in range(n_t)]
    m_pat = jnp.full(s_max.shape, jnp.inf, F32)
    for t in range(n_t):
        m_pat = jnp.where(res == t, m_t[t], m_pat)
    e = jnp.exp(s_row - jnp.concatenate([m_pat] * n_col, axis=2))
    acc_fold = sum(columns(v_win * e))
    e_fold = sum(columns(e))
    out = []
    for t in range(n_t):
        e_new = jnp.exp(s_new[:, :, t:t + 1] - m_t[t])
        den = jnp.sum(jnp.where(res == t, e_fold, 0.0), axis=2, keepdims=True) + e_new
        o = (jnp.sum(jnp.where(res == t, acc_fold, 0.0), axis=2, keepdims=True) + e_new * v_new[:, :, t:t + 1]) / den
        out.append((o, m_t[t] + jnp.log(den)))
    return out


def _sample_attention(sample, new_ref, k_ref, v_ref, o_ref):
    n_t = o_ref.shape[1]
    n_s = k_ref.shape[-1]
    heads = k_ref.shape[0]
    new = jnp.transpose(new_ref[sample]).reshape(N_GROUPS + 2, heads, DIL_HD, n_t)
    pos_new = lax.broadcasted_iota(jnp.int32, (1, 1, n_t), 2)
    k_new, v_new = new[N_GROUPS], new[N_GROUPS + 1]
    groups = []
    for g, dil in enumerate(DIL_DILATIONS):
        lo = n_s - DIL_STEPS * dil
        k_win, v_win = k_ref[:, :, lo:], v_ref[:, :, lo:]
        q_g = new[g]
        if dil == 1:
            groups.append([_sample_group_dense(t, q_g[:, :, t:t + 1], k_win, v_win, k_new, v_new)
                           for t in range(n_t)])
        else:
            groups.append(_sample_group_dilated(dil, q_g, k_win, v_win, k_new, v_new))
    merged = jnp.zeros(k_new.shape, F32)
    for t in range(n_t):
        parts = [grp[t] for grp in groups]
        m = functools.reduce(jnp.maximum, [lse for _, lse in parts])
        wts = [jnp.exp(lse - m) for _, lse in parts]
        o = sum(wt * o for wt, (o, _) in zip(wts, parts)) / sum(wts)
        merged = jnp.where(pos_new == t, o, merged)
    o_ref[sample] = jnp.transpose(merged.reshape(heads * DIL_HD, n_t))


class _SampleAttentionGuest:
    def __init__(self, q, cache_k, cache_v, k_new, v_new):
        self.n_b, self.n_t = q.shape[:2]
        chunks = DIL_HEADS // SAMPLE_HEAD_CHUNK
        parts = [a.reshape(self.n_b, self.n_t, -1, chunks, SAMPLE_HEAD_CHUNK * DIL_HD) for a in (q, k_new, v_new)]
        new = jnp.concatenate(parts, axis=2).transpose(0, 1, 3, 2, 4).reshape(self.n_b, self.n_t, -1)
        self.operands = (new, cache_k, cache_v)
        wb = cache_k.shape[-1]
        assert wb == DIL_STEPS * DIL_DILATIONS[-1] and DIL_DILATIONS[0] == 1 and self.n_t <= DIL_DILATIONS[1]
        assert all(wb % dil == 0 and LANES % dil == 0 for dil in DIL_DILATIONS)
        self.wb = wb
        self.outputs = []

    def specs(self, chunk):
        hc = SAMPLE_HEAD_CHUNK
        width = hc * DIL_HD
        new = pl.BlockSpec((self.n_b, self.n_t, (N_GROUPS + 2) * width), lambda i: (0, 0, chunk))
        cache = pl.BlockSpec((None, hc, DIL_HD, self.wb), lambda i: (i, chunk, 0, 0))
        out = pl.BlockSpec((self.n_b, self.n_t, width), lambda i: (0, 0, 0))
        return [new, cache, cache], out, jax.ShapeDtypeStruct((self.n_b, self.n_t, width), F32)

    def result(self):
        assert len(self.outputs) * SAMPLE_HEAD_CHUNK == DIL_HEADS
        return jnp.concatenate(self.outputs, axis=-1).reshape(self.n_b * self.n_t, D_MODEL)


def _att_merge(o_refs, lse_refs):
    lses = [l_ref[...] for l_ref in lse_refs]
    m = functools.reduce(jnp.maximum, lses)
    e = [jnp.exp(l - m) for l in lses]
    den = sum(e)
    head = lax.broadcasted_iota(jnp.int32, (LANES, D_MODEL), 0)
    col_head = lax.broadcasted_iota(jnp.int32, (LANES, D_MODEL), 1) // DIL_HD
    spread = (head == col_head).astype(BF16)
    merged = None
    for o_ref, eg in zip(o_refs, e):
        wt = eg / den
        wt_hi = wt.astype(BF16)
        wt_lo = (wt - wt_hi.astype(F32)).astype(BF16)
        wide = _dot(wt_hi, spread) + _dot(wt_lo, spread)
        part = wide * jnp.concatenate([o_ref[c] for c in range(o_ref.shape[0])], axis=1)
        merged = part if merged is None else merged + part
    return merged


def _att_out_kernel(merged_input, *refs):
    if merged_input:
        o_ref, x_ref, gt_ref, wout_ref, g_ref, b_ref, y_ref = refs
        o = o_ref[...]
    else:
        o_refs, lse_refs = refs[0:N_GROUPS], refs[N_GROUPS:2 * N_GROUPS]
        x_ref, gt_ref, wout_ref, g_ref, b_ref, y_ref = refs[2 * N_GROUPS:]
        o = _att_merge(o_refs, lse_refs)
    y = _dot(o.astype(BF16), wout_ref[...])
    z = ALPHA * x_ref[...] + (1.0 + gt_ref[...]) * y
    y_ref[...] = _layer_norm(z, g_ref[...], b_ref[...])


def _att_out_call(rows, att, x, layer, w_out, ln_g, ln_b):
    merged_input = rows.per_row
    if merged_input:
        att_specs = [rows.row_spec(D_MODEL)]
        operands = [att]
    else:
        outs, lses = att
        tps = rows.tiles_per_sample
        o_spec = pl.BlockSpec((None, D_MODEL // LANES, rows.tile, LANES), lambda i: (i // tps, 0, i % tps, 0))
        lse_spec = pl.BlockSpec((None, rows.tile, LANES), lambda i: (i // tps, i % tps, 0))
        att_specs = [o_spec] * N_GROUPS + [lse_spec] * N_GROUPS
        operands = list(outs) + list(lses)
    return pl.pallas_call(
        functools.partial(_att_out_kernel, merged_input),
        grid=(rows.steps,),
        in_specs=att_specs + [
            rows.row_spec(D_MODEL),
            rows.mod_spec(layer, 5),
            _const_spec((D_MODEL, D_MODEL), (0, 0)),
        ] + _ln_specs(layer, 1),
        out_specs=rows.row_spec(D_MODEL),
        out_shape=jax.ShapeDtypeStruct((rows.n_rows, D_MODEL), F32),
        compiler_params=_params("arbitrary"),
        name="att_out",
    )(*operands, x, rows.mods, w_out, ln_g, ln_b)


def _rope_tables(positions):
    half = DIL_HD // 2
    inv = ROPE_THETA ** (-jnp.arange(half, dtype=F32) / half)
    ang = positions.astype(F32)[:, None] * inv[None, :]
    cos, sin = jnp.cos(ang), jnp.sin(ang)
    return jnp.concatenate([cos, cos, cos, cos], axis=1), jnp.concatenate([-sin, -sin, sin, sin], axis=1)


def _trunk(rows, ffn, x, n_b, n_t, positions, gla_state, w):
    cos_t, sin_t = _rope_tables(positions)
    if rows.per_row:
        cos_t, sin_t = jnp.tile(cos_t, (n_b, 1)), jnp.tile(sin_t, (n_b, 1))

    x = ffn(x, 0, 1)
    q, k, v, la, r = _gla_proj_call(rows.wide(), x, 0, w['gla_qkv'], w['gla_g'], w['gla_r'], w['gla_g2'], w['gla_bg'])
    recurrence = _gla_step_call if rows.per_row else _gla_rec_call
    o, s_new = recurrence(q.reshape(n_b, n_t, GLA_QK), k.reshape(n_b, n_t, GLA_QK), la.reshape(n_b, n_t, GLA_QK),
                          v.reshape(n_b, n_t, GLA_V), gla_state)
    x = _gla_out_call(rows.wide(), o.reshape(n_b * n_t, GLA_V), r, x, 0, w['gla_gon'], w['gla_out'], w['ln_g'], w['ln_b'])
    x = ffn(x, 0, 2)

    k_rows, v_rows, *kv_groups = _kv_call(rows, x, cos_t, sin_t, w['kv'])
    if rows.per_row:
        w['kv'] = kv_groups.pop()
    if rows.per_row:
        k_rows, v_rows = (a.reshape(n_b, n_t, DIL_HEADS, DIL_HD) for a in (k_rows, v_rows))
    else:
        k_rows, v_rows = (a.reshape(n_b, DIL_HEADS, DIL_HD, n_t).transpose(0, 3, 1, 2) for a in (k_rows, v_rows))

    x = ffn(x, 1, 1)
    qd = _q_call(rows, x, 1, cos_t, sin_t, w['q'])
    if rows.per_row:
        qd, w['q'] = qd
    att = yield qd, k_rows, v_rows, kv_groups
    x = _att_out_call(rows, att, x, 1, w['att_out'], w['ln_g'], w['ln_b'])
    x = ffn(x, 1, 2)
    return x, s_new, k_rows, v_rows


def _finish(trunk, att):
    try:
        trunk.send(att)
    except StopIteration as done:
        return done.value
    raise AssertionError("trunk yielded twice")


def kernel(x_prompt, x_sample, state_gla, cache_k, cache_v, c_prompt, c_sample, w_ada, b_ada, ln_g, ln_b, w_ffn1_up, w_ffn1_down, w_ffn2_up, w_ffn2_down, w_in_a, w_gate2_a, b_gate_a, g_onorm_a, w_out_a, w_ada_kv, b_ada_kv, w_kv, w_q_b, w_out_b):
    n_bp, n_tp = x_prompt.shape[:2]
    n_bs, n_ts = x_sample.shape[:2]
    assert N_A_LAYERS == 1 and DEPTH == 2
    assert n_tp % (ATT_BLOCK * DIL_DILATIONS[-1]) == 0 and n_tp % GLA_TILE == 0 and n_tp <= MAX_WINDOW
    assert n_ts < 8

    n_rs = n_bs * n_ts
    c_all = jnp.concatenate([jnp.repeat(c_sample, n_ts, axis=0), c_prompt], axis=0)
    mods = _ada_call(c_all, w_ada, b_ada, 1536)
    kv_mods = _ada_call(c_all, w_ada_kv[None], b_ada_kv[None], 1024)
    rows_p = _Rows(n_bp * n_tp, n_tp, mods[:, n_rs:, None, :], kv_mods[:, n_rs:, None, :])
    rows_s = _Rows(n_rs, n_ts, mods, kv_mods)

    w_in = w_in_a[0]
    gate_lo = 2 * GLA_QK + GLA_V
    w = {
        'ln_g': ln_g[:, :, None, :], 'ln_b': ln_b[:, :, None, :],
        'gla_qkv': w_in[:, :gate_lo].astype(BF16),
        'gla_g': jnp.pad(w_in[:, gate_lo:gate_lo + GLA_RANK], ((0, 0), (0, LANES - GLA_RANK))).astype(BF16),
        'gla_r': w_in[:, gate_lo + GLA_RANK:].astype(BF16),
        'gla_g2': jnp.pad(w_gate2_a[0], ((0, LANES - GLA_RANK), (0, 0))).astype(BF16),
        'gla_bg': b_gate_a[0][None, :],
        'gla_gon': g_onorm_a[0][None, :],
        'gla_out': w_out_a[0].astype(BF16),
        'kv': w_kv, 'q': w_q_b[0],
        'att_out': w_out_b[0].astype(BF16),
    }

    ffn_f32 = {1: (w_ffn1_up, w_ffn1_down), 2: (w_ffn2_up, w_ffn2_down)}
    ffn_args = {1: (0, 0), 2: (6, 2)}
    ffn_bf16 = {}

    def ffn_sample(x, layer, which):
        if (layer, which) in ffn_bf16:
            return _ffn_call(rows_s, x, layer, *ffn_args[which], *ffn_bf16[layer, which], w['ln_g'], w['ln_b'])
        ahead = (layer, which) == (DEPTH - 1, 1)
        y, *casts = _ffn_cast_call(rows_s, x, layer, *ffn_args[which], *ffn_f32[which], w['ln_g'], w['ln_b'],
                                   cast_only=(ffn_f32[2],) if ahead else ())
        ffn_bf16[layer, which] = casts[:3]
        if ahead:
            ffn_bf16[layer, 2] = casts[3:]
        return y

    def ffn_prompt(x, layer, which):
        return _ffn_call(rows_p, x, layer, *ffn_args[which], *ffn_bf16[layer, which], w['ln_g'], w['ln_b'],
                         guest=guest)

    trunk_s = _trunk(rows_s, ffn_sample, x_sample.reshape(-1, D_MODEL), n_bs, n_ts,
                     PAST_LEN + jnp.arange(n_ts), state_gla[0], w)
    q_s, _, _, (k_flat, v_flat) = next(trunk_s)
    guest = _SampleAttentionGuest(
        q_s.reshape(n_bs, n_ts, -1), cache_k.transpose(0, 2, 3, 1), cache_v.transpose(0, 2, 3, 1),
        k_flat.reshape(n_bs, n_ts, D_MODEL), v_flat.reshape(n_bs, n_ts, D_MODEL))

    zero_state = jnp.zeros((n_bp, GLA_HEADS, GLA_DK, GLA_DV), F32)
    trunk_p = _trunk(rows_p, ffn_prompt, x_prompt.reshape(-1, D_MODEL),
                     n_bp, n_tp, jnp.arange(n_tp), zero_state, w)
    q_p, _, _, kv_p = next(trunk_p)
    att_p = tuple(zip(*[_dil_prompt_call(q_p[g], kv_p[g]) for g in range(N_GROUPS)]))
    y_p, s_p, k_p, v_p = _finish(trunk_p, att_p)
    y_s, s_s, k_s, v_s = _finish(trunk_s, guest.result())
    return (y_p.reshape(n_bp, n_tp, D_MODEL), y_s.reshape(n_bs, n_ts, D_MODEL),
            s_p[None], s_s[None], k_p, v_p, k_s, v_s)
```

```python
import functools

import jax
import jax.numpy as jnp
from jax import lax
from jax.experimental import pallas as pl
from jax.experimental.pallas import tpu as pltpu

F32 = jnp.float32
BF16 = jnp.bfloat16

D_MODEL = 1024
DEPTH = 2
PAST_LEN = 16384
N_A_LAYERS = DEPTH // 2
GLA_HEADS = 4
GLA_QK = D_MODEL // 2
GLA_V = D_MODEL
GLA_DK = GLA_QK // GLA_HEADS
GLA_DV = GLA_V // GLA_HEADS
GLA_RANK = 16
GLA_TAU = 16.0
GLA_NORM_EPS = 1e-6
DIL_HEADS = 16
DIL_HD = D_MODEL // DIL_HEADS
DIL_WINDOWS = (128, 512, 2048)
DIL_DILATIONS = (1, 4, 16)
N_GROUPS = len(DIL_WINDOWS)
DIL_STEPS = 128
MAX_WINDOW = max(DIL_WINDOWS)
DIL_SCALE = DIL_HD ** -0.5
ROPE_THETA = 10000.0
D_FF = 2816
FFN_RES = 0.5
ALPHA = (2 * DEPTH) ** 0.25
LN_EPS = 1e-5

LANES = 128
SUBLANES = 8
ADA_MAX_COLS = 1536
ROW_TILE = 512
WIDE_TILE = 1024
LOG2_E = 1.4426950408889634
LN_2 = 0.6931471805599453
FF_CHUNK = 256
GLA_CHUNK = 128
GLA_TILE = 512
GLA_FACTOR_BOUND = 80.0
ATT_BLOCK = DIL_STEPS
VMEM_LIMIT = 56 * 1024 * 1024


def _params(*sem):
    return pltpu.CompilerParams(dimension_semantics=sem, vmem_limit_bytes=VMEM_LIMIT)


def _silu(x):
    return x / (1.0 + jnp.exp(-x))


def _layer_norm(z, g, b):
    mu = jnp.mean(z, axis=-1, keepdims=True)
    zc = z - mu
    var = jnp.mean(zc * zc, axis=-1, keepdims=True)
    return zc * lax.rsqrt(var + LN_EPS) * g + b


def _dot(a, b):
    return jnp.dot(a, b, preferred_element_type=F32)


def _dot_nt(a, b):
    return lax.dot_general(a, b, (((1,), (1,)), ((), ())), preferred_element_type=F32)


def _by_pair_block(fn, x):
    return jnp.concatenate([fn(x[:, c * LANES:(c + 1) * LANES]) for c in range(x.shape[1] // LANES)], axis=1)


def _rope(x, cos_t, sin_t):
    return _by_pair_block(lambda blk: blk * cos_t + pltpu.roll(blk, LANES // 2, 1) * sin_t, x)


def _pairs_to_heads(x):
    quarter = lax.broadcasted_iota(jnp.int32, (x.shape[0], LANES), 1) // (LANES // 4)

    def swap(blk):
        return jnp.where(quarter == 1, pltpu.roll(blk, 3 * LANES // 4, 1),
                         jnp.where(quarter == 2, pltpu.roll(blk, LANES // 4, 1), blk))

    return _by_pair_block(swap, x)


def _ada_kernel(c_ref, w_ref, b_ref, o_ref):
    s = _silu(c_ref[...]).astype(BF16)
    o_ref[...] = _dot(s, w_ref[...].astype(BF16)) + b_ref[...]


def _ada_call(c_all, w, b):
    n_l, _, n = w.shape
    rows = c_all.shape[0]
    tn = max(t for t in range(LANES, ADA_MAX_COLS + 1, LANES) if n % t == 0)
    return pl.pallas_call(
        _ada_kernel,
        grid=(n_l, n // tn),
        in_specs=[
            pl.BlockSpec((rows, D_MODEL), lambda l, j: (0, 0)),
            pl.BlockSpec((None, D_MODEL, tn), lambda l, j: (l, 0, j)),
            pl.BlockSpec((None, 1, tn), lambda l, j: (l, 0, j)),
        ],
        out_specs=pl.BlockSpec((None, rows, tn), lambda l, j: (l, 0, j)),
        out_shape=jax.ShapeDtypeStruct((n_l, rows, n), F32),
        compiler_params=_params("arbitrary", "arbitrary"),
        name="ada_mod",
    )(c_all, w, b.reshape(n_l, 1, n))


class _Rows:
    def __init__(self, n_rows, rows_per_sample, mods, kv_mods, tile=ROW_TILE):
        self.n_rows = n_rows
        self.rows_per_sample = rows_per_sample
        self.per_row = rows_per_sample < SUBLANES
        self.tile = n_rows if self.per_row else min(tile, rows_per_sample)
        self.steps = n_rows // self.tile
        self.tiles_per_sample = 1 if self.per_row else rows_per_sample // self.tile
        self.mods = mods
        self.kv_mods = kv_mods

    def wide(self):
        return _Rows(self.n_rows, self.rows_per_sample, self.mods, self.kv_mods, WIDE_TILE)

    def row_spec(self, width):
        return pl.BlockSpec((self.tile, width), lambda i: (i, 0))

    def mod_spec(self, layer, k):
        if self.per_row:
            return pl.BlockSpec((None, self.tile, D_MODEL), lambda i: (layer, 0, k))
        tps = self.tiles_per_sample
        return pl.BlockSpec((None, None, 1, D_MODEL), lambda i: (layer, i // tps, 0, k))


def _const_spec(shape, index):
    return pl.BlockSpec(shape, lambda i: index, pipeline_mode=pl.Buffered(1))


def _ln_specs(layer, j):
    return [_const_spec((None, None, 1, D_MODEL), (layer, j, 0, 0))] * 2


def _ffn_kernel(has_guest, x_ref, sh_ref, sc_ref, gt_ref, wa_ref, wu_ref, wdn_ref, g_ref, b_ref, *rest):
    if has_guest:
        *guest_in, o_ref, guest_out = rest
        _sample_attention(pl.program_id(0), *guest_in, guest_out)
    else:
        o_ref, = rest
    x = x_ref[...]
    h = (x * (1.0 + sc_ref[...]) + sh_ref[...]).astype(BF16)
    y = None
    for lo in range(0, D_FF, FF_CHUNK):
        a = _dot(h, wa_ref[:, lo:lo + FF_CHUNK])
        u = _dot(h, wu_ref[:, lo:lo + FF_CHUNK])
        part = _dot((_silu(a) * u).astype(BF16), wdn_ref[lo:lo + FF_CHUNK, :])
        y = part if y is None else y + part
    z = ALPHA * x + (1.0 + gt_ref[...]) * (FFN_RES * y)
    o_ref[...] = _layer_norm(z, g_ref[...], b_ref[...])


def _ffn_cast_kernel(n_cast_only, x_ref, sh_ref, sc_ref, gt_ref, wa_ref, wu_ref, wdn_ref, *rest):
    extra_in, (g_ref, b_ref, o_ref, wa_out, wu_out, wdn_out), rest = rest[:n_cast_only], rest[n_cast_only:n_cast_only + 6], rest[n_cast_only + 6:]
    extra_out, y_scr = rest[:n_cast_only], rest[n_cast_only]
    for src, dst in zip(extra_in, extra_out):
        dst[...] = src[...].astype(BF16)
    c = pl.program_id(0)
    wa, wu, wdn = wa_ref[...].astype(BF16), wu_ref[...].astype(BF16), wdn_ref[...].astype(BF16)
    wa_out[...], wu_out[...], wdn_out[...] = wa, wu, wdn
    x = x_ref[...]
    h = (x * (1.0 + sc_ref[...]) + sh_ref[...]).astype(BF16)
    part = _dot((_silu(_dot(h, wa)) * _dot(h, wu)).astype(BF16), wdn)

    @pl.when(c == 0)
    def _():
        y_scr[...] = part

    @pl.when(c > 0)
    def _():
        y_scr[...] += part

    @pl.when(c == pl.num_programs(0) - 1)
    def _():
        z = ALPHA * x + (1.0 + gt_ref[...]) * (FFN_RES * y_scr[...])
        o_ref[...] = _layer_norm(z, g_ref[...], b_ref[...])


def _ffn_cast_call(rows, x, layer, mod0, ln_j, w_up, w_down, ln_g, ln_b, cast_only=()):
    assert rows.steps == 1
    n_chunks = D_FF // FF_CHUNK

    def whole(spec):
        return pl.BlockSpec(spec.block_shape, lambda c, index_map=spec.index_map: index_map(0))

    f32_specs = [pl.BlockSpec((None, D_MODEL, FF_CHUNK), lambda c: (layer, 0, c)),
                 pl.BlockSpec((None, D_MODEL, FF_CHUNK), lambda c: (layer, 0, n_chunks + c)),
                 pl.BlockSpec((None, FF_CHUNK, D_MODEL), lambda c: (layer, c, 0))]
    bf16_specs = [pl.BlockSpec((D_MODEL, FF_CHUNK), lambda c: (0, c)),
                  pl.BlockSpec((D_MODEL, FF_CHUNK), lambda c: (0, c)),
                  pl.BlockSpec((FF_CHUNK, D_MODEL), lambda c: (c, 0))]
    bf16_shapes = [jax.ShapeDtypeStruct((D_MODEL, D_FF), BF16), jax.ShapeDtypeStruct((D_MODEL, D_FF), BF16),
                   jax.ShapeDtypeStruct((D_FF, D_MODEL), BF16)]
    extra = [w for up, down in cast_only for w in (up, up, down)]
    return pl.pallas_call(
        functools.partial(_ffn_cast_kernel, len(extra)),
        grid=(n_chunks,),
        in_specs=[whole(s) for s in (rows.row_spec(D_MODEL), rows.mod_spec(layer, mod0),
                                     rows.mod_spec(layer, mod0 + 1), rows.mod_spec(layer, mod0 + 2))]
        + f32_specs + f32_specs * len(cast_only) + [whole(s) for s in _ln_specs(layer, ln_j)],
        out_specs=[whole(rows.row_spec(D_MODEL))] + bf16_specs * (1 + len(cast_only)),
        out_shape=[jax.ShapeDtypeStruct((rows.n_rows, D_MODEL), F32)] + bf16_shapes * (1 + len(cast_only)),
        scratch_shapes=[pltpu.VMEM((rows.n_rows, D_MODEL), F32)],
        compiler_params=_params("arbitrary"),
        name="ffn_cast",
    )(x, rows.mods, rows.mods, rows.mods, w_up, w_up, w_down, *extra, ln_g, ln_b)


def _ffn_call(rows, x, layer, mod0, ln_j, w_gate, w_up, w_down, ln_g, ln_b, guest=None):
    in_specs = [
        rows.row_spec(D_MODEL),
        rows.mod_spec(layer, mod0), rows.mod_spec(layer, mod0 + 1), rows.mod_spec(layer, mod0 + 2),
        _const_spec((D_MODEL, D_FF), (0, 0)),
        _const_spec((D_MODEL, D_FF), (0, 0)),
        _const_spec((D_FF, D_MODEL), (0, 0)),
    ] + _ln_specs(layer, ln_j)
    operands = [x, rows.mods, rows.mods, rows.mods, w_gate, w_up, w_down, ln_g, ln_b]
    out_specs = [rows.row_spec(D_MODEL)]
    out_shape = [jax.ShapeDtypeStruct((rows.n_rows, D_MODEL), F32)]
    if guest is not None:
        assert guest.n_b == rows.steps
        guest_in, guest_out, guest_shape = guest.specs(len(guest.outputs))
        in_specs += guest_in
        operands += guest.operands
        out_specs.append(guest_out)
        out_shape.append(guest_shape)
    outs = pl.pallas_call(
        functools.partial(_ffn_kernel, guest is not None),
        grid=(rows.steps,),
        in_specs=in_specs,
        out_specs=out_specs,
        out_shape=out_shape,
        compiler_params=_params("arbitrary"),
        name="ffn",
    )(*operands)
    if guest is not None:
        guest.outputs.append(outs[1])
    return outs[0]


def _gla_proj_kernel(x_ref, sh_ref, sc_ref, wqkv_ref, wg_ref, wr_ref, wg2_ref, bg_ref,
                     q_ref, k_ref, v_ref, la_ref, r_ref):
    h = (x_ref[...] * (1.0 + sc_ref[...]) + sh_ref[...]).astype(BF16)
    qkv = _dot(h, wqkv_ref[...])
    q_ref[...] = qkv[:, :GLA_QK] * (GLA_DK ** -0.5)
    k_ref[...] = qkv[:, GLA_QK:2 * GLA_QK]
    v_ref[...] = qkv[:, 2 * GLA_QK:].astype(v_ref.dtype)
    g_lr = _dot(h, wg_ref[...]).astype(BF16)
    gate = _dot(g_lr, wg2_ref[...]) + bg_ref[...]
    log_sig = jnp.minimum(gate, 0.0) - jnp.log(1.0 + jnp.exp(-jnp.abs(gate)))
    la_ref[...] = log_sig / GLA_TAU
    r_ref[...] = _silu(_dot(h, wr_ref[...])).astype(r_ref.dtype)


def _gla_proj_call(rows, x, layer, w_qkv, w_g, w_r, w_g2, b_g):
    widths = (GLA_QK, GLA_QK, GLA_V, GLA_QK, GLA_V)
    narrow = F32 if rows.per_row else BF16
    dtypes = (F32, F32, narrow, F32, narrow)
    return pl.pallas_call(
        _gla_proj_kernel,
        grid=(rows.steps,),
        in_specs=[
            rows.row_spec(D_MODEL),
            rows.mod_spec(layer, 3), rows.mod_spec(layer, 4),
            _const_spec((D_MODEL, 2 * GLA_QK + GLA_V), (0, 0)),
            _const_spec((D_MODEL, LANES), (0, 0)),
            _const_spec((D_MODEL, GLA_V), (0, 0)),
            _const_spec((LANES, GLA_QK), (0, 0)),
            _const_spec((1, GLA_QK), (0, 0)),
        ],
        out_specs=[rows.row_spec(w) for w in widths],
        out_shape=[jax.ShapeDtypeStruct((rows.n_rows, w), dt) for w, dt in zip(widths, dtypes)],
        compiler_params=_params("arbitrary"),
        name="gla_proj",
    )(x, rows.mods, rows.mods, w_qkv, w_g, w_r, w_g2, b_g)


def _prefix_sum(tri, x):
    rest, total = x, None
    for _ in range(3):
        piece = rest.astype(BF16)
        rest = rest - piece.astype(F32)
        part = _dot(tri, piece)
        total = part if total is None else total + part
    return total


def _gla_scores_factored(qh, kh, bh, b_last):
    mid = 0.5 * b_last
    q_t = (qh * jnp.exp(bh - mid)).astype(BF16)
    k_t = (kh * jnp.exp(mid - bh)).astype(BF16)
    return _dot_nt(q_t, k_t)


def _gla_scores_elementwise(qh, kh, bh, k_scr, b_scr):
    k_scr[...] = kh
    b_scr[...] = bh
    col = lax.broadcasted_iota(jnp.int32, (GLA_CHUNK, GLA_CHUNK), 1)

    def body(j, acc):
        kj = k_scr[pl.ds(j, 1), :]
        bj = b_scr[pl.ds(j, 1), :]
        term = qh * kj * jnp.exp(jnp.minimum(bh - bj, 0.0))
        return acc + jnp.where(col == j, jnp.sum(term, axis=1, keepdims=True), 0.0)

    return lax.fori_loop(0, GLA_CHUNK, body, jnp.zeros((GLA_CHUNK, GLA_CHUNK), F32))


def _gla_chunk(rows, b, states, score_fn, q_ref, k_ref, v_ref, o_ref, causal):
    q, k = q_ref[rows, :], k_ref[rows, :]
    b_last = b[GLA_CHUNK - 1:GLA_CHUNK, :]
    q_in = q * jnp.exp(b)
    k_out = k * jnp.exp(b_last - b)
    new_states = []
    for hd, s_prev in enumerate(states):
        ks = slice(hd * GLA_DK, (hd + 1) * GLA_DK)
        vs = slice(hd * GLA_DV, (hd + 1) * GLA_DV)
        vh = v_ref[rows, vs].astype(BF16)
        o_inter = _dot(q_in[:, ks].astype(BF16), s_prev.astype(BF16))
        scores = jnp.where(causal, score_fn(q[:, ks], k[:, ks], b[:, ks], b_last[:, ks]), 0.0)
        o_ref[rows, vs] = (o_inter + _dot(scores.astype(BF16), vh)).astype(o_ref.dtype)
        decay_col = jnp.transpose(jnp.broadcast_to(jnp.exp(b_last[:, ks]), (GLA_DK, GLA_DK)))[:, :1]
        k_out_t = jnp.transpose(k_out[:, ks]).astype(BF16)
        new_states.append(decay_col * s_prev + _dot(k_out_t, vh))
    return new_states


def _gla_rec_kernel(q_ref, k_ref, v_ref, la_ref, s0_ref, o_ref, sfin_ref, s_scr, k_scr, b_scr):
    t = pl.program_id(1)

    @pl.when(t == 0)
    def _():
        s_scr[...] = s0_ref[...]

    row = lax.broadcasted_iota(jnp.int32, (GLA_CHUNK, GLA_CHUNK), 0)
    col = lax.broadcasted_iota(jnp.int32, (GLA_CHUNK, GLA_CHUNK), 1)
    causal = col <= row
    tri = causal.astype(BF16)
    n_chunks = GLA_TILE // GLA_CHUNK

    def decay_prefix(rows):
        return _prefix_sum(tri, la_ref[rows, :])

    chunk_rows = [slice(c * GLA_CHUNK, (c + 1) * GLA_CHUNK) for c in range(n_chunks)]
    prefixes = [decay_prefix(rows) for rows in chunk_rows]
    total_decay = functools.reduce(jnp.maximum, [-b[GLA_CHUNK - 1:GLA_CHUNK, :] for b in prefixes])
    factorable = jnp.max(total_decay) <= GLA_FACTOR_BOUND

    def refs():
        return q_ref, k_ref, v_ref, o_ref, causal

    @pl.when(factorable)
    def _():
        states = [s_scr[hd] for hd in range(GLA_HEADS)]
        for rows, b in zip(chunk_rows, prefixes):
            states = _gla_chunk(rows, b, states, _gla_scores_factored, *refs())
        for hd in range(GLA_HEADS):
            s_scr[hd] = states[hd]

    @pl.when(jnp.logical_not(factorable))
    def _():
        def chunk(c, carry):
            rows = pl.ds(pl.multiple_of(c * GLA_CHUNK, GLA_CHUNK), GLA_CHUNK)
            states = [s_scr[hd] for hd in range(GLA_HEADS)]
            elementwise = functools.partial(_gla_scores_elementwise, k_scr=k_scr, b_scr=b_scr)
            states = _gla_chunk(rows, decay_prefix(rows), states,
                                lambda qh, kh, bh, _: elementwise(qh, kh, bh), *refs())
            for hd in range(GLA_HEADS):
                s_scr[hd] = states[hd]
            return carry

        lax.fori_loop(0, n_chunks, chunk, 0)

    @pl.when(t == pl.num_programs(1) - 1)
    def _():
        sfin_ref[...] = s_scr[...]


def _gla_rec_call(q, k, la, v, s0):
    n_b, n_t = q.shape[:2]

    def seq(width):
        return pl.BlockSpec((None, GLA_TILE, width), lambda b, t: (b, t, 0))

    state = pl.BlockSpec((None, GLA_HEADS, GLA_DK, GLA_DV), lambda b, t: (b, 0, 0, 0))
    return pl.pallas_call(
        _gla_rec_kernel,
        grid=(n_b, n_t // GLA_TILE),
        in_specs=[seq(GLA_QK), seq(GLA_QK), seq(GLA_V), seq(GLA_QK), state],
        out_specs=[seq(GLA_V), state],
        out_shape=[jax.ShapeDtypeStruct((n_b, n_t, GLA_V), BF16),
                   jax.ShapeDtypeStruct((n_b, GLA_HEADS, GLA_DK, GLA_DV), F32)],
        scratch_shapes=[pltpu.VMEM((GLA_HEADS, GLA_DK, GLA_DV), F32),
                        pltpu.VMEM((GLA_CHUNK, GLA_DK), F32),
                        pltpu.VMEM((GLA_CHUNK, GLA_DK), F32)],
        compiler_params=_params("arbitrary", "arbitrary"),
        name="gla_recurrence",
    )(q, k, v, la, s0)


def _gla_step_kernel(q_ref, k_ref, la_ref, v_ref, s0_ref, o_ref, sfin_ref):
    n_t = v_ref.shape[0]
    q, k = q_ref[...], k_ref[...]
    row = lax.broadcasted_iota(jnp.int32, (n_t, n_t), 0)
    col = lax.broadcasted_iota(jnp.int32, (n_t, n_t), 1)
    b = _prefix_sum((col <= row).astype(BF16), la_ref[...])
    b_last = b[n_t - 1:n_t, :]
    q_in = q * jnp.exp(b)
    k_out = k * jnp.exp(b_last - b)
    for hd in range(GLA_HEADS):
        ks = slice(hd * GLA_DK, (hd + 1) * GLA_DK)
        vs = slice(hd * GLA_DV, (hd + 1) * GLA_DV)
        vh = v_ref[:, vs]
        s_prev = s0_ref[hd]
        o = _dot(q_in[:, ks].astype(BF16), s_prev.astype(BF16))
        for u in range(n_t):
            decay = jnp.exp(jnp.minimum(b[:, ks] - b[u:u + 1, ks], 0.0))
            score = jnp.sum(q[:, ks] * k[u:u + 1, ks] * decay, axis=1, keepdims=True)
            o = o + jnp.where(row[:, :1] >= u, score, 0.0) * vh[u:u + 1, :]
        o_ref[:, vs] = o
        decay_col = jnp.transpose(jnp.broadcast_to(jnp.exp(b_last[:, ks]), (SUBLANES, GLA_DK)))[:, :1]
        sfin_ref[hd] = decay_col * s_prev + lax.dot_general(
            k_out[:, ks].astype(BF16), vh.astype(BF16), (((0,), (0,)), ((), ())), preferred_element_type=F32)


def _gla_step_call(q, k, la, v, s0):
    n_b, n_t = q.shape[:2]

    def tok(width):
        return pl.BlockSpec((None, n_t, width), lambda b: (b, 0, 0))

    state = pl.BlockSpec((None, GLA_HEADS, GLA_DK, GLA_DV), lambda b: (b, 0, 0, 0))
    return pl.pallas_call(
        _gla_step_kernel,
        grid=(n_b,),
        in_specs=[tok(GLA_QK), tok(GLA_QK), tok(GLA_QK), tok(GLA_V), state],
        out_specs=[tok(GLA_V), state],
        out_shape=[jax.ShapeDtypeStruct((n_b, n_t, GLA_V), F32),
                   jax.ShapeDtypeStruct((n_b, GLA_HEADS, GLA_DK, GLA_DV), F32)],
        compiler_params=_params("arbitrary"),
        name="gla_step",
    )(q, k, la, v, s0)


def _gla_out_kernel(o_ref, r_ref, x_ref, gt_ref, gon_ref, wout_ref, g_ref, b_ref, y_ref):
    o = o_ref[...].astype(F32)
    parts = []
    for hd in range(GLA_HEADS):
        oh = o[:, hd * GLA_DV:(hd + 1) * GLA_DV]
        ms = jnp.mean(oh * oh, axis=-1, keepdims=True)
        parts.append(oh * lax.rsqrt(ms + GLA_NORM_EPS) * gon_ref[...])
    gated = (jnp.concatenate(parts, axis=1) * r_ref[...].astype(F32)).astype(BF16)
    y = _dot(gated, wout_ref[...])
    z = ALPHA * x_ref[...] + (1.0 + gt_ref[...]) * y
    y_ref[...] = _layer_norm(z, g_ref[...], b_ref[...])


def _gla_out_call(rows, o, r, x, layer, g_onorm, w_out, ln_g, ln_b):
    return pl.pallas_call(
        _gla_out_kernel,
        grid=(rows.steps,),
        in_specs=[
            rows.row_spec(GLA_V), rows.row_spec(GLA_V), rows.row_spec(D_MODEL),
            rows.mod_spec(layer, 5),
            _const_spec((1, GLA_DV), (0, 0)),
            _const_spec((GLA_V, D_MODEL), (0, 0)),
        ] + _ln_specs(layer, 1),
        out_specs=rows.row_spec(D_MODEL),
        out_shape=jax.ShapeDtypeStruct((rows.n_rows, D_MODEL), F32),
        compiler_params=_params("arbitrary"),
        name="gla_out",
    )(o, r, x, rows.mods, g_onorm, w_out, ln_g, ln_b)


def _gather_stride(scr, start, count, stride):
    return jnp.concatenate([scr[c, pl.ds(start, count, stride=stride), :] for c in range(scr.shape[0])], axis=1)


def _fill_slabs(scr, row0, x):
    for c in range(scr.shape[0]):
        scr[c, row0:row0 + x.shape[0], :] = x[:, c * LANES:(c + 1) * LANES]


def _store_by_residue(x, out_refs, scr_a, scr_b):
    n_rows = x.shape[0]
    s = DIL_DILATIONS[1]
    assert DIL_DILATIONS == (1, s, s * s)
    outs = {o.shape[0]: o for o in out_refs}
    if 1 in outs:
        outs[1][0] = x.astype(BF16)
    if s not in outs and s * s not in outs:
        return
    _fill_slabs(scr_a, 0, x)
    per_class = n_rows // s
    for r in range(s):
        xr = _gather_stride(scr_a, r, per_class, s)
        if s in outs:
            outs[s][r] = xr.astype(BF16)
        if s * s in outs:
            _fill_slabs(scr_b, r * per_class, xr)
    if s * s in outs:
        for r in range(s * s):
            xr = _gather_stride(scr_b, (r % s) * per_class + r // s, per_class // s, s)
            outs[s * s][r] = xr.astype(BF16)


def _kv_kernel(by_residue, x_ref, sh_ref, sc_ref, cos_ref, sin_ref, w_ref, k_ref, v_ref, *rest):
    h = (x_ref[...] * (1.0 + sc_ref[...]) + sh_ref[...]).astype(BF16)
    if by_residue:
        w = w_ref[...]
    else:
        w = jnp.concatenate([_pairs_to_heads(w_ref[:, :D_MODEL]), w_ref[:, D_MODEL:]], axis=1).astype(BF16)
        rest[2][...] = w
    kv = _dot(h, w)
    k = _rope(kv[:, :D_MODEL], cos_ref[...], sin_ref[...])
    v = kv[:, D_MODEL:]
    if by_residue:
        k_t = jnp.transpose(k)
        quarter = LANES // 4
        k_ref[...] = jnp.concatenate(
            [k_t[blk * LANES + j * quarter:blk * LANES + (j + 1) * quarter]
             for blk in range(D_MODEL // LANES) for j in (0, 2, 1, 3)], axis=0)
        v_ref[...] = jnp.transpose(v)
        _store_by_residue(jnp.concatenate([k, v], axis=1), rest[0:N_GROUPS], *rest[-2:])
    else:
        k = _pairs_to_heads(k)
        k_ref[...] = k.reshape(k_ref.shape)
        v_ref[...] = v.reshape(v_ref.shape)
        rest[0][...] = k
        rest[1][...] = v


def _rope_spec(rows):
    if rows.per_row:
        return pl.BlockSpec((rows.tile, LANES), lambda i: (0, 0))
    tps = rows.tiles_per_sample
    return pl.BlockSpec((rows.tile, LANES), lambda i: (i % tps, 0))


def _kv_mod_spec(rows, k):
    if rows.per_row:
        return pl.BlockSpec((None, rows.tile, D_MODEL), lambda i: (0, 0, k))
    tps = rows.tiles_per_sample
    return pl.BlockSpec((None, None, 1, D_MODEL), lambda i: (0, i // tps, 0, k))


def _residue_spec(rows, dil, width):
    tps = rows.tiles_per_sample
    return pl.BlockSpec((None, dil, rows.tile // dil, width), lambda i: (i // tps, 0, i % tps, 0))


def _residue_shapes(rows, width, dtype):
    n_b = rows.n_rows // rows.rows_per_sample
    return [jax.ShapeDtypeStruct((n_b, dil, rows.rows_per_sample // dil, width), dtype) for dil in DIL_DILATIONS]


def _permute_scratch(width=D_MODEL):
    return [pltpu.VMEM((width // LANES, ROW_TILE, LANES), F32)] * 2


def _kv_call(rows, x, cos_t, sin_t, w_kv):
    by_residue = not rows.per_row
    scratch = []
    if by_residue:
        tps = rows.tiles_per_sample
        n_b = rows.n_rows // rows.rows_per_sample
        out_specs = [pl.BlockSpec((None, D_MODEL, rows.tile), lambda i: (i // tps, 0, i % tps))] * 2
        out_shape = [jax.ShapeDtypeStruct((n_b, D_MODEL, rows.rows_per_sample), F32)] * 2
    else:
        assert rows.steps == 1
        out_specs = ([pl.BlockSpec((rows.tile, DIL_HEADS, DIL_HD), lambda i: (i, 0, 0))] * 2
                     + [rows.row_spec(D_MODEL)] * 2 + [pl.BlockSpec(w_kv.shape, lambda i: (0, 0))])
        out_shape = ([jax.ShapeDtypeStruct((rows.n_rows, DIL_HEADS, DIL_HD), F32)] * 2
                     + [jax.ShapeDtypeStruct((rows.n_rows, D_MODEL), F32)] * 2
                     + [jax.ShapeDtypeStruct(w_kv.shape, BF16)])
    if by_residue:
        out_specs += [_residue_spec(rows, dil, 2 * D_MODEL) for dil in DIL_DILATIONS]
        out_shape += _residue_shapes(rows, 2 * D_MODEL, BF16)
        scratch = _permute_scratch(2 * D_MODEL)
    return pl.pallas_call(
        functools.partial(_kv_kernel, by_residue),
        grid=(rows.steps,),
        in_specs=[
            rows.row_spec(D_MODEL),
            _kv_mod_spec(rows, 0), _kv_mod_spec(rows, 1),
            _rope_spec(rows), _rope_spec(rows),
            _const_spec((D_MODEL, 2 * D_MODEL), (0, 0)),
        ],
        out_specs=out_specs,
        out_shape=out_shape,
        scratch_shapes=scratch,
        compiler_params=_params("arbitrary"),
        name="kv_proj",
    )(x, rows.kv_mods, rows.kv_mods, cos_t, sin_t, w_kv)


def _q_kernel(by_residue, x_ref, sh_ref, sc_ref, cos_ref, sin_ref, w_ref, *rest):
    h = (x_ref[...] * (1.0 + sc_ref[...]) + sh_ref[...]).astype(BF16)
    scale = DIL_SCALE * LOG2_E if by_residue else DIL_SCALE
    if by_residue:
        w = w_ref[...]
    else:
        w = _pairs_to_heads(w_ref[...]).astype(BF16)
        rest[1][...] = w
    q = _rope(_dot(h, w), cos_ref[...] * scale, sin_ref[...] * scale)
    if by_residue:
        for g in range(N_GROUPS):
            _store_by_residue(q[:, g * D_MODEL:(g + 1) * D_MODEL], rest[g:g + 1], *rest[-2:])
    else:
        rest[0][...] = _pairs_to_heads(q)


def _q_call(rows, x, layer, cos_t, sin_t, w_q):
    width = N_GROUPS * D_MODEL
    by_residue = not rows.per_row
    if by_residue:
        out_specs = [_residue_spec(rows, dil, D_MODEL) for dil in DIL_DILATIONS]
        out_shape = _residue_shapes(rows, D_MODEL, BF16)
        scratch = _permute_scratch()
    else:
        assert rows.steps == 1
        out_specs = [rows.row_spec(width), pl.BlockSpec(w_q.shape, lambda i: (0, 0))]
        out_shape = [jax.ShapeDtypeStruct((rows.n_rows, width), F32), jax.ShapeDtypeStruct(w_q.shape, BF16)]
        scratch = []
    return pl.pallas_call(
        functools.partial(_q_kernel, by_residue),
        grid=(rows.steps,),
        in_specs=[
            rows.row_spec(D_MODEL),
            rows.mod_spec(layer, 3), rows.mod_spec(layer, 4),
            _rope_spec(rows), _rope_spec(rows),
            _const_spec((D_MODEL, width), (0, 0)),
        ],
        out_specs=out_specs,
        out_shape=out_shape,
        scratch_shapes=scratch,
        compiler_params=_params("arbitrary"),
        name="q_proj",
    )(x, rows.mods, rows.mods, cos_t, sin_t, w_q)


def _pair_softmax_pv(scores, v_pair, even_lane):
    half = scores.shape[0] // 2
    m = jnp.max(scores, axis=-1, keepdims=True)
    e = jnp.exp2(scores - m)
    den = jnp.sum(e, axis=-1, keepdims=True)
    o2 = _dot(e.astype(BF16), v_pair)
    o = jnp.where(even_lane, o2[:half], o2[half:]) / jnp.where(even_lane, den[:half], den[half:])
    lse2 = (m + jnp.log2(den)) * LN_2
    return o, lse2[:half], lse2[half:]


def _dil_prompt_kernel(has_prev, dil, q_ref, *refs):
    if has_prev:
        kvp_ref, kvc_ref, o_ref, lse_ref = refs
    else:
        kvc_ref, o_ref, lse_ref = refs
    blk = pl.program_id(2)
    first_token = blk * (ATT_BLOCK * dil) + pl.program_id(1)
    if dil == 1:
        token_rows = pl.ds(pl.multiple_of(first_token, ATT_BLOCK), ATT_BLOCK)
    else:
        token_rows = pl.ds(first_token, ATT_BLOCK, stride=dil)
    n_keys = (2 if has_prev else 1) * ATT_BLOCK
    lane = lax.broadcasted_iota(jnp.int32, (ATT_BLOCK, LANES), 1)
    even_lane = lane < DIL_HD
    even_head_qk = (lane % DIL_HD) < DIL_HD // 2
    qi = lax.broadcasted_iota(jnp.int32, (2 * ATT_BLOCK, n_keys), 0) % ATT_BLOCK
    kj = lax.broadcasted_iota(jnp.int32, (2 * ATT_BLOCK, n_keys), 1)
    if has_prev:
        dist = qi + ATT_BLOCK - kj
        valid = (dist >= 0) & (dist <= DIL_STEPS) & ((kj >= ATT_BLOCK) | (blk > 0))
    else:
        valid = kj <= qi
    lse_all = jnp.zeros((ATT_BLOCK, LANES), F32)
    for p in range(DIL_HEADS // 2):
        sl = slice(p * LANES, (p + 1) * LANES)
        vsl = slice(D_MODEL + p * LANES, D_MODEL + (p + 1) * LANES)
        qp = q_ref[:, sl]
        zero = jnp.zeros_like(qp)
        q2 = jnp.concatenate([jnp.where(even_head_qk, qp, zero), jnp.where(even_head_qk, zero, qp)], axis=0)
        if has_prev:
            k_pair = jnp.concatenate([kvp_ref[:, sl], kvc_ref[:, sl]], axis=0)
            v_pair = jnp.concatenate([kvp_ref[:, vsl], kvc_ref[:, vsl]], axis=0)
        else:
            k_pair, v_pair = kvc_ref[:, sl], kvc_ref[:, vsl]
        scores = jnp.where(valid, _dot_nt(q2, k_pair), -jnp.inf)
        o, lse_even, lse_odd = _pair_softmax_pv(scores, v_pair, even_lane)
        o_ref[p, token_rows, :] = o
        lse_all = jnp.where(lane == 2 * p, lse_even, jnp.where(lane == 2 * p + 1, lse_odd, lse_all))
    lse_ref[token_rows, :] = lse_all


def _dil_prompt_call(q, kvb):
    n_b, dil, length, _ = q.shape
    n_t = dil * length
    n_blk = length // ATT_BLOCK
    has_prev = n_blk > 1
    cur = pl.BlockSpec((None, None, ATT_BLOCK, D_MODEL), lambda b, r, i: (b, r, i, 0))
    kv_cur = pl.BlockSpec((None, None, ATT_BLOCK, 2 * D_MODEL), lambda b, r, i: (b, r, i, 0))
    kv_prev = pl.BlockSpec((None, None, ATT_BLOCK, 2 * D_MODEL), lambda b, r, i: (b, r, jnp.maximum(i - 1, 0), 0))
    if has_prev:
        in_specs, operands = [cur, kv_prev, kv_cur], (q, kvb, kvb)
    else:
        in_specs, operands = [cur, kv_cur], (q, kvb)
    return pl.pallas_call(
        functools.partial(_dil_prompt_kernel, has_prev, dil),
        grid=(n_b, dil, n_blk),
        in_specs=in_specs,
        out_specs=[pl.BlockSpec((None, D_MODEL // LANES, n_t, LANES), lambda b, r, i: (b, 0, 0, 0)),
                   pl.BlockSpec((None, n_t, LANES), lambda b, r, i: (b, 0, 0))],
        out_shape=[jax.ShapeDtypeStruct((n_b, D_MODEL // LANES, n_t, LANES), F32),
                   jax.ShapeDtypeStruct((n_b, n_t, LANES), F32)],
        compiler_params=_params("arbitrary", "arbitrary", "arbitrary"),
        name="dilated_prompt",
    )(*operands)


SAMPLE_HEAD_CHUNK = 4


def _sample_group_dense(t, q_col, k_win, v_win, k_new, v_new):
    n_t = k_new.shape[-1]
    pos = lax.broadcasted_iota(jnp.int32, (1, 1, k_win.shape[-1]), 2)
    pos_new = lax.broadcasted_iota(jnp.int32, (1, 1, n_t), 2)
    s_c = jnp.where(pos >= t, jnp.sum(k_win * q_col, axis=1, keepdims=True), -jnp.inf)
    s_n = jnp.where(pos_new <= t, jnp.sum(k_new * q_col, axis=1, keepdims=True), -jnp.inf)
    m = jnp.maximum(jnp.max(s_c, axis=2, keepdims=True), jnp.max(s_n, axis=2, keepdims=True))
    e_c, e_n = jnp.exp(s_c - m), jnp.exp(s_n - m)
    den = jnp.sum(e_c, axis=2, keepdims=True) + jnp.sum(e_n, axis=2, keepdims=True)
    o = (jnp.sum(v_win * e_c, axis=2, keepdims=True) + jnp.sum(v_new * e_n, axis=2, keepdims=True)) / den
    return o, m + jnp.log(den)


def _sample_group_dilated(dil, q_g, k_win, v_win, k_new, v_new):
    n_t = q_g.shape[-1]
    width = k_win.shape[-1]
    n_col = width // LANES
    res = lax.broadcasted_iota(jnp.int32, (1, 1, LANES), 2) % dil

    def columns(x):
        return [x[:, :, c * LANES:(c + 1) * LANES] for c in range(n_col)]

    q_pat = jnp.zeros(q_g.shape[:2] + (LANES,), F32)
    for t in range(n_t):
        q_pat = jnp.where(res == t, q_g[:, :, t:t + 1], q_pat)
    s_row = jnp.sum(k_win * jnp.concatenate([q_pat] * n_col, axis=2), axis=1, keepdims=True)
    s_new = jnp.sum(k_new * q_g, axis=1, keepdims=True)
    s_max = functools.reduce(jnp.maximum, columns(s_row))
    m_t = [jnp.maximum(jnp.max(jnp.where(res == t, s_max, -jnp.inf), axis=2, keepdims=True), s_new[:, :, t:t + 1])
           for t in range(n_t)]
    m_pat = jnp.full(s_max.shape, jnp.inf, F32)
    for t in range(n_t):
        m_pat = jnp.where(res == t, m_t[t], m_pat)
    e = jnp.exp(s_row - jnp.concatenate([m_pat] * n_col, axis=2))
    acc_fold = sum(columns(v_win * e))
    e_fold = sum(columns(e))
    out = []
    for t in range(n_t):
        e_new = jnp.exp(s_new[:, :, t:t + 1] - m_t[t])
        den = jnp.sum(jnp.where(res == t, e_fold, 0.0), axis=2, keepdims=True) + e_new
        o = (jnp.sum(jnp.where(res == t, acc_fold, 0.0), axis=2, keepdims=True) + e_new * v_new[:, :, t:t + 1]) / den
        out.append((o, m_t[t] + jnp.log(den)))
    return out


def _sample_attention(sample, new_ref, k_ref, v_ref, o_ref):
    n_t = o_ref.shape[1]
    n_s = k_ref.shape[-1]
    heads = k_ref.shape[0]
    new = jnp.transpose(new_ref[sample]).reshape(N_GROUPS + 2, heads, DIL_HD, n_t)
    pos_new = lax.broadcasted_iota(jnp.int32, (1, 1, n_t), 2)
    k_new, v_new = new[N_GROUPS], new[N_GROUPS + 1]
    groups = []
    for g, dil in enumerate(DIL_DILATIONS):
        lo = n_s - DIL_STEPS * dil
        k_win, v_win = k_ref[:, :, lo:], v_ref[:, :, lo:]
        q_g = new[g]
        if dil == 1:
            groups.append([_sample_group_dense(t, q_g[:, :, t:t + 1], k_win, v_win, k_new, v_new)
                           for t in range(n_t)])
        else:
            groups.append(_sample_group_dilated(dil, q_g, k_win, v_win, k_new, v_new))
    merged = jnp.zeros(k_new.shape, F32)
    for t in range(n_t):
        parts = [grp[t] for grp in groups]
        m = functools.reduce(jnp.maximum, [lse for _, lse in parts])
        wts = [jnp.exp(lse - m) for _, lse in parts]
        o = sum(wt * o for wt, (o, _) in zip(wts, parts)) / sum(wts)
        merged = jnp.where(pos_new == t, o, merged)
    o_ref[sample] = jnp.transpose(merged.reshape(heads * DIL_HD, n_t))


class _SampleAttentionGuest:
    def __init__(self, q, cache_k, cache_v, k_new, v_new):
        self.n_b, self.n_t = q.shape[:2]
        chunks = DIL_HEADS // SAMPLE_HEAD_CHUNK
        parts = [a.reshape(self.n_b, self.n_t, -1, chunks, SAMPLE_HEAD_CHUNK * DIL_HD) for a in (q, k_new, v_new)]
        new = jnp.concatenate(parts, axis=2).transpose(0, 1, 3, 2, 4).reshape(self.n_b, self.n_t, -1)
        self.operands = (new, cache_k, cache_v)
        wb = cache_k.shape[-1]
        assert wb == DIL_STEPS * DIL_DILATIONS[-1] and DIL_DILATIONS[0] == 1 and self.n_t <= DIL_DILATIONS[1]
        assert all(wb % dil == 0 and LANES % dil == 0 for dil in DIL_DILATIONS)
        self.wb = wb
        self.outputs = []

    def specs(self, chunk):
        hc = SAMPLE_HEAD_CHUNK
        width = hc * DIL_HD
        new = pl.BlockSpec((self.n_b, self.n_t, (N_GROUPS + 2) * width), lambda i: (0, 0, chunk))
        cache = pl.BlockSpec((None, hc, DIL_HD, self.wb), lambda i: (i, chunk, 0, 0))
        out = pl.BlockSpec((self.n_b, self.n_t, width), lambda i: (0, 0, 0))
        return [new, cache, cache], out, jax.ShapeDtypeStruct((self.n_b, self.n_t, width), F32)

    def result(self):
        assert len(self.outputs) * SAMPLE_HEAD_CHUNK == DIL_HEADS
        return jnp.concatenate(self.outputs, axis=-1).reshape(self.n_b * self.n_t, D_MODEL)


def _att_merge(o_refs, lse_refs):
    lses = [l_ref[...] for l_ref in lse_refs]
    m = functools.reduce(jnp.maximum, lses)
    e = [jnp.exp(l - m) for l in lses]
    den = sum(e)
    head = lax.broadcasted_iota(jnp.int32, (LANES, D_MODEL), 0)
    col_head = lax.broadcasted_iota(jnp.int32, (LANES, D_MODEL), 1) // DIL_HD
    spread = (head == col_head).astype(BF16)
    merged = None
    for o_ref, eg in zip(o_refs, e):
        wt = eg / den
        wt_hi = wt.astype(BF16)
        wt_lo = (wt - wt_hi.astype(F32)).astype(BF16)
        wide = _dot(wt_hi, spread) + _dot(wt_lo, spread)
        part = wide * jnp.concatenate([o_ref[c] for c in range(o_ref.shape[0])], axis=1)
        merged = part if merged is None else merged + part
    return merged


def _att_out_kernel(merged_input, *refs):
    if merged_input:
        o_ref, x_ref, gt_ref, wout_ref, g_ref, b_ref, y_ref = refs
        o = o_ref[...]
    else:
        o_refs, lse_refs = refs[0:N_GROUPS], refs[N_GROUPS:2 * N_GROUPS]
        x_ref, gt_ref, wout_ref, g_ref, b_ref, y_ref = refs[2 * N_GROUPS:]
        o = _att_merge(o_refs, lse_refs)
    y = _dot(o.astype(BF16), wout_ref[...])
    z = ALPHA * x_ref[...] + (1.0 + gt_ref[...]) * y
    y_ref[...] = _layer_norm(z, g_ref[...], b_ref[...])


def _att_out_call(rows, att, x, layer, w_out, ln_g, ln_b):
    merged_input = rows.per_row
    if merged_input:
        att_specs = [rows.row_spec(D_MODEL)]
        operands = [att]
    else:
        outs, lses = att
        tps = rows.tiles_per_sample
        o_spec = pl.BlockSpec((None, D_MODEL // LANES, rows.tile, LANES), lambda i: (i // tps, 0, i % tps, 0))
        lse_spec = pl.BlockSpec((None, rows.tile, LANES), lambda i: (i // tps, i % tps, 0))
        att_specs = [o_spec] * N_GROUPS + [lse_spec] * N_GROUPS
        operands = list(outs) + list(lses)
    return pl.pallas_call(
        functools.partial(_att_out_kernel, merged_input),
        grid=(rows.steps,),
        in_specs=att_specs + [
            rows.row_spec(D_MODEL),
            rows.mod_spec(layer, 5),
            _const_spec((D_MODEL, D_MODEL), (0, 0)),
        ] + _ln_specs(layer, 1),
        out_specs=rows.row_spec(D_MODEL),
        out_shape=jax.ShapeDtypeStruct((rows.n_rows, D_MODEL), F32),
        compiler_params=_params("arbitrary"),
        name="att_out",
    )(*operands, x, rows.mods, w_out, ln_g, ln_b)


def _rope_tables(positions):
    half = DIL_HD // 2
    inv = ROPE_THETA ** (-jnp.arange(half, dtype=F32) / half)
    ang = positions.astype(F32)[:, None] * inv[None, :]
    cos, sin = jnp.cos(ang), jnp.sin(ang)
    return jnp.concatenate([cos, cos, cos, cos], axis=1), jnp.concatenate([-sin, -sin, sin, sin], axis=1)


def _trunk(rows, ffn, x, n_b, n_t, positions, gla_state, w):
    cos_t, sin_t = _rope_tables(positions)
    if rows.per_row:
        cos_t, sin_t = jnp.tile(cos_t, (n_b, 1)), jnp.tile(sin_t, (n_b, 1))

    x = ffn(x, 0, 1)
    q, k, v, la, r = _gla_proj_call(rows.wide(), x, 0, w['gla_qkv'], w['gla_g'], w['gla_r'], w['gla_g2'], w['gla_bg'])
    recurrence = _gla_step_call if rows.per_row else _gla_rec_call
    o, s_new = recurrence(q.reshape(n_b, n_t, GLA_QK), k.reshape(n_b, n_t, GLA_QK), la.reshape(n_b, n_t, GLA_QK),
                          v.reshape(n_b, n_t, GLA_V), gla_state)
    x = _gla_out_call(rows.wide(), o.reshape(n_b * n_t, GLA_V), r, x, 0, w['gla_gon'], w['gla_out'], w['ln_g'], w['ln_b'])
    x = ffn(x, 0, 2)

    k_rows, v_rows, *kv_groups = _kv_call(rows, x, cos_t, sin_t, w['kv'])
    if rows.per_row:
        w['kv'] = kv_groups.pop()
    if rows.per_row:
        k_rows, v_rows = (a.reshape(n_b, n_t, DIL_HEADS, DIL_HD) for a in (k_rows, v_rows))
    else:
        k_rows, v_rows = (a.reshape(n_b, DIL_HEADS, DIL_HD, n_t).transpose(0, 3, 1, 2) for a in (k_rows, v_rows))

    x = ffn(x, 1, 1)
    qd = _q_call(rows, x, 1, cos_t, sin_t, w['q'])
    if rows.per_row:
        qd, w['q'] = qd
    att = yield qd, k_rows, v_rows, kv_groups
    x = _att_out_call(rows, att, x, 1, w['att_out'], w['ln_g'], w['ln_b'])
    x = ffn(x, 1, 2)
    return x, s_new, k_rows, v_rows


def _finish(trunk, att):
    try:
        trunk.send(att)
    except StopIteration as done:
        return done.value
    raise AssertionError("trunk yielded twice")


def kernel(x_prompt, x_sample, state_gla, cache_k, cache_v, c_prompt, c_sample, w_ada, b_ada, ln_g, ln_b, w_ffn1_up, w_ffn1_down, w_ffn2_up, w_ffn2_down, w_in_a, w_gate2_a, b_gate_a, g_onorm_a, w_out_a, w_ada_kv, b_ada_kv, w_kv, w_q_b, w_out_b):
    n_bp, n_tp = x_prompt.shape[:2]
    n_bs, n_ts = x_sample.shape[:2]
    assert N_A_LAYERS == 1 and DEPTH == 2
    assert n_tp % (ATT_BLOCK * DIL_DILATIONS[-1]) == 0 and n_tp % GLA_TILE == 0 and n_tp <= MAX_WINDOW
    assert n_ts < SUBLANES

    n_rs = n_bs * n_ts
    c_all = jnp.concatenate([jnp.repeat(c_sample, n_ts, axis=0), c_prompt], axis=0)
    mods = _ada_call(c_all, w_ada, b_ada)
    kv_mods = _ada_call(c_all, w_ada_kv[None], b_ada_kv[None])
    rows_p = _Rows(n_bp * n_tp, n_tp, mods[:, n_rs:, None, :], kv_mods[:, n_rs:, None, :])
    rows_s = _Rows(n_rs, n_ts, mods, kv_mods)

    w_in = w_in_a[0]
    gate_lo = 2 * GLA_QK + GLA_V
    w = {
        'ln_g': ln_g[:, :, None, :], 'ln_b': ln_b[:, :, None, :],
        'gla_qkv': w_in[:, :gate_lo].astype(BF16),
        'gla_g': jnp.pad(w_in[:, gate_lo:gate_lo + GLA_RANK], ((0, 0), (0, LANES - GLA_RANK))).astype(BF16),
        'gla_r': w_in[:, gate_lo + GLA_RANK:].astype(BF16),
        'gla_g2': jnp.pad(w_gate2_a[0], ((0, LANES - GLA_RANK), (0, 0))).astype(BF16),
        'gla_bg': b_gate_a[0][None, :],
        'gla_gon': g_onorm_a[0][None, :],
        'gla_out': w_out_a[0].astype(BF16),
        'kv': w_kv, 'q': w_q_b[0],
        'att_out': w_out_b[0].astype(BF16),
    }

    ffn_f32 = {1: (w_ffn1_up, w_ffn1_down), 2: (w_ffn2_up, w_ffn2_down)}
    ffn_args = {1: (0, 0), 2: (6, 2)}
    ffn_bf16 = {}

    def ffn_sample(x, layer, which):
        if (layer, which) in ffn_bf16:
            return _ffn_call(rows_s, x, layer, *ffn_args[which], *ffn_bf16[layer, which], w['ln_g'], w['ln_b'])
        ahead = (layer, which) == (DEPTH - 1, 1)
        y, *casts = _ffn_cast_call(rows_s, x, layer, *ffn_args[which], *ffn_f32[which], w['ln_g'], w['ln_b'],
                                   cast_only=(ffn_f32[2],) if ahead else ())
        ffn_bf16[layer, which] = casts[:3]
        if ahead:
            ffn_bf16[layer, 2] = casts[3:]
        return y

    def ffn_prompt(x, layer, which):
        return _ffn_call(rows_p, x, layer, *ffn_args[which], *ffn_bf16[layer, which], w['ln_g'], w['ln_b'],
                         guest=guest)

    trunk_s = _trunk(rows_s, ffn_sample, x_sample.reshape(-1, D_MODEL), n_bs, n_ts,
                     PAST_LEN + jnp.arange(n_ts), state_gla[0], w)
    q_s, _, _, (k_flat, v_flat) = next(trunk_s)
    guest = _SampleAttentionGuest(
        q_s.reshape(n_bs, n_ts, -1), cache_k.transpose(0, 2, 3, 1), cache_v.transpose(0, 2, 3, 1),
        k_flat.reshape(n_bs, n_ts, D_MODEL), v_flat.reshape(n_bs, n_ts, D_MODEL))

    zero_state = jnp.zeros((n_bp, GLA_HEADS, GLA_DK, GLA_DV), F32)
    trunk_p = _trunk(rows_p, ffn_prompt, x_prompt.reshape(-1, D_MODEL),
                     n_bp, n_tp, jnp.arange(n_tp), zero_state, w)
    q_p, _, _, kv_p = next(trunk_p)
    att_p = tuple(zip(*[_dil_prompt_call(q_p[g], kv_p[g]) for g in range(N_GROUPS)]))
    y_p, s_p, k_p, v_p = _finish(trunk_p, att_p)
    y_s, s_s, k_s, v_s = _finish(trunk_s, guest.result())
    return (y_p.reshape(n_bp, n_tp, D_MODEL), y_s.reshape(n_bs, n_ts, D_MODEL),
            s_p[None], s_s[None], k_p, v_p, k_s, v_s)
```

```python
import functools

import jax
import jax.numpy as jnp
from jax import lax
from jax.experimental import pallas as pl
from jax.experimental.pallas import tpu as pltpu

F32 = jnp.float32
BF16 = jnp.bfloat16

D_MODEL = 1024
DEPTH = 2
PAST_LEN = 16384
N_A_LAYERS = DEPTH // 2
GLA_HEADS = 4
GLA_QK = D_MODEL // 2
GLA_V = D_MODEL
GLA_DK = GLA_QK // GLA_HEADS
GLA_DV = GLA_V // GLA_HEADS
GLA_RANK = 16
GLA_TAU = 16.0
GLA_NORM_EPS = 1e-6
DIL_HEADS = 16
DIL_HD = D_MODEL // DIL_HEADS
DIL_WINDOWS = (128, 512, 2048)
DIL_DILATIONS = (1, 4, 16)
N_GROUPS = len(DIL_WINDOWS)
DIL_STEPS = 128
MAX_WINDOW = max(DIL_WINDOWS)
DIL_SCALE = DIL_HD ** -0.5
ROPE_THETA = 10000.0
D_FF = 2816
FFN_RES = 0.5
ALPHA = (2 * DEPTH) ** 0.25
LN_EPS = 1e-5

LANES = 128
SUBLANES = 8
ADA_MAX_COLS = 1536
ROW_TILE = 512
WIDE_TILE = 1024
LOG2_E = 1.4426950408889634
LN_2 = 0.6931471805599453
FF_CHUNK = 256
GLA_CHUNK = 128
GLA_TILE = 512
GLA_FACTOR_BOUND = 80.0
ATT_BLOCK = DIL_STEPS
VMEM_LIMIT = 56 * 1024 * 1024


def _params(*sem):
    return pltpu.CompilerParams(dimension_semantics=sem, vmem_limit_bytes=VMEM_LIMIT)


def _silu(x):
    return x / (1.0 + jnp.exp(-x))


def _layer_norm(z, g, b):
    mu = jnp.mean(z, axis=-1, keepdims=True)
    zc = z - mu
    var = jnp.mean(zc * zc, axis=-1, keepdims=True)
    return zc * lax.rsqrt(var + LN_EPS) * g + b


def _dot(a, b):
    return jnp.dot(a, b, preferred_element_type=F32)


def _dot_nt(a, b):
    return lax.dot_general(a, b, (((1,), (1,)), ((), ())), preferred_element_type=F32)


def _by_pair_block(fn, x):
    return jnp.concatenate([fn(x[:, c * LANES:(c + 1) * LANES]) for c in range(x.shape[1] // LANES)], axis=1)


def _rope(x, cos_t, sin_t):
    return _by_pair_block(lambda blk: blk * cos_t + pltpu.roll(blk, LANES // 2, 1) * sin_t, x)


def _pairs_to_heads(x):
    quarter = lax.broadcasted_iota(jnp.int32, (x.shape[0], LANES), 1) // (LANES // 4)

    def swap(blk):
        return jnp.where(quarter == 1, pltpu.roll(blk, 3 * LANES // 4, 1),
                         jnp.where(quarter == 2, pltpu.roll(blk, LANES // 4, 1), blk))

    return _by_pair_block(swap, x)


def _ada_kernel(c_ref, w_ref, b_ref, o_ref):
    s = _silu(c_ref[...]).astype(BF16)
    o_ref[...] = _dot(s, w_ref[...].astype(BF16)) + b_ref[...]


def _ada_call(c_all, w, b):
    n_l, _, n = w.shape
    rows = c_all.shape[0]
    tn = max(t for t in range(LANES, ADA_MAX_COLS + 1, LANES) if n % t == 0)
    return pl.pallas_call(
        _ada_kernel,
        grid=(n_l, n // tn),
        in_specs=[
            pl.BlockSpec((rows, D_MODEL), lambda l, j: (0, 0)),
            pl.BlockSpec((None, D_MODEL, tn), lambda l, j: (l, 0, j)),
            pl.BlockSpec((None, 1, tn), lambda l, j: (l, 0, j)),
        ],
        out_specs=pl.BlockSpec((None, rows, tn), lambda l, j: (l, 0, j)),
        out_shape=jax.ShapeDtypeStruct((n_l, rows, n), F32),
        compiler_params=_params("arbitrary", "arbitrary"),
        name="ada_mod",
    )(c_all, w, b.reshape(n_l, 1, n))


class _Rows:
    def __init__(self, n_rows, rows_per_sample, mods, kv_mods, tile=ROW_TILE):
        self.n_rows = n_rows
        self.rows_per_sample = rows_per_sample
        self.per_row = rows_per_sample < SUBLANES
        self.tile = n_rows if self.per_row else min(tile, rows_per_sample)
        self.steps = n_rows // self.tile
        self.tiles_per_sample = 1 if self.per_row else rows_per_sample // self.tile
        self.mods = mods
        self.kv_mods = kv_mods

    def wide(self):
        return _Rows(self.n_rows, self.rows_per_sample, self.mods, self.kv_mods, WIDE_TILE)

    def row_spec(self, width):
        return pl.BlockSpec((self.tile, width), lambda i: (i, 0))

    def mod_spec(self, layer, k):
        if self.per_row:
            return pl.BlockSpec((None, self.tile, D_MODEL), lambda i: (layer, 0, k))
        tps = self.tiles_per_sample
        return pl.BlockSpec((None, None, 1, D_MODEL), lambda i: (layer, i // tps, 0, k))


def _const_spec(shape, index):
    return pl.BlockSpec(shape, lambda i: index, pipeline_mode=pl.Buffered(1))


def _ln_specs(layer, j):
    return [_const_spec((None, None, 1, D_MODEL), (layer, j, 0, 0))] * 2


def _ffn_kernel(has_guest, x_ref, sh_ref, sc_ref, gt_ref, wa_ref, wu_ref, wdn_ref, g_ref, b_ref, *rest):
    if has_guest:
        *guest_in, o_ref, guest_out = rest
        _sample_attention(pl.program_id(0), *guest_in, guest_out)
    else:
        o_ref, = rest
    x = x_ref[...]
    h = (x * (1.0 + sc_ref[...]) + sh_ref[...]).astype(BF16)
    y = None
    for lo in range(0, D_FF, FF_CHUNK):
        a = _dot(h, wa_ref[:, lo:lo + FF_CHUNK])
        u = _dot(h, wu_ref[:, lo:lo + FF_CHUNK])
        part = _dot((_silu(a) * u).astype(BF16), wdn_ref[lo:lo + FF_CHUNK, :])
        y = part if y is None else y + part
    z = ALPHA * x + (1.0 + gt_ref[...]) * (FFN_RES * y)
    o_ref[...] = _layer_norm(z, g_ref[...], b_ref[...])


def _ffn_cast_kernel(n_cast_only, x_ref, sh_ref, sc_ref, gt_ref, wa_ref, wu_ref, wdn_ref, *rest):
    extra_in, (g_ref, b_ref, o_ref, wa_out, wu_out, wdn_out), rest = rest[:n_cast_only], rest[n_cast_only:n_cast_only + 6], rest[n_cast_only + 6:]
    extra_out, y_scr = rest[:n_cast_only], rest[n_cast_only]
    for src, dst in zip(extra_in, extra_out):
        dst[...] = src[...].astype(BF16)
    c = pl.program_id(0)
    wa, wu, wdn = wa_ref[...].astype(BF16), wu_ref[...].astype(BF16), wdn_ref[...].astype(BF16)
    wa_out[...], wu_out[...], wdn_out[...] = wa, wu, wdn
    x = x_ref[...]
    h = (x * (1.0 + sc_ref[...]) + sh_ref[...]).astype(BF16)
    part = _dot((_silu(_dot(h, wa)) * _dot(h, wu)).astype(BF16), wdn)

    @pl.when(c == 0)
    def _():
        y_scr[...] = part

    @pl.when(c > 0)
    def _():
        y_scr[...] += part

    @pl.when(c == pl.num_programs(0) - 1)
    def _():
        z = ALPHA * x + (1.0 + gt_ref[...]) * (FFN_RES * y_scr[...])
        o_ref[...] = _layer_norm(z, g_ref[...], b_ref[...])


def _ffn_cast_call(rows, x, layer, mod0, ln_j, w_up, w_down, ln_g, ln_b, cast_only=()):
    assert rows.steps == 1
    n_chunks = D_FF // FF_CHUNK

    def whole(spec):
        return pl.BlockSpec(spec.block_shape, lambda c, index_map=spec.index_map: index_map(0))

    f32_specs = [pl.BlockSpec((None, D_MODEL, FF_CHUNK), lambda c: (layer, 0, c)),
                 pl.BlockSpec((None, D_MODEL, FF_CHUNK), lambda c: (layer, 0, n_chunks + c)),
                 pl.BlockSpec((None, FF_CHUNK, D_MODEL), lambda c: (layer, c, 0))]
    bf16_specs = [pl.BlockSpec((D_MODEL, FF_CHUNK), lambda c: (0, c)),
                  pl.BlockSpec((D_MODEL, FF_CHUNK), lambda c: (0, c)),
                  pl.BlockSpec((FF_CHUNK, D_MODEL), lambda c: (c, 0))]
    bf16_shapes = [jax.ShapeDtypeStruct((D_MODEL, D_FF), BF16), jax.ShapeDtypeStruct((D_MODEL, D_FF), BF16),
                   jax.ShapeDtypeStruct((D_FF, D_MODEL), BF16)]
    extra = [w for up, down in cast_only for w in (up, up, down)]
    return pl.pallas_call(
        functools.partial(_ffn_cast_kernel, len(extra)),
        grid=(n_chunks,),
        in_specs=[whole(s) for s in (rows.row_spec(D_MODEL), rows.mod_spec(layer, mod0),
                                     rows.mod_spec(layer, mod0 + 1), rows.mod_spec(layer, mod0 + 2))]
        + f32_specs + f32_specs * len(cast_only) + [whole(s) for s in _ln_specs(layer, ln_j)],
        out_specs=[whole(rows.row_spec(D_MODEL))] + bf16_specs * (1 + len(cast_only)),
        out_shape=[jax.ShapeDtypeStruct((rows.n_rows, D_MODEL), F32)] + bf16_shapes * (1 + len(cast_only)),
        scratch_shapes=[pltpu.VMEM((rows.n_rows, D_MODEL), F32)],
        compiler_params=_params("arbitrary"),
        name="ffn_cast",
    )(x, rows.mods, rows.mods, rows.mods, w_up, w_up, w_down, *extra, ln_g, ln_b)


def _ffn_call(rows, x, layer, mod0, ln_j, w_gate, w_up, w_down, ln_g, ln_b, guest=None):
    in_specs = [
        rows.row_spec(D_MODEL),
        rows.mod_spec(layer, mod0), rows.mod_spec(layer, mod0 + 1), rows.mod_spec(layer, mod0 + 2),
        _const_spec((D_MODEL, D_FF), (0, 0)),
        _const_spec((D_MODEL, D_FF), (0, 0)),
        _const_spec((D_FF, D_MODEL), (0, 0)),
    ] + _ln_specs(layer, ln_j)
    operands = [x, rows.mods, rows.mods, rows.mods, w_gate, w_up, w_down, ln_g, ln_b]
    out_specs = [rows.row_spec(D_MODEL)]
    out_shape = [jax.ShapeDtypeStruct((rows.n_rows, D_MODEL), F32)]
    if guest is not None:
        assert guest.n_b == rows.steps
        guest_in, guest_out, guest_shape = guest.specs(len(guest.outputs))
        in_specs += guest_in
        operands += guest.operands
        out_specs.append(guest_out)
        out_shape.append(guest_shape)
    outs = pl.pallas_call(
        functools.partial(_ffn_kernel, guest is not None),
        grid=(rows.steps,),
        in_specs=in_specs,
        out_specs=out_specs,
        out_shape=out_shape,
        compiler_params=_params("arbitrary"),
        name="ffn",
    )(*operands)
    if guest is not None:
        guest.outputs.append(outs[1])
    return outs[0]


def _gla_proj_kernel(x_ref, sh_ref, sc_ref, wqkv_ref, wg_ref, wr_ref, wg2_ref, bg_ref,
                     q_ref, k_ref, v_ref, la_ref, r_ref):
    h = (x_ref[...] * (1.0 + sc_ref[...]) + sh_ref[...]).astype(BF16)
    qkv = _dot(h, wqkv_ref[...])
    q_ref[...] = qkv[:, :GLA_QK] * (GLA_DK ** -0.5)
    k_ref[...] = qkv[:, GLA_QK:2 * GLA_QK]
    v_ref[...] = qkv[:, 2 * GLA_QK:].astype(v_ref.dtype)
    g_lr = _dot(h, wg_ref[...]).astype(BF16)
    gate = _dot(g_lr, wg2_ref[...]) + bg_ref[...]
    log_sig = jnp.minimum(gate, 0.0) - jnp.log(1.0 + jnp.exp(-jnp.abs(gate)))
    la_ref[...] = log_sig / GLA_TAU
    r_ref[...] = _silu(_dot(h, wr_ref[...])).astype(r_ref.dtype)


def _gla_proj_call(rows, x, layer, w_qkv, w_g, w_r, w_g2, b_g):
    widths = (GLA_QK, GLA_QK, GLA_V, GLA_QK, GLA_V)
    narrow = F32 if rows.per_row else BF16
    dtypes = (F32, F32, narrow, F32, narrow)
    return pl.pallas_call(
        _gla_proj_kernel,
        grid=(rows.steps,),
        in_specs=[
            rows.row_spec(D_MODEL),
            rows.mod_spec(layer, 3), rows.mod_spec(layer, 4),
            _const_spec((D_MODEL, 2 * GLA_QK + GLA_V), (0, 0)),
            _const_spec((D_MODEL, LANES), (0, 0)),
            _const_spec((D_MODEL, GLA_V), (0, 0)),
            _const_spec((LANES, GLA_QK), (0, 0)),
            _const_spec((1, GLA_QK), (0, 0)),
        ],
        out_specs=[rows.row_spec(w) for w in widths],
        out_shape=[jax.ShapeDtypeStruct((rows.n_rows, w), dt) for w, dt in zip(widths, dtypes)],
        compiler_params=_params("arbitrary"),
        name="gla_proj",
    )(x, rows.mods, rows.mods, w_qkv, w_g, w_r, w_g2, b_g)


def _prefix_sum(tri, x):
    rest, total = x, None
    for _ in range(2):
        piece = rest.astype(BF16)
        rest = rest - piece.astype(F32)
        part = _dot(tri, piece)
        total = part if total is None else total + part
    return total


def _gla_scores_factored(qh, kh, bh, b_last):
    mid = 0.5 * b_last
    q_t = (qh * jnp.exp(bh - mid)).astype(BF16)
    k_t = (kh * jnp.exp(mid - bh)).astype(BF16)
    return _dot_nt(q_t, k_t)


def _gla_scores_elementwise(qh, kh, bh, k_scr, b_scr):
    k_scr[...] = kh
    b_scr[...] = bh
    col = lax.broadcasted_iota(jnp.int32, (GLA_CHUNK, GLA_CHUNK), 1)

    def body(j, acc):
        kj = k_scr[pl.ds(j, 1), :]
        bj = b_scr[pl.ds(j, 1), :]
        term = qh * kj * jnp.exp(jnp.minimum(bh - bj, 0.0))
        return acc + jnp.where(col == j, jnp.sum(term, axis=1, keepdims=True), 0.0)

    return lax.fori_loop(0, GLA_CHUNK, body, jnp.zeros((GLA_CHUNK, GLA_CHUNK), F32))


def _gla_chunk(rows, b, states, score_fn, q_ref, k_ref, v_ref, o_ref, causal):
    q, k = q_ref[rows, :], k_ref[rows, :]
    b_last = b[GLA_CHUNK - 1:GLA_CHUNK, :]
    q_in = q * jnp.exp(b)
    k_out = k * jnp.exp(b_last - b)
    new_states = []
    for hd, s_prev in enumerate(states):
        ks = slice(hd * GLA_DK, (hd + 1) * GLA_DK)
        vs = slice(hd * GLA_DV, (hd + 1) * GLA_DV)
        vh = v_ref[rows, vs].astype(BF16)
        o_inter = _dot(q_in[:, ks].astype(BF16), s_prev.astype(BF16))
        scores = jnp.where(causal, score_fn(q[:, ks], k[:, ks], b[:, ks], b_last[:, ks]), 0.0)
        o_ref[rows, vs] = (o_inter + _dot(scores.astype(BF16), vh)).astype(o_ref.dtype)
        decay_col = jnp.transpose(jnp.broadcast_to(jnp.exp(b_last[:, ks]), (GLA_DK, GLA_DK)))[:, :1]
        k_out_t = jnp.transpose(k_out[:, ks]).astype(BF16)
        new_states.append(decay_col * s_prev + _dot(k_out_t, vh))
    return new_states


def _gla_rec_kernel(q_ref, k_ref, v_ref, la_ref, s0_ref, o_ref, sfin_ref, s_scr, k_scr, b_scr):
    t = pl.program_id(1)

    @pl.when(t == 0)
    def _():
        s_scr[...] = s0_ref[...]

    row = lax.broadcasted_iota(jnp.int32, (GLA_CHUNK, GLA_CHUNK), 0)
    col = lax.broadcasted_iota(jnp.int32, (GLA_CHUNK, GLA_CHUNK), 1)
    causal = col <= row
    tri = causal.astype(BF16)
    n_chunks = GLA_TILE // GLA_CHUNK

    def decay_prefix(rows):
        return _prefix_sum(tri, la_ref[rows, :])

    chunk_rows = [slice(c * GLA_CHUNK, (c + 1) * GLA_CHUNK) for c in range(n_chunks)]
    prefixes = [decay_prefix(rows) for rows in chunk_rows]
    total_decay = functools.reduce(jnp.maximum, [-b[GLA_CHUNK - 1:GLA_CHUNK, :] for b in prefixes])
    factorable = jnp.max(total_decay) <= GLA_FACTOR_BOUND

    def refs():
        return q_ref, k_ref, v_ref, o_ref, causal

    @pl.when(factorable)
    def _():
        states = [s_scr[hd] for hd in range(GLA_HEADS)]
        for rows, b in zip(chunk_rows, prefixes):
            states = _gla_chunk(rows, b, states, _gla_scores_factored, *refs())
        for hd in range(GLA_HEADS):
            s_scr[hd] = states[hd]

    @pl.when(jnp.logical_not(factorable))
    def _():
        def chunk(c, carry):
            rows = pl.ds(pl.multiple_of(c * GLA_CHUNK, GLA_CHUNK), GLA_CHUNK)
            states = [s_scr[hd] for hd in range(GLA_HEADS)]
            elementwise = functools.partial(_gla_scores_elementwise, k_scr=k_scr, b_scr=b_scr)
            states = _gla_chunk(rows, decay_prefix(rows), states,
                                lambda qh, kh, bh, _: elementwise(qh, kh, bh), *refs())
            for hd in range(GLA_HEADS):
                s_scr[hd] = states[hd]
            return carry

        lax.fori_loop(0, n_chunks, chunk, 0)

    @pl.when(t == pl.num_programs(1) - 1)
    def _():
        sfin_ref[...] = s_scr[...]


def _gla_rec_call(q, k, la, v, s0):
    n_b, n_t = q.shape[:2]

    def seq(width):
        return pl.BlockSpec((None, GLA_TILE, width), lambda b, t: (b, t, 0))

    state = pl.BlockSpec((None, GLA_HEADS, GLA_DK, GLA_DV), lambda b, t: (b, 0, 0, 0))
    return pl.pallas_call(
        _gla_rec_kernel,
        grid=(n_b, n_t // GLA_TILE),
        in_specs=[seq(GLA_QK), seq(GLA_QK), seq(GLA_V), seq(GLA_QK), state],
        out_specs=[seq(GLA_V), state],
        out_shape=[jax.ShapeDtypeStruct((n_b, n_t, GLA_V), BF16),
                   jax.ShapeDtypeStruct((n_b, GLA_HEADS, GLA_DK, GLA_DV), F32)],
        scratch_shapes=[pltpu.VMEM((GLA_HEADS, GLA_DK, GLA_DV), F32),
                        pltpu.VMEM((GLA_CHUNK, GLA_DK), F32),
                        pltpu.VMEM((GLA_CHUNK, GLA_DK), F32)],
        compiler_params=_params("arbitrary", "arbitrary"),
        name="gla_recurrence",
    )(q, k, v, la, s0)


def _gla_step_kernel(q_ref, k_ref, la_ref, v_ref, s0_ref, o_ref, sfin_ref):
    n_t = v_ref.shape[0]
    q, k = q_ref[...], k_ref[...]
    row = lax.broadcasted_iota(jnp.int32, (n_t, n_t), 0)
    col = lax.broadcasted_iota(jnp.int32, (n_t, n_t), 1)
    b = _prefix_sum((col <= row).astype(BF16), la_ref[...])
    b_last = b[n_t - 1:n_t, :]
    q_in = q * jnp.exp(b)
    k_out = k * jnp.exp(b_last - b)
    for hd in range(GLA_HEADS):
        ks = slice(hd * GLA_DK, (hd + 1) * GLA_DK)
        vs = slice(hd * GLA_DV, (hd + 1) * GLA_DV)
        vh = v_ref[:, vs]
        s_prev = s0_ref[hd]
        o = _dot(q_in[:, ks].astype(BF16), s_prev.astype(BF16))
        for u in range(n_t):
            decay = jnp.exp(jnp.minimum(b[:, ks] - b[u:u + 1, ks], 0.0))
            score = jnp.sum(q[:, ks] * k[u:u + 1, ks] * decay, axis=1, keepdims=True)
            o = o + jnp.where(row[:, :1] >= u, score, 0.0) * vh[u:u + 1, :]
        o_ref[:, vs] = o
        decay_col = jnp.transpose(jnp.broadcast_to(jnp.exp(b_last[:, ks]), (SUBLANES, GLA_DK)))[:, :1]
        sfin_ref[hd] = decay_col * s_prev + lax.dot_general(
            k_out[:, ks].astype(BF16), vh.astype(BF16), (((0,), (0,)), ((), ())), preferred_element_type=F32)


def _gla_step_call(q, k, la, v, s0):
    n_b, n_t = q.shape[:2]

    def tok(width):
        return pl.BlockSpec((None, n_t, width), lambda b: (b, 0, 0))

    state = pl.BlockSpec((None, GLA_HEADS, GLA_DK, GLA_DV), lambda b: (b, 0, 0, 0))
    return pl.pallas_call(
        _gla_step_kernel,
        grid=(n_b,),
        in_specs=[tok(GLA_QK), tok(GLA_QK), tok(GLA_QK), tok(GLA_V), state],
        out_specs=[tok(GLA_V), state],
        out_shape=[jax.ShapeDtypeStruct((n_b, n_t, GLA_V), F32),
                   jax.ShapeDtypeStruct((n_b, GLA_HEADS, GLA_DK, GLA_DV), F32)],
        compiler_params=_params("arbitrary"),
        name="gla_step",
    )(q, k, la, v, s0)


def _gla_out_kernel(o_ref, r_ref, x_ref, gt_ref, gon_ref, wout_ref, g_ref, b_ref, y_ref):
    o = o_ref[...].astype(F32)
    parts = []
    for hd in range(GLA_HEADS):
        oh = o[:, hd * GLA_DV:(hd + 1) * GLA_DV]
        ms = jnp.mean(oh * oh, axis=-1, keepdims=True)
        parts.append(oh * lax.rsqrt(ms + GLA_NORM_EPS) * gon_ref[...])
    gated = (jnp.concatenate(parts, axis=1) * r_ref[...].astype(F32)).astype(BF16)
    y = _dot(gated, wout_ref[...])
    z = ALPHA * x_ref[...] + (1.0 + gt_ref[...]) * y
    y_ref[...] = _layer_norm(z, g_ref[...], b_ref[...])


def _gla_out_call(rows, o, r, x, layer, g_onorm, w_out, ln_g, ln_b):
    return pl.pallas_call(
        _gla_out_kernel,
        grid=(rows.steps,),
        in_specs=[
            rows.row_spec(GLA_V), rows.row_spec(GLA_V), rows.row_spec(D_MODEL),
            rows.mod_spec(layer, 5),
            _const_spec((1, GLA_DV), (0, 0)),
            _const_spec((GLA_V, D_MODEL), (0, 0)),
        ] + _ln_specs(layer, 1),
        out_specs=rows.row_spec(D_MODEL),
        out_shape=jax.ShapeDtypeStruct((rows.n_rows, D_MODEL), F32),
        compiler_params=_params("arbitrary"),
        name="gla_out",
    )(o, r, x, rows.mods, g_onorm, w_out, ln_g, ln_b)


def _gather_stride(scr, start, count, stride):
    return jnp.concatenate([scr[c, pl.ds(start, count, stride=stride), :] for c in range(scr.shape[0])], axis=1)


def _fill_slabs(scr, row0, x):
    for c in range(scr.shape[0]):
        scr[c, row0:row0 + x.shape[0], :] = x[:, c * LANES:(c + 1) * LANES]


def _store_by_residue(x, out_refs, scr_a, scr_b):
    n_rows = x.shape[0]
    s = DIL_DILATIONS[1]
    assert DIL_DILATIONS == (1, s, s * s)
    outs = {o.shape[0]: o for o in out_refs}
    if 1 in outs:
        outs[1][0] = x.astype(BF16)
    if s not in outs and s * s not in outs:
        return
    _fill_slabs(scr_a, 0, x)
    per_class = n_rows // s
    for r in range(s):
        xr = _gather_stride(scr_a, r, per_class, s)
        if s in outs:
            outs[s][r] = xr.astype(BF16)
        if s * s in outs:
            _fill_slabs(scr_b, r * per_class, xr)
    if s * s in outs:
        for r in range(s * s):
            xr = _gather_stride(scr_b, (r % s) * per_class + r // s, per_class // s, s)
            outs[s * s][r] = xr.astype(BF16)


def _kv_kernel(by_residue, x_ref, sh_ref, sc_ref, cos_ref, sin_ref, w_ref, k_ref, v_ref, *rest):
    h = (x_ref[...] * (1.0 + sc_ref[...]) + sh_ref[...]).astype(BF16)
    if by_residue:
        w = w_ref[...]
    else:
        w = jnp.concatenate([_pairs_to_heads(w_ref[:, :D_MODEL]), w_ref[:, D_MODEL:]], axis=1).astype(BF16)
        rest[2][...] = w
    kv = _dot(h, w)
    k = _rope(kv[:, :D_MODEL], cos_ref[...], sin_ref[...])
    v = kv[:, D_MODEL:]
    if by_residue:
        k_t = jnp.transpose(k)
        quarter = LANES // 4
        k_ref[...] = jnp.concatenate(
            [k_t[blk * LANES + j * quarter:blk * LANES + (j + 1) * quarter]
             for blk in range(D_MODEL // LANES) for j in (0, 2, 1, 3)], axis=0)
        v_ref[...] = jnp.transpose(v)
        _store_by_residue(jnp.concatenate([k, v], axis=1), rest[0:N_GROUPS], *rest[-2:])
    else:
        k = _pairs_to_heads(k)
        k_ref[...] = k.reshape(k_ref.shape)
        v_ref[...] = v.reshape(v_ref.shape)
        rest[0][...] = k
        rest[1][...] = v


def _rope_spec(rows):
    if rows.per_row:
        return pl.BlockSpec((rows.tile, LANES), lambda i: (0, 0))
    tps = rows.tiles_per_sample
    return pl.BlockSpec((rows.tile, LANES), lambda i: (i % tps, 0))


def _kv_mod_spec(rows, k):
    if rows.per_row:
        return pl.BlockSpec((None, rows.tile, D_MODEL), lambda i: (0, 0, k))
    tps = rows.tiles_per_sample
    return pl.BlockSpec((None, None, 1, D_MODEL), lambda i: (0, i // tps, 0, k))


def _residue_spec(rows, dil, width):
    tps = rows.tiles_per_sample
    return pl.BlockSpec((None, dil, rows.tile // dil, width), lambda i: (i // tps, 0, i % tps, 0))


def _residue_shapes(rows, width, dtype):
    n_b = rows.n_rows // rows.rows_per_sample
    return [jax.ShapeDtypeStruct((n_b, dil, rows.rows_per_sample // dil, width), dtype) for dil in DIL_DILATIONS]


def _permute_scratch(width=D_MODEL):
    return [pltpu.VMEM((width // LANES, ROW_TILE, LANES), F32)] * 2


def _kv_call(rows, x, cos_t, sin_t, w_kv):
    by_residue = not rows.per_row
    scratch = []
    if by_residue:
        tps = rows.tiles_per_sample
        n_b = rows.n_rows // rows.rows_per_sample
        out_specs = [pl.BlockSpec((None, D_MODEL, rows.tile), lambda i: (i // tps, 0, i % tps))] * 2
        out_shape = [jax.ShapeDtypeStruct((n_b, D_MODEL, rows.rows_per_sample), F32)] * 2
    else:
        assert rows.steps == 1
        out_specs = ([pl.BlockSpec((rows.tile, DIL_HEADS, DIL_HD), lambda i: (i, 0, 0))] * 2
                     + [rows.row_spec(D_MODEL)] * 2 + [pl.BlockSpec(w_kv.shape, lambda i: (0, 0))])
        out_shape = ([jax.ShapeDtypeStruct((rows.n_rows, DIL_HEADS, DIL_HD), F32)] * 2
                     + [jax.ShapeDtypeStruct((rows.n_rows, D_MODEL), F32)] * 2
                     + [jax.ShapeDtypeStruct(w_kv.shape, BF16)])
    if by_residue:
        out_specs += [_residue_spec(rows, dil, 2 * D_MODEL) for dil in DIL_DILATIONS]
        out_shape += _residue_shapes(rows, 2 * D_MODEL, BF16)
        scratch = _permute_scratch(2 * D_MODEL)
    return pl.pallas_call(
        functools.partial(_kv_kernel, by_residue),
        grid=(rows.steps,),
        in_specs=[
            rows.row_spec(D_MODEL),
            _kv_mod_spec(rows, 0), _kv_mod_spec(rows, 1),
            _rope_spec(rows), _rope_spec(rows),
            _const_spec((D_MODEL, 2 * D_MODEL), (0, 0)),
        ],
        out_specs=out_specs,
        out_shape=out_shape,
        scratch_shapes=scratch,
        compiler_params=_params("arbitrary"),
        name="kv_proj",
    )(x, rows.kv_mods, rows.kv_mods, cos_t, sin_t, w_kv)


def _q_kernel(by_residue, x_ref, sh_ref, sc_ref, cos_ref, sin_ref, w_ref, *rest):
    h = (x_ref[...] * (1.0 + sc_ref[...]) + sh_ref[...]).astype(BF16)
    scale = DIL_SCALE * LOG2_E if by_residue else DIL_SCALE
    if by_residue:
        w = w_ref[...]
    else:
        w = _pairs_to_heads(w_ref[...]).astype(BF16)
        rest[1][...] = w
    q = _rope(_dot(h, w), cos_ref[...] * scale, sin_ref[...] * scale)
    if by_residue:
        for g in range(N_GROUPS):
            _store_by_residue(q[:, g * D_MODEL:(g + 1) * D_MODEL], rest[g:g + 1], *rest[-2:])
    else:
        rest[0][...] = _pairs_to_heads(q)


def _q_call(rows, x, layer, cos_t, sin_t, w_q):
    width = N_GROUPS * D_MODEL
    by_residue = not rows.per_row
    if by_residue:
        out_specs = [_residue_spec(rows, dil, D_MODEL) for dil in DIL_DILATIONS]
        out_shape = _residue_shapes(rows, D_MODEL, BF16)
        scratch = _permute_scratch()
    else:
        assert rows.steps == 1
        out_specs = [rows.row_spec(width), pl.BlockSpec(w_q.shape, lambda i: (0, 0))]
        out_shape = [jax.ShapeDtypeStruct((rows.n_rows, width), F32), jax.ShapeDtypeStruct(w_q.shape, BF16)]
        scratch = []
    return pl.pallas_call(
        functools.partial(_q_kernel, by_residue),
        grid=(rows.steps,),
        in_specs=[
            rows.row_spec(D_MODEL),
            rows.mod_spec(layer, 3), rows.mod_spec(layer, 4),
            _rope_spec(rows), _rope_spec(rows),
            _const_spec((D_MODEL, width), (0, 0)),
        ],
        out_specs=out_specs,
        out_shape=out_shape,
        scratch_shapes=scratch,
        compiler_params=_params("arbitrary"),
        name="q_proj",
    )(x, rows.mods, rows.mods, cos_t, sin_t, w_q)


def _pair_softmax_pv(scores, v_pair, even_lane):
    half = scores.shape[0] // 2
    m = jnp.max(scores, axis=-1, keepdims=True)
    e = jnp.exp2(scores - m)
    den = jnp.sum(e, axis=-1, keepdims=True)
    o2 = _dot(e.astype(BF16), v_pair)
    o = jnp.where(even_lane, o2[:half], o2[half:]) / jnp.where(even_lane, den[:half], den[half:])
    lse2 = (m + jnp.log2(den)) * LN_2
    return o, lse2[:half], lse2[half:]


def _dil_prompt_kernel(has_prev, dil, q_ref, *refs):
    if has_prev:
        kvp_ref, kvc_ref, o_ref, lse_ref = refs
    else:
        kvc_ref, o_ref, lse_ref = refs
    blk = pl.program_id(2)
    first_token = blk * (ATT_BLOCK * dil) + pl.program_id(1)
    if dil == 1:
        token_rows = pl.ds(pl.multiple_of(first_token, ATT_BLOCK), ATT_BLOCK)
    else:
        token_rows = pl.ds(first_token, ATT_BLOCK, stride=dil)
    n_keys = (2 if has_prev else 1) * ATT_BLOCK
    lane = lax.broadcasted_iota(jnp.int32, (ATT_BLOCK, LANES), 1)
    even_lane = lane < DIL_HD
    even_head_qk = (lane % DIL_HD) < DIL_HD // 2
    qi = lax.broadcasted_iota(jnp.int32, (2 * ATT_BLOCK, n_keys), 0) % ATT_BLOCK
    kj = lax.broadcasted_iota(jnp.int32, (2 * ATT_BLOCK, n_keys), 1)
    if has_prev:
        dist = qi + ATT_BLOCK - kj
        valid = (dist >= 0) & (dist <= DIL_STEPS) & ((kj >= ATT_BLOCK) | (blk > 0))
    else:
        valid = kj <= qi
    lse_all = jnp.zeros((ATT_BLOCK, LANES), F32)
    for p in range(DIL_HEADS // 2):
        sl = slice(p * LANES, (p + 1) * LANES)
        vsl = slice(D_MODEL + p * LANES, D_MODEL + (p + 1) * LANES)
        qp = q_ref[:, sl]
        zero = jnp.zeros_like(qp)
        q2 = jnp.concatenate([jnp.where(even_head_qk, qp, zero), jnp.where(even_head_qk, zero, qp)], axis=0)
        if has_prev:
            k_pair = jnp.concatenate([kvp_ref[:, sl], kvc_ref[:, sl]], axis=0)
            v_pair = jnp.concatenate([kvp_ref[:, vsl], kvc_ref[:, vsl]], axis=0)
        else:
            k_pair, v_pair = kvc_ref[:, sl], kvc_ref[:, vsl]
        scores = jnp.where(valid, _dot_nt(q2, k_pair), -jnp.inf)
        o, lse_even, lse_odd = _pair_softmax_pv(scores, v_pair, even_lane)
        o_ref[p, token_rows, :] = o
        lse_all = jnp.where(lane == 2 * p, lse_even, jnp.where(lane == 2 * p + 1, lse_odd, lse_all))
    lse_ref[token_rows, :] = lse_all


def _dil_prompt_call(q, kvb):
    n_b, dil, length, _ = q.shape
    n_t = dil * length
    n_blk = length // ATT_BLOCK
    has_prev = n_blk > 1
    cur = pl.BlockSpec((None, None, ATT_BLOCK, D_MODEL), lambda b, r, i: (b, r, i, 0))
    kv_cur = pl.BlockSpec((None, None, ATT_BLOCK, 2 * D_MODEL), lambda b, r, i: (b, r, i, 0))
    kv_prev = pl.BlockSpec((None, None, ATT_BLOCK, 2 * D_MODEL), lambda b, r, i: (b, r, jnp.maximum(i - 1, 0), 0))
    if has_prev:
        in_specs, operands = [cur, kv_prev, kv_cur], (q, kvb, kvb)
    else:
        in_specs, operands = [cur, kv_cur], (q, kvb)
    return pl.pallas_call(
        functools.partial(_dil_prompt_kernel, has_prev, dil),
        grid=(n_b, dil, n_blk),
        in_specs=in_specs,
        out_specs=[pl.BlockSpec((None, D_MODEL // LANES, n_t, LANES), lambda b, r, i: (b, 0, 0, 0)),
                   pl.BlockSpec((None, n_t, LANES), lambda b, r, i: (b, 0, 0))],
        out_shape=[jax.ShapeDtypeStruct((n_b, D_MODEL // LANES, n_t, LANES), F32),
                   jax.ShapeDtypeStruct((n_b, n_t, LANES), F32)],
        compiler_params=_params("arbitrary", "arbitrary", "arbitrary"),
        name="dilated_prompt",
    )(*operands)


SAMPLE_HEAD_CHUNK = 4


def _sample_group_dense(t, q_col, k_win, v_win, k_new, v_new):
    n_t = k_new.shape[-1]
    pos = lax.broadcasted_iota(jnp.int32, (1, 1, k_win.shape[-1]), 2)
    pos_new = lax.broadcasted_iota(jnp.int32, (1, 1, n_t), 2)
    s_c = jnp.where(pos >= t, jnp.sum(k_win * q_col, axis=1, keepdims=True), -jnp.inf)
    s_n = jnp.where(pos_new <= t, jnp.sum(k_new * q_col, axis=1, keepdims=True), -jnp.inf)
    m = jnp.maximum(jnp.max(s_c, axis=2, keepdims=True), jnp.max(s_n, axis=2, keepdims=True))
    e_c, e_n = jnp.exp(s_c - m), jnp.exp(s_n - m)
    den = jnp.sum(e_c, axis=2, keepdims=True) + jnp.sum(e_n, axis=2, keepdims=True)
    o = (jnp.sum(v_win * e_c, axis=2, keepdims=True) + jnp.sum(v_new * e_n, axis=2, keepdims=True)) / den
    return o, m + jnp.log(den)


def _sample_group_dilated(dil, q_g, k_win, v_win, k_new, v_new):
    n_t = q_g.shape[-1]
    width = k_win.shape[-1]
    n_col = width // LANES
    res = lax.broadcasted_iota(jnp.int32, (1, 1, LANES), 2) % dil

    def columns(x):
        return [x[:, :, c * LANES:(c + 1) * LANES] for c in range(n_col)]

    q_pat = jnp.zeros(q_g.shape[:2] + (LANES,), F32)
    for t in range(n_t):
        q_pat = jnp.where(res == t, q_g[:, :, t:t + 1], q_pat)
    s_row = jnp.sum(k_win * jnp.concatenate([q_pat] * n_col, axis=2), axis=1, keepdims=True)
    s_new = jnp.sum(k_new * q_g, axis=1, keepdims=True)
    s_max = functools.reduce(jnp.maximum, columns(s_row))
    m_t = [jnp.maximum(jnp.max(jnp.where(res == t, s_max, -jnp.inf), axis=2, keepdims=True), s_new[:, :, t:t + 1])
           for t in range(n_t)]
    m_pat = jnp.full(s_max.shape, jnp.inf, F32)
    for t in range(n_t):
        m_pat = jnp.where(res == t, m_t[t], m_pat)
    e = jnp.exp(s_row - jnp.concatenate([m_pat] * n_col, axis=2))
    acc_fold = sum(columns(v_win * e))
    e_fold = sum(columns(e))
    out = []
    for t in range(n_t):
        e_new = jnp.exp(s_new[:, :, t:t + 1] - m_t[t])
        den = jnp.sum(jnp.where(res == t, e_fold, 0.0), axis=2, keepdims=True) + e_new
        o = (jnp.sum(jnp.where(res == t, acc_fold, 0.0), axis=2, keepdims=True) + e_new * v_new[:, :, t:t + 1]) / den
        out.append((o, m_t[t] + jnp.log(den)))
    return out


def _sample_attention(sample, new_ref, k_ref, v_ref, o_ref):
    n_t = o_ref.shape[1]
    n_s = k_ref.shape[-1]
    heads = k_ref.shape[0]
    new = jnp.transpose(new_ref[sample]).reshape(N_GROUPS + 2, heads, DIL_HD, n_t)
    pos_new = lax.broadcasted_iota(jnp.int32, (1, 1, n_t), 2)
    k_new, v_new = new[N_GROUPS], new[N_GROUPS + 1]
    groups = []
    for g, dil in enumerate(DIL_DILATIONS):
        lo = n_s - DIL_STEPS * dil
        k_win, v_win = k_ref[:, :, lo:], v_ref[:, :, lo:]
        q_g = new[g]
        if dil == 1:
            groups.append([_sample_group_dense(t, q_g[:, :, t:t + 1], k_win, v_win, k_new, v_new)
                           for t in range(n_t)])
        else:
            groups.append(_sample_group_dilated(dil, q_g, k_win, v_win, k_new, v_new))
    merged = jnp.zeros(k_new.shape, F32)
    for t in range(n_t):
        parts = [grp[t] for grp in groups]
        m = functools.reduce(jnp.maximum, [lse for _, lse in parts])
        wts = [jnp.exp(lse - m) for _, lse in parts]
        o = sum(wt * o for wt, (o, _) in zip(wts, parts)) / sum(wts)
        merged = jnp.where(pos_new == t, o, merged)
    o_ref[sample] = jnp.transpose(merged.reshape(heads * DIL_HD, n_t))


class _SampleAttentionGuest:
    def __init__(self, q, cache_k, cache_v, k_new, v_new):
        self.n_b, self.n_t = q.shape[:2]
        chunks = DIL_HEADS // SAMPLE_HEAD_CHUNK
        parts = [a.reshape(self.n_b, self.n_t, -1, chunks, SAMPLE_HEAD_CHUNK * DIL_HD) for a in (q, k_new, v_new)]
        new = jnp.concatenate(parts, axis=2).transpose(0, 1, 3, 2, 4).reshape(self.n_b, self.n_t, -1)
        self.operands = (new, cache_k, cache_v)
        wb = cache_k.shape[-1]
        assert wb == DIL_STEPS * DIL_DILATIONS[-1] and DIL_DILATIONS[0] == 1 and self.n_t <= DIL_DILATIONS[1]
        assert all(wb % dil == 0 and LANES % dil == 0 for dil in DIL_DILATIONS)
        self.wb = wb
        self.outputs = []

    def specs(self, chunk):
        hc = SAMPLE_HEAD_CHUNK
        width = hc * DIL_HD
        new = pl.BlockSpec((self.n_b, self.n_t, (N_GROUPS + 2) * width), lambda i: (0, 0, chunk))
        cache = pl.BlockSpec((None, hc, DIL_HD, self.wb), lambda i: (i, chunk, 0, 0))
        out = pl.BlockSpec((self.n_b, self.n_t, width), lambda i: (0, 0, 0))
        return [new, cache, cache], out, jax.ShapeDtypeStruct((self.n_b, self.n_t, width), F32)

    def result(self):
        assert len(self.outputs) * SAMPLE_HEAD_CHUNK == DIL_HEADS
        return jnp.concatenate(self.outputs, axis=-1).reshape(self.n_b * self.n_t, D_MODEL)


def _att_merge(o_refs, lse_refs):
    lses = [l_ref[...] for l_ref in lse_refs]
    m = functools.reduce(jnp.maximum, lses)
    e = [jnp.exp(l - m) for l in lses]
    den = sum(e)
    head = lax.broadcasted_iota(jnp.int32, (LANES, D_MODEL), 0)
    col_head = lax.broadcasted_iota(jnp.int32, (LANES, D_MODEL), 1) // DIL_HD
    spread = (head == col_head).astype(BF16)
    merged = None
    for o_ref, eg in zip(o_refs, e):
        wt = eg / den
        wt_hi = wt.astype(BF16)
        wt_lo = (wt - wt_hi.astype(F32)).astype(BF16)
        wide = _dot(wt_hi, spread) + _dot(wt_lo, spread)
        part = wide * jnp.concatenate([o_ref[c] for c in range(o_ref.shape[0])], axis=1)
        merged = part if merged is None else merged + part
    return merged


def _att_out_kernel(merged_input, *refs):
    if merged_input:
        o_ref, x_ref, gt_ref, wout_ref, g_ref, b_ref, y_ref = refs
        o = o_ref[...]
    else:
        o_refs, lse_refs = refs[0:N_GROUPS], refs[N_GROUPS:2 * N_GROUPS]
        x_ref, gt_ref, wout_ref, g_ref, b_ref, y_ref = refs[2 * N_GROUPS:]
        o = _att_merge(o_refs, lse_refs)
    y = _dot(o.astype(BF16), wout_ref[...])
    z = ALPHA * x_ref[...] + (1.0 + gt_ref[...]) * y
    y_ref[...] = _layer_norm(z, g_ref[...], b_ref[...])


def _att_out_call(rows, att, x, layer, w_out, ln_g, ln_b):
    merged_input = rows.per_row
    if merged_input:
        att_specs = [rows.row_spec(D_MODEL)]
        operands = [att]
    else:
        outs, lses = att
        tps = rows.tiles_per_sample
        o_spec = pl.BlockSpec((None, D_MODEL // LANES, rows.tile, LANES), lambda i: (i // tps, 0, i % tps, 0))
        lse_spec = pl.BlockSpec((None, rows.tile, LANES), lambda i: (i // tps, i % tps, 0))
        att_specs = [o_spec] * N_GROUPS + [lse_spec] * N_GROUPS
        operands = list(outs) + list(lses)
    return pl.pallas_call(
        functools.partial(_att_out_kernel, merged_input),
        grid=(rows.steps,),
        in_specs=att_specs + [
            rows.row_spec(D_MODEL),
            rows.mod_spec(layer, 5),
            _const_spec((D_MODEL, D_MODEL), (0, 0)),
        ] + _ln_specs(layer, 1),
        out_specs=rows.row_spec(D_MODEL),
        out_shape=jax.ShapeDtypeStruct((rows.n_rows, D_MODEL), F32),
        compiler_params=_params("arbitrary"),
        name="att_out",
    )(*operands, x, rows.mods, w_out, ln_g, ln_b)


def _rope_tables(positions):
    half = DIL_HD // 2
    inv = ROPE_THETA ** (-jnp.arange(half, dtype=F32) / half)
    ang = positions.astype(F32)[:, None] * inv[None, :]
    cos, sin = jnp.cos(ang), jnp.sin(ang)
    return jnp.concatenate([cos, cos, cos, cos], axis=1), jnp.concatenate([-sin, -sin, sin, sin], axis=1)


def _trunk(rows, ffn, x, n_b, n_t, positions, gla_state, w):
    cos_t, sin_t = _rope_tables(positions)
    if rows.per_row:
        cos_t, sin_t = jnp.tile(cos_t, (n_b, 1)), jnp.tile(sin_t, (n_b, 1))

    x = ffn(x, 0, 1)
    q, k, v, la, r = _gla_proj_call(rows.wide(), x, 0, w['gla_qkv'], w['gla_g'], w['gla_r'], w['gla_g2'], w['gla_bg'])
    recurrence = _gla_step_call if rows.per_row else _gla_rec_call
    o, s_new = recurrence(q.reshape(n_b, n_t, GLA_QK), k.reshape(n_b, n_t, GLA_QK), la.reshape(n_b, n_t, GLA_QK),
                          v.reshape(n_b, n_t, GLA_V), gla_state)
    x = _gla_out_call(rows.wide(), o.reshape(n_b * n_t, GLA_V), r, x, 0, w['gla_gon'], w['gla_out'], w['ln_g'], w['ln_b'])
    x = ffn(x, 0, 2)

    k_rows, v_rows, *kv_groups = _kv_call(rows, x, cos_t, sin_t, w['kv'])
    if rows.per_row:
        w['kv'] = kv_groups.pop()
    if rows.per_row:
        k_rows, v_rows = (a.reshape(n_b, n_t, DIL_HEADS, DIL_HD) for a in (k_rows, v_rows))
    else:
        k_rows, v_rows = (a.reshape(n_b, DIL_HEADS, DIL_HD, n_t).transpose(0, 3, 1, 2) for a in (k_rows, v_rows))

    x = ffn(x, 1, 1)
    qd = _q_call(rows, x, 1, cos_t, sin_t, w['q'])
    if rows.per_row:
        qd, w['q'] = qd
    att = yield qd, k_rows, v_rows, kv_groups
    x = _att_out_call(rows, att, x, 1, w['att_out'], w['ln_g'], w['ln_b'])
    x = ffn(x, 1, 2)
    return x, s_new, k_rows, v_rows


def _finish(trunk, att):
    try:
        trunk.send(att)
    except StopIteration as done:
        return done.value
    raise AssertionError("trunk yielded twice")


def kernel(x_prompt, x_sample, state_gla, cache_k, cache_v, c_prompt, c_sample, w_ada, b_ada, ln_g, ln_b, w_ffn1_up, w_ffn1_down, w_ffn2_up, w_ffn2_down, w_in_a, w_gate2_a, b_gate_a, g_onorm_a, w_out_a, w_ada_kv, b_ada_kv, w_kv, w_q_b, w_out_b):
    n_bp, n_tp = x_prompt.shape[:2]
    n_bs, n_ts = x_sample.shape[:2]
    assert N_A_LAYERS == 1 and DEPTH == 2
    assert n_tp % (ATT_BLOCK * DIL_DILATIONS[-1]) == 0 and n_tp % GLA_TILE == 0 and n_tp <= MAX_WINDOW
    assert n_ts < SUBLANES

    n_rs = n_bs * n_ts
    c_all = jnp.concatenate([jnp.repeat(c_sample, n_ts, axis=0), c_prompt], axis=0)
    mods = _ada_call(c_all, w_ada, b_ada)
    kv_mods = _ada_call(c_all, w_ada_kv[None], b_ada_kv[None])
    rows_p = _Rows(n_bp * n_tp, n_tp, mods[:, n_rs:, None, :], kv_mods[:, n_rs:, None, :])
    rows_s = _Rows(n_rs, n_ts, mods, kv_mods)

    w_in = w_in_a[0]
    gate_lo = 2 * GLA_QK + GLA_V
    w = {
        'ln_g': ln_g[:, :, None, :], 'ln_b': ln_b[:, :, None, :],
        'gla_qkv': w_in[:, :gate_lo].astype(BF16),
        'gla_g': jnp.pad(w_in[:, gate_lo:gate_lo + GLA_RANK], ((0, 0), (0, LANES - GLA_RANK))).astype(BF16),
        'gla_r': w_in[:, gate_lo + GLA_RANK:].astype(BF16),
        'gla_g2': jnp.pad(w_gate2_a[0], ((0, LANES - GLA_RANK), (0, 0))).astype(BF16),
        'gla_bg': b_gate_a[0][None, :],
        'gla_gon': g_onorm_a[0][None, :],
        'gla_out': w_out_a[0].astype(BF16),
        'kv': w_kv, 'q': w_q_b[0],
        'att_out': w_out_b[0].astype(BF16),
    }

    ffn_f32 = {1: (w_ffn1_up, w_ffn1_down), 2: (w_ffn2_up, w_ffn2_down)}
    ffn_args = {1: (0, 0), 2: (6, 2)}
    ffn_bf16 = {}

    def ffn_sample(x, layer, which):
        if (layer, which) in ffn_bf16:
            return _ffn_call(rows_s, x, layer, *ffn_args[which], *ffn_bf16[layer, which], w['ln_g'], w['ln_b'])
        ahead = (layer, which) == (DEPTH - 1, 1)
        y, *casts = _ffn_cast_call(rows_s, x, layer, *ffn_args[which], *ffn_f32[which], w['ln_g'], w['ln_b'],
                                   cast_only=(ffn_f32[2],) if ahead else ())
        ffn_bf16[layer, which] = casts[:3]
        if ahead:
            ffn_bf16[layer, 2] = casts[3:]
        return y

    def ffn_prompt(x, layer, which):
        return _ffn_call(rows_p, x, layer, *ffn_args[which], *ffn_bf16[layer, which], w['ln_g'], w['ln_b'],
                         guest=guest)

    trunk_s = _trunk(rows_s, ffn_sample, x_sample.reshape(-1, D_MODEL), n_bs, n_ts,
                     PAST_LEN + jnp.arange(n_ts), state_gla[0], w)
    q_s, _, _, (k_flat, v_flat) = next(trunk_s)
    guest = _SampleAttentionGuest(
        q_s.reshape(n_bs, n_ts, -1), cache_k.transpose(0, 2, 3, 1), cache_v.transpose(0, 2, 3, 1),
        k_flat.reshape(n_bs, n_ts, D_MODEL), v_flat.reshape(n_bs, n_ts, D_MODEL))

    zero_state = jnp.zeros((n_bp, GLA_HEADS, GLA_DK, GLA_DV), F32)
    trunk_p = _trunk(rows_p, ffn_prompt, x_prompt.reshape(-1, D_MODEL),
                     n_bp, n_tp, jnp.arange(n_tp), zero_state, w)
    q_p, _, _, kv_p = next(trunk_p)
    att_p = tuple(zip(*[_dil_prompt_call(q_p[g], kv_p[g]) for g in range(N_GROUPS)]))
    y_p, s_p, k_p, v_p = _finish(trunk_p, att_p)
    y_s, s_s, k_s, v_s = _finish(trunk_s, guest.result())
    return (y_p.reshape(n_bp, n_tp, D_MODEL), y_s.reshape(n_bs, n_ts, D_MODEL),
            s_p[None], s_s[None], k_p, v_p, k_s, v_s)
```

```python
import functools

import jax
import jax.numpy as jnp
from jax import lax
from jax.experimental import pallas as pl
from jax.experimental.pallas import tpu as pltpu

F32 = jnp.float32
BF16 = jnp.bfloat16

D_MODEL = 1024
DEPTH = 2
PAST_LEN = 16384
N_A_LAYERS = DEPTH // 2
GLA_HEADS = 4
GLA_QK = D_MODEL // 2
GLA_V = D_MODEL
GLA_DK = GLA_QK // GLA_HEADS
GLA_DV = GLA_V // GLA_HEADS
GLA_RANK = 16
GLA_TAU = 16.0
GLA_NORM_EPS = 1e-6
DIL_HEADS = 16
DIL_HD = D_MODEL // DIL_HEADS
DIL_WINDOWS = (128, 512, 2048)
DIL_DILATIONS = (1, 4, 16)
N_GROUPS = len(DIL_WINDOWS)
DIL_STEPS = 128
MAX_WINDOW = max(DIL_WINDOWS)
DIL_SCALE = DIL_HD ** -0.5
ROPE_THETA = 10000.0
D_FF = 2816
FFN_RES = 0.5
ALPHA = (2 * DEPTH) ** 0.25
LN_EPS = 1e-5

LANES = 128
SUBLANES = 8
ADA_MAX_COLS = 1536
ROW_TILE = 512
WIDE_TILE = 1024
LOG2_E = 1.4426950408889634
LN_2 = 0.6931471805599453
FF_CHUNK = 256
GLA_CHUNK = 128
GLA_TILE = 512
GLA_FACTOR_BOUND = 80.0
ATT_BLOCK = DIL_STEPS
VMEM_LIMIT = 56 * 1024 * 1024


def _params(*sem):
    return pltpu.CompilerParams(dimension_semantics=sem, vmem_limit_bytes=VMEM_LIMIT)


def _silu(x):
    return x / (1.0 + jnp.exp(-x))


def _layer_norm(z, g, b):
    mu = jnp.mean(z, axis=-1, keepdims=True)
    zc = z - mu
    var = jnp.mean(zc * zc, axis=-1, keepdims=True)
    return zc * lax.rsqrt(var + LN_EPS) * g + b


def _dot(a, b):
    return jnp.dot(a, b, preferred_element_type=F32)


def _dot_nt(a, b):
    return lax.dot_general(a, b, (((1,), (1,)), ((), ())), preferred_element_type=F32)


def _by_pair_block(fn, x):
    return jnp.concatenate([fn(x[:, c * LANES:(c + 1) * LANES]) for c in range(x.shape[1] // LANES)], axis=1)


def _rope(x, cos_t, sin_t):
    return _by_pair_block(lambda blk: blk * cos_t + pltpu.roll(blk, LANES // 2, 1) * sin_t, x)


def _pairs_to_heads(x):
    quarter = lax.broadcasted_iota(jnp.int32, (x.shape[0], LANES), 1) // (LANES // 4)

    def swap(blk):
        return jnp.where(quarter == 1, pltpu.roll(blk, 3 * LANES // 4, 1),
                         jnp.where(quarter == 2, pltpu.roll(blk, LANES // 4, 1), blk))

    return _by_pair_block(swap, x)


def _ada_kernel(c_ref, w_ref, b_ref, o_ref):
    s = _silu(c_ref[...]).astype(BF16)
    o_ref[...] = _dot(s, w_ref[...].astype(BF16)) + b_ref[...]


def _ada_call(c_all, w, b):
    n_l, _, n = w.shape
    rows = c_all.shape[0]
    tn = max(t for t in range(LANES, ADA_MAX_COLS + 1, LANES) if n % t == 0)
    return pl.pallas_call(
        _ada_kernel,
        grid=(n_l, n // tn),
        in_specs=[
            pl.BlockSpec((rows, D_MODEL), lambda l, j: (0, 0)),
            pl.BlockSpec((None, D_MODEL, tn), lambda l, j: (l, 0, j)),
            pl.BlockSpec((None, 1, tn), lambda l, j: (l, 0, j)),
        ],
        out_specs=pl.BlockSpec((None, rows, tn), lambda l, j: (l, 0, j)),
        out_shape=jax.ShapeDtypeStruct((n_l, rows, n), F32),
        compiler_params=_params("arbitrary", "arbitrary"),
        name="ada_mod",
    )(c_all, w, b.reshape(n_l, 1, n))


class _Rows:
    def __init__(self, n_rows, rows_per_sample, mods, kv_mods, tile=ROW_TILE):
        self.n_rows = n_rows
        self.rows_per_sample = rows_per_sample
        self.per_row = rows_per_sample < SUBLANES
        self.tile = n_rows if self.per_row else min(tile, rows_per_sample)
        self.steps = n_rows // self.tile
        self.tiles_per_sample = 1 if self.per_row else rows_per_sample // self.tile
        self.mods = mods
        self.kv_mods = kv_mods

    def wide(self):
        return _Rows(self.n_rows, self.rows_per_sample, self.mods, self.kv_mods, WIDE_TILE)

    def row_spec(self, width):
        return pl.BlockSpec((self.tile, width), lambda i: (i, 0))

    def mod_spec(self, layer, k):
        if self.per_row:
            return pl.BlockSpec((None, self.tile, D_MODEL), lambda i: (layer, 0, k))
        tps = self.tiles_per_sample
        return pl.BlockSpec((None, None, 1, D_MODEL), lambda i: (layer, i // tps, 0, k))


def _const_spec(shape, index):
    return pl.BlockSpec(shape, lambda i: index, pipeline_mode=pl.Buffered(1))


def _ln_specs(layer, j):
    return [_const_spec((None, None, 1, D_MODEL), (layer, j, 0, 0))] * 2


def _ffn_kernel(has_guest, x_ref, sh_ref, sc_ref, gt_ref, wa_ref, wu_ref, wdn_ref, g_ref, b_ref, *rest):
    if has_guest:
        *guest_in, o_ref, guest_out = rest
        _sample_attention(pl.program_id(0), *guest_in, guest_out)
    else:
        o_ref, = rest
    x = x_ref[...]
    h = (x * (1.0 + sc_ref[...]) + sh_ref[...]).astype(BF16)
    y = None
    for lo in range(0, D_FF, FF_CHUNK):
        a = _dot(h, wa_ref[:, lo:lo + FF_CHUNK])
        u = _dot(h, wu_ref[:, lo:lo + FF_CHUNK])
        part = _dot((_silu(a) * u).astype(BF16), wdn_ref[lo:lo + FF_CHUNK, :])
        y = part if y is None else y + part
    z = ALPHA * x + (1.0 + gt_ref[...]) * (FFN_RES * y)
    o_ref[...] = _layer_norm(z, g_ref[...], b_ref[...])


def _ffn_cast_kernel(n_cast_only, x_ref, sh_ref, sc_ref, gt_ref, wa_ref, wu_ref, wdn_ref, *rest):
    extra_in, (g_ref, b_ref, o_ref, wa_out, wu_out, wdn_out), rest = rest[:n_cast_only], rest[n_cast_only:n_cast_only + 6], rest[n_cast_only + 6:]
    extra_out, y_scr = rest[:n_cast_only], rest[n_cast_only]
    for src, dst in zip(extra_in, extra_out):
        dst[...] = src[...].astype(BF16)
    c = pl.program_id(0)
    wa, wu, wdn = wa_ref[...].astype(BF16), wu_ref[...].astype(BF16), wdn_ref[...].astype(BF16)
    wa_out[...], wu_out[...], wdn_out[...] = wa, wu, wdn
    x = x_ref[...]
    h = (x * (1.0 + sc_ref[...]) + sh_ref[...]).astype(BF16)
    part = _dot((_silu(_dot(h, wa)) * _dot(h, wu)).astype(BF16), wdn)

    @pl.when(c == 0)
    def _():
        y_scr[...] = part

    @pl.when(c > 0)
    def _():
        y_scr[...] += part

    @pl.when(c == pl.num_programs(0) - 1)
    def _():
        z = ALPHA * x + (1.0 + gt_ref[...]) * (FFN_RES * y_scr[...])
        o_ref[...] = _layer_norm(z, g_ref[...], b_ref[...])


def _ffn_cast_call(rows, x, layer, mod0, ln_j, w_up, w_down, ln_g, ln_b, cast_only=()):
    assert rows.steps == 1
    n_chunks = D_FF // FF_CHUNK

    def whole(spec):
        return pl.BlockSpec(spec.block_shape, lambda c, index_map=spec.index_map: index_map(0))

    f32_specs = [pl.BlockSpec((None, D_MODEL, FF_CHUNK), lambda c: (layer, 0, c)),
                 pl.BlockSpec((None, D_MODEL, FF_CHUNK), lambda c: (layer, 0, n_chunks + c)),
                 pl.BlockSpec((None, FF_CHUNK, D_MODEL), lambda c: (layer, c, 0))]
    bf16_specs = [pl.BlockSpec((D_MODEL, FF_CHUNK), lambda c: (0, c)),
                  pl.BlockSpec((D_MODEL, FF_CHUNK), lambda c: (0, c)),
                  pl.BlockSpec((FF_CHUNK, D_MODEL), lambda c: (c, 0))]
    bf16_shapes = [jax.ShapeDtypeStruct((D_MODEL, D_FF), BF16), jax.ShapeDtypeStruct((D_MODEL, D_FF), BF16),
                   jax.ShapeDtypeStruct((D_FF, D_MODEL), BF16)]
    extra = [w for up, down in cast_only for w in (up, up, down)]
    return pl.pallas_call(
        functools.partial(_ffn_cast_kernel, len(extra)),
        grid=(n_chunks,),
        in_specs=[whole(s) for s in (rows.row_spec(D_MODEL), rows.mod_spec(layer, mod0),
                                     rows.mod_spec(layer, mod0 + 1), rows.mod_spec(layer, mod0 + 2))]
        + f32_specs + f32_specs * len(cast_only) + [whole(s) for s in _ln_specs(layer, ln_j)],
        out_specs=[whole(rows.row_spec(D_MODEL))] + bf16_specs * (1 + len(cast_only)),
        out_shape=[jax.ShapeDtypeStruct((rows.n_rows, D_MODEL), F32)] + bf16_shapes * (1 + len(cast_only)),
        scratch_shapes=[pltpu.VMEM((rows.n_rows, D_MODEL), F32)],
        compiler_params=_params("arbitrary"),
        name="ffn_cast",
    )(x, rows.mods, rows.mods, rows.mods, w_up, w_up, w_down, *extra, ln_g, ln_b)


def _ffn_call(rows, x, layer, mod0, ln_j, w_gate, w_up, w_down, ln_g, ln_b, guest=None):
    in_specs = [
        rows.row_spec(D_MODEL),
        rows.mod_spec(layer, mod0), rows.mod_spec(layer, mod0 + 1), rows.mod_spec(layer, mod0 + 2),
        _const_spec((D_MODEL, D_FF), (0, 0)),
        _const_spec((D_MODEL, D_FF), (0, 0)),
        _const_spec((D_FF, D_MODEL), (0, 0)),
    ] + _ln_specs(layer, ln_j)
    operands = [x, rows.mods, rows.mods, rows.mods, w_gate, w_up, w_down, ln_g, ln_b]
    out_specs = [rows.row_spec(D_MODEL)]
    out_shape = [jax.ShapeDtypeStruct((rows.n_rows, D_MODEL), F32)]
    if guest is not None:
        assert guest.n_b == rows.steps
        guest_in, guest_out, guest_shape = guest.specs(len(guest.outputs))
        in_specs += guest_in
        operands += guest.operands
        out_specs.append(guest_out)
        out_shape.append(guest_shape)
    outs = pl.pallas_call(
        functools.partial(_ffn_kernel, guest is not None),
        grid=(rows.steps,),
        in_specs=in_specs,
        out_specs=out_specs,
        out_shape=out_shape,
        compiler_params=_params("arbitrary"),
        name="ffn",
    )(*operands)
    if guest is not None:
        guest.outputs.append(outs[1])
    return outs[0]


def _gla_proj_kernel(x_ref, sh_ref, sc_ref, wqkv_ref, wg_ref, wr_ref, wg2_ref, bg_ref,
                     q_ref, k_ref, v_ref, la_ref, r_ref):
    h = (x_ref[...] * (1.0 + sc_ref[...]) + sh_ref[...]).astype(BF16)
    qkv = _dot(h, wqkv_ref[...])
    q_ref[...] = qkv[:, :GLA_QK] * (GLA_DK ** -0.5)
    k_ref[...] = qkv[:, GLA_QK:2 * GLA_QK]
    v_ref[...] = qkv[:, 2 * GLA_QK:].astype(v_ref.dtype)
    g_lr = _dot(h, wg_ref[...]).astype(BF16)
    gate = _dot(g_lr, wg2_ref[...]) + bg_ref[...]
    log_sig = jnp.minimum(gate, 0.0) - jnp.log(1.0 + jnp.exp(-jnp.abs(gate)))
    la_ref[...] = log_sig / GLA_TAU
    r_ref[...] = _silu(_dot(h, wr_ref[...])).astype(r_ref.dtype)


def _gla_proj_call(rows, x, layer, w_qkv, w_g, w_r, w_g2, b_g):
    widths = (GLA_QK, GLA_QK, GLA_V, GLA_QK, GLA_V)
    narrow = F32 if rows.per_row else BF16
    dtypes = (F32, F32, narrow, F32, narrow)
    return pl.pallas_call(
        _gla_proj_kernel,
        grid=(rows.steps,),
        in_specs=[
            rows.row_spec(D_MODEL),
            rows.mod_spec(layer, 3), rows.mod_spec(layer, 4),
            _const_spec((D_MODEL, 2 * GLA_QK + GLA_V), (0, 0)),
            _const_spec((D_MODEL, LANES), (0, 0)),
            _const_spec((D_MODEL, GLA_V), (0, 0)),
            _const_spec((LANES, GLA_QK), (0, 0)),
            _const_spec((1, GLA_QK), (0, 0)),
        ],
        out_specs=[rows.row_spec(w) for w in widths],
        out_shape=[jax.ShapeDtypeStruct((rows.n_rows, w), dt) for w, dt in zip(widths, dtypes)],
        compiler_params=_params("arbitrary"),
        name="gla_proj",
    )(x, rows.mods, rows.mods, w_qkv, w_g, w_r, w_g2, b_g)


def _prefix_sum(tri, x):
    rest, total = x, None
    for _ in range(2):
        piece = rest.astype(BF16)
        rest = rest - piece.astype(F32)
        part = _dot(tri, piece)
        total = part if total is None else total + part
    return total


def _gla_scores_factored(qh, kh, bh, b_last):
    mid = 0.5 * b_last
    q_t = (qh * jnp.exp(bh - mid)).astype(BF16)
    k_t = (kh * jnp.exp(mid - bh)).astype(BF16)
    return _dot_nt(q_t, k_t)


def _gla_scores_elementwise(qh, kh, bh, k_scr, b_scr):
    k_scr[...] = kh
    b_scr[...] = bh
    col = lax.broadcasted_iota(jnp.int32, (GLA_CHUNK, GLA_CHUNK), 1)

    def body(j, acc):
        kj = k_scr[pl.ds(j, 1), :]
        bj = b_scr[pl.ds(j, 1), :]
        term = qh * kj * jnp.exp(jnp.minimum(bh - bj, 0.0))
        return acc + jnp.where(col == j, jnp.sum(term, axis=1, keepdims=True), 0.0)

    return lax.fori_loop(0, GLA_CHUNK, body, jnp.zeros((GLA_CHUNK, GLA_CHUNK), F32))


def _gla_chunk(rows, b, states, score_fn, q_ref, k_ref, v_ref, o_ref, causal):
    q, k = q_ref[rows, :], k_ref[rows, :]
    b_last = b[GLA_CHUNK - 1:GLA_CHUNK, :]
    q_in = q * jnp.exp(b)
    k_out = k * jnp.exp(b_last - b)
    new_states = []
    for hd, s_prev in enumerate(states):
        ks = slice(hd * GLA_DK, (hd + 1) * GLA_DK)
        vs = slice(hd * GLA_DV, (hd + 1) * GLA_DV)
        vh = v_ref[rows, vs].astype(BF16)
        o_inter = _dot(q_in[:, ks].astype(BF16), s_prev.astype(BF16))
        scores = jnp.where(causal, score_fn(q[:, ks], k[:, ks], b[:, ks], b_last[:, ks]), 0.0)
        o_ref[rows, vs] = (o_inter + _dot(scores.astype(BF16), vh)).astype(o_ref.dtype)
        decay_col = jnp.transpose(jnp.broadcast_to(jnp.exp(b_last[:, ks]), (GLA_DK, GLA_DK)))[:, :1]
        k_out_t = jnp.transpose(k_out[:, ks]).astype(BF16)
        new_states.append(decay_col * s_prev + _dot(k_out_t, vh))
    return new_states


def _gla_rec_kernel(q_ref, k_ref, v_ref, la_ref, s0_ref, o_ref, sfin_ref, s_scr, k_scr, b_scr):
    t = pl.program_id(1)

    @pl.when(t == 0)
    def _():
        s_scr[...] = s0_ref[...]

    row = lax.broadcasted_iota(jnp.int32, (GLA_CHUNK, GLA_CHUNK), 0)
    col = lax.broadcasted_iota(jnp.int32, (GLA_CHUNK, GLA_CHUNK), 1)
    causal = col <= row
    tri = causal.astype(BF16)
    n_chunks = GLA_TILE // GLA_CHUNK

    def decay_prefix(rows):
        return _prefix_sum(tri, la_ref[rows, :])

    chunk_rows = [slice(c * GLA_CHUNK, (c + 1) * GLA_CHUNK) for c in range(n_chunks)]
    prefixes = [decay_prefix(rows) for rows in chunk_rows]
    total_decay = functools.reduce(jnp.maximum, [-b[GLA_CHUNK - 1:GLA_CHUNK, :] for b in prefixes])
    factorable = jnp.max(total_decay) <= GLA_FACTOR_BOUND

    def refs():
        return q_ref, k_ref, v_ref, o_ref, causal

    @pl.when(factorable)
    def _():
        states = [s_scr[hd] for hd in range(GLA_HEADS)]
        for rows, b in zip(chunk_rows, prefixes):
            states = _gla_chunk(rows, b, states, _gla_scores_factored, *refs())
        for hd in range(GLA_HEADS):
            s_scr[hd] = states[hd]

    @pl.when(jnp.logical_not(factorable))
    def _():
        def chunk(c, carry):
            rows = pl.ds(pl.multiple_of(c * GLA_CHUNK, GLA_CHUNK), GLA_CHUNK)
            states = [s_scr[hd] for hd in range(GLA_HEADS)]
            elementwise = functools.partial(_gla_scores_elementwise, k_scr=k_scr, b_scr=b_scr)
            states = _gla_chunk(rows, decay_prefix(rows), states,
                                lambda qh, kh, bh, _: elementwise(qh, kh, bh), *refs())
            for hd in range(GLA_HEADS):
                s_scr[hd] = states[hd]
            return carry

        lax.fori_loop(0, n_chunks, chunk, 0)

    @pl.when(t == pl.num_programs(1) - 1)
    def _():
        sfin_ref[...] = s_scr[...]


def _gla_rec_call(q, k, la, v, s0):
    n_b, n_t = q.shape[:2]

    def seq(width):
        return pl.BlockSpec((None, GLA_TILE, width), lambda b, t: (b, t, 0))

    state = pl.BlockSpec((None, GLA_HEADS, GLA_DK, GLA_DV), lambda b, t: (b, 0, 0, 0))
    return pl.pallas_call(
        _gla_rec_kernel,
        grid=(n_b, n_t // GLA_TILE),
        in_specs=[seq(GLA_QK), seq(GLA_QK), seq(GLA_V), seq(GLA_QK), state],
        out_specs=[seq(GLA_V), state],
        out_shape=[jax.ShapeDtypeStruct((n_b, n_t, GLA_V), BF16),
                   jax.ShapeDtypeStruct((n_b, GLA_HEADS, GLA_DK, GLA_DV), F32)],
        scratch_shapes=[pltpu.VMEM((GLA_HEADS, GLA_DK, GLA_DV), F32),
                        pltpu.VMEM((GLA_CHUNK, GLA_DK), F32),
                        pltpu.VMEM((GLA_CHUNK, GLA_DK), F32)],
        compiler_params=_params("arbitrary", "arbitrary"),
        name="gla_recurrence",
    )(q, k, v, la, s0)


def _gla_step_kernel(q_ref, k_ref, la_ref, v_ref, s0_ref, o_ref, sfin_ref):
    n_t = v_ref.shape[0]
    q, k = q_ref[...], k_ref[...]
    row = lax.broadcasted_iota(jnp.int32, (n_t, n_t), 0)
    col = lax.broadcasted_iota(jnp.int32, (n_t, n_t), 1)
    b = _prefix_sum((col <= row).astype(BF16), la_ref[...])
    b_last = b[n_t - 1:n_t, :]
    q_in = q * jnp.exp(b)
    k_out = k * jnp.exp(b_last - b)
    for hd in range(GLA_HEADS):
        ks = slice(hd * GLA_DK, (hd + 1) * GLA_DK)
        vs = slice(hd * GLA_DV, (hd + 1) * GLA_DV)
        vh = v_ref[:, vs]
        s_prev = s0_ref[hd]
        o = _dot(q_in[:, ks].astype(BF16), s_prev.astype(BF16))
        for u in range(n_t):
            decay = jnp.exp(jnp.minimum(b[:, ks] - b[u:u + 1, ks], 0.0))
            score = jnp.sum(q[:, ks] * k[u:u + 1, ks] * decay, axis=1, keepdims=True)
            o = o + jnp.where(row[:, :1] >= u, score, 0.0) * vh[u:u + 1, :]
        o_ref[:, vs] = o
        decay_col = jnp.transpose(jnp.broadcast_to(jnp.exp(b_last[:, ks]), (SUBLANES, GLA_DK)))[:, :1]
        sfin_ref[hd] = decay_col * s_prev + lax.dot_general(
            k_out[:, ks].astype(BF16), vh.astype(BF16), (((0,), (0,)), ((), ())), preferred_element_type=F32)


def _gla_step_call(q, k, la, v, s0):
    n_b, n_t = q.shape[:2]

    def tok(width):
        return pl.BlockSpec((None, n_t, width), lambda b: (b, 0, 0))

    state = pl.BlockSpec((None, GLA_HEADS, GLA_DK, GLA_DV), lambda b: (b, 0, 0, 0))
    return pl.pallas_call(
        _gla_step_kernel,
        grid=(n_b,),
        in_specs=[tok(GLA_QK), tok(GLA_QK), tok(GLA_QK), tok(GLA_V), state],
        out_specs=[tok(GLA_V), state],
        out_shape=[jax.ShapeDtypeStruct((n_b, n_t, GLA_V), F32),
                   jax.ShapeDtypeStruct((n_b, GLA_HEADS, GLA_DK, GLA_DV), F32)],
        compiler_params=_params("arbitrary"),
        name="gla_step",
    )(q, k, la, v, s0)


def _gla_out_kernel(o_ref, r_ref, x_ref, gt_ref, gon_ref, wout_ref, g_ref, b_ref, y_ref):
    o = o_ref[...].astype(F32)
    parts = []
    for hd in range(GLA_HEADS):
        oh = o[:, hd * GLA_DV:(hd + 1) * GLA_DV]
        ms = jnp.mean(oh * oh, axis=-1, keepdims=True)
        parts.append(oh * lax.rsqrt(ms + GLA_NORM_EPS) * gon_ref[...])
    gated = (jnp.concatenate(parts, axis=1) * r_ref[...].astype(F32)).astype(BF16)
    y = _dot(gated, wout_ref[...])
    z = ALPHA * x_ref[...] + (1.0 + gt_ref[...]) * y
    y_ref[...] = _layer_norm(z, g_ref[...], b_ref[...])


def _gla_out_call(rows, o, r, x, layer, g_onorm, w_out, ln_g, ln_b):
    return pl.pallas_call(
        _gla_out_kernel,
        grid=(rows.steps,),
        in_specs=[
            rows.row_spec(GLA_V), rows.row_spec(GLA_V), rows.row_spec(D_MODEL),
            rows.mod_spec(layer, 5),
            _const_spec((1, GLA_DV), (0, 0)),
            _const_spec((GLA_V, D_MODEL), (0, 0)),
        ] + _ln_specs(layer, 1),
        out_specs=rows.row_spec(D_MODEL),
        out_shape=jax.ShapeDtypeStruct((rows.n_rows, D_MODEL), F32),
        compiler_params=_params("arbitrary"),
        name="gla_out",
    )(o, r, x, rows.mods, g_onorm, w_out, ln_g, ln_b)


def _gather_stride(scr, start, count, stride):
    return jnp.concatenate([scr[c, pl.ds(start, count, stride=stride), :] for c in range(scr.shape[0])], axis=1)


def _fill_slabs(scr, row0, x):
    for c in range(scr.shape[0]):
        scr[c, row0:row0 + x.shape[0], :] = x[:, c * LANES:(c + 1) * LANES]


def _store_by_residue(x, out_refs, scr_a, scr_b):
    n_rows = x.shape[0]
    s = DIL_DILATIONS[1]
    assert DIL_DILATIONS == (1, s, s * s)
    outs = {o.shape[0]: o for o in out_refs}
    if 1 in outs:
        outs[1][0] = x.astype(BF16)
    if s not in outs and s * s not in outs:
        return
    _fill_slabs(scr_a, 0, x)
    per_class = n_rows // s
    for r in range(s):
        xr = _gather_stride(scr_a, r, per_class, s)
        if s in outs:
            outs[s][r] = xr.astype(BF16)
        if s * s in outs:
            _fill_slabs(scr_b, r * per_class, xr)
    if s * s in outs:
        for r in range(s * s):
            xr = _gather_stride(scr_b, (r % s) * per_class + r // s, per_class // s, s)
            outs[s * s][r] = xr.astype(BF16)


def _kv_kernel(by_residue, x_ref, sh_ref, sc_ref, cos_ref, sin_ref, w_ref, k_ref, v_ref, *rest):
    h = (x_ref[...] * (1.0 + sc_ref[...]) + sh_ref[...]).astype(BF16)
    if by_residue:
        w = w_ref[...]
    else:
        w = jnp.concatenate([_pairs_to_heads(w_ref[:, :D_MODEL]), w_ref[:, D_MODEL:]], axis=1).astype(BF16)
        rest[2][...] = w
    kv = _dot(h, w)
    k = _rope(kv[:, :D_MODEL], cos_ref[...], sin_ref[...])
    v = kv[:, D_MODEL:]
    if by_residue:
        k_t = jnp.transpose(k)
        quarter = LANES // 4
        k_ref[...] = jnp.concatenate(
            [k_t[blk * LANES + j * quarter:blk * LANES + (j + 1) * quarter]
             for blk in range(D_MODEL // LANES) for j in (0, 2, 1, 3)], axis=0)
        v_ref[...] = jnp.transpose(v)
        _store_by_residue(jnp.concatenate([k, v], axis=1), rest[0:N_GROUPS], *rest[-2:])
    else:
        k = _pairs_to_heads(k)
        k_ref[...] = k.reshape(k_ref.shape)
        v_ref[...] = v.reshape(v_ref.shape)
        rest[0][...] = k
        rest[1][...] = v


def _rope_spec(rows):
    if rows.per_row:
        return pl.BlockSpec((rows.tile, LANES), lambda i: (0, 0))
    tps = rows.tiles_per_sample
    return pl.BlockSpec((rows.tile, LANES), lambda i: (i % tps, 0))


def _kv_mod_spec(rows, k):
    if rows.per_row:
        return pl.BlockSpec((None, rows.tile, D_MODEL), lambda i: (0, 0, k))
    tps = rows.tiles_per_sample
    return pl.BlockSpec((None, None, 1, D_MODEL), lambda i: (0, i // tps, 0, k))


def _residue_spec(rows, dil, width):
    tps = rows.tiles_per_sample
    return pl.BlockSpec((None, dil, rows.tile // dil, width), lambda i: (i // tps, 0, i % tps, 0))


def _residue_shapes(rows, width, dtype):
    n_b = rows.n_rows // rows.rows_per_sample
    return [jax.ShapeDtypeStruct((n_b, dil, rows.rows_per_sample // dil, width), dtype) for dil in DIL_DILATIONS]


def _permute_scratch(width=D_MODEL):
    return [pltpu.VMEM((width // LANES, ROW_TILE, LANES), F32)] * 2


def _kv_call(rows, x, cos_t, sin_t, w_kv):
    by_residue = not rows.per_row
    scratch = []
    if by_residue:
        tps = rows.tiles_per_sample
        n_b = rows.n_rows // rows.rows_per_sample
        out_specs = [pl.BlockSpec((None, D_MODEL, rows.tile), lambda i: (i // tps, 0, i % tps))] * 2
        out_shape = [jax.ShapeDtypeStruct((n_b, D_MODEL, rows.rows_per_sample), F32)] * 2
    else:
        assert rows.steps == 1
        out_specs = ([pl.BlockSpec((rows.tile, DIL_HEADS, DIL_HD), lambda i: (i, 0, 0))] * 2
                     + [rows.row_spec(D_MODEL)] * 2 + [pl.BlockSpec(w_kv.shape, lambda i: (0, 0))])
        out_shape = ([jax.ShapeDtypeStruct((rows.n_rows, DIL_HEADS, DIL_HD), F32)] * 2
                     + [jax.ShapeDtypeStruct((rows.n_rows, D_MODEL), F32)] * 2
                     + [jax.ShapeDtypeStruct(w_kv.shape, BF16)])
    if by_residue:
        out_specs += [_residue_spec(rows, dil, 2 * D_MODEL) for dil in DIL_DILATIONS]
        out_shape += _residue_shapes(rows, 2 * D_MODEL, BF16)
        scratch = _permute_scratch(2 * D_MODEL)
    return pl.pallas_call(
        functools.partial(_kv_kernel, by_residue),
        grid=(rows.steps,),
        in_specs=[
            rows.row_spec(D_MODEL),
            _kv_mod_spec(rows, 0), _kv_mod_spec(rows, 1),
            _rope_spec(rows), _rope_spec(rows),
            _const_spec((D_MODEL, 2 * D_MODEL), (0, 0)),
        ],
        out_specs=out_specs,
        out_shape=out_shape,
        scratch_shapes=scratch,
        compiler_params=_params("arbitrary"),
        name="kv_proj",
    )(x, rows.kv_mods, rows.kv_mods, cos_t, sin_t, w_kv)


def _q_kernel(by_residue, x_ref, sh_ref, sc_ref, cos_ref, sin_ref, w_ref, *rest):
    h = (x_ref[...] * (1.0 + sc_ref[...]) + sh_ref[...]).astype(BF16)
    scale = DIL_SCALE * LOG2_E if by_residue else DIL_SCALE
    if by_residue:
        w = w_ref[...]
    else:
        w = _pairs_to_heads(w_ref[...]).astype(BF16)
        rest[1][...] = w
    q = _rope(_dot(h, w), cos_ref[...] * scale, sin_ref[...] * scale)
    if by_residue:
        for g in range(N_GROUPS):
            _store_by_residue(q[:, g * D_MODEL:(g + 1) * D_MODEL], rest[g:g + 1], *rest[-2:])
    else:
        rest[0][...] = _pairs_to_heads(q)


def _q_call(rows, x, layer, cos_t, sin_t, w_q):
    width = N_GROUPS * D_MODEL
    by_residue = not rows.per_row
    if by_residue:
        out_specs = [_residue_spec(rows, dil, D_MODEL) for dil in DIL_DILATIONS]
        out_shape = _residue_shapes(rows, D_MODEL, BF16)
        scratch = _permute_scratch()
    else:
        assert rows.steps == 1
        out_specs = [rows.row_spec(width), pl.BlockSpec(w_q.shape, lambda i: (0, 0))]
        out_shape = [jax.ShapeDtypeStruct((rows.n_rows, width), F32), jax.ShapeDtypeStruct(w_q.shape, BF16)]
        scratch = []
    return pl.pallas_call(
        functools.partial(_q_kernel, by_residue),
        grid=(rows.steps,),
        in_specs=[
            rows.row_spec(D_MODEL),
            rows.mod_spec(layer, 3), rows.mod_spec(layer, 4),
            _rope_spec(rows), _rope_spec(rows),
            _const_spec((D_MODEL, width), (0, 0)),
        ],
        out_specs=out_specs,
        out_shape=out_shape,
        scratch_shapes=scratch,
        compiler_params=_params("arbitrary"),
        name="q_proj",
    )(x, rows.mods, rows.mods, cos_t, sin_t, w_q)


def _pair_softmax_pv(scores, v_pair, even_lane):
    half = scores.shape[0] // 2
    m = jnp.max(scores, axis=-1, keepdims=True)
    e = jnp.exp2(scores - m)
    den = jnp.sum(e, axis=-1, keepdims=True)
    o2 = _dot(e.astype(BF16), v_pair)
    o = jnp.where(even_lane, o2[:half], o2[half:]) / jnp.where(even_lane, den[:half], den[half:])
    lse2 = (m + jnp.log2(den)) * LN_2
    return o, lse2[:half], lse2[half:]


def _dil_prompt_kernel(has_prev, dil, q_ref, *refs):
    if has_prev:
        kvp_ref, kvc_ref, o_ref, lse_ref = refs
    else:
        kvc_ref, o_ref, lse_ref = refs
    blk = pl.program_id(2)
    first_token = blk * (ATT_BLOCK * dil) + pl.program_id(1)
    if dil == 1:
        token_rows = pl.ds(pl.multiple_of(first_token, ATT_BLOCK), ATT_BLOCK)
    else:
        token_rows = pl.ds(first_token, ATT_BLOCK, stride=dil)
    n_keys = (2 if has_prev else 1) * ATT_BLOCK
    lane = lax.broadcasted_iota(jnp.int32, (ATT_BLOCK, LANES), 1)
    even_lane = lane < DIL_HD
    even_head_qk = (lane % DIL_HD) < DIL_HD // 2
    qi = lax.broadcasted_iota(jnp.int32, (2 * ATT_BLOCK, n_keys), 0) % ATT_BLOCK
    kj = lax.broadcasted_iota(jnp.int32, (2 * ATT_BLOCK, n_keys), 1)
    if has_prev:
        dist = qi + ATT_BLOCK - kj
        valid = (dist >= 0) & (dist <= DIL_STEPS) & ((kj >= ATT_BLOCK) | (blk > 0))
    else:
        valid = kj <= qi
    lse_all = jnp.zeros((ATT_BLOCK, LANES), F32)
    for p in range(DIL_HEADS // 2):
        sl = slice(p * LANES, (p + 1) * LANES)
        vsl = slice(D_MODEL + p * LANES, D_MODEL + (p + 1) * LANES)
        qp = q_ref[:, sl]
        zero = jnp.zeros_like(qp)
        q2 = jnp.concatenate([jnp.where(even_head_qk, qp, zero), jnp.where(even_head_qk, zero, qp)], axis=0)
        if has_prev:
            k_pair = jnp.concatenate([kvp_ref[:, sl], kvc_ref[:, sl]], axis=0)
            v_pair = jnp.concatenate([kvp_ref[:, vsl], kvc_ref[:, vsl]], axis=0)
        else:
            k_pair, v_pair = kvc_ref[:, sl], kvc_ref[:, vsl]
        scores = jnp.where(valid, _dot_nt(q2, k_pair), -jnp.inf)
        o, lse_even, lse_odd = _pair_softmax_pv(scores, v_pair, even_lane)
        o_ref[p, token_rows, :] = o.astype(o_ref.dtype)
        lse_all = jnp.where(lane == 2 * p, lse_even, jnp.where(lane == 2 * p + 1, lse_odd, lse_all))
    lse_ref[token_rows, :] = lse_all


def _dil_prompt_call(q, kvb):
    n_b, dil, length, _ = q.shape
    n_t = dil * length
    n_blk = length // ATT_BLOCK
    has_prev = n_blk > 1
    cur = pl.BlockSpec((None, None, ATT_BLOCK, D_MODEL), lambda b, r, i: (b, r, i, 0))
    kv_cur = pl.BlockSpec((None, None, ATT_BLOCK, 2 * D_MODEL), lambda b, r, i: (b, r, i, 0))
    kv_prev = pl.BlockSpec((None, None, ATT_BLOCK, 2 * D_MODEL), lambda b, r, i: (b, r, jnp.maximum(i - 1, 0), 0))
    if has_prev:
        in_specs, operands = [cur, kv_prev, kv_cur], (q, kvb, kvb)
    else:
        in_specs, operands = [cur, kv_cur], (q, kvb)
    return pl.pallas_call(
        functools.partial(_dil_prompt_kernel, has_prev, dil),
        grid=(n_b, dil, n_blk),
        in_specs=in_specs,
        out_specs=[pl.BlockSpec((None, D_MODEL // LANES, n_t, LANES), lambda b, r, i: (b, 0, 0, 0)),
                   pl.BlockSpec((None, n_t, LANES), lambda b, r, i: (b, 0, 0))],
        out_shape=[jax.ShapeDtypeStruct((n_b, D_MODEL // LANES, n_t, LANES), BF16 if dil == 1 else F32),
                   jax.ShapeDtypeStruct((n_b, n_t, LANES), F32)],
        compiler_params=_params("arbitrary", "arbitrary", "arbitrary"),
        name="dilated_prompt",
    )(*operands)


SAMPLE_HEAD_CHUNK = 4


def _sample_group_dense(t, q_col, k_win, v_win, k_new, v_new):
    n_t = k_new.shape[-1]
    pos = lax.broadcasted_iota(jnp.int32, (1, 1, k_win.shape[-1]), 2)
    pos_new = lax.broadcasted_iota(jnp.int32, (1, 1, n_t), 2)
    s_c = jnp.where(pos >= t, jnp.sum(k_win * q_col, axis=1, keepdims=True), -jnp.inf)
    s_n = jnp.where(pos_new <= t, jnp.sum(k_new * q_col, axis=1, keepdims=True), -jnp.inf)
    m = jnp.maximum(jnp.max(s_c, axis=2, keepdims=True), jnp.max(s_n, axis=2, keepdims=True))
    e_c, e_n = jnp.exp(s_c - m), jnp.exp(s_n - m)
    den = jnp.sum(e_c, axis=2, keepdims=True) + jnp.sum(e_n, axis=2, keepdims=True)
    o = (jnp.sum(v_win * e_c, axis=2, keepdims=True) + jnp.sum(v_new * e_n, axis=2, keepdims=True)) / den
    return o, m + jnp.log(den)


def _sample_group_dilated(dil, q_g, k_win, v_win, k_new, v_new):
    n_t = q_g.shape[-1]
    width = k_win.shape[-1]
    n_col = width // LANES
    res = lax.broadcasted_iota(jnp.int32, (1, 1, LANES), 2) % dil

    def columns(x):
        return [x[:, :, c * LANES:(c + 1) * LANES] for c in range(n_col)]

    q_pat = jnp.zeros(q_g.shape[:2] + (LANES,), F32)
    for t in range(n_t):
        q_pat = jnp.where(res == t, q_g[:, :, t:t + 1], q_pat)
    s_row = jnp.sum(k_win * jnp.concatenate([q_pat] * n_col, axis=2), axis=1, keepdims=True)
    s_new = jnp.sum(k_new * q_g, axis=1, keepdims=True)
    s_max = functools.reduce(jnp.maximum, columns(s_row))
    m_t = [jnp.maximum(jnp.max(jnp.where(res == t, s_max, -jnp.inf), axis=2, keepdims=True), s_new[:, :, t:t + 1])
           for t in range(n_t)]
    m_pat = jnp.full(s_max.shape, jnp.inf, F32)
    for t in range(n_t):
        m_pat = jnp.where(res == t, m_t[t], m_pat)
    e = jnp.exp(s_row - jnp.concatenate([m_pat] * n_col, axis=2))
    acc_fold = sum(columns(v_win * e))
    e_fold = sum(columns(e))
    out = []
    for t in range(n_t):
        e_new = jnp.exp(s_new[:, :, t:t + 1] - m_t[t])
        den = jnp.sum(jnp.where(res == t, e_fold, 0.0), axis=2, keepdims=True) + e_new
        o = (jnp.sum(jnp.where(res == t, acc_fold, 0.0), axis=2, keepdims=True) + e_new * v_new[:, :, t:t + 1]) / den
        out.append((o, m_t[t] + jnp.log(den)))
    return out


def _sample_attention(sample, new_ref, k_ref, v_ref, o_ref):
    n_t = o_ref.shape[1]
    n_s = k_ref.shape[-1]
    heads = k_ref.shape[0]
    new = jnp.transpose(new_ref[sample]).reshape(N_GROUPS + 2, heads, DIL_HD, n_t)
    pos_new = lax.broadcasted_iota(jnp.int32, (1, 1, n_t), 2)
    k_new, v_new = new[N_GROUPS], new[N_GROUPS + 1]
    groups = []
    for g, dil in enumerate(DIL_DILATIONS):
        lo = n_s - DIL_STEPS * dil
        k_win, v_win = k_ref[:, :, lo:], v_ref[:, :, lo:]
        q_g = new[g]
        if dil == 1:
            groups.append([_sample_group_dense(t, q_g[:, :, t:t + 1], k_win, v_win, k_new, v_new)
                           for t in range(n_t)])
        else:
            groups.append(_sample_group_dilated(dil, q_g, k_win, v_win, k_new, v_new))
    merged = jnp.zeros(k_new.shape, F32)
    for t in range(n_t):
        parts = [grp[t] for grp in groups]
        m = functools.reduce(jnp.maximum, [lse for _, lse in parts])
        wts = [jnp.exp(lse - m) for _, lse in parts]
        o = sum(wt * o for wt, (o, _) in zip(wts, parts)) / sum(wts)
        merged = jnp.where(pos_new == t, o, merged)
    o_ref[sample] = jnp.transpose(merged.reshape(heads * DIL_HD, n_t))


class _SampleAttentionGuest:
    def __init__(self, q, cache_k, cache_v, k_new, v_new):
        self.n_b, self.n_t = q.shape[:2]
        chunks = DIL_HEADS // SAMPLE_HEAD_CHUNK
        parts = [a.reshape(self.n_b, self.n_t, -1, chunks, SAMPLE_HEAD_CHUNK * DIL_HD) for a in (q, k_new, v_new)]
        new = jnp.concatenate(parts, axis=2).transpose(0, 1, 3, 2, 4).reshape(self.n_b, self.n_t, -1)
        self.operands = (new, cache_k, cache_v)
        wb = cache_k.shape[-1]
        assert wb == DIL_STEPS * DIL_DILATIONS[-1] and DIL_DILATIONS[0] == 1 and self.n_t <= DIL_DILATIONS[1]
        assert all(wb % dil == 0 and LANES % dil == 0 for dil in DIL_DILATIONS)
        self.wb = wb
        self.outputs = []

    def specs(self, chunk):
        hc = SAMPLE_HEAD_CHUNK
        width = hc * DIL_HD
        new = pl.BlockSpec((self.n_b, self.n_t, (N_GROUPS + 2) * width), lambda i: (0, 0, chunk))
        cache = pl.BlockSpec((None, hc, DIL_HD, self.wb), lambda i: (i, chunk, 0, 0))
        out = pl.BlockSpec((self.n_b, self.n_t, width), lambda i: (0, 0, 0))
        return [new, cache, cache], out, jax.ShapeDtypeStruct((self.n_b, self.n_t, width), F32)

    def result(self):
        assert len(self.outputs) * SAMPLE_HEAD_CHUNK == DIL_HEADS
        return jnp.concatenate(self.outputs, axis=-1).reshape(self.n_b * self.n_t, D_MODEL)


def _att_merge(o_refs, lse_refs):
    lses = [l_ref[...] for l_ref in lse_refs]
    m = functools.reduce(jnp.maximum, lses)
    e = [jnp.exp(l - m) for l in lses]
    den = sum(e)
    head = lax.broadcasted_iota(jnp.int32, (LANES, D_MODEL), 0)
    col_head = lax.broadcasted_iota(jnp.int32, (LANES, D_MODEL), 1) // DIL_HD
    spread = (head == col_head).astype(BF16)
    merged = None
    for o_ref, eg in zip(o_refs, e):
        wt = eg / den
        wt_hi = wt.astype(BF16)
        wt_lo = (wt - wt_hi.astype(F32)).astype(BF16)
        wide = _dot(wt_hi, spread) + _dot(wt_lo, spread)
        part = wide * jnp.concatenate([o_ref[c].astype(F32) for c in range(o_ref.shape[0])], axis=1)
        merged = part if merged is None else merged + part
    return merged


def _att_out_kernel(merged_input, *refs):
    if merged_input:
        o_ref, x_ref, gt_ref, wout_ref, g_ref, b_ref, y_ref = refs
        o = o_ref[...]
    else:
        o_refs, lse_refs = refs[0:N_GROUPS], refs[N_GROUPS:2 * N_GROUPS]
        x_ref, gt_ref, wout_ref, g_ref, b_ref, y_ref = refs[2 * N_GROUPS:]
        o = _att_merge(o_refs, lse_refs)
    y = _dot(o.astype(BF16), wout_ref[...])
    z = ALPHA * x_ref[...] + (1.0 + gt_ref[...]) * y
    y_ref[...] = _layer_norm(z, g_ref[...], b_ref[...])


def _att_out_call(rows, att, x, layer, w_out, ln_g, ln_b):
    merged_input = rows.per_row
    if merged_input:
        att_specs = [rows.row_spec(D_MODEL)]
        operands = [att]
    else:
        outs, lses = att
        tps = rows.tiles_per_sample
        o_spec = pl.BlockSpec((None, D_MODEL // LANES, rows.tile, LANES), lambda i: (i // tps, 0, i % tps, 0))
        lse_spec = pl.BlockSpec((None, rows.tile, LANES), lambda i: (i // tps, i % tps, 0))
        att_specs = [o_spec] * N_GROUPS + [lse_spec] * N_GROUPS
        operands = list(outs) + list(lses)
    return pl.pallas_call(
        functools.partial(_att_out_kernel, merged_input),
        grid=(rows.steps,),
        in_specs=att_specs + [
            rows.row_spec(D_MODEL),
            rows.mod_spec(layer, 5),
            _const_spec((D_MODEL, D_MODEL), (0, 0)),
        ] + _ln_specs(layer, 1),
        out_specs=rows.row_spec(D_MODEL),
        out_shape=jax.ShapeDtypeStruct((rows.n_rows, D_MODEL), F32),
        compiler_params=_params("arbitrary"),
        name="att_out",
    )(*operands, x, rows.mods, w_out, ln_g, ln_b)


def _rope_tables(positions):
    half = DIL_HD // 2
    inv = ROPE_THETA ** (-jnp.arange(half, dtype=F32) / half)
    ang = positions.astype(F32)[:, None] * inv[None, :]
    cos, sin = jnp.cos(ang), jnp.sin(ang)
    return jnp.concatenate([cos, cos, cos, cos], axis=1), jnp.concatenate([-sin, -sin, sin, sin], axis=1)


def _trunk(rows, ffn, x, n_b, n_t, positions, gla_state, w):
    cos_t, sin_t = _rope_tables(positions)
    if rows.per_row:
        cos_t, sin_t = jnp.tile(cos_t, (n_b, 1)), jnp.tile(sin_t, (n_b, 1))

    x = ffn(x, 0, 1)
    q, k, v, la, r = _gla_proj_call(rows.wide(), x, 0, w['gla_qkv'], w['gla_g'], w['gla_r'], w['gla_g2'], w['gla_bg'])
    recurrence = _gla_step_call if rows.per_row else _gla_rec_call
    o, s_new = recurrence(q.reshape(n_b, n_t, GLA_QK), k.reshape(n_b, n_t, GLA_QK), la.reshape(n_b, n_t, GLA_QK),
                          v.reshape(n_b, n_t, GLA_V), gla_state)
    x = _gla_out_call(rows.wide(), o.reshape(n_b * n_t, GLA_V), r, x, 0, w['gla_gon'], w['gla_out'], w['ln_g'], w['ln_b'])
    x = ffn(x, 0, 2)

    k_rows, v_rows, *kv_groups = _kv_call(rows, x, cos_t, sin_t, w['kv'])
    if rows.per_row:
        w['kv'] = kv_groups.pop()
    if rows.per_row:
        k_rows, v_rows = (a.reshape(n_b, n_t, DIL_HEADS, DIL_HD) for a in (k_rows, v_rows))
    else:
        k_rows, v_rows = (a.reshape(n_b, DIL_HEADS, DIL_HD, n_t).transpose(0, 3, 1, 2) for a in (k_rows, v_rows))

    x = ffn(x, 1, 1)
    qd = _q_call(rows, x, 1, cos_t, sin_t, w['q'])
    if rows.per_row:
        qd, w['q'] = qd
    att = yield qd, k_rows, v_rows, kv_groups
    x = _att_out_call(rows, att, x, 1, w['att_out'], w['ln_g'], w['ln_b'])
    x = ffn(x, 1, 2)
    return x, s_new, k_rows, v_rows


def _finish(trunk, att):
    try:
        trunk.send(att)
    except StopIteration as done:
        return done.value
    raise AssertionError("trunk yielded twice")


def kernel(x_prompt, x_sample, state_gla, cache_k, cache_v, c_prompt, c_sample, w_ada, b_ada, ln_g, ln_b, w_ffn1_up, w_ffn1_down, w_ffn2_up, w_ffn2_down, w_in_a, w_gate2_a, b_gate_a, g_onorm_a, w_out_a, w_ada_kv, b_ada_kv, w_kv, w_q_b, w_out_b):
    n_bp, n_tp = x_prompt.shape[:2]
    n_bs, n_ts = x_sample.shape[:2]
    assert N_A_LAYERS == 1 and DEPTH == 2
    assert n_tp % (ATT_BLOCK * DIL_DILATIONS[-1]) == 0 and n_tp % GLA_TILE == 0 and n_tp <= MAX_WINDOW
    assert n_ts < SUBLANES

    n_rs = n_bs * n_ts
    c_all = jnp.concatenate([jnp.repeat(c_sample, n_ts, axis=0), c_prompt], axis=0)
    mods = _ada_call(c_all, w_ada, b_ada)
    kv_mods = _ada_call(c_all, w_ada_kv[None], b_ada_kv[None])
    rows_p = _Rows(n_bp * n_tp, n_tp, mods[:, n_rs:, None, :], kv_mods[:, n_rs:, None, :])
    rows_s = _Rows(n_rs, n_ts, mods, kv_mods)

    w_in = w_in_a[0]
    gate_lo = 2 * GLA_QK + GLA_V
    w = {
        'ln_g': ln_g[:, :, None, :], 'ln_b': ln_b[:, :, None, :],
        'gla_qkv': w_in[:, :gate_lo].astype(BF16),
        'gla_g': jnp.pad(w_in[:, gate_lo:gate_lo + GLA_RANK], ((0, 0), (0, LANES - GLA_RANK))).astype(BF16),
        'gla_r': w_in[:, gate_lo + GLA_RANK:].astype(BF16),
        'gla_g2': jnp.pad(w_gate2_a[0], ((0, LANES - GLA_RANK), (0, 0))).astype(BF16),
        'gla_bg': b_gate_a[0][None, :],
        'gla_gon': g_onorm_a[0][None, :],
        'gla_out': w_out_a[0].astype(BF16),
        'kv': w_kv, 'q': w_q_b[0],
        'att_out': w_out_b[0].astype(BF16),
    }

    ffn_f32 = {1: (w_ffn1_up, w_ffn1_down), 2: (w_ffn2_up, w_ffn2_down)}
    ffn_args = {1: (0, 0), 2: (6, 2)}
    ffn_bf16 = {}

    def ffn_sample(x, layer, which):
        if (layer, which) in ffn_bf16:
            return _ffn_call(rows_s, x, layer, *ffn_args[which], *ffn_bf16[layer, which], w['ln_g'], w['ln_b'])
        ahead = (layer, which) == (DEPTH - 1, 1)
        y, *casts = _ffn_cast_call(rows_s, x, layer, *ffn_args[which], *ffn_f32[which], w['ln_g'], w['ln_b'],
                                   cast_only=(ffn_f32[2],) if ahead else ())
        ffn_bf16[layer, which] = casts[:3]
        if ahead:
            ffn_bf16[layer, 2] = casts[3:]
        return y

    def ffn_prompt(x, layer, which):
        return _ffn_call(rows_p, x, layer, *ffn_args[which], *ffn_bf16[layer, which], w['ln_g'], w['ln_b'],
                         guest=guest)

    trunk_s = _trunk(rows_s, ffn_sample, x_sample.reshape(-1, D_MODEL), n_bs, n_ts,
                     PAST_LEN + jnp.arange(n_ts), state_gla[0], w)
    q_s, _, _, (k_flat, v_flat) = next(trunk_s)
    guest = _SampleAttentionGuest(
        q_s.reshape(n_bs, n_ts, -1), cache_k.transpose(0, 2, 3, 1), cache_v.transpose(0, 2, 3, 1),
        k_flat.reshape(n_bs, n_ts, D_MODEL), v_flat.reshape(n_bs, n_ts, D_MODEL))

    zero_state = jnp.zeros((n_bp, GLA_HEADS, GLA_DK, GLA_DV), F32)
    trunk_p = _trunk(rows_p, ffn_prompt, x_prompt.reshape(-1, D_MODEL),
                     n_bp, n_tp, jnp.arange(n_tp), zero_state, w)
    q_p, _, _, kv_p = next(trunk_p)
    att_p = tuple(zip(*[_dil_prompt_call(q_p[g], kv_p[g]) for g in range(N_GROUPS)]))
    y_p, s_p, k_p, v_p = _finish(trunk_p, att_p)
    y_s, s_s, k_s, v_s = _finish(trunk_s, guest.result())
    return (y_p.reshape(n_bp, n_tp, D_MODEL), y_s.reshape(n_bs, n_ts, D_MODEL),
            s_p[None], s_s[None], k_p, v_p, k_s, v_s)
```
